```python
import math
import jax, jax.numpy as jnp
from jax import lax
import numpy as np

D_MODEL = 1024
BATCH = 8
SEQ = 2048
DEPTH = 2
DEC_BATCH = 128
DEC_SEQ = 8
PAST_LEN = 16384
PAGE_SIZE = 128

D_MIX = D_MODEL
RWKV_DIM = D_MIX // 2
RWKV_HEAD = 64
RWKV_HEADS = RWKV_DIM // RWKV_HEAD
LRU_DIM = D_MIX // 4
LRU_BLOCKS = 4
LRU_BLOCK = LRU_DIM // LRU_BLOCKS
CONV_W = 4
LRU_C = 8.0
S5_DIM = D_MIX - RWKV_DIM - LRU_DIM
S5_GROUP = 16
S5_GROUPS = S5_DIM // S5_GROUP
S5_STATE = 64
D_IN = 3 * RWKV_DIM + 2 * LRU_DIM + S5_DIM
LORA_DECAY = 64
LORA_A = 64
LORA_MV = 32
LORA_GATE = 160
N_MEM = 256
MEM_HEADS = 4
MEM_HEAD_DIM = D_MODEL // MEM_HEADS
D_FF = -(-8 * D_MODEL // (3 * 256)) * 256
RMS_EPS = 1e-6
GN_EPS = 64e-5

kernel_name = 'hybrid_rwkv7_rglru_s5_decoder'


def rms_norm(x, g):
    xf = x.astype(jnp.float32)
    y = xf * lax.rsqrt(jnp.mean(xf * xf, axis=-1, keepdims=True) + RMS_EPS)
    return (y * g.astype(jnp.float32)).astype(x.dtype)


def _shift(seq, first):
    return jnp.concatenate([first[:, None].astype(seq.dtype), seq[:, :-1]], axis=1)


def _split_heads(t):
    return t.reshape(t.shape[:-1] + (RWKV_HEADS, RWKV_HEAD))


def _linear_combine(c1, c2):
    a1, b1 = c1
    a2, b2 = c2
    return a1 * a2, a2 * b1 + b2


def rwkv7_time_mix(h, h_prev, z_rkv, w_rkv, wkv0, v_first, vres, mu_rkv, mu_wag,
                   w0, w1, w2, a0, a1, a2, g1, g2, k_k, k_a, r_k, ln_w, ln_b):
    f32 = jnp.float32
    B, L, _ = h.shape
    dh = _shift(h, h_prev) - h
    z_sh = _shift(z_rkv, h_prev @ w_rkv)
    z_mix = z_rkv + (z_sh - z_rkv) * mu_rkv
    r, k, v = jnp.split(z_mix, 3, axis=-1)
    xw = h + dh * mu_wag[0]
    xa = h + dh * mu_wag[1]
    xg = h + dh * mu_wag[2]
    w_log = -jax.nn.softplus(-(w0 + jnp.tanh(xw @ w1) @ w2)) - 0.5
    a = jax.nn.sigmoid(a0 + (xa @ a1) @ a2)
    g = jax.nn.sigmoid(xg @ g1) @ g2
    if vres is None:
        v_first = v
    else:
        mu_v, v0, v1, v2 = vres
        xv = h + dh * mu_v
        v = v + (v_first - v) * jax.nn.sigmoid(v0 + (xv @ v1) @ v2)
    kk = _split_heads((k * k_k).astype(f32))
    kk = kk / jnp.maximum(jnp.linalg.norm(kk, axis=-1, keepdims=True), 1e-12)
    k = k * (1.0 + (a - 1.0) * k_a)
    rh, kh, vh, ah = [_split_heads(t.astype(f32)) for t in (r, k, v, a)]
    decay = jnp.exp(-jnp.exp(_split_heads(w_log.astype(f32))))
    seq = tuple(jnp.moveaxis(t, 1, 0) for t in (rh, decay, kh, vh, -kk, kk * ah))

    def step(S, inp):
        r_t, w_t, k_t, v_t, a_t, b_t = inp
        sa = jnp.einsum('bhij,bhj->bhi', S, a_t)
        S = (S * w_t[:, :, None, :] + sa[..., None] * b_t[:, :, None, :]
             + v_t[..., None] * k_t[:, :, None, :])
        return S, jnp.einsum('bhij,bhj->bhi', S, r_t)

    S_last, ys = lax.scan(step, wkv0.astype(f32), seq)
    y = jnp.moveaxis(ys, 0, 1)
    mu = jnp.mean(y, axis=-1, keepdims=True)
    var = jnp.mean(jnp.square(y - mu), axis=-1, keepdims=True)
    y = ((y - mu) * lax.rsqrt(var + GN_EPS)).reshape(B, L, RWKV_DIM)
    y = y * ln_w.astype(f32) + ln_b.astype(f32)
    bonus = jnp.sum(rh * kh * r_k.astype(f32), axis=-1, keepdims=True) * vh
    y = (y + bonus.reshape(B, L, RWKV_DIM)) * g.astype(f32)
    return y.astype(h.dtype), S_last, v_first


def rg_lru_branch(zx, zg, conv0, h0, conv_w, conv_b, wa, ba, wi, bi, lam, reset_first):
    f32 = jnp.float32
    B, L, C = zx.shape
    ext = jnp.concatenate([conv0.astype(zx.dtype), zx], axis=1)
    xc = conv_b + sum(ext[:, j:j + L] * conv_w[j] for j in range(CONV_W))
    new_conv = ext[:, L:]
    xb = xc.reshape(B, L, LRU_BLOCKS, LRU_BLOCK)
    gate_a = jax.nn.sigmoid((jnp.einsum('blhi,hij->blhj', xb, wa).reshape(B, L, C) + ba).astype(f32))
    gate_i = jax.nn.sigmoid((jnp.einsum('blhi,hij->blhj', xb, wi).reshape(B, L, C) + bi).astype(f32))
    log_a = -LRU_C * gate_a * jax.nn.softplus(-lam.astype(f32))
    a = jnp.exp(log_a)
    mult = jnp.sqrt(-jnp.expm1(2.0 * log_a))
    if reset_first:
        mult = mult.at[:, 0].set(1.0)
    b = mult * gate_i * xc.astype(f32)
    b = b.at[:, 0].add(a[:, 0] * h0.astype(f32))
    _, hs = lax.associative_scan(_linear_combine, (a, b), axis=1)
    y = hs * jax.nn.gelu(zg.astype(f32))
    return y.astype(zx.dtype), new_conv, hs[:, -1]


def s5_branch(u, s_re0, s_im0, a_re, a_im, log_dt, b_re, b_im, c_re, c_im, d, w_glu, b_glu):
    f32 = jnp.float32
    B, L, _ = u.shape
    lam = lax.complex(a_re.astype(f32), a_im.astype(f32))
    dt = jnp.exp(log_dt.astype(f32))[:, None]
    a_bar = jnp.exp(lam * dt)
    b_bar = ((a_bar - 1.0) / lam)[..., None] * lax.complex(b_re.astype(f32), b_im.astype(f32))
    uf = u.astype(f32)
    ug = uf.reshape(B, L, S5_GROUPS, S5_GROUP).astype(jnp.complex64)
    bu = jnp.einsum('blgp,gnp->blgn', ug, b_bar)
    bu = bu.at[:, 0].add(a_bar * lax.complex(s_re0.astype(f32), s_im0.astype(f32)))
    _, xs = lax.associative_scan(_linear_combine, (jnp.broadcast_to(a_bar, bu.shape), bu), axis=1)
    c = lax.complex(c_re.astype(f32), c_im.astype(f32))
    y = jnp.real(jnp.einsum('blgn,gpn->blgp', xs, c)).reshape(B, L, S5_DIM) + d.astype(f32) * uf
    y = jax.nn.gelu(y).astype(u.dtype)
    y1, y2 = jnp.split(y @ w_glu + b_glu, 2, axis=-1)
    out = y1 * jax.nn.sigmoid(y2)
    return out, jnp.real(xs[:, -1]), jnp.imag(xs[:, -1])


def mem_project(mem, g, wk, wv):
    B, M, _ = mem.shape
    m = rms_norm(mem, g)
    k = (m @ wk).reshape(B, M, MEM_HEADS, MEM_HEAD_DIM)
    v = (m @ wv).reshape(B, M, MEM_HEADS, MEM_HEAD_DIM)
    return k, v


def mem_attend(hq, mk, mv, wq, wo):
    B, L, _ = hq.shape
    q = (hq @ wq).reshape(B, L, MEM_HEADS, MEM_HEAD_DIM)
    s = jnp.einsum('blhd,bmhd->bhlm', q, mk).astype(jnp.float32) * (MEM_HEAD_DIM ** -0.5)
    p = jax.nn.softmax(s, axis=-1).astype(mv.dtype)
    o = jnp.einsum('bhlm,bmhd->blhd', p, mv).reshape(B, L, D_MODEL)
    return o @ wo


def swiglu(h, w_up, w_down):
    gt, up = jnp.split(h @ w_up, 2, axis=-1)
    return (jax.nn.silu(gt) * up) @ w_down


def setup_inputs(seed: int = 0) -> dict:
    key = jax.random.key(seed)
    ks = iter(jax.random.split(key, 80))
    f32 = jnp.float32

    def nrm(shape, scale):
        return scale * jax.random.normal(next(ks), shape, f32)

    def unif(shape, lo, hi):
        return jax.random.uniform(next(ks), shape, f32, lo, hi)

    def gain(shape):
        return 1.0 + nrm(shape, 0.01)

    Dn = DEPTH
    a_c = unif((Dn, LRU_DIM), 0.9, 0.999)
    a_base = a_c ** (1.0 / LRU_C)
    lru_lambda = jnp.log(a_base) - jnp.log1p(-a_base)
    s5_a_im = jnp.pi * jnp.broadcast_to(jnp.arange(S5_STATE, dtype=f32), (Dn, S5_GROUPS, S5_STATE))
    return {
        'x_prompt': nrm((BATCH, SEQ, D_MODEL), 1.0),
        'x_sample': nrm((DEC_BATCH, DEC_SEQ, D_MODEL), 1.0),
        'mem_prompt': nrm((BATCH, N_MEM, D_MODEL), 1.0),
        'state_shift': nrm((Dn, DEC_BATCH, D_MODEL), 1.0),
        'state_wkv': nrm((Dn, DEC_BATCH, RWKV_HEADS, RWKV_HEAD, RWKV_HEAD), 0.5),
        'state_conv': nrm((Dn, DEC_BATCH, CONV_W - 1, LRU_DIM), 1.0),
        'state_lru': nrm((Dn, DEC_BATCH, LRU_DIM), 0.5),
        'state_s5_re': nrm((Dn, DEC_BATCH, S5_GROUPS, S5_STATE), 0.3),
        'state_s5_im': nrm((Dn, DEC_BATCH, S5_GROUPS, S5_STATE), 0.3),
        'cache_mem_k': nrm((Dn, DEC_BATCH, N_MEM, MEM_HEADS, MEM_HEAD_DIM), 1.0),
        'cache_mem_v': nrm((Dn, DEC_BATCH, N_MEM, MEM_HEADS, MEM_HEAD_DIM), 1.0),
        'norm_mix': gain((Dn, D_MODEL)),
        'w_in': nrm((Dn, D_MODEL, D_IN), D_MODEL ** -0.5),
        'w_out': nrm((Dn, D_MIX, D_MODEL), 0.5 * D_MIX ** -0.5),
        'mu_rkv': unif((Dn, 3 * RWKV_DIM), 0.0, 1.0),
        'mu_wag': unif((Dn, 3, D_MODEL), 0.0, 1.0),
        'mu_v': unif((Dn - 1, D_MODEL), 0.0, 1.0),
        'w0': unif((Dn, RWKV_DIM), -6.0, -1.0),
        'w1': nrm((Dn, D_MODEL, LORA_DECAY), D_MODEL ** -0.5),
        'w2': nrm((Dn, LORA_DECAY, RWKV_DIM), 0.1 * LORA_DECAY ** -0.5),
        'a0': nrm((Dn, RWKV_DIM), 0.1),
        'a1': nrm((Dn, D_MODEL, LORA_A), D_MODEL ** -0.5),
        'a2': nrm((Dn, LORA_A, RWKV_DIM), 0.1 * LORA_A ** -0.5),
        'v0': nrm((Dn - 1, RWKV_DIM), 0.1),
        'v1': nrm((Dn - 1, D_MODEL, LORA_MV), D_MODEL ** -0.5),
        'v2': nrm((Dn - 1, LORA_MV, RWKV_DIM), 0.1 * LORA_MV ** -0.5),
        'g1': nrm((Dn, D_MODEL, LORA_GATE), D_MODEL ** -0.5),
        'g2': nrm((Dn, LORA_GATE, RWKV_DIM), LORA_GATE ** -0.5),
        'k_k': 0.85 + nrm((Dn, RWKV_DIM), 0.05),
        'k_a': 1.0 + nrm((Dn, RWKV_DIM), 0.05),
        'r_k': nrm((Dn, RWKV_HEADS, RWKV_HEAD), 0.1),
        'ln_x_w': gain((Dn, RWKV_DIM)),
        'ln_x_b': nrm((Dn, RWKV_DIM), 0.01),
        'conv_w': nrm((Dn, CONV_W, LRU_DIM), CONV_W ** -0.5),
        'conv_b': nrm((Dn, LRU_DIM), 0.01),
        'lru_wa': nrm((Dn, LRU_BLOCKS, LRU_BLOCK, LRU_BLOCK), LRU_BLOCK ** -0.5),
        'lru_ba': nrm((Dn, LRU_DIM), 0.01),
        'lru_wi': nrm((Dn, LRU_BLOCKS, LRU_BLOCK, LRU_BLOCK), LRU_BLOCK ** -0.5),
        'lru_bi': nrm((Dn, LRU_DIM), 0.01),
        'lru_lambda': lru_lambda,
        's5_a_re': -0.5 + nrm((Dn, S5_GROUPS, S5_STATE), 0.01),
        's5_a_im': s5_a_im + nrm((Dn, S5_GROUPS, S5_STATE), 0.01),
        's5_log_dt': unif((Dn, S5_GROUPS), math.log(0.001), math.log(0.1)),
        's5_b_re': nrm((Dn, S5_GROUPS, S5_STATE, S5_GROUP), (2 * S5_GROUP) ** -0.5),
        's5_b_im': nrm((Dn, S5_GROUPS, S5_STATE, S5_GROUP), (2 * S5_GROUP) ** -0.5),
        's5_c_re': nrm((Dn, S5_GROUPS, S5_GROUP, S5_STATE), (2 * S5_STATE) ** -0.5),
        's5_c_im': nrm((Dn, S5_GROUPS, S5_GROUP, S5_STATE), (2 * S5_STATE) ** -0.5),
        's5_d': nrm((Dn, S5_DIM), 0.5),
        's5_w_glu': nrm((Dn, S5_DIM, 2 * S5_DIM), S5_DIM ** -0.5),
        's5_b_glu': nrm((Dn, 2 * S5_DIM), 0.01),
        'norm_mem_q': gain((Dn, D_MODEL)),
        'norm_mem_kv': gain((Dn, D_MODEL)),
        'mem_wq': nrm((Dn, D_MODEL, D_MODEL), D_MODEL ** -0.5),
        'mem_wk': nrm((Dn, D_MODEL, D_MODEL), D_MODEL ** -0.5),
        'mem_wv': nrm((Dn, D_MODEL, D_MODEL), D_MODEL ** -0.5),
        'mem_wo': nrm((Dn, D_MODEL, D_MODEL), 0.5 * D_MODEL ** -0.5),
        'norm_ffn': gain((Dn, D_MODEL)),
        'ffn_w_up': nrm((Dn, D_MODEL, 2 * D_FF), D_MODEL ** -0.5),
        'ffn_w_down': nrm((Dn, D_FF, D_MODEL), 0.5 * D_FF ** -0.5),
        'norm_final': gain((D_MODEL,)),
    }


def reference(x_prompt, x_sample, mem_prompt, state_shift, state_wkv, state_conv, state_lru,
              state_s5_re, state_s5_im, cache_mem_k, cache_mem_v,
              norm_mix, w_in, w_out, mu_rkv, mu_wag, mu_v, w0, w1, w2, a0, a1, a2, v0, v1, v2,
              g1, g2, k_k, k_a, r_k, ln_x_w, ln_x_b, conv_w, conv_b, lru_wa, lru_ba, lru_wi, lru_bi,
              lru_lambda, s5_a_re, s5_a_im, s5_log_dt, s5_b_re, s5_b_im, s5_c_re, s5_c_im, s5_d,
              s5_w_glu, s5_b_glu, norm_mem_q, norm_mem_kv, mem_wq, mem_wk, mem_wv, mem_wo,
              norm_ffn, ffn_w_up, ffn_w_down, norm_final):
    R3 = 3 * RWKV_DIM

    def run_group(x, mem_ks, mem_vs, shift0, wkv0, conv0, lru0, re0, im0, reset_first):
        sh_l, wkv_l, conv_l, lru_l, re_l, im_l = [], [], [], [], [], []
        v_first = None
        for l in range(DEPTH):
            h = rms_norm(x, norm_mix[l])
            z = h @ w_in[l]
            z_rkv = z[..., :R3]
            z_lx = z[..., R3:R3 + LRU_DIM]
            z_lg = z[..., R3 + LRU_DIM:R3 + 2 * LRU_DIM]
            z_s5 = z[..., R3 + 2 * LRU_DIM:]
            vres = None if l == 0 else (mu_v[l - 1], v0[l - 1], v1[l - 1], v2[l - 1])
            y_a, wkv_new, v_first = rwkv7_time_mix(
                h, shift0[l], z_rkv, w_in[l][:, :R3], wkv0[l], v_first, vres, mu_rkv[l], mu_wag[l],
                w0[l], w1[l], w2[l], a0[l], a1[l], a2[l], g1[l], g2[l], k_k[l], k_a[l], r_k[l],
                ln_x_w[l], ln_x_b[l])
            y_b, conv_new, lru_new = rg_lru_branch(
                z_lx, z_lg, conv0[l], lru0[l], conv_w[l], conv_b[l], lru_wa[l], lru_ba[l],
                lru_wi[l], lru_bi[l], lru_lambda[l], reset_first)
            y_c, re_new, im_new = s5_branch(
                z_s5, re0[l], im0[l], s5_a_re[l], s5_a_im[l], s5_log_dt[l], s5_b_re[l], s5_b_im[l],
                s5_c_re[l], s5_c_im[l], s5_d[l], s5_w_glu[l], s5_b_glu[l])
            x = x + jnp.concatenate([y_a, y_b, y_c], axis=-1) @ w_out[l]
            x = x + mem_attend(rms_norm(x, norm_mem_q[l]), mem_ks[l], mem_vs[l], mem_wq[l], mem_wo[l])
            x = x + swiglu(rms_norm(x, norm_ffn[l]), ffn_w_up[l], ffn_w_down[l])
            sh_l.append(h[:, -1])
            wkv_l.append(wkv_new)
            conv_l.append(conv_new)
            lru_l.append(lru_new)
            re_l.append(re_new)
            im_l.append(im_new)
        return (rms_norm(x, norm_final), jnp.stack(sh_l), jnp.stack(wkv_l), jnp.stack(conv_l),
                jnp.stack(lru_l), jnp.stack(re_l), jnp.stack(im_l))

    Bp = x_prompt.shape[0]
    f32 = jnp.float32
    mem_kv = [mem_project(mem_prompt, norm_mem_kv[l], mem_wk[l], mem_wv[l]) for l in range(DEPTH)]
    mem_k_p = jnp.stack([kv[0] for kv in mem_kv])
    mem_v_p = jnp.stack([kv[1] for kv in mem_kv])
    (y_prompt, shift_p, wkv_p, conv_p, lru_p, re_p, im_p) = run_group(
        x_prompt, mem_k_p, mem_v_p,
        jnp.zeros((DEPTH, Bp, D_MODEL), x_prompt.dtype),
        jnp.zeros((DEPTH, Bp, RWKV_HEADS, RWKV_HEAD, RWKV_HEAD), f32),
        jnp.zeros((DEPTH, Bp, CONV_W - 1, LRU_DIM), x_prompt.dtype),
        jnp.zeros((DEPTH, Bp, LRU_DIM), f32),
        jnp.zeros((DEPTH, Bp, S5_GROUPS, S5_STATE), f32),
        jnp.zeros((DEPTH, Bp, S5_GROUPS, S5_STATE), f32),
        True)
    (y_sample, shift_s, wkv_s, conv_s, lru_s, re_s, im_s) = run_group(
        x_sample, cache_mem_k, cache_mem_v, state_shift, state_wkv, state_conv, state_lru,
        state_s5_re, state_s5_im, False)
    return (y_prompt, y_sample, shift_p, shift_s, wkv_p, wkv_s, conv_p, conv_s, lru_p, lru_s,
            re_p, re_s, im_p, im_s, mem_k_p, mem_v_p)
```

```python
import functools
import math

import jax
import jax.numpy as jnp
from jax import lax
from jax.experimental import pallas as pl
from jax.experimental.pallas import tpu as pltpu

F32 = jnp.float32
BF16 = jnp.bfloat16

D_MODEL = 1024
RWKV_DIM = 512
RWKV_HEAD = 64
RWKV_HEADS = 8
HEAD_PAIR = 2 * RWKV_HEAD
N_PAIRS = RWKV_DIM // HEAD_PAIR
LRU_DIM = 256
LRU_BLOCKS = 4
CONV_W = 4
LRU_C = 8.0
S5_DIM = 256
S5_GROUP = 16
S5_GROUPS = 16
S5_STATE = 64
S5_WIDTH = S5_GROUPS * S5_STATE
R3 = 3 * RWKV_DIM
D_IN = R3 + 2 * LRU_DIM + S5_DIM
N_MEM = 256
MEM_HEADS = 4
MEM_HEAD_DIM = 256
RMS_EPS = 1e-6
GN_EPS = 64e-5
WKV_CHUNK = 64

V7X_VMEM_LIMIT = 56 * 1024 * 1024


def _params(*sem):
    return pltpu.CompilerParams(dimension_semantics=sem, vmem_limit_bytes=V7X_VMEM_LIMIT)


def _tile(m, mult, target):
    best = mult
    t = mult
    while t <= min(m, target):
        if m % t == 0:
            best = t
        t += mult
    assert m % best == 0, (m, mult, target)
    return best


def _const_spec(shape):
    nd = len(shape)
    return pl.BlockSpec(shape, lambda *_: (0,) * nd)


def _bdot(a, b):
    return jnp.dot(a.astype(BF16), b.astype(BF16), preferred_element_type=F32)


def _bdot_nt(a, b):
    return lax.dot_general(a.astype(BF16), b.astype(BF16), (((1,), (1,)), ((), ())),
                           preferred_element_type=F32)


def _bdot_tn(a, b):
    return lax.dot_general(a.astype(BF16), b.astype(BF16), (((0,), (0,)), ((), ())),
                           preferred_element_type=F32)


def _split2(x):
    hi = x.astype(BF16)
    lo = (x - hi.astype(F32)).astype(BF16)
    return hi, lo


def _dot_exact_rhs(x, w):
    hi, lo = _split2(x)
    return (jnp.dot(hi, w, preferred_element_type=F32)
            + jnp.dot(lo, w, preferred_element_type=F32))


def _rms(x, g):
    return x * lax.rsqrt(jnp.mean(x * x, axis=-1, keepdims=True) + RMS_EPS) * g


def _softplus(x):
    return jnp.maximum(x, 0.0) + jnp.log1p(jnp.exp(-jnp.abs(x)))


def _sigmoid(x):
    return 1.0 / (1.0 + jnp.exp(-x))


def _gelu(x):
    c = math.sqrt(2.0 / math.pi)
    return x * (0.5 * (1.0 + jnp.tanh(c * (x + 0.044715 * (x * x * x)))))


def _mix_in_kernel(has_vres, nb, tm, *refs):
    if has_vres:
        (x_ref, first_ref, g_ref, win_ref, murkv_ref, muwag_ref, w0_ref, w1_ref, w2_ref,
         a0_ref, a1_ref, a2_ref, g1_ref, g2_ref, kk_ref, ka_ref, ones_ref,
         muv_ref, v0_ref, v1_ref, v2_ref, vfirst_ref,
         zrest_ref, r_ref, lw_ref, k_ref, v_ref, na_ref, bb_ref, gate_ref, hlast_ref,
         hbuf, zbuf) = refs
    else:
        (x_ref, first_ref, g_ref, win_ref, murkv_ref, muwag_ref, w0_ref, w1_ref, w2_ref,
         a0_ref, a1_ref, a2_ref, g1_ref, g2_ref, kk_ref, ka_ref, ones_ref,
         zrest_ref, r_ref, lw_ref, k_ref, v_ref, na_ref, bb_ref, gate_ref, hlast_ref,
         hbuf, zbuf) = refs

    @pl.when(pl.program_id(0) == 0)
    def _():
        first = first_ref[...]
        hbuf[0:nb, :] = first
        zbuf[0:nb, :] = _bdot(first, win_ref[:, 0:R3])

    h = _rms(x_ref[...], g_ref[...])
    hbuf[nb:nb + tm, :] = h
    hp = hbuf[0:tm, :]
    hbuf[0:nb, :] = hbuf[tm:tm + nb, :]
    hlast_ref[...] = h[tm - nb:tm, :]

    z = _bdot(h, win_ref[...])
    zrest_ref[...] = z[:, R3:D_IN]
    z_rkv = z[:, 0:R3]
    zbuf[nb:nb + tm, :] = z_rkv
    z_sh = zbuf[0:tm, :]
    zbuf[0:nb, :] = zbuf[tm:tm + nb, :]
    zmix = z_rkv + (z_sh - z_rkv) * murkv_ref[...]
    r = zmix[:, 0:RWKV_DIM]
    k = zmix[:, RWKV_DIM:2 * RWKV_DIM]
    v = zmix[:, 2 * RWKV_DIM:R3]

    dh = hp - h
    xw = h + dh * muwag_ref[0:1, :]
    xa = h + dh * muwag_ref[1:2, :]
    xg = h + dh * muwag_ref[2:3, :]
    w_pre = w0_ref[...] + _bdot(jnp.tanh(_bdot(xw, w1_ref[...])), w2_ref[...])
    w_log = -_softplus(-w_pre) - 0.5
    lw_ref[...] = -jnp.exp(w_log)
    a = _sigmoid(a0_ref[...] + _bdot(_bdot(xa, a1_ref[...]), a2_ref[...]))
    gate_ref[...] = _bdot(_sigmoid(_bdot(xg, g1_ref[...])), g2_ref[...])
    if has_vres:
        xv = h + dh * muv_ref[...]
        mix = _sigmoid(v0_ref[...] + _bdot(_bdot(xv, v1_ref[...]), v2_ref[...]))
        v = v + (vfirst_ref[...] - v) * mix
    kk = k * kk_ref[...]
    ss = _dot_exact_rhs(kk * kk, ones_ref[...])
    kk = kk / jnp.maximum(jnp.sqrt(ss), 1e-12)
    r_ref[...] = r
    k_ref[...] = k * (1.0 + (a - 1.0) * ka_ref[...])
    v_ref[...] = v
    na_ref[...] = -kk
    bb_ref[...] = kk * a


def _mix_in(x, first, p, vres, v_first, nb):
    m = x.shape[0]
    tm = _tile(m, max(nb, 8), 256)
    has_vres = vres is not None
    row = lambda w: pl.BlockSpec((tm, w), lambda i: (i, 0))
    ins = [x, first, p["norm_mix"], p["w_in"], p["mu_rkv"], p["mu_wag"], p["w0"], p["w1"], p["w2"],
           p["a0"], p["a1"], p["a2"], p["g1"], p["g2"], p["k_k"], p["k_a"], p["ones_head"]]
    specs = [row(D_MODEL)] + [_const_spec(a.shape) for a in ins[1:]]
    if has_vres:
        extra = [vres["mu_v"], vres["v0"], vres["v1"], vres["v2"]]
        ins += extra + [v_first]
        specs += [_const_spec(a.shape) for a in extra] + [row(RWKV_DIM)]
    wide = jax.ShapeDtypeStruct((m, RWKV_DIM), F32)
    outs = [jax.ShapeDtypeStruct((m, D_IN - R3), F32)] + [wide] * 7 + [jax.ShapeDtypeStruct((nb, D_MODEL), F32)]
    out_specs = [row(D_IN - R3)] + [row(RWKV_DIM)] * 7 + [_const_spec((nb, D_MODEL))]
    return pl.pallas_call(
        functools.partial(_mix_in_kernel, has_vres, nb, tm),
        grid=(m // tm,),
        in_specs=specs,
        out_specs=out_specs,
        out_shape=outs,
        scratch_shapes=[pltpu.VMEM((tm + nb, D_MODEL), F32), pltpu.VMEM((tm + nb, R3), F32)],
        compiler_params=_params("arbitrary"),
        name="mix_in",
    )(*ins)


def _wkv_kernel(c, n_chunks, r_ref, lw_ref, k_ref, v_ref, na_ref, bb_ref, gate_ref, s0_ref,
                rk_ref, lnw_ref, lnb_ref, tri_ref, mean_ref, ones_ref,
                y_ref, sout_ref, s_scr):
    li = pl.program_id(1)

    @pl.when(li == 0)
    def _():
        s_scr[...] = s0_ref[0]

    lane = lax.broadcasted_iota(jnp.int32, (1, HEAD_PAIR), 1)
    head_masks = (lane < RWKV_HEAD, lane >= RWKV_HEAD)
    trow = lax.broadcasted_iota(jnp.int32, (c, 2 * c), 0)
    tcol = lax.broadcasted_iota(jnp.int32, (c, 2 * c), 1)
    scol = jnp.where(tcol >= c, tcol - c, tcol)
    strict_k = (scol < trow) & (tcol >= c)
    incl = scol <= trow
    lrow = lax.broadcasted_iota(jnp.int32, (c, c), 0)
    lcol = lax.broadcasted_iota(jnp.int32, (c, c), 1)
    strict = lcol < lrow
    eye = (lrow == lcol).astype(F32)
    brow = lax.broadcasted_iota(jnp.int32, (HEAD_PAIR, HEAD_PAIR), 0)
    bcol = lax.broadcasted_iota(jnp.int32, (HEAD_PAIR, HEAD_PAIR), 1)
    same_head = jnp.bitwise_xor(brow - RWKV_HEAD, bcol - RWKV_HEAD) >= 0
    n_double = max(int(math.log2(c)) - 1, 0)
    tri = tri_ref[...]
    zeros_c = jnp.zeros((c, HEAD_PAIR), F32)

    def chunk(ci, carry):
        rows = pl.ds(pl.multiple_of(ci * c, c), c)
        for p in range(N_PAIRS):
            cols = slice(p * HEAD_PAIR, (p + 1) * HEAD_PAIR)
            lw = lw_ref[rows, cols]
            r = r_ref[rows, cols]
            k = k_ref[rows, cols]
            v = v_ref[rows, cols]
            hi = lw.astype(BF16)
            rem = lw - hi.astype(F32)
            mid = rem.astype(BF16)
            lo = (rem - mid.astype(F32)).astype(BF16)
            cl = (jnp.dot(tri, hi, preferred_element_type=F32)
                  + jnp.dot(tri, mid, preferred_element_type=F32)
                  + jnp.dot(tri, lo, preferred_element_type=F32))
            e_in = jnp.exp(cl)
            e_neg = jnp.exp(-cl)
            g_last = jnp.exp(jnp.sum(lw, axis=0, keepdims=True))
            at = na_ref[rows, cols] * jnp.exp(cl - lw)
            rt = r * e_in
            bt = bb_ref[rows, cols] * e_neg
            kt = k * e_neg
            ar = jnp.concatenate([at, rt], axis=0)
            bk = jnp.concatenate([bt, kt], axis=0)
            sp = s_scr[p]
            a_s = _bdot_nt(ar, sp)
            zv = jnp.concatenate([zeros_c, v], axis=0)
            u_heads = []
            gr_heads = []
            for hm in head_masks:
                g = _bdot_nt(jnp.where(hm, ar, 0.0), bk)
                ga = g[0:c, :]
                gr_heads.append(jnp.where(incl, g[c:2 * c, :], 0.0))
                low = jnp.where(strict, ga[:, 0:c], 0.0)
                t_inv = eye + low
                pw = low
                for _ in range(n_double):
                    pw = _bdot(pw, pw)
                    t_inv = t_inv + _bdot(t_inv, pw)
                rhs = a_s[0:c, :] + _bdot(jnp.where(strict_k, ga, 0.0), zv)
                u_heads.append(_bdot(t_inv, rhs))
            u = jnp.where(head_masks[0], u_heads[0], u_heads[1])
            uv = jnp.concatenate([u, v], axis=0)
            y = a_s[c:2 * c, :] + jnp.where(head_masks[0], _bdot(gr_heads[0], uv), _bdot(gr_heads[1], uv))
            upd = _bdot_tn(uv, bk * g_last)
            s_scr[p] = sp * g_last + jnp.where(same_head, upd, 0.0)

            mu = _dot_exact_rhs(y, mean_ref[...])
            d = y - mu
            var = _dot_exact_rhs(d * d, mean_ref[...])
            yn = d * lax.rsqrt(var + GN_EPS) * lnw_ref[:, cols] + lnb_ref[:, cols]
            bonus = _dot_exact_rhs(r * k * rk_ref[:, cols], ones_ref[...]) * v
            y_ref[rows, cols] = (yn + bonus) * gate_ref[rows, cols]
        return carry

    lax.fori_loop(0, n_chunks, chunk, 0)

    @pl.when(li == pl.num_programs(1) - 1)
    def _():
        sout_ref[0] = s_scr[...]


def _wkv(r, lw, k, v, na, bb, gate, s0, p, nb, seq):
    tl = _tile(seq, 8, 256)
    c = min(WKV_CHUNK, tl)
    view = lambda a: a.reshape(seq, nb * RWKV_DIM)
    blk = pl.BlockSpec((tl, RWKV_DIM), lambda b, l: (l, b))
    tri = jnp.tril(jnp.ones((c, c), F32)).astype(BF16)
    st_spec = pl.BlockSpec((1, N_PAIRS, HEAD_PAIR, HEAD_PAIR), lambda b, l: (b, 0, 0, 0))
    consts = [p["r_k"], p["ln_x_w"], p["ln_x_b"], tri, p["mean_pair"], p["ones_pair"]]
    y, s_out = pl.pallas_call(
        functools.partial(_wkv_kernel, c, tl // c),
        grid=(nb, seq // tl),
        in_specs=[blk] * 7 + [st_spec] + [_const_spec(a.shape) for a in consts],
        out_specs=[blk, st_spec],
        out_shape=[jax.ShapeDtypeStruct((seq, nb * RWKV_DIM), F32),
                   jax.ShapeDtypeStruct(s0.shape, F32)],
        scratch_shapes=[pltpu.VMEM((N_PAIRS, HEAD_PAIR, HEAD_PAIR), F32)],
        compiler_params=_params("arbitrary", "arbitrary"),
        name="wkv",
    )(*[view(a) for a in (r, lw, k, v, na, bb, gate)], s0, *consts)
    return y.reshape(seq * nb, RWKV_DIM), s_out


def _lru_kernel(reset_first, nb, tm, zx_ref, zg_ref, conv0_ref, h0_ref, cw_ref, cb_ref, wa_ref, ba_ref,
                wi_ref, bi_ref, lam_ref, y_ref, conv_out_ref, h_out_ref, xbuf, abuf, bbuf, hcar):
    i = pl.program_id(0)
    hist = (CONV_W - 1) * nb

    @pl.when(i == 0)
    def _():
        xbuf[0:hist, :] = conv0_ref[...]
        hcar[...] = h0_ref[...]

    xbuf[hist:hist + tm, :] = zx_ref[...]
    xc = cb_ref[...]
    for j in range(CONV_W):
        xc = xc + xbuf[j * nb:j * nb + tm, :] * cw_ref[j:j + 1, :]
    conv_out_ref[...] = xbuf[tm:tm + hist, :]
    xbuf[0:hist, :] = xbuf[tm:tm + hist, :]

    gate_a = _sigmoid(_bdot(xc, wa_ref[...]) + ba_ref[...])
    gate_i = _sigmoid(_bdot(xc, wi_ref[...]) + bi_ref[...])
    log_a = -LRU_C * gate_a * _softplus(-lam_ref[...])
    a_sq = jnp.exp(2.0 * log_a)
    mult = jnp.sqrt(-jnp.tanh(log_a) * (a_sq + 1.0))
    if reset_first:
        row = lax.broadcasted_iota(jnp.int32, (tm, 1), 0)
        mult = jnp.where(row < jnp.where(i == 0, nb, 0), 1.0, mult)
    abuf[...] = jnp.exp(log_a)
    bbuf[...] = mult * gate_i * xc

    def step(t, h):
        rows = pl.ds(pl.multiple_of(t * nb, nb), nb)
        h = abuf[rows, :] * h + bbuf[rows, :]
        bbuf[rows, :] = h
        return h

    h = lax.fori_loop(0, tm // nb, step, hcar[...])
    hcar[...] = h
    h_out_ref[...] = h
    y_ref[...] = bbuf[...] * _gelu(zg_ref[...])


def _lru(zrest, conv0, h0, p, reset_first, nb):
    m = zrest.shape[0]
    tm = _tile(m, max(nb, 8), 512)
    hist = (CONV_W - 1) * nb
    consts = [conv0, h0, p["conv_w"], p["conv_b"], p["lru_wa"], p["lru_ba"], p["lru_wi"], p["lru_bi"],
              p["lru_lambda"]]
    return pl.pallas_call(
        functools.partial(_lru_kernel, reset_first, nb, tm),
        grid=(m // tm,),
        in_specs=[pl.BlockSpec((tm, LRU_DIM), lambda i: (i, 0)),
                  pl.BlockSpec((tm, LRU_DIM), lambda i: (i, 1))] + [_const_spec(a.shape) for a in consts],
        out_specs=[pl.BlockSpec((tm, LRU_DIM), lambda i: (i, 0)), _const_spec((hist, LRU_DIM)),
                   _const_spec((nb, LRU_DIM))],
        out_shape=[jax.ShapeDtypeStruct((m, LRU_DIM), F32), jax.ShapeDtypeStruct((hist, LRU_DIM), F32),
                   jax.ShapeDtypeStruct((nb, LRU_DIM), F32)],
        scratch_shapes=[pltpu.VMEM((tm + hist, LRU_DIM), F32), pltpu.VMEM((tm, LRU_DIM), F32),
                        pltpu.VMEM((tm, LRU_DIM), F32), pltpu.VMEM((nb, LRU_DIM), F32)],
        compiler_params=_params("arbitrary"),
        name="rg_lru",
    )(zrest, zrest, *consts)


def _s5_kernel(nb, tm, u_ref, re0_ref, im0_ref, are_ref, aim_ref, bre_ref, bim_ref, cre_ref, cim_ref,
               d_ref, wglu_ref, bglu_ref, y_ref, re_out_ref, im_out_ref, xre, xim, car_re, car_im):
    @pl.when(pl.program_id(0) == 0)
    def _():
        car_re[...] = re0_ref[...]
        car_im[...] = im0_ref[...]

    u = u_ref[...]
    xre[...] = _bdot(u, bre_ref[...])
    xim[...] = _bdot(u, bim_ref[...])
    a_re = jnp.broadcast_to(are_ref[...], (nb, S5_WIDTH))
    a_im = jnp.broadcast_to(aim_ref[...], (nb, S5_WIDTH))

    def step(t, carry):
        s_re, s_im = carry
        rows = pl.ds(pl.multiple_of(t * nb, nb), nb)
        n_re = a_re * s_re - a_im * s_im + xre[rows, :]
        n_im = a_re * s_im + a_im * s_re + xim[rows, :]
        xre[rows, :] = n_re
        xim[rows, :] = n_im
        return n_re, n_im

    s_re, s_im = lax.fori_loop(0, tm // nb, step, (car_re[...], car_im[...]))
    car_re[...] = s_re
    car_im[...] = s_im
    re_out_ref[...] = s_re
    im_out_ref[...] = s_im

    y = _bdot(xre[...], cre_ref[...]) - _bdot(xim[...], cim_ref[...]) + d_ref[...] * u
    o = _bdot(_gelu(y), wglu_ref[...]) + bglu_ref[...]
    y_ref[...] = o[:, 0:S5_DIM] * _sigmoid(o[:, S5_DIM:2 * S5_DIM])


def _s5(zrest, re0, im0, p, nb):
    m = zrest.shape[0]
    tm = _tile(m, max(nb, 8), 512)
    consts = [re0, im0, p["s5_abar_re"], p["s5_abar_im"], p["s5_bbar_re"], p["s5_bbar_im"],
              p["s5_c_re"], p["s5_c_im"], p["s5_d"], p["s5_w_glu"], p["s5_b_glu"]]
    st = jax.ShapeDtypeStruct((nb, S5_WIDTH), F32)
    return pl.pallas_call(
        functools.partial(_s5_kernel, nb, tm),
        grid=(m // tm,),
        in_specs=[pl.BlockSpec((tm, S5_DIM), lambda i: (i, 2))] + [_const_spec(a.shape) for a in consts],
        out_specs=[pl.BlockSpec((tm, S5_DIM), lambda i: (i, 0)), _const_spec((nb, S5_WIDTH)),
                   _const_spec((nb, S5_WIDTH))],
        out_shape=[jax.ShapeDtypeStruct((m, S5_DIM), F32), st, st],
        scratch_shapes=[pltpu.VMEM((tm, S5_WIDTH), F32), pltpu.VMEM((tm, S5_WIDTH), F32),
                        pltpu.VMEM((nb, S5_WIDTH), F32), pltpu.VMEM((nb, S5_WIDTH), F32)],
        compiler_params=_params("arbitrary"),
        name="s5",
    )(zrest, *consts)


def _mix_out_kernel(x_ref, ya_ref, yb_ref, yc_ref, wa_ref, wb_ref, wc_ref, gq_ref, wq_ref, x1_ref, q_ref):
    x1 = (x_ref[...] + _bdot(ya_ref[...], wa_ref[...]) + _bdot(yb_ref[...], wb_ref[...])
          + _bdot(yc_ref[...], wc_ref[...]))
    x1_ref[...] = x1
    q_ref[...] = _bdot(_rms(x1, gq_ref[...]), wq_ref[...])


def _mix_out(x, ya, yb, yc, p):
    m = x.shape[0]
    tm = _tile(m, 8, 512)
    row = lambda w: pl.BlockSpec((tm, w), lambda i: (i, 0))
    consts = [p["w_out_a"], p["w_out_b"], p["w_out_c"], p["norm_mem_q"], p["mem_wq"]]
    out = jax.ShapeDtypeStruct((m, D_MODEL), F32)
    return pl.pallas_call(
        _mix_out_kernel,
        grid=(m // tm,),
        in_specs=[row(D_MODEL), row(RWKV_DIM), row(LRU_DIM), row(S5_DIM)] + [_const_spec(a.shape) for a in consts],
        out_specs=[row(D_MODEL), row(D_MODEL)],
        out_shape=[out, out],
        compiler_params=_params("parallel"),
        name="mix_out",
    )(x, ya, yb, yc, *consts)


def _attn_kernel(q_ref, x_ref, mk_ref, mv_ref, wo_ref, o_ref):
    q = q_ref[...]
    scale = MEM_HEAD_DIM ** -0.5
    heads = []
    for hh in range(MEM_HEADS):
        cols = slice(hh * MEM_HEAD_DIM, (hh + 1) * MEM_HEAD_DIM)
        s = _bdot_nt(q[:, cols], mk_ref[0, :, cols]) * scale
        e = jnp.exp(s - jnp.max(s, axis=-1, keepdims=True))
        prob = e / jnp.sum(e, axis=-1, keepdims=True)
        heads.append(_bdot(prob, mv_ref[0, :, cols]))
    o = jnp.concatenate(heads, axis=-1)
    o_ref[...] = x_ref[...] + _bdot(o, wo_ref[...])


def _attn(q, x1, mk, mv, kv_offset, wo, nb, seq):
    tl = _tile(seq, 8, 512)
    blk = pl.BlockSpec((tl, D_MODEL), lambda b, l: (l, b))
    kv = pl.BlockSpec((1, N_MEM, D_MODEL), lambda b, l: (kv_offset + b, 0, 0))
    out = pl.pallas_call(
        _attn_kernel,
        grid=(nb, seq // tl),
        in_specs=[blk, blk, kv, kv, _const_spec(wo.shape)],
        out_specs=blk,
        out_shape=jax.ShapeDtypeStruct((seq, nb * D_MODEL), F32),
        compiler_params=_params("parallel", "parallel"),
        name="mem_attn",
    )(q.reshape(seq, nb * D_MODEL), x1.reshape(seq, nb * D_MODEL), mk, mv, wo)
    return out.reshape(seq * nb, D_MODEL)


def _ffn_kernel(d_ff, x_ref, g_ref, wup_ref, wdown_ref, o_ref):
    x = x_ref[...]
    h = _rms(x, g_ref[...]).astype(BF16)
    gt = jnp.dot(h, wup_ref[:, 0:d_ff], preferred_element_type=F32)
    up = jnp.dot(h, wup_ref[:, d_ff:2 * d_ff], preferred_element_type=F32)
    act = gt * _sigmoid(gt) * up
    o_ref[...] = x + _bdot(act, wdown_ref[...])


def _ffn(x, p):
    m = x.shape[0]
    tm = _tile(m, 8, 256)
    d_ff = p["ffn_w_down"].shape[0]
    row = pl.BlockSpec((tm, D_MODEL), lambda i: (i, 0))
    consts = [p["norm_ffn"], p["ffn_w_up"], p["ffn_w_down"]]
    return pl.pallas_call(
        functools.partial(_ffn_kernel, d_ff),
        grid=(m // tm,),
        in_specs=[row] + [_const_spec(a.shape) for a in consts],
        out_specs=row,
        out_shape=jax.ShapeDtypeStruct((m, D_MODEL), F32),
        compiler_params=_params("parallel"),
        name="ffn",
    )(x, *consts)


def _norm_kernel(x_ref, g_ref, o_ref):
    o_ref[...] = _rms(x_ref[...], g_ref[...])


def _norm(x, g):
    m = x.shape[0]
    tm = _tile(m, 8, 512)
    row = pl.BlockSpec((tm, D_MODEL), lambda i: (i, 0))
    return pl.pallas_call(
        _norm_kernel, grid=(m // tm,), in_specs=[row, _const_spec(g.shape)], out_specs=row,
        out_shape=jax.ShapeDtypeStruct((m, D_MODEL), F32), compiler_params=_params("parallel"),
        name="final_norm",
    )(x, g)


def _mem_project_kernel(x_ref, g_ref, wk_ref, wv_ref, k_ref, v_ref):
    m = _rms(x_ref[...], g_ref[...]).astype(BF16)
    k_ref[...] = jnp.dot(m, wk_ref[...], preferred_element_type=F32)
    v_ref[...] = jnp.dot(m, wv_ref[...], preferred_element_type=F32)


def _mem_project(mem, g, wk, wv):
    m = mem.shape[0]
    tm = _tile(m, 8, 512)
    row = pl.BlockSpec((tm, D_MODEL), lambda i: (i, 0))
    out = jax.ShapeDtypeStruct((m, D_MODEL), F32)
    return pl.pallas_call(
        _mem_project_kernel, grid=(m // tm,),
        in_specs=[row, _const_spec(g.shape), _const_spec(wk.shape), _const_spec(wv.shape)],
        out_specs=[row, row], out_shape=[out, out], compiler_params=_params("parallel"),
        name="mem_project",
    )(mem, g, wk, wv)


def _block_diag(blocks):
    n, r, c = blocks.shape
    eye = jnp.eye(n, dtype=blocks.dtype)
    return (eye[:, None, :, None] * blocks[:, :, None, :]).reshape(n * r, n * c)


def _layer_params(w, l):
    row = lambda a: a.reshape(1, -1).astype(F32)
    bf = lambda a: a.astype(BF16)
    ones64 = jnp.ones((RWKV_HEADS, RWKV_HEAD, RWKV_HEAD), F32)
    p = {
        "norm_mix": row(w["norm_mix"][l]), "w_in": bf(w["w_in"][l]), "mu_rkv": row(w["mu_rkv"][l]),
        "mu_wag": w["mu_wag"][l], "w0": row(w["w0"][l]), "w1": bf(w["w1"][l]), "w2": bf(w["w2"][l]),
        "a0": row(w["a0"][l]), "a1": bf(w["a1"][l]), "a2": bf(w["a2"][l]),
        "g1": bf(w["g1"][l]), "g2": bf(w["g2"][l]), "k_k": row(w["k_k"][l]), "k_a": row(w["k_a"][l]),
        "ones_head": bf(_block_diag(ones64)),
        "ones_pair": bf(_block_diag(ones64[:2])),
        "mean_pair": bf(_block_diag(ones64[:2]) / RWKV_HEAD),
        "r_k": row(w["r_k"][l]), "ln_x_w": row(w["ln_x_w"][l]), "ln_x_b": row(w["ln_x_b"][l]),
        "conv_w": w["conv_w"][l], "conv_b": row(w["conv_b"][l]),
        "lru_wa": bf(_block_diag(w["lru_wa"][l])), "lru_ba": row(w["lru_ba"][l]),
        "lru_wi": bf(_block_diag(w["lru_wi"][l])), "lru_bi": row(w["lru_bi"][l]),
        "lru_lambda": row(w["lru_lambda"][l]),
        "s5_d": row(w["s5_d"][l]), "s5_w_glu": bf(w["s5_w_glu"][l]), "s5_b_glu": row(w["s5_b_glu"][l]),
        "w_out_a": bf(w["w_out"][l][0:RWKV_DIM]),
        "w_out_b": bf(w["w_out"][l][RWKV_DIM:RWKV_DIM + LRU_DIM]),
        "w_out_c": bf(w["w_out"][l][RWKV_DIM + LRU_DIM:]),
        "norm_mem_q": row(w["norm_mem_q"][l]), "mem_wq": bf(w["mem_wq"][l]), "mem_wo": bf(w["mem_wo"][l]),
        "norm_ffn": row(w["norm_ffn"][l]), "ffn_w_up": bf(w["ffn_w_up"][l]), "ffn_w_down": bf(w["ffn_w_down"][l]),
    }
    lam = lax.complex(w["s5_a_re"][l].astype(F32), w["s5_a_im"][l].astype(F32))
    dt = jnp.exp(w["s5_log_dt"][l].astype(F32))[:, None]
    a_bar = jnp.exp(lam * dt)
    b_bar = ((a_bar - 1.0) / lam)[..., None] * lax.complex(w["s5_b_re"][l].astype(F32), w["s5_b_im"][l].astype(F32))
    p["s5_abar_re"] = jnp.real(a_bar).reshape(1, S5_WIDTH)
    p["s5_abar_im"] = jnp.imag(a_bar).reshape(1, S5_WIDTH)
    b_gpn = jnp.swapaxes(b_bar, 1, 2)
    p["s5_bbar_re"] = bf(_block_diag(jnp.real(b_gpn)))
    p["s5_bbar_im"] = bf(_block_diag(jnp.imag(b_gpn)))
    p["s5_c_re"] = bf(_block_diag(jnp.swapaxes(w["s5_c_re"][l], 1, 2)))
    p["s5_c_im"] = bf(_block_diag(jnp.swapaxes(w["s5_c_im"][l], 1, 2)))
    vres = None
    if l > 0:
        vres = {"mu_v": row(w["mu_v"][l - 1]), "v0": row(w["v0"][l - 1]),
                "v1": bf(w["v1"][l - 1]), "v2": bf(w["v2"][l - 1])}
    return p, vres


def _pair_states(s):
    nb = s.shape[0]
    s = s.reshape(nb, N_PAIRS, 2, RWKV_HEAD, RWKV_HEAD)
    eye = jnp.eye(2, dtype=s.dtype)
    return (eye[None, None, :, None, :, None] * s[:, :, :, :, None, :]).reshape(nb, N_PAIRS, HEAD_PAIR, HEAD_PAIR)


def _unpair_states(sp):
    nb = sp.shape[0]
    s = sp.reshape(nb, N_PAIRS, 2, RWKV_HEAD, 2, RWKV_HEAD)
    return jnp.stack([s[:, :, 0, :, 0, :], s[:, :, 1, :, 1, :]], axis=2).reshape(nb, RWKV_HEADS, RWKV_HEAD, RWKV_HEAD)


def _time_major(a):
    return jnp.swapaxes(a, 0, 1).reshape((a.shape[0] * a.shape[1],) + a.shape[2:])


def _run_group(x, mem_k, mem_v, kv_stride, shift0, wkv0, conv0, lru0, re0, im0, reset_first, layers, norm_final):
    nb, seq, _ = x.shape
    xt = _time_major(x)
    outs = {k: [] for k in ("shift", "wkv", "conv", "lru", "re", "im")}
    v_first = None
    for l, (p, vres) in enumerate(layers):
        zrest, r, lw, k, v, na, bb, gate, h_last = _mix_in(xt, shift0[l], p, vres, v_first, nb)
        if l == 0:
            v_first = v
        ya, s_out = _wkv(r, lw, k, v, na, bb, gate, _pair_states(wkv0[l]), p, nb, seq)
        yb, conv_new, lru_new = _lru(zrest, _time_major(conv0[l]), lru0[l], p, reset_first, nb)
        yc, re_new, im_new = _s5(zrest, re0[l].reshape(nb, S5_WIDTH), im0[l].reshape(nb, S5_WIDTH), p, nb)
        x1, q = _mix_out(xt, ya, yb, yc, p)
        x2 = _attn(q, x1, mem_k, mem_v, l * kv_stride, p["mem_wo"], nb, seq)
        xt = _ffn(x2, p)
        outs["shift"].append(h_last)
        outs["wkv"].append(_unpair_states(s_out))
        outs["conv"].append(jnp.swapaxes(conv_new.reshape(CONV_W - 1, nb, LRU_DIM), 0, 1))
        outs["lru"].append(lru_new)
        outs["re"].append(re_new.reshape(nb, S5_GROUPS, S5_STATE))
        outs["im"].append(im_new.reshape(nb, S5_GROUPS, S5_STATE))
    y = _norm(xt, norm_final)
    y = jnp.swapaxes(y.reshape(seq, nb, D_MODEL), 0, 1)
    return (y,) + tuple(jnp.stack(outs[k]) for k in ("shift", "wkv", "conv", "lru", "re", "im"))


def kernel(x_prompt, x_sample, mem_prompt, state_shift, state_wkv, state_conv, state_lru, state_s5_re, state_s5_im, cache_mem_k, cache_mem_v, norm_mix, w_in, w_out, mu_rkv, mu_wag, mu_v, w0, w1, w2, a0, a1, a2, v0, v1, v2, g1, g2, k_k, k_a, r_k, ln_x_w, ln_x_b, conv_w, conv_b, lru_wa, lru_ba, lru_wi, lru_bi, lru_lambda, s5_a_re, s5_a_im, s5_log_dt, s5_b_re, s5_b_im, s5_c_re, s5_c_im, s5_d, s5_w_glu, s5_b_glu, norm_mem_q, norm_mem_kv, mem_wq, mem_wk, mem_wv, mem_wo, norm_ffn, ffn_w_up, ffn_w_down, norm_final):
    w = dict(norm_mix=norm_mix, w_in=w_in, w_out=w_out, mu_rkv=mu_rkv, mu_wag=mu_wag, mu_v=mu_v, w0=w0, w1=w1,
             w2=w2, a0=a0, a1=a1, a2=a2, v0=v0, v1=v1, v2=v2, g1=g1, g2=g2, k_k=k_k, k_a=k_a, r_k=r_k,
             ln_x_w=ln_x_w, ln_x_b=ln_x_b, conv_w=conv_w, conv_b=conv_b, lru_wa=lru_wa, lru_ba=lru_ba,
             lru_wi=lru_wi, lru_bi=lru_bi, lru_lambda=lru_lambda, s5_a_re=s5_a_re, s5_a_im=s5_a_im,
             s5_log_dt=s5_log_dt, s5_b_re=s5_b_re, s5_b_im=s5_b_im, s5_c_re=s5_c_re, s5_c_im=s5_c_im,
             s5_d=s5_d, s5_w_glu=s5_w_glu, s5_b_glu=s5_b_glu, norm_mem_q=norm_mem_q, mem_wq=mem_wq,
             mem_wo=mem_wo, norm_ffn=norm_ffn, ffn_w_up=ffn_w_up, ffn_w_down=ffn_w_down)
    depth = norm_mix.shape[0]
    layers = [_layer_params(w, l) for l in range(depth)]
    g_final = norm_final.reshape(1, D_MODEL)

    bp, n_mem, _ = mem_prompt.shape
    mem_flat = mem_prompt.reshape(bp * n_mem, D_MODEL)
    mem_kv = [_mem_project(mem_flat, norm_mem_kv[l].reshape(1, D_MODEL), mem_wk[l].astype(BF16),
                           mem_wv[l].astype(BF16)) for l in range(depth)]
    mem_k_p = jnp.stack([kv[0] for kv in mem_kv]).reshape(depth, bp, n_mem, MEM_HEADS, MEM_HEAD_DIM)
    mem_v_p = jnp.stack([kv[1] for kv in mem_kv]).reshape(depth, bp, n_mem, MEM_HEADS, MEM_HEAD_DIM)

    zeros = lambda *s: jnp.zeros((depth, bp) + s, F32)
    prompt = _run_group(
        x_prompt, mem_k_p.reshape(depth * bp, n_mem, D_MODEL), mem_v_p.reshape(depth * bp, n_mem, D_MODEL), bp,
        zeros(D_MODEL), zeros(RWKV_HEADS, RWKV_HEAD, RWKV_HEAD), zeros(CONV_W - 1, LRU_DIM), zeros(LRU_DIM),
        zeros(S5_GROUPS, S5_STATE), zeros(S5_GROUPS, S5_STATE), True, layers, g_final)
    bs = x_sample.shape[0]
    sample = _run_group(
        x_sample, cache_mem_k.reshape(depth * bs, n_mem, D_MODEL), cache_mem_v.reshape(depth * bs, n_mem, D_MODEL),
        bs, state_shift, state_wkv, state_conv, state_lru, state_s5_re, state_s5_im, False, layers, g_final)
    (y_p, sh_p, wkv_p, conv_p, lru_p, re_p, im_p) = prompt
    (y_s, sh_s, wkv_s, conv_s, lru_s, re_s, im_s) = sample
    return (y_p, y_s, sh_p, sh_s, wkv_p, wkv_s, conv_p, conv_s, lru_p, lru_s, re_p, re_s, im_p, im_s,
            mem_k_p, mem_v_p)
```

```python
import functools
import math

import jax
import jax.numpy as jnp
from jax import lax
from jax.experimental import pallas as pl
from jax.experimental.pallas import tpu as pltpu

F32 = jnp.float32
BF16 = jnp.bfloat16

D_MODEL = 1024
RWKV_DIM = 512
RWKV_HEAD = 64
RWKV_HEADS = 8
HEAD_PAIR = 2 * RWKV_HEAD
N_PAIRS = RWKV_DIM // HEAD_PAIR
LRU_DIM = 256
LRU_BLOCKS = 4
CONV_W = 4
LRU_C = 8.0
S5_DIM = 256
S5_GROUP = 16
S5_GROUPS = 16
S5_STATE = 64
S5_WIDTH = S5_GROUPS * S5_STATE
R3 = 3 * RWKV_DIM
D_IN = R3 + 2 * LRU_DIM + S5_DIM
N_MEM = 256
MEM_HEADS = 4
MEM_HEAD_DIM = 256
RMS_EPS = 1e-6
GN_EPS = 64e-5
WKV_CHUNK = 64

V7X_VMEM_LIMIT = 56 * 1024 * 1024


def _params(*sem):
    return pltpu.CompilerParams(dimension_semantics=sem, vmem_limit_bytes=V7X_VMEM_LIMIT)


def _tile(m, mult, target):
    best = mult
    t = mult
    while t <= min(m, target):
        if m % t == 0:
            best = t
        t += mult
    assert m % best == 0, (m, mult, target)
    return best


def _const_spec(shape):
    nd = len(shape)
    return pl.BlockSpec(shape, lambda *_: (0,) * nd)


def _bdot(a, b):
    return jnp.dot(a.astype(BF16), b.astype(BF16), preferred_element_type=F32)


def _bdot_nt(a, b):
    return lax.dot_general(a.astype(BF16), b.astype(BF16), (((1,), (1,)), ((), ())),
                           preferred_element_type=F32)


def _bdot_tn(a, b):
    return lax.dot_general(a.astype(BF16), b.astype(BF16), (((0,), (0,)), ((), ())),
                           preferred_element_type=F32)


def _split2(x):
    hi = x.astype(BF16)
    lo = (x - hi.astype(F32)).astype(BF16)
    return hi, lo


def _dot_exact_rhs(x, w):
    hi, lo = _split2(x)
    return (jnp.dot(hi, w, preferred_element_type=F32)
            + jnp.dot(lo, w, preferred_element_type=F32))


def _rms(x, g):
    return x * lax.rsqrt(jnp.mean(x * x, axis=-1, keepdims=True) + RMS_EPS) * g


def _softplus(x):
    return jnp.maximum(x, 0.0) + jnp.log1p(jnp.exp(-jnp.abs(x)))


def _sigmoid(x):
    return 1.0 / (1.0 + jnp.exp(-x))


def _gelu(x):
    c = math.sqrt(2.0 / math.pi)
    return x * (0.5 * (1.0 + jnp.tanh(c * (x + 0.044715 * (x * x * x)))))


def _mix_in_kernel(has_vres, nb, tm, *refs):
    if has_vres:
        (x_ref, first_ref, g_ref, win_ref, murkv_ref, muwag_ref, w0_ref, w1_ref, w2_ref,
         a0_ref, a1_ref, a2_ref, g1_ref, g2_ref, kk_ref, ka_ref, ones_ref,
         muv_ref, v0_ref, v1_ref, v2_ref, vfirst_ref,
         zrest_ref, r_ref, lw_ref, k_ref, v_ref, na_ref, bb_ref, gate_ref, hlast_ref,
         hbuf, zbuf) = refs
    else:
        (x_ref, first_ref, g_ref, win_ref, murkv_ref, muwag_ref, w0_ref, w1_ref, w2_ref,
         a0_ref, a1_ref, a2_ref, g1_ref, g2_ref, kk_ref, ka_ref, ones_ref,
         zrest_ref, r_ref, lw_ref, k_ref, v_ref, na_ref, bb_ref, gate_ref, hlast_ref,
         hbuf, zbuf) = refs

    @pl.when(pl.program_id(0) == 0)
    def _():
        first = first_ref[...]
        hbuf[0:nb, :] = first
        zbuf[0:nb, :] = _bdot(first, win_ref[:, 0:R3])

    h = _rms(x_ref[...], g_ref[...])
    hbuf[nb:nb + tm, :] = h
    hp = hbuf[0:tm, :]
    hbuf[0:nb, :] = hbuf[tm:tm + nb, :]
    hlast_ref[...] = h[tm - nb:tm, :]

    z = _bdot(h, win_ref[...])
    zrest_ref[...] = z[:, R3:D_IN]
    z_rkv = z[:, 0:R3]
    zbuf[nb:nb + tm, :] = z_rkv
    z_sh = zbuf[0:tm, :]
    zbuf[0:nb, :] = zbuf[tm:tm + nb, :]
    zmix = z_rkv + (z_sh - z_rkv) * murkv_ref[...]
    r = zmix[:, 0:RWKV_DIM]
    k = zmix[:, RWKV_DIM:2 * RWKV_DIM]
    v = zmix[:, 2 * RWKV_DIM:R3]

    dh = hp - h
    xw = h + dh * muwag_ref[0:1, :]
    xa = h + dh * muwag_ref[1:2, :]
    xg = h + dh * muwag_ref[2:3, :]
    w_pre = w0_ref[...] + _bdot(jnp.tanh(_bdot(xw, w1_ref[...])), w2_ref[...])
    w_log = -_softplus(-w_pre) - 0.5
    a = _sigmoid(a0_ref[...] + _bdot(_bdot(xa, a1_ref[...]), a2_ref[...]))
    gate = _bdot(_sigmoid(_bdot(xg, g1_ref[...])), g2_ref[...])
    if has_vres:
        xv = h + dh * muv_ref[...]
        mix = _sigmoid(v0_ref[...] + _bdot(_bdot(xv, v1_ref[...]), v2_ref[...]))
        v_first = jnp.concatenate([vfirst_ref[p] for p in range(N_PAIRS)], axis=-1)
        v = v + (v_first - v) * mix
    kk = k * kk_ref[...]
    ss = _dot_exact_rhs(kk * kk, ones_ref[...])
    kk = kk / jnp.maximum(jnp.sqrt(ss), 1e-12)
    outs = ((r_ref, r), (lw_ref, -jnp.exp(w_log)), (k_ref, k * (1.0 + (a - 1.0) * ka_ref[...])), (v_ref, v),
            (na_ref, -kk), (bb_ref, kk * a), (gate_ref, gate))
    for ref, val in outs:
        for p in range(N_PAIRS):
            ref[p] = val[:, p * HEAD_PAIR:(p + 1) * HEAD_PAIR]


def _mix_in(x, first, p, vres, v_first, nb):
    m = x.shape[0]
    tm = _tile(m, max(nb, 8), 256)
    has_vres = vres is not None
    row = lambda w: pl.BlockSpec((tm, w), lambda i: (i, 0))
    pair_row = pl.BlockSpec((N_PAIRS, tm, HEAD_PAIR), lambda i: (0, i, 0))
    ins = [x, first, p["norm_mix"], p["w_in"], p["mu_rkv"], p["mu_wag"], p["w0"], p["w1"], p["w2"],
           p["a0"], p["a1"], p["a2"], p["g1"], p["g2"], p["k_k"], p["k_a"], p["ones_head"]]
    specs = [row(D_MODEL)] + [_const_spec(a.shape) for a in ins[1:]]
    if has_vres:
        extra = [vres["mu_v"], vres["v0"], vres["v1"], vres["v2"]]
        ins += extra + [v_first]
        specs += [_const_spec(a.shape) for a in extra] + [pair_row]
    wide = jax.ShapeDtypeStruct((N_PAIRS, m, HEAD_PAIR), F32)
    outs = [jax.ShapeDtypeStruct((m, D_IN - R3), F32)] + [wide] * 7 + [jax.ShapeDtypeStruct((nb, D_MODEL), F32)]
    out_specs = [row(D_IN - R3)] + [pair_row] * 7 + [_const_spec((nb, D_MODEL))]
    return pl.pallas_call(
        functools.partial(_mix_in_kernel, has_vres, nb, tm),
        grid=(m // tm,),
        in_specs=specs,
        out_specs=out_specs,
        out_shape=outs,
        scratch_shapes=[pltpu.VMEM((tm + nb, D_MODEL), F32), pltpu.VMEM((tm + nb, R3), F32)],
        compiler_params=_params("arbitrary"),
        name="mix_in",
    )(*ins)


def _wkv_kernel(c, n_seq, group, stride, r_ref, lw_ref, k_ref, v_ref, na_ref, bb_ref, gate_ref, s0_ref,
                rk_ref, lnw_ref, lnb_ref, tri_ref, mean_ref, ones_ref,
                y_ref, sout_ref, s_scr):
    li = pl.program_id(1)

    @pl.when(li == 0)
    def _():
        s_scr[...] = s0_ref[...]

    lane = lax.broadcasted_iota(jnp.int32, (1, HEAD_PAIR), 1)
    head_masks = (lane < RWKV_HEAD, lane >= RWKV_HEAD)
    trow = lax.broadcasted_iota(jnp.int32, (c, 2 * c), 0)
    tcol = lax.broadcasted_iota(jnp.int32, (c, 2 * c), 1)
    scol = jnp.where(tcol >= c, tcol - c, tcol)
    strict_k = (scol < trow) & (tcol >= c)
    incl = scol <= trow
    lrow = lax.broadcasted_iota(jnp.int32, (c, c), 0)
    lcol = lax.broadcasted_iota(jnp.int32, (c, c), 1)
    strict = lcol < lrow
    eye = (lrow == lcol).astype(F32)
    brow = lax.broadcasted_iota(jnp.int32, (HEAD_PAIR, HEAD_PAIR), 0)
    bcol = lax.broadcasted_iota(jnp.int32, (HEAD_PAIR, HEAD_PAIR), 1)
    same_head = jnp.bitwise_xor(brow - RWKV_HEAD, bcol - RWKV_HEAD) >= 0
    n_double = max(int(math.log2(c)) - 1, 0)
    tri = tri_ref[...]
    zeros_c = jnp.zeros((c, HEAD_PAIR), F32)

    def pair_chunk(lw, r, k, v, na, bb, gate, sp, cols):
        hi = lw.astype(BF16)
        rem = lw - hi.astype(F32)
        mid = rem.astype(BF16)
        lo = (rem - mid.astype(F32)).astype(BF16)
        cl = (jnp.dot(tri, hi, preferred_element_type=F32)
              + jnp.dot(tri, mid, preferred_element_type=F32)
              + jnp.dot(tri, lo, preferred_element_type=F32))
        yield
        e_in = jnp.exp(cl)
        e_neg = jnp.exp(-cl)
        g_last = jnp.exp(jnp.sum(lw, axis=0, keepdims=True))
        at = na * jnp.exp(cl - lw)
        rt = r * e_in
        bt = bb * e_neg
        kt = k * e_neg
        ar = jnp.concatenate([at, rt], axis=0)
        bk = jnp.concatenate([bt, kt], axis=0)
        a_s = _bdot_nt(ar, sp)
        zv = jnp.concatenate([zeros_c, v], axis=0)
        gs = [_bdot_nt(jnp.where(hm, ar, 0.0), bk) for hm in head_masks]
        yield
        gas = [g[0:c, :] for g in gs]
        grs = [jnp.where(incl, g[c:2 * c, :], 0.0) for g in gs]
        pws = [jnp.where(strict, ga[:, 0:c], 0.0) for ga in gas]
        t_invs = [eye + low for low in pws]
        rhs = [a_s[0:c, :] + _bdot(jnp.where(strict_k, ga, 0.0), zv) for ga in gas]
        for _ in range(n_double):
            pws = [_bdot(pw, pw) for pw in pws]
            yield
            t_invs = [t_inv + _bdot(t_inv, pw) for t_inv, pw in zip(t_invs, pws)]
        yield
        us = [_bdot(t_inv, x) for t_inv, x in zip(t_invs, rhs)]
        yield
        u = jnp.where(head_masks[0], us[0], us[1])
        uv = jnp.concatenate([u, v], axis=0)
        y = a_s[c:2 * c, :] + jnp.where(head_masks[0], _bdot(grs[0], uv), _bdot(grs[1], uv))
        upd = _bdot_tn(uv, bk * g_last)
        s_new = sp * g_last + jnp.where(same_head, upd, 0.0)
        yield
        mu = _dot_exact_rhs(y, mean_ref[...])
        bonus = _dot_exact_rhs(r * k * rk_ref[:, cols], ones_ref[...]) * v
        yield
        d = y - mu
        var = _dot_exact_rhs(d * d, mean_ref[...])
        yield
        yn = d * lax.rsqrt(var + GN_EPS) * lnw_ref[:, cols] + lnb_ref[:, cols]
        return (yn + bonus) * gate, s_new

    def seq_rows(j):
        if stride == 1:
            return pl.ds(pl.multiple_of(j * c, c), c)
        return pl.ds(j, c, stride=stride)

    def seq_group(gi, carry):
        units = []
        for jj in range(group):
            j = gi * group + jj
            rows = seq_rows(j)
            for p in range(N_PAIRS):
                cols = slice(p * HEAD_PAIR, (p + 1) * HEAD_PAIR)
                vals = [ref[p, rows, :] for ref in (lw_ref, r_ref, k_ref, v_ref, na_ref, bb_ref, gate_ref)]
                units.append((j, p, rows, pair_chunk(*vals, s_scr[j, p], cols)))
        results = {}
        while len(results) < len(units):
            for i, (_, _, _, gen) in enumerate(units):
                if i not in results:
                    try:
                        next(gen)
                    except StopIteration as stop:
                        results[i] = stop.value
        for i, (j, p, rows, _) in enumerate(units):
            y, s_new = results[i]
            y_ref[p, rows, :] = y
            s_scr[j, p] = s_new
        return carry

    lax.fori_loop(0, n_seq // group, seq_group, 0)

    @pl.when(li == pl.num_programs(1) - 1)
    def _():
        sout_ref[...] = s_scr[...]


WKV_SEQ_PER_STEP = 8
WKV_SEQ_GROUP = 2


def _wkv(arrs, s0, p, nb, seq, time_major):
    if time_major:
        c = min(WKV_CHUNK, seq)
        n_seq, stride, grid = nb, nb, (1, seq // c)
        blk = pl.BlockSpec((N_PAIRS, c * nb, HEAD_PAIR), lambda g, l: (0, l, 0))
    else:
        c = seq
        n_seq, stride, grid = WKV_SEQ_PER_STEP, 1, (nb // WKV_SEQ_PER_STEP, 1)
        blk = pl.BlockSpec((N_PAIRS, n_seq * c, HEAD_PAIR), lambda g, l: (0, g, 0))
    assert seq % c == 0 and c % 8 == 0 and c <= WKV_CHUNK and nb % n_seq == 0 and n_seq % WKV_SEQ_GROUP == 0
    tri = jnp.tril(jnp.ones((c, c), F32)).astype(BF16)
    st_spec = pl.BlockSpec((n_seq, N_PAIRS, HEAD_PAIR, HEAD_PAIR), lambda g, l: (g, 0, 0, 0))
    consts = [p["r_k"], p["ln_x_w"], p["ln_x_b"], tri, p["mean_pair"], p["ones_pair"]]
    return pl.pallas_call(
        functools.partial(_wkv_kernel, c, n_seq, WKV_SEQ_GROUP, stride),
        grid=grid,
        in_specs=[blk] * 7 + [st_spec] + [_const_spec(a.shape) for a in consts],
        out_specs=[blk, st_spec],
        out_shape=[jax.ShapeDtypeStruct((N_PAIRS, seq * nb, HEAD_PAIR), F32), jax.ShapeDtypeStruct(s0.shape, F32)],
        scratch_shapes=[pltpu.VMEM((n_seq, N_PAIRS, HEAD_PAIR, HEAD_PAIR), F32)],
        compiler_params=_params("arbitrary", "arbitrary"),
        name="wkv",
    )(*arrs, s0, *consts)


def _lru_kernel(reset_first, nb, tm, zx_ref, zg_ref, conv0_ref, h0_ref, cw_ref, cb_ref, wa_ref, ba_ref,
                wi_ref, bi_ref, lam_ref, y_ref, conv_out_ref, h_out_ref, xbuf, abuf, bbuf, hcar):
    i = pl.program_id(0)
    hist = (CONV_W - 1) * nb

    @pl.when(i == 0)
    def _():
        xbuf[0:hist, :] = conv0_ref[...]
        hcar[...] = h0_ref[...]

    xbuf[hist:hist + tm, :] = zx_ref[...]
    xc = cb_ref[...]
    for j in range(CONV_W):
        xc = xc + xbuf[j * nb:j * nb + tm, :] * cw_ref[j:j + 1, :]
    conv_out_ref[...] = xbuf[tm:tm + hist, :]
    xbuf[0:hist, :] = xbuf[tm:tm + hist, :]

    gate_a = _sigmoid(_bdot(xc, wa_ref[...]) + ba_ref[...])
    gate_i = _sigmoid(_bdot(xc, wi_ref[...]) + bi_ref[...])
    log_a = -LRU_C * gate_a * _softplus(-lam_ref[...])
    a_sq = jnp.exp(2.0 * log_a)
    mult = jnp.sqrt(-jnp.tanh(log_a) * (a_sq + 1.0))
    if reset_first:
        row = lax.broadcasted_iota(jnp.int32, (tm, 1), 0)
        mult = jnp.where(row < jnp.where(i == 0, nb, 0), 1.0, mult)
    abuf[...] = jnp.exp(log_a)
    bbuf[...] = mult * gate_i * xc

    def step(t, h):
        rows = pl.ds(pl.multiple_of(t * nb, nb), nb)
        h = abuf[rows, :] * h + bbuf[rows, :]
        bbuf[rows, :] = h
        return h

    h = lax.fori_loop(0, tm // nb, step, hcar[...])
    hcar[...] = h
    h_out_ref[...] = h
    y_ref[...] = bbuf[...] * _gelu(zg_ref[...])


def _lru(zrest, conv0, h0, p, reset_first, nb):
    m = zrest.shape[0]
    tm = _tile(m, max(nb, 8), 512)
    hist = (CONV_W - 1) * nb
    consts = [conv0, h0, p["conv_w"], p["conv_b"], p["lru_wa"], p["lru_ba"], p["lru_wi"], p["lru_bi"],
              p["lru_lambda"]]
    return pl.pallas_call(
        functools.partial(_lru_kernel, reset_first, nb, tm),
        grid=(m // tm,),
        in_specs=[pl.BlockSpec((tm, LRU_DIM), lambda i: (i, 0)),
                  pl.BlockSpec((tm, LRU_DIM), lambda i: (i, 1))] + [_const_spec(a.shape) for a in consts],
        out_specs=[pl.BlockSpec((tm, LRU_DIM), lambda i: (i, 0)), _const_spec((hist, LRU_DIM)),
                   _const_spec((nb, LRU_DIM))],
        out_shape=[jax.ShapeDtypeStruct((m, LRU_DIM), F32), jax.ShapeDtypeStruct((hist, LRU_DIM), F32),
                   jax.ShapeDtypeStruct((nb, LRU_DIM), F32)],
        scratch_shapes=[pltpu.VMEM((tm + hist, LRU_DIM), F32), pltpu.VMEM((tm, LRU_DIM), F32),
                        pltpu.VMEM((tm, LRU_DIM), F32), pltpu.VMEM((nb, LRU_DIM), F32)],
        compiler_params=_params("arbitrary"),
        name="rg_lru",
    )(zrest, zrest, *consts)


def _s5_kernel(nb, tm, u_ref, re0_ref, im0_ref, are_ref, aim_ref, bre_ref, bim_ref, cre_ref, cim_ref,
               d_ref, wglu_ref, bglu_ref, y_ref, re_out_ref, im_out_ref, xre, xim, car_re, car_im):
    @pl.when(pl.program_id(0) == 0)
    def _():
        car_re[...] = re0_ref[...]
        car_im[...] = im0_ref[...]

    u = u_ref[...]
    xre[...] = _bdot(u, bre_ref[...])
    xim[...] = _bdot(u, bim_ref[...])
    a_re = jnp.broadcast_to(are_ref[...], (nb, S5_WIDTH))
    a_im = jnp.broadcast_to(aim_ref[...], (nb, S5_WIDTH))

    def step(t, carry):
        s_re, s_im = carry
        rows = pl.ds(pl.multiple_of(t * nb, nb), nb)
        n_re = a_re * s_re - a_im * s_im + xre[rows, :]
        n_im = a_re * s_im + a_im * s_re + xim[rows, :]
        xre[rows, :] = n_re
        xim[rows, :] = n_im
        return n_re, n_im

    s_re, s_im = lax.fori_loop(0, tm // nb, step, (car_re[...], car_im[...]))
    car_re[...] = s_re
    car_im[...] = s_im
    re_out_ref[...] = s_re
    im_out_ref[...] = s_im

    y = _bdot(xre[...], cre_ref[...]) - _bdot(xim[...], cim_ref[...]) + d_ref[...] * u
    o = _bdot(_gelu(y), wglu_ref[...]) + bglu_ref[...]
    y_ref[...] = o[:, 0:S5_DIM] * _sigmoid(o[:, S5_DIM:2 * S5_DIM])


def _s5(zrest, re0, im0, p, nb):
    m = zrest.shape[0]
    tm = _tile(m, max(nb, 8), 512)
    consts = [re0, im0, p["s5_abar_re"], p["s5_abar_im"], p["s5_bbar_re"], p["s5_bbar_im"],
              p["s5_c_re"], p["s5_c_im"], p["s5_d"], p["s5_w_glu"], p["s5_b_glu"]]
    st = jax.ShapeDtypeStruct((nb, S5_WIDTH), F32)
    return pl.pallas_call(
        functools.partial(_s5_kernel, nb, tm),
        grid=(m // tm,),
        in_specs=[pl.BlockSpec((tm, S5_DIM), lambda i: (i, 2))] + [_const_spec(a.shape) for a in consts],
        out_specs=[pl.BlockSpec((tm, S5_DIM), lambda i: (i, 0)), _const_spec((nb, S5_WIDTH)),
                   _const_spec((nb, S5_WIDTH))],
        out_shape=[jax.ShapeDtypeStruct((m, S5_DIM), F32), st, st],
        scratch_shapes=[pltpu.VMEM((tm, S5_WIDTH), F32), pltpu.VMEM((tm, S5_WIDTH), F32),
                        pltpu.VMEM((nb, S5_WIDTH), F32), pltpu.VMEM((nb, S5_WIDTH), F32)],
        compiler_params=_params("arbitrary"),
        name="s5",
    )(zrest, *consts)


def _mix_out_kernel(x_ref, ya_ref, yb_ref, yc_ref, wa_ref, wb_ref, wc_ref, gq_ref, wq_ref, x1_ref, q_ref):
    ya = jnp.concatenate([ya_ref[p] for p in range(N_PAIRS)], axis=-1)
    x1 = (x_ref[...] + _bdot(ya, wa_ref[...]) + _bdot(yb_ref[...], wb_ref[...])
          + _bdot(yc_ref[...], wc_ref[...]))
    x1_ref[...] = x1
    q_ref[...] = _bdot(_rms(x1, gq_ref[...]), wq_ref[...])


def _mix_out(x, ya, yb, yc, p):
    m = x.shape[0]
    tm = _tile(m, 8, 512)
    row = lambda w: pl.BlockSpec((tm, w), lambda i: (i, 0))
    consts = [p["w_out_a"], p["w_out_b"], p["w_out_c"], p["norm_mem_q"], p["mem_wq"]]
    out = jax.ShapeDtypeStruct((m, D_MODEL), F32)
    return pl.pallas_call(
        _mix_out_kernel,
        grid=(m // tm,),
        in_specs=[row(D_MODEL), pl.BlockSpec((N_PAIRS, tm, HEAD_PAIR), lambda i: (0, i, 0)), row(LRU_DIM),
                  row(S5_DIM)] + [_const_spec(a.shape) for a in consts],
        out_specs=[row(D_MODEL), row(D_MODEL)],
        out_shape=[out, out],
        compiler_params=_params("parallel"),
        name="mix_out",
    )(x, ya, yb, yc, *consts)


def _attn_kernel(tl, n_seq, q_ref, x_ref, mk_ref, mv_ref, wo_ref, o_ref):
    scale = MEM_HEAD_DIM ** -0.5
    for j in range(n_seq):
        rows = pl.ds(j * tl, tl)
        q = q_ref[rows, :]
        heads = []
        for hh in range(MEM_HEADS):
            cols = slice(hh * MEM_HEAD_DIM, (hh + 1) * MEM_HEAD_DIM)
            s = _bdot_nt(q[:, cols], mk_ref[j, :, cols]) * scale
            e = jnp.exp(s - jnp.max(s, axis=-1, keepdims=True))
            prob = e / jnp.sum(e, axis=-1, keepdims=True)
            heads.append(_bdot(prob, mv_ref[j, :, cols]))
        o = jnp.concatenate(heads, axis=-1)
        o_ref[rows, :] = x_ref[rows, :] + _bdot(o, wo_ref[...])


ATTN_SEQ_PER_STEP = 4


def _attn(q, x1, mk, mv, kv_offset, wo, nb, seq, time_major):
    if time_major:
        tl = _tile(seq, 8, 512)
        n_seq, grid, shape = 1, (nb, seq // tl), (seq, nb * D_MODEL)
        blk = pl.BlockSpec((tl, D_MODEL), lambda g, l: (l, g))
    else:
        tl = seq
        n_seq, grid, shape = ATTN_SEQ_PER_STEP, (nb // ATTN_SEQ_PER_STEP, 1), (seq * nb, D_MODEL)
        blk = pl.BlockSpec((n_seq * tl, D_MODEL), lambda g, l: (g, 0))
    assert nb % n_seq == 0 and kv_offset % n_seq == 0
    kv = pl.BlockSpec((n_seq, N_MEM, D_MODEL), lambda g, l: (kv_offset // n_seq + g, 0, 0))
    out = pl.pallas_call(
        functools.partial(_attn_kernel, tl, n_seq),
        grid=grid,
        in_specs=[blk, blk, kv, kv, _const_spec(wo.shape)],
        out_specs=blk,
        out_shape=jax.ShapeDtypeStruct(shape, F32),
        compiler_params=_params("parallel", "parallel"),
        name="mem_attn",
    )(q.reshape(shape), x1.reshape(shape), mk, mv, wo)
    return out.reshape(seq * nb, D_MODEL)


def _ffn_kernel(d_ff, x_ref, g_ref, wup_ref, wdown_ref, o_ref):
    x = x_ref[...]
    h = _rms(x, g_ref[...]).astype(BF16)
    gt = jnp.dot(h, wup_ref[:, 0:d_ff], preferred_element_type=F32)
    up = jnp.dot(h, wup_ref[:, d_ff:2 * d_ff], preferred_element_type=F32)
    act = gt * _sigmoid(gt) * up
    o_ref[...] = x + _bdot(act, wdown_ref[...])


def _ffn(x, p):
    m = x.shape[0]
    tm = _tile(m, 8, 256)
    d_ff = p["ffn_w_down"].shape[0]
    row = pl.BlockSpec((tm, D_MODEL), lambda i: (i, 0))
    consts = [p["norm_ffn"], p["ffn_w_up"], p["ffn_w_down"]]
    return pl.pallas_call(
        functools.partial(_ffn_kernel, d_ff),
        grid=(m // tm,),
        in_specs=[row] + [_const_spec(a.shape) for a in consts],
        out_specs=row,
        out_shape=jax.ShapeDtypeStruct((m, D_MODEL), F32),
        compiler_params=_params("parallel"),
        name="ffn",
    )(x, *consts)


def _norm_kernel(x_ref, g_ref, o_ref):
    o_ref[...] = _rms(x_ref[...], g_ref[...])


def _norm(x, g):
    m = x.shape[0]
    tm = _tile(m, 8, 512)
    row = pl.BlockSpec((tm, D_MODEL), lambda i: (i, 0))
    return pl.pallas_call(
        _norm_kernel, grid=(m // tm,), in_specs=[row, _const_spec(g.shape)], out_specs=row,
        out_shape=jax.ShapeDtypeStruct((m, D_MODEL), F32), compiler_params=_params("parallel"),
        name="final_norm",
    )(x, g)


def _mem_project_kernel(x_ref, g_ref, wk_ref, wv_ref, k_ref, v_ref, kb_ref, vb_ref):
    m = _rms(x_ref[...], g_ref[...]).astype(BF16)
    k = jnp.dot(m, wk_ref[...], preferred_element_type=F32)
    v = jnp.dot(m, wv_ref[...], preferred_element_type=F32)
    k_ref[...] = k
    v_ref[...] = v
    kb_ref[...] = k.astype(BF16)
    vb_ref[...] = v.astype(BF16)


def _mem_project(mem, g, wk, wv):
    m = mem.shape[0]
    tm = _tile(m, 16, 512)
    row = pl.BlockSpec((tm, D_MODEL), lambda i: (i, 0))
    out = jax.ShapeDtypeStruct((m, D_MODEL), F32)
    outb = jax.ShapeDtypeStruct((m, D_MODEL), BF16)
    return pl.pallas_call(
        _mem_project_kernel, grid=(m // tm,),
        in_specs=[row, _const_spec(g.shape), _const_spec(wk.shape), _const_spec(wv.shape)],
        out_specs=[row] * 4, out_shape=[out, out, outb, outb], compiler_params=_params("parallel"),
        name="mem_project",
    )(mem, g, wk, wv)


def _block_diag(blocks):
    n, r, c = blocks.shape
    eye = jnp.eye(n, dtype=blocks.dtype)
    return (eye[:, None, :, None] * blocks[:, :, None, :]).reshape(n * r, n * c)


def _layer_params(w, l):
    row = lambda a: a.reshape(1, -1).astype(F32)
    bf = lambda a: a.astype(BF16)
    ones64 = jnp.ones((RWKV_HEADS, RWKV_HEAD, RWKV_HEAD), F32)
    p = {
        "norm_mix": row(w["norm_mix"][l]), "w_in": bf(w["w_in"][l]), "mu_rkv": row(w["mu_rkv"][l]),
        "mu_wag": w["mu_wag"][l], "w0": row(w["w0"][l]), "w1": bf(w["w1"][l]), "w2": bf(w["w2"][l]),
        "a0": row(w["a0"][l]), "a1": bf(w["a1"][l]), "a2": bf(w["a2"][l]),
        "g1": bf(w["g1"][l]), "g2": bf(w["g2"][l]), "k_k": row(w["k_k"][l]), "k_a": row(w["k_a"][l]),
        "ones_head": bf(_block_diag(ones64)),
        "ones_pair": bf(_block_diag(ones64[:2])),
        "mean_pair": bf(_block_diag(ones64[:2]) / RWKV_HEAD),
        "r_k": row(w["r_k"][l]), "ln_x_w": row(w["ln_x_w"][l]), "ln_x_b": row(w["ln_x_b"][l]),
        "conv_w": w["conv_w"][l], "conv_b": row(w["conv_b"][l]),
        "lru_wa": bf(_block_diag(w["lru_wa"][l])), "lru_ba": row(w["lru_ba"][l]),
        "lru_wi": bf(_block_diag(w["lru_wi"][l])), "lru_bi": row(w["lru_bi"][l]),
        "lru_lambda": row(w["lru_lambda"][l]),
        "s5_d": row(w["s5_d"][l]), "s5_w_glu": bf(w["s5_w_glu"][l]), "s5_b_glu": row(w["s5_b_glu"][l]),
        "w_out_a": bf(w["w_out"][l][0:RWKV_DIM]),
        "w_out_b": bf(w["w_out"][l][RWKV_DIM:RWKV_DIM + LRU_DIM]),
        "w_out_c": bf(w["w_out"][l][RWKV_DIM + LRU_DIM:]),
        "norm_mem_q": row(w["norm_mem_q"][l]), "mem_wq": bf(w["mem_wq"][l]), "mem_wo": bf(w["mem_wo"][l]),
        "norm_ffn": row(w["norm_ffn"][l]), "ffn_w_up": bf(w["ffn_w_up"][l]), "ffn_w_down": bf(w["ffn_w_down"][l]),
    }
    lam_re, lam_im = w["s5_a_re"][l].astype(F32), w["s5_a_im"][l].astype(F32)
    dt = jnp.exp(w["s5_log_dt"][l].astype(F32))[:, None]
    mag = jnp.exp(lam_re * dt)
    abar_re, abar_im = mag * jnp.cos(lam_im * dt), mag * jnp.sin(lam_im * dt)
    den = lam_re * lam_re + lam_im * lam_im
    q_re = ((abar_re - 1.0) * lam_re + abar_im * lam_im) / den
    q_im = (abar_im * lam_re - (abar_re - 1.0) * lam_im) / den
    b_re, b_im = w["s5_b_re"][l].astype(F32), w["s5_b_im"][l].astype(F32)
    bbar_re = q_re[..., None] * b_re - q_im[..., None] * b_im
    bbar_im = q_re[..., None] * b_im + q_im[..., None] * b_re
    p["s5_abar_re"] = abar_re.reshape(1, S5_WIDTH)
    p["s5_abar_im"] = abar_im.reshape(1, S5_WIDTH)
    p["s5_bbar_re"] = bf(_block_diag(jnp.swapaxes(bbar_re, 1, 2)))
    p["s5_bbar_im"] = bf(_block_diag(jnp.swapaxes(bbar_im, 1, 2)))
    p["s5_c_re"] = bf(_block_diag(jnp.swapaxes(w["s5_c_re"][l], 1, 2)))
    p["s5_c_im"] = bf(_block_diag(jnp.swapaxes(w["s5_c_im"][l], 1, 2)))
    vres = None
    if l > 0:
        vres = {"mu_v": row(w["mu_v"][l - 1]), "v0": row(w["v0"][l - 1]),
                "v1": bf(w["v1"][l - 1]), "v2": bf(w["v2"][l - 1])}
    return p, vres


def _pair_states(s):
    nb = s.shape[0]
    s = s.reshape(nb, N_PAIRS, 2, RWKV_HEAD, RWKV_HEAD)
    eye = jnp.eye(2, dtype=s.dtype)
    return (eye[None, None, :, None, :, None] * s[:, :, :, :, None, :]).reshape(nb, N_PAIRS, HEAD_PAIR, HEAD_PAIR)


def _unpair_states(sp):
    nb = sp.shape[0]
    s = sp.reshape(nb, N_PAIRS, 2, RWKV_HEAD, 2, RWKV_HEAD)
    return jnp.stack([s[:, :, 0, :, 0, :], s[:, :, 1, :, 1, :]], axis=2).reshape(nb, RWKV_HEADS, RWKV_HEAD, RWKV_HEAD)


def _time_major(a):
    return jnp.swapaxes(a, 0, 1).reshape((a.shape[0] * a.shape[1],) + a.shape[2:])


def _swap_major(a, n_outer, n_inner):
    lead = a.shape[:-2]
    split = a.reshape(lead + (n_outer, n_inner, a.shape[-1]))
    return jnp.swapaxes(split, -3, -2).reshape(a.shape)


def _run_group(x, mem_k, mem_v, kv_stride, shift0, wkv0, conv0, lru0, re0, im0, reset_first, layers, norm_final,
               strided_seq):
    nb, seq, _ = x.shape
    xt = _time_major(x)
    to_seq = (lambda a: a) if strided_seq else (lambda a: _swap_major(a, seq, nb))
    to_time = (lambda a: a) if strided_seq else (lambda a: _swap_major(a, nb, seq))
    outs = {k: [] for k in ("shift", "wkv", "conv", "lru", "re", "im")}
    v_first = None
    for l, (p, vres) in enumerate(layers):
        zrest, r, lw, k, v, na, bb, gate, h_last = _mix_in(xt, shift0[l], p, vres, v_first, nb)
        if l == 0:
            v_first = v
        ya, s_out = _wkv([to_seq(a) for a in (r, lw, k, v, na, bb, gate)], _pair_states(wkv0[l]), p, nb, seq,
                         strided_seq)
        ya = to_time(ya)
        yb, conv_new, lru_new = _lru(zrest, _time_major(conv0[l]), lru0[l], p, reset_first, nb)
        yc, re_new, im_new = _s5(zrest, re0[l].reshape(nb, S5_WIDTH), im0[l].reshape(nb, S5_WIDTH), p, nb)
        x1, q = _mix_out(xt, ya, yb, yc, p)
        x2 = to_time(_attn(to_seq(q), to_seq(x1), mem_k, mem_v, l * kv_stride, p["mem_wo"], nb, seq, strided_seq))
        xt = _ffn(x2, p)
        outs["shift"].append(h_last)
        outs["wkv"].append(_unpair_states(s_out))
        outs["conv"].append(jnp.swapaxes(conv_new.reshape(CONV_W - 1, nb, LRU_DIM), 0, 1))
        outs["lru"].append(lru_new)
        outs["re"].append(re_new.reshape(nb, S5_GROUPS, S5_STATE))
        outs["im"].append(im_new.reshape(nb, S5_GROUPS, S5_STATE))
    y = _norm(xt, norm_final)
    y = jnp.swapaxes(y.reshape(seq, nb, D_MODEL), 0, 1)
    return (y,) + tuple(jnp.stack(outs[k]) for k in ("shift", "wkv", "conv", "lru", "re", "im"))


def kernel(x_prompt, x_sample, mem_prompt, state_shift, state_wkv, state_conv, state_lru, state_s5_re, state_s5_im, cache_mem_k, cache_mem_v, norm_mix, w_in, w_out, mu_rkv, mu_wag, mu_v, w0, w1, w2, a0, a1, a2, v0, v1, v2, g1, g2, k_k, k_a, r_k, ln_x_w, ln_x_b, conv_w, conv_b, lru_wa, lru_ba, lru_wi, lru_bi, lru_lambda, s5_a_re, s5_a_im, s5_log_dt, s5_b_re, s5_b_im, s5_c_re, s5_c_im, s5_d, s5_w_glu, s5_b_glu, norm_mem_q, norm_mem_kv, mem_wq, mem_wk, mem_wv, mem_wo, norm_ffn, ffn_w_up, ffn_w_down, norm_final):
    w = dict(norm_mix=norm_mix, w_in=w_in, w_out=w_out, mu_rkv=mu_rkv, mu_wag=mu_wag, mu_v=mu_v, w0=w0, w1=w1,
             w2=w2, a0=a0, a1=a1, a2=a2, v0=v0, v1=v1, v2=v2, g1=g1, g2=g2, k_k=k_k, k_a=k_a, r_k=r_k,
             ln_x_w=ln_x_w, ln_x_b=ln_x_b, conv_w=conv_w, conv_b=conv_b, lru_wa=lru_wa, lru_ba=lru_ba,
             lru_wi=lru_wi, lru_bi=lru_bi, lru_lambda=lru_lambda, s5_a_re=s5_a_re, s5_a_im=s5_a_im,
             s5_log_dt=s5_log_dt, s5_b_re=s5_b_re, s5_b_im=s5_b_im, s5_c_re=s5_c_re, s5_c_im=s5_c_im,
             s5_d=s5_d, s5_w_glu=s5_w_glu, s5_b_glu=s5_b_glu, norm_mem_q=norm_mem_q, mem_wq=mem_wq,
             mem_wo=mem_wo, norm_ffn=norm_ffn, ffn_w_up=ffn_w_up, ffn_w_down=ffn_w_down)
    depth = norm_mix.shape[0]
    layers = [_layer_params(w, l) for l in range(depth)]
    g_final = norm_final.reshape(1, D_MODEL)

    bp, n_mem, _ = mem_prompt.shape
    mem_flat = mem_prompt.reshape(bp * n_mem, D_MODEL)
    mem_kv = [_mem_project(mem_flat, norm_mem_kv[l].reshape(1, D_MODEL), mem_wk[l].astype(BF16),
                           mem_wv[l].astype(BF16)) for l in range(depth)]
    mem_k_p = jnp.stack([kv[0] for kv in mem_kv]).reshape(depth, bp, n_mem, MEM_HEADS, MEM_HEAD_DIM)
    mem_v_p = jnp.stack([kv[1] for kv in mem_kv]).reshape(depth, bp, n_mem, MEM_HEADS, MEM_HEAD_DIM)
    mem_k_b = jnp.stack([kv[2] for kv in mem_kv]).reshape(depth * bp, n_mem, D_MODEL)
    mem_v_b = jnp.stack([kv[3] for kv in mem_kv]).reshape(depth * bp, n_mem, D_MODEL)

    zeros = lambda *s: jnp.zeros((depth, bp) + s, F32)
    prompt = _run_group(
        x_prompt, mem_k_b, mem_v_b, bp,
        zeros(D_MODEL), zeros(RWKV_HEADS, RWKV_HEAD, RWKV_HEAD), zeros(CONV_W - 1, LRU_DIM), zeros(LRU_DIM),
        zeros(S5_GROUPS, S5_STATE), zeros(S5_GROUPS, S5_STATE), True, layers, g_final, True)
    bs = x_sample.shape[0]
    sample = _run_group(
        x_sample, cache_mem_k.reshape(depth * bs, n_mem, D_MODEL), cache_mem_v.reshape(depth * bs, n_mem, D_MODEL),
        bs, state_shift, state_wkv, state_conv, state_lru, state_s5_re, state_s5_im, False, layers, g_final, False)
    (y_p, sh_p, wkv_p, conv_p, lru_p, re_p, im_p) = prompt
    (y_s, sh_s, wkv_s, conv_s, lru_s, re_s, im_s) = sample
    return (y_p, y_s, sh_p, sh_s, wkv_p, wkv_s, conv_p, conv_s, lru_p, lru_s, re_p, re_s, im_p, im_s,
            mem_k_p, mem_v_p)
```

```python
import functools
import math

import jax
import jax.numpy as jnp
from jax import lax
from jax.experimental import pallas as pl
from jax.experimental.pallas import tpu as pltpu

F32 = jnp.float32
BF16 = jnp.bfloat16

D_MODEL = 1024
RWKV_DIM = 512
RWKV_HEAD = 64
RWKV_HEADS = 8
LANES = 128
LANE_BLOCKS = D_MODEL // LANES
HEAD_PAIR = 2 * RWKV_HEAD
N_PAIRS = RWKV_DIM // HEAD_PAIR
LRU_DIM = 256
LRU_BLOCKS = 4
CONV_W = 4
LRU_C = 8.0
S5_DIM = 256
S5_GROUP = 16
S5_GROUPS = 16
S5_STATE = 64
S5_WIDTH = S5_GROUPS * S5_STATE
R3 = 3 * RWKV_DIM
D_IN = R3 + 2 * LRU_DIM + S5_DIM
N_MEM = 256
MEM_HEADS = 4
MEM_HEAD_DIM = 256
RMS_EPS = 1e-6
GN_EPS = 64e-5
WKV_CHUNK = 64

V7X_VMEM_LIMIT = 56 * 1024 * 1024


def _params(*sem):
    return pltpu.CompilerParams(dimension_semantics=sem, vmem_limit_bytes=V7X_VMEM_LIMIT)


def _tile(m, mult, target):
    best = mult
    t = mult
    while t <= min(m, target):
        if m % t == 0:
            best = t
        t += mult
    assert m % best == 0, (m, mult, target)
    return best


def _const_spec(shape):
    nd = len(shape)
    return pl.BlockSpec(shape, lambda *_: (0,) * nd)


def _bdot(a, b):
    return jnp.dot(a.astype(BF16), b.astype(BF16), preferred_element_type=F32)


def _bdot_nt(a, b):
    return lax.dot_general(a.astype(BF16), b.astype(BF16), (((1,), (1,)), ((), ())),
                           preferred_element_type=F32)


def _bdot_tn(a, b):
    return lax.dot_general(a.astype(BF16), b.astype(BF16), (((0,), (0,)), ((), ())),
                           preferred_element_type=F32)


def _split2(x):
    hi = x.astype(BF16)
    lo = (x - hi.astype(F32)).astype(BF16)
    return hi, lo


def _dot_exact_rhs(x, w):
    hi, lo = _split2(x)
    return (jnp.dot(hi, w, preferred_element_type=F32)
            + jnp.dot(lo, w, preferred_element_type=F32))


def _rms(x, g):
    return x * lax.rsqrt(jnp.mean(x * x, axis=-1, keepdims=True) + RMS_EPS) * g


def _softplus(x):
    return jnp.maximum(x, 0.0) + jnp.log1p(jnp.exp(-jnp.abs(x)))


def _sigmoid(x):
    return 1.0 / (1.0 + jnp.exp(-x))


def _gelu(x):
    c = math.sqrt(2.0 / math.pi)
    return x * (0.5 * (1.0 + jnp.tanh(c * (x + 0.044715 * (x * x * x)))))


def _mix_in_kernel(has_vres, nb, tm, *refs):
    if has_vres:
        (x_ref, first_ref, g_ref, win_ref, murkv_ref, muwag_ref, w0_ref, w1_ref, w2_ref,
         a0_ref, a1_ref, a2_ref, g1_ref, g2_ref, kk_ref, ka_ref, ones_ref,
         muv_ref, v0_ref, v1_ref, v2_ref, vfirst_ref,
         zrest_ref, r_ref, lw_ref, k_ref, v_ref, na_ref, bb_ref, gate_ref, hlast_ref,
         hbuf, zbuf) = refs
    else:
        (x_ref, first_ref, g_ref, win_ref, murkv_ref, muwag_ref, w0_ref, w1_ref, w2_ref,
         a0_ref, a1_ref, a2_ref, g1_ref, g2_ref, kk_ref, ka_ref, ones_ref,
         zrest_ref, r_ref, lw_ref, k_ref, v_ref, na_ref, bb_ref, gate_ref, hlast_ref,
         hbuf, zbuf) = refs

    @pl.when(pl.program_id(0) == 0)
    def _():
        first = first_ref[...]
        hbuf[0:nb, :] = first
        zbuf[0:nb, :] = _bdot(first, win_ref[:, 0:R3])

    h = _rms(x_ref[...], g_ref[...])
    hbuf[nb:nb + tm, :] = h
    hp = hbuf[0:tm, :]
    hbuf[0:nb, :] = hbuf[tm:tm + nb, :]
    hlast_ref[...] = h[tm - nb:tm, :]

    z = _bdot(h, win_ref[...])
    zrest_ref[...] = z[:, R3:D_IN]
    z_rkv = z[:, 0:R3]
    zbuf[nb:nb + tm, :] = z_rkv
    z_sh = zbuf[0:tm, :]
    zbuf[0:nb, :] = zbuf[tm:tm + nb, :]
    zmix = z_rkv + (z_sh - z_rkv) * murkv_ref[...]
    r = zmix[:, 0:RWKV_DIM]
    k = zmix[:, RWKV_DIM:2 * RWKV_DIM]
    v = zmix[:, 2 * RWKV_DIM:R3]

    dh = hp - h
    xw = h + dh * muwag_ref[0:1, :]
    xa = h + dh * muwag_ref[1:2, :]
    xg = h + dh * muwag_ref[2:3, :]
    w_pre = w0_ref[...] + _bdot(jnp.tanh(_bdot(xw, w1_ref[...])), w2_ref[...])
    w_log = -_softplus(-w_pre) - 0.5
    a = _sigmoid(a0_ref[...] + _bdot(_bdot(xa, a1_ref[...]), a2_ref[...]))
    gate = _bdot(_sigmoid(_bdot(xg, g1_ref[...])), g2_ref[...])
    if has_vres:
        xv = h + dh * muv_ref[...]
        mix = _sigmoid(v0_ref[...] + _bdot(_bdot(xv, v1_ref[...]), v2_ref[...]))
        v_first = jnp.concatenate([vfirst_ref[p] for p in range(N_PAIRS)], axis=-1)
        v = v + (v_first - v) * mix
    kk = k * kk_ref[...]
    ss = _dot_exact_rhs(kk * kk, ones_ref[...])
    kk = kk / jnp.maximum(jnp.sqrt(ss), 1e-12)
    outs = ((r_ref, r), (lw_ref, -jnp.exp(w_log)), (k_ref, k * (1.0 + (a - 1.0) * ka_ref[...])), (v_ref, v),
            (na_ref, -kk), (bb_ref, kk * a), (gate_ref, gate))
    for ref, val in outs:
        for p in range(N_PAIRS):
            ref[p] = val[:, p * HEAD_PAIR:(p + 1) * HEAD_PAIR]


def _mix_in(x, first, p, vres, v_first, nb):
    m = x.shape[0]
    tm = _tile(m, max(nb, 8), 256)
    has_vres = vres is not None
    row = lambda w: pl.BlockSpec((tm, w), lambda i: (i, 0))
    pair_row = pl.BlockSpec((N_PAIRS, tm, HEAD_PAIR), lambda i: (0, i, 0))
    ins = [x, first, p["norm_mix"], p["w_in"], p["mu_rkv"], p["mu_wag"], p["w0"], p["w1"], p["w2"],
           p["a0"], p["a1"], p["a2"], p["g1"], p["g2"], p["k_k"], p["k_a"], p["ones_head"]]
    specs = [row(D_MODEL)] + [_const_spec(a.shape) for a in ins[1:]]
    if has_vres:
        extra = [vres["mu_v"], vres["v0"], vres["v1"], vres["v2"]]
        ins += extra + [v_first]
        specs += [_const_spec(a.shape) for a in extra] + [pair_row]
    wide = jax.ShapeDtypeStruct((N_PAIRS, m, HEAD_PAIR), F32)
    outs = [jax.ShapeDtypeStruct((m, D_IN - R3), F32)] + [wide] * 7 + [jax.ShapeDtypeStruct((nb, D_MODEL), F32)]
    out_specs = [row(D_IN - R3)] + [pair_row] * 7 + [_const_spec((nb, D_MODEL))]
    return pl.pallas_call(
        functools.partial(_mix_in_kernel, has_vres, nb, tm),
        grid=(m // tm,),
        in_specs=specs,
        out_specs=out_specs,
        out_shape=outs,
        scratch_shapes=[pltpu.VMEM((tm + nb, D_MODEL), F32), pltpu.VMEM((tm + nb, R3), F32)],
        compiler_params=_params("arbitrary"),
        name="mix_in",
    )(*ins)


def _wkv_kernel(c, n_seq, group, stride, r_ref, lw_ref, k_ref, v_ref, na_ref, bb_ref, gate_ref, s0_ref,
                rk_ref, lnw_ref, lnb_ref, tri_ref, mean_ref, ones_ref,
                y_ref, sout_ref, s_scr):
    li = pl.program_id(1)

    @pl.when(li == 0)
    def _():
        s_scr[...] = s0_ref[...]

    lane = lax.broadcasted_iota(jnp.int32, (1, HEAD_PAIR), 1)
    head_masks = (lane < RWKV_HEAD, lane >= RWKV_HEAD)
    trow = lax.broadcasted_iota(jnp.int32, (c, 2 * c), 0)
    tcol = lax.broadcasted_iota(jnp.int32, (c, 2 * c), 1)
    scol = jnp.where(tcol >= c, tcol - c, tcol)
    strict_k = (scol < trow) & (tcol >= c)
    incl = scol <= trow
    lrow = lax.broadcasted_iota(jnp.int32, (c, c), 0)
    lcol = lax.broadcasted_iota(jnp.int32, (c, c), 1)
    strict = lcol < lrow
    eye = (lrow == lcol).astype(F32)
    brow = lax.broadcasted_iota(jnp.int32, (HEAD_PAIR, HEAD_PAIR), 0)
    bcol = lax.broadcasted_iota(jnp.int32, (HEAD_PAIR, HEAD_PAIR), 1)
    same_head = jnp.bitwise_xor(brow - RWKV_HEAD, bcol - RWKV_HEAD) >= 0
    n_double = max(int(math.log2(c)) - 1, 0)
    tri = tri_ref[...]
    zeros_c = jnp.zeros((c, HEAD_PAIR), F32)

    def pair_chunk(lw, r, k, v, na, bb, gate, sp, cols):
        hi = lw.astype(BF16)
        rem = lw - hi.astype(F32)
        mid = rem.astype(BF16)
        lo = (rem - mid.astype(F32)).astype(BF16)
        cl = (jnp.dot(tri, hi, preferred_element_type=F32)
              + jnp.dot(tri, mid, preferred_element_type=F32)
              + jnp.dot(tri, lo, preferred_element_type=F32))
        yield
        e_in = jnp.exp(cl)
        e_neg = jnp.exp(-cl)
        g_last = jnp.exp(jnp.sum(lw, axis=0, keepdims=True))
        at = na * jnp.exp(cl - lw)
        rt = r * e_in
        bt = bb * e_neg
        kt = k * e_neg
        ar = jnp.concatenate([at, rt], axis=0)
        bk = jnp.concatenate([bt, kt], axis=0)
        a_s = _bdot_nt(ar, sp)
        zv = jnp.concatenate([zeros_c, v], axis=0)
        gs = [_bdot_nt(jnp.where(hm, ar, 0.0), bk) for hm in head_masks]
        yield
        gas = [g[0:c, :] for g in gs]
        grs = [jnp.where(incl, g[c:2 * c, :], 0.0) for g in gs]
        pws = [jnp.where(strict, ga[:, 0:c], 0.0) for ga in gas]
        t_invs = [eye + low for low in pws]
        rhs = [a_s[0:c, :] + _bdot(jnp.where(strict_k, ga, 0.0), zv) for ga in gas]
        for _ in range(n_double):
            pws = [_bdot(pw, pw) for pw in pws]
            yield
            t_invs = [t_inv + _bdot(t_inv, pw) for t_inv, pw in zip(t_invs, pws)]
        yield
        us = [_bdot(t_inv, x) for t_inv, x in zip(t_invs, rhs)]
        yield
        u = jnp.where(head_masks[0], us[0], us[1])
        uv = jnp.concatenate([u, v], axis=0)
        y = a_s[c:2 * c, :] + jnp.where(head_masks[0], _bdot(grs[0], uv), _bdot(grs[1], uv))
        upd = _bdot_tn(uv, bk * g_last)
        s_new = sp * g_last + jnp.where(same_head, upd, 0.0)
        yield
        mu = _dot_exact_rhs(y, mean_ref[...])
        bonus = _dot_exact_rhs(r * k * rk_ref[:, cols], ones_ref[...]) * v
        yield
        d = y - mu
        var = _dot_exact_rhs(d * d, mean_ref[...])
        yield
        yn = d * lax.rsqrt(var + GN_EPS) * lnw_ref[:, cols] + lnb_ref[:, cols]
        return (yn + bonus) * gate, s_new

    first_seq = pl.program_id(0) * n_seq

    def seq_group(gi, carry):
        units = []
        for jj in range(group):
            j = gi * group + jj
            rows = pl.ds(first_seq + j, c, stride=stride)
            for p in range(N_PAIRS):
                cols = slice(p * HEAD_PAIR, (p + 1) * HEAD_PAIR)
                vals = [ref[p, rows, :] for ref in (lw_ref, r_ref, k_ref, v_ref, na_ref, bb_ref, gate_ref)]
                units.append((j, p, rows, pair_chunk(*vals, s_scr[j, p], cols)))
        results = {}
        while len(results) < len(units):
            for i, (_, _, _, gen) in enumerate(units):
                if i not in results:
                    try:
                        next(gen)
                    except StopIteration as stop:
                        results[i] = stop.value
        for i, (j, p, rows, _) in enumerate(units):
            y, s_new = results[i]
            y_ref[p, rows, :] = y
            s_scr[j, p] = s_new
        return carry

    lax.fori_loop(0, n_seq // group, seq_group, 0)

    @pl.when(li == pl.num_programs(1) - 1)
    def _():
        sout_ref[...] = s_scr[...]


WKV_SEQ_PER_STEP = 8
WKV_SEQ_GROUP = 2


def _wkv(arrs, s0, p, nb, seq):
    c = min(WKV_CHUNK, seq)
    n_seq = nb if seq > c else min(nb, WKV_SEQ_PER_STEP)
    assert seq % c == 0 and c % 8 == 0 and nb % n_seq == 0 and n_seq % WKV_SEQ_GROUP == 0
    blk = pl.BlockSpec((N_PAIRS, c * nb, HEAD_PAIR), lambda g, l: (0, l, 0))
    tri = jnp.tril(jnp.ones((c, c), F32)).astype(BF16)
    st_spec = pl.BlockSpec((n_seq, N_PAIRS, HEAD_PAIR, HEAD_PAIR), lambda g, l: (g, 0, 0, 0))
    consts = [p["r_k"], p["ln_x_w"], p["ln_x_b"], tri, p["mean_pair"], p["ones_pair"]]
    return pl.pallas_call(
        functools.partial(_wkv_kernel, c, n_seq, WKV_SEQ_GROUP, nb),
        grid=(nb // n_seq, seq // c),
        in_specs=[blk] * 7 + [st_spec] + [_const_spec(a.shape) for a in consts],
        out_specs=[blk, st_spec],
        out_shape=[jax.ShapeDtypeStruct((N_PAIRS, seq * nb, HEAD_PAIR), F32), jax.ShapeDtypeStruct(s0.shape, F32)],
        scratch_shapes=[pltpu.VMEM((n_seq, N_PAIRS, HEAD_PAIR, HEAD_PAIR), F32)],
        compiler_params=_params("arbitrary", "arbitrary"),
        name="wkv",
    )(*arrs, s0, *consts)


def _lru_kernel(reset_first, nb, tm, zx_ref, zg_ref, conv0_ref, h0_ref, cw_ref, cb_ref, wa_ref, ba_ref,
                wi_ref, bi_ref, lam_ref, y_ref, conv_out_ref, h_out_ref, xbuf, abuf, bbuf, hcar):
    i = pl.program_id(0)
    hist = (CONV_W - 1) * nb

    @pl.when(i == 0)
    def _():
        xbuf[0:hist, :] = conv0_ref[...]
        hcar[...] = h0_ref[...]

    xbuf[hist:hist + tm, :] = zx_ref[...]
    xc = cb_ref[...]
    for j in range(CONV_W):
        xc = xc + xbuf[j * nb:j * nb + tm, :] * cw_ref[j:j + 1, :]
    conv_out_ref[...] = xbuf[tm:tm + hist, :]
    xbuf[0:hist, :] = xbuf[tm:tm + hist, :]

    gate_a = _sigmoid(_bdot(xc, wa_ref[...]) + ba_ref[...])
    gate_i = _sigmoid(_bdot(xc, wi_ref[...]) + bi_ref[...])
    log_a = -LRU_C * gate_a * _softplus(-lam_ref[...])
    a_sq = jnp.exp(2.0 * log_a)
    mult = jnp.sqrt(-jnp.tanh(log_a) * (a_sq + 1.0))
    if reset_first:
        row = lax.broadcasted_iota(jnp.int32, (tm, 1), 0)
        mult = jnp.where(row < jnp.where(i == 0, nb, 0), 1.0, mult)
    abuf[...] = jnp.exp(log_a)
    bbuf[...] = mult * gate_i * xc

    def step(t, h):
        rows = pl.ds(pl.multiple_of(t * nb, nb), nb)
        h = abuf[rows, :] * h + bbuf[rows, :]
        bbuf[rows, :] = h
        return h

    h = lax.fori_loop(0, tm // nb, step, hcar[...])
    hcar[...] = h
    h_out_ref[...] = h
    y_ref[...] = bbuf[...] * _gelu(zg_ref[...])


def _lru(zrest, conv0, h0, p, reset_first, nb):
    m = zrest.shape[0]
    tm = _tile(m, max(nb, 8), 512)
    hist = (CONV_W - 1) * nb
    consts = [conv0, h0, p["conv_w"], p["conv_b"], p["lru_wa"], p["lru_ba"], p["lru_wi"], p["lru_bi"],
              p["lru_lambda"]]
    return pl.pallas_call(
        functools.partial(_lru_kernel, reset_first, nb, tm),
        grid=(m // tm,),
        in_specs=[pl.BlockSpec((tm, LRU_DIM), lambda i: (i, 0)),
                  pl.BlockSpec((tm, LRU_DIM), lambda i: (i, 1))] + [_const_spec(a.shape) for a in consts],
        out_specs=[pl.BlockSpec((tm, LRU_DIM), lambda i: (i, 0)), _const_spec((hist, LRU_DIM)),
                   _const_spec((nb, LRU_DIM))],
        out_shape=[jax.ShapeDtypeStruct((m, LRU_DIM), F32), jax.ShapeDtypeStruct((hist, LRU_DIM), F32),
                   jax.ShapeDtypeStruct((nb, LRU_DIM), F32)],
        scratch_shapes=[pltpu.VMEM((tm + hist, LRU_DIM), F32), pltpu.VMEM((tm, LRU_DIM), F32),
                        pltpu.VMEM((tm, LRU_DIM), F32), pltpu.VMEM((nb, LRU_DIM), F32)],
        compiler_params=_params("arbitrary"),
        name="rg_lru",
    )(zrest, zrest, *consts)


def _s5_kernel(nb, tm, u_ref, re0_ref, im0_ref, are_ref, aim_ref, bre_ref, bim_ref, cre_ref, cim_ref,
               d_ref, wglu_ref, bglu_ref, y_ref, re_out_ref, im_out_ref, xre, xim, car_re, car_im):
    @pl.when(pl.program_id(0) == 0)
    def _():
        car_re[...] = re0_ref[...]
        car_im[...] = im0_ref[...]

    u = u_ref[...]
    xre[...] = _bdot(u, bre_ref[...])
    xim[...] = _bdot(u, bim_ref[...])
    a_re = jnp.broadcast_to(are_ref[...], (nb, S5_WIDTH))
    a_im = jnp.broadcast_to(aim_ref[...], (nb, S5_WIDTH))

    def step(t, carry):
        s_re, s_im = carry
        rows = pl.ds(pl.multiple_of(t * nb, nb), nb)
        n_re = a_re * s_re - a_im * s_im + xre[rows, :]
        n_im = a_re * s_im + a_im * s_re + xim[rows, :]
        xre[rows, :] = n_re
        xim[rows, :] = n_im
        return n_re, n_im

    s_re, s_im = lax.fori_loop(0, tm // nb, step, (car_re[...], car_im[...]))
    car_re[...] = s_re
    car_im[...] = s_im
    re_out_ref[...] = s_re
    im_out_ref[...] = s_im

    y = _bdot(xre[...], cre_ref[...]) - _bdot(xim[...], cim_ref[...]) + d_ref[...] * u
    o = _bdot(_gelu(y), wglu_ref[...]) + bglu_ref[...]
    y_ref[...] = o[:, 0:S5_DIM] * _sigmoid(o[:, S5_DIM:2 * S5_DIM])


def _s5(zrest, re0, im0, p, nb):
    m = zrest.shape[0]
    tm = _tile(m, max(nb, 8), 512)
    consts = [re0, im0, p["s5_abar_re"], p["s5_abar_im"], p["s5_bbar_re"], p["s5_bbar_im"],
              p["s5_c_re"], p["s5_c_im"], p["s5_d"], p["s5_w_glu"], p["s5_b_glu"]]
    st = jax.ShapeDtypeStruct((nb, S5_WIDTH), F32)
    return pl.pallas_call(
        functools.partial(_s5_kernel, nb, tm),
        grid=(m // tm,),
        in_specs=[pl.BlockSpec((tm, S5_DIM), lambda i: (i, 2))] + [_const_spec(a.shape) for a in consts],
        out_specs=[pl.BlockSpec((tm, S5_DIM), lambda i: (i, 0)), _const_spec((nb, S5_WIDTH)),
                   _const_spec((nb, S5_WIDTH))],
        out_shape=[jax.ShapeDtypeStruct((m, S5_DIM), F32), st, st],
        scratch_shapes=[pltpu.VMEM((tm, S5_WIDTH), F32), pltpu.VMEM((tm, S5_WIDTH), F32),
                        pltpu.VMEM((nb, S5_WIDTH), F32), pltpu.VMEM((nb, S5_WIDTH), F32)],
        compiler_params=_params("arbitrary"),
        name="s5",
    )(zrest, *consts)


def _mix_out_kernel(x_ref, ya_ref, yb_ref, yc_ref, wa_ref, wb_ref, wc_ref, gq_ref, wq_ref, x1_ref, q_ref):
    ya = jnp.concatenate([ya_ref[p] for p in range(N_PAIRS)], axis=-1)
    x1 = (x_ref[...] + _bdot(ya, wa_ref[...]) + _bdot(yb_ref[...], wb_ref[...])
          + _bdot(yc_ref[...], wc_ref[...]))
    q = _bdot(_rms(x1, gq_ref[...]), wq_ref[...])
    for cb in range(LANE_BLOCKS):
        cols = slice(cb * LANES, (cb + 1) * LANES)
        x1_ref[cb] = x1[:, cols]
        q_ref[cb] = q[:, cols]


def _mix_out(x, ya, yb, yc, p):
    m = x.shape[0]
    tm = _tile(m, 8, 512)
    row = lambda w: pl.BlockSpec((tm, w), lambda i: (i, 0))
    split = pl.BlockSpec((LANE_BLOCKS, tm, LANES), lambda i: (0, i, 0))
    consts = [p["w_out_a"], p["w_out_b"], p["w_out_c"], p["norm_mem_q"], p["mem_wq"]]
    out = jax.ShapeDtypeStruct((LANE_BLOCKS, m, LANES), F32)
    return pl.pallas_call(
        _mix_out_kernel,
        grid=(m // tm,),
        in_specs=[row(D_MODEL), pl.BlockSpec((N_PAIRS, tm, HEAD_PAIR), lambda i: (0, i, 0)), row(LRU_DIM),
                  row(S5_DIM)] + [_const_spec(a.shape) for a in consts],
        out_specs=[split, split],
        out_shape=[out, out],
        compiler_params=_params("parallel"),
        name="mix_out",
    )(x, ya, yb, yc, *consts)


def _load_seq_rows(ref, rows):
    return jnp.concatenate([ref[cb, rows, :] for cb in range(LANE_BLOCKS)], axis=-1)


def _store_seq_rows(ref, rows, val):
    for cb in range(LANE_BLOCKS):
        ref[cb, rows, :] = val[:, cb * LANES:(cb + 1) * LANES]


def _attn_kernel(tl, n_seq, stride, q_ref, x_ref, mk_ref, mv_ref, wo_ref, o_ref):
    scale = MEM_HEAD_DIM ** -0.5
    first_seq = pl.program_id(0) * n_seq
    for j in range(n_seq):
        rows = pl.ds(first_seq + j, tl, stride=stride)
        q = _load_seq_rows(q_ref, rows)
        heads = []
        for hh in range(MEM_HEADS):
            cols = slice(hh * MEM_HEAD_DIM, (hh + 1) * MEM_HEAD_DIM)
            s = _bdot_nt(q[:, cols], mk_ref[j, :, cols]) * scale
            e = jnp.exp(s - jnp.max(s, axis=-1, keepdims=True))
            prob = e / jnp.sum(e, axis=-1, keepdims=True)
            heads.append(_bdot(prob, mv_ref[j, :, cols]))
        o = jnp.concatenate(heads, axis=-1)
        _store_seq_rows(o_ref, rows, _load_seq_rows(x_ref, rows) + _bdot(o, wo_ref[...]))


def _attn_cache_kernel(tl, n_seq, stride, q_ref, x_ref, mk_ref, mv_ref, wo_ref, o_ref):
    scale = MEM_HEAD_DIM ** -0.5
    first_seq = pl.program_id(0) * n_seq
    qrow = lax.broadcasted_iota(jnp.int32, (MEM_HEADS * tl, N_MEM * MEM_HEADS), 0)
    kcol = lax.broadcasted_iota(jnp.int32, (MEM_HEADS * tl, N_MEM * MEM_HEADS), 1)
    q_head = jnp.zeros_like(qrow)
    for hh in range(1, MEM_HEADS):
        q_head = q_head + (qrow >= hh * tl).astype(jnp.int32)
    same_head = q_head == jnp.bitwise_and(kcol, MEM_HEADS - 1)
    for j in range(n_seq):
        rows = pl.ds(first_seq + j, tl, stride=stride)
        q = _load_seq_rows(q_ref, rows)
        qh = jnp.concatenate([q[:, hh * MEM_HEAD_DIM:(hh + 1) * MEM_HEAD_DIM] for hh in range(MEM_HEADS)], axis=0)
        k2 = mk_ref[j].reshape(N_MEM * MEM_HEADS, MEM_HEAD_DIM)
        v2 = mv_ref[j].reshape(N_MEM * MEM_HEADS, MEM_HEAD_DIM)
        s = jnp.where(same_head, _bdot_nt(qh, k2) * scale, MASKED_SCORE)
        e = jnp.exp(s - jnp.max(s, axis=-1, keepdims=True))
        prob = e / jnp.sum(e, axis=-1, keepdims=True)
        oh = _bdot(prob, v2)
        o = jnp.concatenate([oh[hh * tl:(hh + 1) * tl, :] for hh in range(MEM_HEADS)], axis=-1)
        _store_seq_rows(o_ref, rows, _load_seq_rows(x_ref, rows) + _bdot(o, wo_ref[...]))


MASKED_SCORE = -1e30
ATTN_SEQ_PER_STEP = 4
ATTN_ROWS = 128


def _attn(q, x1, mk, mv, kv_offset, wo, nb, seq):
    tl = _tile(seq, 8, ATTN_ROWS)
    n_seq = nb if seq > tl else min(nb, ATTN_SEQ_PER_STEP)
    assert nb % n_seq == 0 and kv_offset % n_seq == 0 and MEM_HEADS & (MEM_HEADS - 1) == 0
    blk = pl.BlockSpec((LANE_BLOCKS, tl * nb, LANES), lambda g, l: (0, l, 0))
    kv_index = lambda g, l: (kv_offset // n_seq + g,) + (0,) * (mk.ndim - 1)
    kv = pl.BlockSpec((n_seq,) + mk.shape[1:], kv_index)
    body = _attn_kernel if mk.ndim == 3 else _attn_cache_kernel
    return pl.pallas_call(
        functools.partial(body, tl, n_seq, nb),
        grid=(nb // n_seq, seq // tl),
        in_specs=[blk, blk, kv, kv, _const_spec(wo.shape)],
        out_specs=blk,
        out_shape=jax.ShapeDtypeStruct((LANE_BLOCKS, seq * nb, LANES), F32),
        compiler_params=_params("arbitrary", "arbitrary"),
        name="mem_attn",
    )(q, x1, mk, mv, wo)


def _ffn_kernel(d_ff, x_ref, g_ref, wup_ref, wdown_ref, o_ref):
    x = jnp.concatenate([x_ref[cb] for cb in range(LANE_BLOCKS)], axis=-1)
    h = _rms(x, g_ref[...]).astype(BF16)
    gt = jnp.dot(h, wup_ref[:, 0:d_ff], preferred_element_type=F32)
    up = jnp.dot(h, wup_ref[:, d_ff:2 * d_ff], preferred_element_type=F32)
    act = gt * _sigmoid(gt) * up
    o_ref[...] = x + _bdot(act, wdown_ref[...])


def _ffn(x, p):
    m = x.shape[1]
    tm = _tile(m, 8, 256)
    d_ff = p["ffn_w_down"].shape[0]
    row = pl.BlockSpec((tm, D_MODEL), lambda i: (i, 0))
    consts = [p["norm_ffn"], p["ffn_w_up"], p["ffn_w_down"]]
    return pl.pallas_call(
        functools.partial(_ffn_kernel, d_ff),
        grid=(m // tm,),
        in_specs=[pl.BlockSpec((LANE_BLOCKS, tm, LANES), lambda i: (0, i, 0))] + [_const_spec(a.shape) for a in consts],
        out_specs=row,
        out_shape=jax.ShapeDtypeStruct((m, D_MODEL), F32),
        compiler_params=_params("parallel"),
        name="ffn",
    )(x, *consts)


def _norm_kernel(x_ref, g_ref, o_ref):
    o_ref[...] = _rms(x_ref[...], g_ref[...])


def _norm(x, g):
    m = x.shape[0]
    tm = _tile(m, 8, 512)
    row = pl.BlockSpec((tm, D_MODEL), lambda i: (i, 0))
    return pl.pallas_call(
        _norm_kernel, grid=(m // tm,), in_specs=[row, _const_spec(g.shape)], out_specs=row,
        out_shape=jax.ShapeDtypeStruct((m, D_MODEL), F32), compiler_params=_params("parallel"),
        name="final_norm",
    )(x, g)


def _mem_project_kernel(x_ref, g_ref, wk_ref, wv_ref, k_ref, v_ref, kb_ref, vb_ref):
    m = _rms(x_ref[...], g_ref[...]).astype(BF16)
    k = jnp.dot(m, wk_ref[...], preferred_element_type=F32)
    v = jnp.dot(m, wv_ref[...], preferred_element_type=F32)
    k_ref[...] = k
    v_ref[...] = v
    kb_ref[...] = k.astype(BF16)
    vb_ref[...] = v.astype(BF16)


def _mem_project(mem, g, wk, wv):
    m = mem.shape[0]
    tm = _tile(m, 16, 512)
    row = pl.BlockSpec((tm, D_MODEL), lambda i: (i, 0))
    out = jax.ShapeDtypeStruct((m, D_MODEL), F32)
    outb = jax.ShapeDtypeStruct((m, D_MODEL), BF16)
    return pl.pallas_call(
        _mem_project_kernel, grid=(m // tm,),
        in_specs=[row, _const_spec(g.shape), _const_spec(wk.shape), _const_spec(wv.shape)],
        out_specs=[row] * 4, out_shape=[out, out, outb, outb], compiler_params=_params("parallel"),
        name="mem_project",
    )(mem, g, wk, wv)


def _block_diag(blocks):
    n, r, c = blocks.shape
    eye = jnp.eye(n, dtype=blocks.dtype)
    return (eye[:, None, :, None] * blocks[:, :, None, :]).reshape(n * r, n * c)


def _layer_params(w, l):
    row = lambda a: a.reshape(1, -1).astype(F32)
    bf = lambda a: a.astype(BF16)
    ones64 = jnp.ones((RWKV_HEADS, RWKV_HEAD, RWKV_HEAD), F32)
    p = {
        "norm_mix": row(w["norm_mix"][l]), "w_in": bf(w["w_in"][l]), "mu_rkv": row(w["mu_rkv"][l]),
        "mu_wag": w["mu_wag"][l], "w0": row(w["w0"][l]), "w1": bf(w["w1"][l]), "w2": bf(w["w2"][l]),
        "a0": row(w["a0"][l]), "a1": bf(w["a1"][l]), "a2": bf(w["a2"][l]),
        "g1": bf(w["g1"][l]), "g2": bf(w["g2"][l]), "k_k": row(w["k_k"][l]), "k_a": row(w["k_a"][l]),
        "ones_head": bf(_block_diag(ones64)),
        "ones_pair": bf(_block_diag(ones64[:2])),
        "mean_pair": bf(_block_diag(ones64[:2]) / RWKV_HEAD),
        "r_k": row(w["r_k"][l]), "ln_x_w": row(w["ln_x_w"][l]), "ln_x_b": row(w["ln_x_b"][l]),
        "conv_w": w["conv_w"][l], "conv_b": row(w["conv_b"][l]),
        "lru_wa": bf(_block_diag(w["lru_wa"][l])), "lru_ba": row(w["lru_ba"][l]),
        "lru_wi": bf(_block_diag(w["lru_wi"][l])), "lru_bi": row(w["lru_bi"][l]),
        "lru_lambda": row(w["lru_lambda"][l]),
        "s5_d": row(w["s5_d"][l]), "s5_w_glu": bf(w["s5_w_glu"][l]), "s5_b_glu": row(w["s5_b_glu"][l]),
        "w_out_a": bf(w["w_out"][l][0:RWKV_DIM]),
        "w_out_b": bf(w["w_out"][l][RWKV_DIM:RWKV_DIM + LRU_DIM]),
        "w_out_c": bf(w["w_out"][l][RWKV_DIM + LRU_DIM:]),
        "norm_mem_q": row(w["norm_mem_q"][l]), "mem_wq": bf(w["mem_wq"][l]), "mem_wo": bf(w["mem_wo"][l]),
        "norm_ffn": row(w["norm_ffn"][l]), "ffn_w_up": bf(w["ffn_w_up"][l]), "ffn_w_down": bf(w["ffn_w_down"][l]),
    }
    lam_re, lam_im = w["s5_a_re"][l].astype(F32), w["s5_a_im"][l].astype(F32)
    dt = jnp.exp(w["s5_log_dt"][l].astype(F32))[:, None]
    mag = jnp.exp(lam_re * dt)
    abar_re, abar_im = mag * jnp.cos(lam_im * dt), mag * jnp.sin(lam_im * dt)
    den = lam_re * lam_re + lam_im * lam_im
    q_re = ((abar_re - 1.0) * lam_re + abar_im * lam_im) / den
    q_im = (abar_im * lam_re - (abar_re - 1.0) * lam_im) / den
    b_re, b_im = w["s5_b_re"][l].astype(F32), w["s5_b_im"][l].astype(F32)
    bbar_re = q_re[..., None] * b_re - q_im[..., None] * b_im
    bbar_im = q_re[..., None] * b_im + q_im[..., None] * b_re
    p["s5_abar_re"] = abar_re.reshape(1, S5_WIDTH)
    p["s5_abar_im"] = abar_im.reshape(1, S5_WIDTH)
    p["s5_bbar_re"] = bf(_block_diag(jnp.swapaxes(bbar_re, 1, 2)))
    p["s5_bbar_im"] = bf(_block_diag(jnp.swapaxes(bbar_im, 1, 2)))
    p["s5_c_re"] = bf(_block_diag(jnp.swapaxes(w["s5_c_re"][l], 1, 2)))
    p["s5_c_im"] = bf(_block_diag(jnp.swapaxes(w["s5_c_im"][l], 1, 2)))
    vres = None
    if l > 0:
        vres = {"mu_v": row(w["mu_v"][l - 1]), "v0": row(w["v0"][l - 1]),
                "v1": bf(w["v1"][l - 1]), "v2": bf(w["v2"][l - 1])}
    return p, vres


def _pair_states(s):
    nb = s.shape[0]
    s = s.reshape(nb, N_PAIRS, 2, RWKV_HEAD, RWKV_HEAD)
    eye = jnp.eye(2, dtype=s.dtype)
    return (eye[None, None, :, None, :, None] * s[:, :, :, :, None, :]).reshape(nb, N_PAIRS, HEAD_PAIR, HEAD_PAIR)


def _unpair_states(sp):
    nb = sp.shape[0]
    s = sp.reshape(nb, N_PAIRS, 2, RWKV_HEAD, 2, RWKV_HEAD)
    return jnp.stack([s[:, :, 0, :, 0, :], s[:, :, 1, :, 1, :]], axis=2).reshape(nb, RWKV_HEADS, RWKV_HEAD, RWKV_HEAD)


def _time_major(a):
    return jnp.swapaxes(a, 0, 1).reshape((a.shape[0] * a.shape[1],) + a.shape[2:])


def _run_group(x, mem_k, mem_v, kv_stride, shift0, wkv0, conv0, lru0, re0, im0, reset_first, layers, norm_final):
    nb, seq, _ = x.shape
    xt = _time_major(x)
    outs = {k: [] for k in ("shift", "wkv", "conv", "lru", "re", "im")}
    v_first = None
    for l, (p, vres) in enumerate(layers):
        zrest, r, lw, k, v, na, bb, gate, h_last = _mix_in(xt, shift0[l], p, vres, v_first, nb)
        if l == 0:
            v_first = v
        ya, s_out = _wkv((r, lw, k, v, na, bb, gate), _pair_states(wkv0[l]), p, nb, seq)
        yb, conv_new, lru_new = _lru(zrest, _time_major(conv0[l]), lru0[l], p, reset_first, nb)
        yc, re_new, im_new = _s5(zrest, re0[l].reshape(nb, S5_WIDTH), im0[l].reshape(nb, S5_WIDTH), p, nb)
        x1, q = _mix_out(xt, ya, yb, yc, p)
        x2 = _attn(q, x1, mem_k, mem_v, l * kv_stride, p["mem_wo"], nb, seq)
        xt = _ffn(x2, p)
        outs["shift"].append(h_last)
        outs["wkv"].append(_unpair_states(s_out))
        outs["conv"].append(jnp.swapaxes(conv_new.reshape(CONV_W - 1, nb, LRU_DIM), 0, 1))
        outs["lru"].append(lru_new)
        outs["re"].append(re_new.reshape(nb, S5_GROUPS, S5_STATE))
        outs["im"].append(im_new.reshape(nb, S5_GROUPS, S5_STATE))
    y = _norm(xt, norm_final)
    y = jnp.swapaxes(y.reshape(seq, nb, D_MODEL), 0, 1)
    return (y,) + tuple(jnp.stack(outs[k]) for k in ("shift", "wkv", "conv", "lru", "re", "im"))


def kernel(x_prompt, x_sample, mem_prompt, state_shift, state_wkv, state_conv, state_lru, state_s5_re, state_s5_im, cache_mem_k, cache_mem_v, norm_mix, w_in, w_out, mu_rkv, mu_wag, mu_v, w0, w1, w2, a0, a1, a2, v0, v1, v2, g1, g2, k_k, k_a, r_k, ln_x_w, ln_x_b, conv_w, conv_b, lru_wa, lru_ba, lru_wi, lru_bi, lru_lambda, s5_a_re, s5_a_im, s5_log_dt, s5_b_re, s5_b_im, s5_c_re, s5_c_im, s5_d, s5_w_glu, s5_b_glu, norm_mem_q, norm_mem_kv, mem_wq, mem_wk, mem_wv, mem_wo, norm_ffn, ffn_w_up, ffn_w_down, norm_final):
    w = dict(norm_mix=norm_mix, w_in=w_in, w_out=w_out, mu_rkv=mu_rkv, mu_wag=mu_wag, mu_v=mu_v, w0=w0, w1=w1,
             w2=w2, a0=a0, a1=a1, a2=a2, v0=v0, v1=v1, v2=v2, g1=g1, g2=g2, k_k=k_k, k_a=k_a, r_k=r_k,
             ln_x_w=ln_x_w, ln_x_b=ln_x_b, conv_w=conv_w, conv_b=conv_b, lru_wa=lru_wa, lru_ba=lru_ba,
             lru_wi=lru_wi, lru_bi=lru_bi, lru_lambda=lru_lambda, s5_a_re=s5_a_re, s5_a_im=s5_a_im,
             s5_log_dt=s5_log_dt, s5_b_re=s5_b_re, s5_b_im=s5_b_im, s5_c_re=s5_c_re, s5_c_im=s5_c_im,
             s5_d=s5_d, s5_w_glu=s5_w_glu, s5_b_glu=s5_b_glu, norm_mem_q=norm_mem_q, mem_wq=mem_wq,
             mem_wo=mem_wo, norm_ffn=norm_ffn, ffn_w_up=ffn_w_up, ffn_w_down=ffn_w_down)
    depth = norm_mix.shape[0]
    layers = [_layer_params(w, l) for l in range(depth)]
    g_final = norm_final.reshape(1, D_MODEL)

    bp, n_mem, _ = mem_prompt.shape
    mem_flat = mem_prompt.reshape(bp * n_mem, D_MODEL)
    mem_kv = [_mem_project(mem_flat, norm_mem_kv[l].reshape(1, D_MODEL), mem_wk[l].astype(BF16),
                           mem_wv[l].astype(BF16)) for l in range(depth)]
    mem_k_p = jnp.stack([kv[0] for kv in mem_kv]).reshape(depth, bp, n_mem, MEM_HEADS, MEM_HEAD_DIM)
    mem_v_p = jnp.stack([kv[1] for kv in mem_kv]).reshape(depth, bp, n_mem, MEM_HEADS, MEM_HEAD_DIM)
    mem_k_b = jnp.stack([kv[2] for kv in mem_kv]).reshape(depth * bp, n_mem, D_MODEL)
    mem_v_b = jnp.stack([kv[3] for kv in mem_kv]).reshape(depth * bp, n_mem, D_MODEL)

    zeros = lambda *s: jnp.zeros((depth, bp) + s, F32)
    prompt = _run_group(
        x_prompt, mem_k_b, mem_v_b, bp,
        zeros(D_MODEL), zeros(RWKV_HEADS, RWKV_HEAD, RWKV_HEAD), zeros(CONV_W - 1, LRU_DIM), zeros(LRU_DIM),
        zeros(S5_GROUPS, S5_STATE), zeros(S5_GROUPS, S5_STATE), True, layers, g_final)
    bs = x_sample.shape[0]
    cache_shape = (depth * bs, n_mem, MEM_HEADS, MEM_HEAD_DIM)
    sample = _run_group(
        x_sample, cache_mem_k.reshape(cache_shape), cache_mem_v.reshape(cache_shape),
        bs, state_shift, state_wkv, state_conv, state_lru, state_s5_re, state_s5_im, False, layers, g_final)
    (y_p, sh_p, wkv_p, conv_p, lru_p, re_p, im_p) = prompt
    (y_s, sh_s, wkv_s, conv_s, lru_s, re_s, im_s) = sample
    return (y_p, y_s, sh_p, sh_s, wkv_p, wkv_s, conv_p, conv_s, lru_p, lru_s, re_p, re_s, im_p, im_s,
            mem_k_p, mem_v_p)
```

```python
import functools
import math

import jax
import jax.numpy as jnp
from jax import lax
from jax.experimental import pallas as pl
from jax.experimental.pallas import tpu as pltpu

F32 = jnp.float32
BF16 = jnp.bfloat16

D_MODEL = 1024
RWKV_DIM = 512
RWKV_HEAD = 64
RWKV_HEADS = 8
LANES = 128
LANE_BLOCKS = D_MODEL // LANES
HEAD_PAIR = 2 * RWKV_HEAD
N_PAIRS = RWKV_DIM // HEAD_PAIR
LRU_DIM = 256
LRU_BLOCKS = 4
CONV_W = 4
LRU_C = 8.0
S5_DIM = 256
S5_GROUP = 16
S5_GROUPS = 16
S5_STATE = 64
S5_WIDTH = S5_GROUPS * S5_STATE
R3 = 3 * RWKV_DIM
D_IN = R3 + 2 * LRU_DIM + S5_DIM
N_MEM = 256
MEM_HEADS = 4
MEM_HEAD_DIM = 256
RMS_EPS = 1e-6
GN_EPS = 64e-5
WKV_CHUNK = 64

V7X_VMEM_LIMIT = 56 * 1024 * 1024


def _params(*sem):
    return pltpu.CompilerParams(dimension_semantics=sem, vmem_limit_bytes=V7X_VMEM_LIMIT)


def _tile(m, mult, target):
    best = mult
    t = mult
    while t <= min(m, target):
        if m % t == 0:
            best = t
        t += mult
    assert m % best == 0, (m, mult, target)
    return best


def _const_spec(shape):
    nd = len(shape)
    return pl.BlockSpec(shape, lambda *_: (0,) * nd)


def _bdot(a, b):
    return jnp.dot(a.astype(BF16), b.astype(BF16), preferred_element_type=F32)


def _bdot_nt(a, b):
    return lax.dot_general(a.astype(BF16), b.astype(BF16), (((1,), (1,)), ((), ())),
                           preferred_element_type=F32)


def _bdot_tn(a, b):
    return lax.dot_general(a.astype(BF16), b.astype(BF16), (((0,), (0,)), ((), ())),
                           preferred_element_type=F32)


def _split2(x):
    hi = x.astype(BF16)
    lo = (x - hi.astype(F32)).astype(BF16)
    return hi, lo


def _dot_exact_rhs(x, w):
    n = x.shape[0]
    hi = x.astype(BF16).astype(F32)
    both = jnp.dot(jnp.concatenate([hi, x - hi], axis=0).astype(BF16), w, preferred_element_type=F32)
    return both[0:n, :] + both[n:2 * n, :]


def _run_lockstep(gens):
    results = {}
    while len(results) < len(gens):
        for i, gen in enumerate(gens):
            if i not in results:
                try:
                    next(gen)
                except StopIteration as stop:
                    results[i] = stop.value
    return [results[i] for i in range(len(gens))]


def _rms(x, g):
    return x * lax.rsqrt(jnp.mean(x * x, axis=-1, keepdims=True) + RMS_EPS) * g


def _softplus(x):
    return jnp.maximum(x, 0.0) + jnp.log1p(jnp.exp(-jnp.abs(x)))


def _sigmoid(x):
    return 1.0 / (1.0 + jnp.exp(-x))


def _gelu(x):
    c = math.sqrt(2.0 / math.pi)
    return x * (0.5 * (1.0 + jnp.tanh(c * (x + 0.044715 * (x * x * x)))))


def _mix_in_kernel(has_vres, nb, tm, *refs):
    if has_vres:
        (x_ref, first_ref, g_ref, win_ref, murkv_ref, muwag_ref, w0_ref, w1_ref, w2_ref,
         a0_ref, a1_ref, a2_ref, g1_ref, g2_ref, kk_ref, ka_ref, ones_ref,
         muv_ref, v0_ref, v1_ref, v2_ref, vfirst_ref,
         zrest_ref, r_ref, lw_ref, k_ref, v_ref, na_ref, bb_ref, gate_ref, hlast_ref,
         hbuf, zbuf) = refs
    else:
        (x_ref, first_ref, g_ref, win_ref, murkv_ref, muwag_ref, w0_ref, w1_ref, w2_ref,
         a0_ref, a1_ref, a2_ref, g1_ref, g2_ref, kk_ref, ka_ref, ones_ref,
         zrest_ref, r_ref, lw_ref, k_ref, v_ref, na_ref, bb_ref, gate_ref, hlast_ref,
         hbuf, zbuf) = refs

    @pl.when(pl.program_id(0) == 0)
    def _():
        first = first_ref[...]
        hbuf[0:nb, :] = first
        zbuf[0:nb, :] = _bdot(first, win_ref[:, 0:R3])

    h = _rms(x_ref[...], g_ref[...])
    hbuf[nb:nb + tm, :] = h
    hp = hbuf[0:tm, :]
    hbuf[0:nb, :] = hbuf[tm:tm + nb, :]
    hlast_ref[...] = h[tm - nb:tm, :]

    z = _bdot(h, win_ref[...])
    zrest_ref[...] = z[:, R3:D_IN]
    z_rkv = z[:, 0:R3]
    zbuf[nb:nb + tm, :] = z_rkv
    z_sh = zbuf[0:tm, :]
    zbuf[0:nb, :] = zbuf[tm:tm + nb, :]
    zmix = z_rkv + (z_sh - z_rkv) * murkv_ref[...]
    r = zmix[:, 0:RWKV_DIM]
    k = zmix[:, RWKV_DIM:2 * RWKV_DIM]
    v = zmix[:, 2 * RWKV_DIM:R3]

    dh = hp - h
    xw = h + dh * muwag_ref[0:1, :]
    xa = h + dh * muwag_ref[1:2, :]
    xg = h + dh * muwag_ref[2:3, :]
    w_pre = w0_ref[...] + _bdot(jnp.tanh(_bdot(xw, w1_ref[...])), w2_ref[...])
    w_log = -_softplus(-w_pre) - 0.5
    a = _sigmoid(a0_ref[...] + _bdot(_bdot(xa, a1_ref[...]), a2_ref[...]))
    gate = _bdot(_sigmoid(_bdot(xg, g1_ref[...])), g2_ref[...])
    if has_vres:
        xv = h + dh * muv_ref[...]
        mix = _sigmoid(v0_ref[...] + _bdot(_bdot(xv, v1_ref[...]), v2_ref[...]))
        v_first = jnp.concatenate([vfirst_ref[p] for p in range(N_PAIRS)], axis=-1)
        v = v + (v_first - v) * mix
    kk = k * kk_ref[...]
    ss = _dot_exact_rhs(kk * kk, ones_ref[...])
    kk = kk / jnp.maximum(jnp.sqrt(ss), 1e-12)
    outs = ((r_ref, r), (lw_ref, -jnp.exp(w_log)), (k_ref, k * (1.0 + (a - 1.0) * ka_ref[...])), (v_ref, v),
            (na_ref, -kk), (bb_ref, kk * a), (gate_ref, gate))
    for ref, val in outs:
        for p in range(N_PAIRS):
            ref[p] = val[:, p * HEAD_PAIR:(p + 1) * HEAD_PAIR]


def _mix_in(x, first, p, vres, v_first, nb):
    m = x.shape[0]
    tm = _tile(m, max(nb, 8), 256)
    has_vres = vres is not None
    row = lambda w: pl.BlockSpec((tm, w), lambda i: (i, 0))
    pair_row = pl.BlockSpec((N_PAIRS, tm, HEAD_PAIR), lambda i: (0, i, 0))
    ins = [x, first, p["norm_mix"], p["w_in"], p["mu_rkv"], p["mu_wag"], p["w0"], p["w1"], p["w2"],
           p["a0"], p["a1"], p["a2"], p["g1"], p["g2"], p["k_k"], p["k_a"], p["ones_head"]]
    specs = [row(D_MODEL)] + [_const_spec(a.shape) for a in ins[1:]]
    if has_vres:
        extra = [vres["mu_v"], vres["v0"], vres["v1"], vres["v2"]]
        ins += extra + [v_first]
        specs += [_const_spec(a.shape) for a in extra] + [pair_row]
    wide = jax.ShapeDtypeStruct((N_PAIRS, m, HEAD_PAIR), F32)
    outs = [jax.ShapeDtypeStruct((m, D_IN - R3), F32)] + [wide] * 7 + [jax.ShapeDtypeStruct((nb, D_MODEL), F32)]
    out_specs = [row(D_IN - R3)] + [pair_row] * 7 + [_const_spec((nb, D_MODEL))]
    return pl.pallas_call(
        functools.partial(_mix_in_kernel, has_vres, nb, tm),
        grid=(m // tm,),
        in_specs=specs,
        out_specs=out_specs,
        out_shape=outs,
        scratch_shapes=[pltpu.VMEM((tm + nb, D_MODEL), F32), pltpu.VMEM((tm + nb, R3), F32)],
        compiler_params=_params("arbitrary"),
        name="mix_in",
    )(*ins)


def _wkv_kernel(c, n_seq, group, stride, r_ref, lw_ref, k_ref, v_ref, na_ref, bb_ref, gate_ref, s0_ref,
                rk_ref, lnw_ref, lnb_ref, tri_ref, mean_ref,
                y_ref, sout_ref, s_scr):
    li = pl.program_id(1)

    @pl.when(li == 0)
    def _():
        zero = jnp.zeros((RWKV_HEAD, RWKV_HEAD), F32)
        for j in range(n_seq):
            for p in range(N_PAIRS):
                s_scr[j, p] = jnp.concatenate(
                    [jnp.concatenate([s0_ref[j, 2 * p], zero], axis=1),
                     jnp.concatenate([zero, s0_ref[j, 2 * p + 1]], axis=1)], axis=0)

    lane = lax.broadcasted_iota(jnp.int32, (1, HEAD_PAIR), 1)
    head_masks = (lane < RWKV_HEAD, lane >= RWKV_HEAD)
    trow = lax.broadcasted_iota(jnp.int32, (c, 2 * c), 0)
    tcol = lax.broadcasted_iota(jnp.int32, (c, 2 * c), 1)
    left_half = tcol < c
    scol = jnp.where(left_half, tcol, tcol - c)
    strict = scol < trow
    strict_left = strict & left_half
    strict_right = strict & (tcol >= c)
    incl = scol <= trow
    eye2 = (scol == trow).astype(F32)
    drow = lax.broadcasted_iota(jnp.int32, (2 * c, 2 * c), 0)
    dcol = lax.broadcasted_iota(jnp.int32, (2 * c, 2 * c), 1)
    diag_blocks = jnp.bitwise_xor(drow - c, dcol - c) >= 0
    brow = lax.broadcasted_iota(jnp.int32, (HEAD_PAIR, HEAD_PAIR), 0)
    bcol = lax.broadcasted_iota(jnp.int32, (HEAD_PAIR, HEAD_PAIR), 1)
    same_head = jnp.bitwise_xor(brow - RWKV_HEAD, bcol - RWKV_HEAD) >= 0
    n_double = max(int(math.log2(c)) - 1, 0)
    tri = tri_ref[...]
    zeros_c = jnp.zeros((c, HEAD_PAIR), F32)

    def pair_chunk(lw, r, k, v, na, bb, gate, sp, cols):
        lw_hi, lw_lo = _split2(lw)
        cl = jnp.dot(tri, lw_hi, preferred_element_type=F32) + jnp.dot(tri, lw_lo, preferred_element_type=F32)
        yield
        e_in = jnp.exp(cl)
        e_neg = jnp.exp(-cl)
        g_last = jnp.exp(jnp.sum(lw, axis=0, keepdims=True))
        at = na * jnp.exp(cl - lw)
        rt = r * e_in
        bt = bb * e_neg
        kt = k * e_neg
        ar = jnp.concatenate([at, rt], axis=0)
        bk = jnp.concatenate([bt, kt], axis=0)
        a_s = _bdot_nt(ar, sp)
        g0 = _bdot_nt(jnp.where(head_masks[0], ar, 0.0), bk)
        g1 = _bdot_nt(jnp.where(head_masks[1], ar, 0.0), jnp.concatenate([kt, bt], axis=0))
        yield
        ga0, ga1 = g0[0:c, :], g1[0:c, :]
        low = jnp.where(strict, jnp.where(left_half, ga0, ga1), 0.0)
        ak0 = _bdot(jnp.where(strict_right, ga0, 0.0), jnp.concatenate([zeros_c, v], axis=0))
        ak1 = _bdot(jnp.where(strict_left, ga1, 0.0), jnp.concatenate([v, zeros_c], axis=0))
        rhs = jnp.concatenate([jnp.where(head_masks[0], a_s[0:c, :] + ak0, 0.0),
                               jnp.where(head_masks[1], a_s[0:c, :] + ak1, 0.0)], axis=0)
        blockdiag = lambda x: jnp.where(diag_blocks, jnp.concatenate([x, x], axis=0), 0.0)
        t_inv = eye2 + low
        pw = low
        if n_double > 0:
            pw = _bdot(pw, blockdiag(pw))
            yield
        for level in range(n_double):
            if level == n_double - 1:
                t_inv = t_inv + _bdot(t_inv, blockdiag(pw))
            else:
                both = _bdot(jnp.concatenate([t_inv, pw], axis=0), blockdiag(pw))
                t_inv = t_inv + both[0:c, :]
                pw = both[c:2 * c, :]
            yield
        u = _bdot(t_inv, rhs)
        yield
        uv = jnp.concatenate([u, v], axis=0)
        y = a_s[c:2 * c, :] + jnp.where(
            head_masks[0], _bdot(jnp.where(incl, g0[c:2 * c, :], 0.0), uv),
            _bdot(jnp.where(incl, g1[c:2 * c, :], 0.0), jnp.concatenate([v, u], axis=0)))
        upd = _bdot_tn(uv, bk * g_last)
        s_new = sp * g_last + jnp.where(same_head, upd, 0.0)
        yield
        stats = _dot_exact_rhs(jnp.concatenate([y, r * k * rk_ref[:, cols]], axis=0), mean_ref[...])
        mu = stats[0:c, :]
        bonus = stats[c:2 * c, :] * float(RWKV_HEAD) * v
        yield
        d = y - mu
        var = _dot_exact_rhs(d * d, mean_ref[...])
        yield
        yn = d * lax.rsqrt(var + GN_EPS) * lnw_ref[:, cols] + lnb_ref[:, cols]
        return (yn + bonus) * gate, s_new

    first_seq = pl.program_id(0) * n_seq

    def seq_group(gi, carry):
        units = []
        for jj in range(group):
            j = gi * group + jj
            rows = pl.ds(first_seq + j, c, stride=stride)
            for p in range(N_PAIRS):
                cols = slice(p * HEAD_PAIR, (p + 1) * HEAD_PAIR)
                vals = [ref[p, rows, :] for ref in (lw_ref, r_ref, k_ref, v_ref, na_ref, bb_ref, gate_ref)]
                units.append((j, p, rows, pair_chunk(*vals, s_scr[j, p], cols)))
        results = _run_lockstep([gen for (_, _, _, gen) in units])
        for (j, p, rows, _), (y, s_new) in zip(units, results):
            y_ref[p, rows, :] = y
            s_scr[j, p] = s_new
        return carry

    lax.fori_loop(0, n_seq // group, seq_group, 0)

    @pl.when(li == pl.num_programs(1) - 1)
    def _():
        for j in range(n_seq):
            for p in range(N_PAIRS):
                sp = s_scr[j, p]
                sout_ref[j, 2 * p] = sp[0:RWKV_HEAD, 0:RWKV_HEAD]
                sout_ref[j, 2 * p + 1] = sp[RWKV_HEAD:HEAD_PAIR, RWKV_HEAD:HEAD_PAIR]


WKV_SEQ_PER_STEP = 8
WKV_SEQ_GROUP = 8


def _wkv(arrs, s0, p, nb, seq):
    c = min(WKV_CHUNK, seq)
    n_seq = nb if seq > c else min(nb, WKV_SEQ_PER_STEP)
    assert seq % c == 0 and c % 8 == 0 and nb % n_seq == 0 and n_seq % WKV_SEQ_GROUP == 0
    blk = pl.BlockSpec((N_PAIRS, c * nb, HEAD_PAIR), lambda g, l: (0, l, 0))
    tri = jnp.tril(jnp.ones((c, c), F32)).astype(BF16)
    st_spec = pl.BlockSpec((n_seq, RWKV_HEADS, RWKV_HEAD, RWKV_HEAD), lambda g, l: (g, 0, 0, 0))
    consts = [p["r_k"], p["ln_x_w"], p["ln_x_b"], tri, p["mean_pair"]]
    return pl.pallas_call(
        functools.partial(_wkv_kernel, c, n_seq, WKV_SEQ_GROUP, nb),
        grid=(nb // n_seq, seq // c),
        in_specs=[blk] * 7 + [st_spec] + [_const_spec(a.shape) for a in consts],
        out_specs=[blk, st_spec],
        out_shape=[jax.ShapeDtypeStruct((N_PAIRS, seq * nb, HEAD_PAIR), F32), jax.ShapeDtypeStruct(s0.shape, F32)],
        scratch_shapes=[pltpu.VMEM((n_seq, N_PAIRS, HEAD_PAIR, HEAD_PAIR), F32)],
        compiler_params=_params("arbitrary", "arbitrary"),
        name="wkv",
    )(*arrs, s0, *consts)


def _lru_kernel(reset_first, nb, tm, zx_ref, zg_ref, conv0_ref, h0_ref, cw_ref, cb_ref, wa_ref, ba_ref,
                wi_ref, bi_ref, lam_ref, y_ref, conv_out_ref, h_out_ref, xbuf, abuf, bbuf, hcar):
    i = pl.program_id(0)
    hist = (CONV_W - 1) * nb

    @pl.when(i == 0)
    def _():
        xbuf[0:hist, :] = conv0_ref[...]
        hcar[...] = h0_ref[...]

    xbuf[hist:hist + tm, :] = zx_ref[...]
    xc = cb_ref[...]
    for j in range(CONV_W):
        xc = xc + xbuf[j * nb:j * nb + tm, :] * cw_ref[j:j + 1, :]
    conv_out_ref[...] = xbuf[tm:tm + hist, :]
    xbuf[0:hist, :] = xbuf[tm:tm + hist, :]

    gate_a = _sigmoid(_bdot(xc, wa_ref[...]) + ba_ref[...])
    gate_i = _sigmoid(_bdot(xc, wi_ref[...]) + bi_ref[...])
    log_a = -LRU_C * gate_a * _softplus(-lam_ref[...])
    a_sq = jnp.exp(2.0 * log_a)
    mult = jnp.sqrt(-jnp.tanh(log_a) * (a_sq + 1.0))
    if reset_first:
        row = lax.broadcasted_iota(jnp.int32, (tm, 1), 0)
        mult = jnp.where(row < jnp.where(i == 0, nb, 0), 1.0, mult)
    abuf[...] = jnp.exp(log_a)
    bbuf[...] = mult * gate_i * xc

    def step(t, h):
        rows = pl.ds(pl.multiple_of(t * nb, nb), nb)
        h = abuf[rows, :] * h + bbuf[rows, :]
        bbuf[rows, :] = h
        return h

    h = lax.fori_loop(0, tm // nb, step, hcar[...])
    hcar[...] = h
    h_out_ref[...] = h
    y_ref[...] = bbuf[...] * _gelu(zg_ref[...])


def _lru(zrest, conv0, h0, p, reset_first, nb):
    m = zrest.shape[0]
    tm = _tile(m, max(nb, 8), 512)
    hist = (CONV_W - 1) * nb
    consts = [conv0, h0, p["conv_w"], p["conv_b"], p["lru_wa"], p["lru_ba"], p["lru_wi"], p["lru_bi"],
              p["lru_lambda"]]
    return pl.pallas_call(
        functools.partial(_lru_kernel, reset_first, nb, tm),
        grid=(m // tm,),
        in_specs=[pl.BlockSpec((tm, LRU_DIM), lambda i: (i, 0)),
                  pl.BlockSpec((tm, LRU_DIM), lambda i: (i, 1))] + [_const_spec(a.shape) for a in consts],
        out_specs=[pl.BlockSpec((tm, LRU_DIM), lambda i: (i, 0)), _const_spec((hist, LRU_DIM)),
                   _const_spec((nb, LRU_DIM))],
        out_shape=[jax.ShapeDtypeStruct((m, LRU_DIM), F32), jax.ShapeDtypeStruct((hist, LRU_DIM), F32),
                   jax.ShapeDtypeStruct((nb, LRU_DIM), F32)],
        scratch_shapes=[pltpu.VMEM((tm + hist, LRU_DIM), F32), pltpu.VMEM((tm, LRU_DIM), F32),
                        pltpu.VMEM((tm, LRU_DIM), F32), pltpu.VMEM((nb, LRU_DIM), F32)],
        compiler_params=_params("arbitrary"),
        name="rg_lru",
    )(zrest, zrest, *consts)


def _s5_kernel(nb, tm, u_ref, re0_ref, im0_ref, are_ref, aim_ref, bre_ref, bim_ref, cre_ref, cim_ref,
               d_ref, wglu_ref, bglu_ref, y_ref, re_out_ref, im_out_ref, xre, xim, car_re, car_im):
    @pl.when(pl.program_id(0) == 0)
    def _():
        car_re[...] = re0_ref[...]
        car_im[...] = im0_ref[...]

    u = u_ref[...]
    xre[...] = _bdot(u, bre_ref[...])
    xim[...] = _bdot(u, bim_ref[...])
    a_re = jnp.broadcast_to(are_ref[...], (nb, S5_WIDTH))
    a_im = jnp.broadcast_to(aim_ref[...], (nb, S5_WIDTH))

    def step(t, carry):
        s_re, s_im = carry
        rows = pl.ds(pl.multiple_of(t * nb, nb), nb)
        n_re = a_re * s_re - a_im * s_im + xre[rows, :]
        n_im = a_re * s_im + a_im * s_re + xim[rows, :]
        xre[rows, :] = n_re
        xim[rows, :] = n_im
        return n_re, n_im

    s_re, s_im = lax.fori_loop(0, tm // nb, step, (car_re[...], car_im[...]))
    car_re[...] = s_re
    car_im[...] = s_im
    re_out_ref[...] = s_re
    im_out_ref[...] = s_im

    y = _bdot(xre[...], cre_ref[...]) - _bdot(xim[...], cim_ref[...]) + d_ref[...] * u
    o = _bdot(_gelu(y), wglu_ref[...]) + bglu_ref[...]
    y_ref[...] = o[:, 0:S5_DIM] * _sigmoid(o[:, S5_DIM:2 * S5_DIM])


def _s5(zrest, re0, im0, p, nb):
    m = zrest.shape[0]
    tm = _tile(m, max(nb, 8), 512)
    consts = [re0, im0, p["s5_abar_re"], p["s5_abar_im"], p["s5_bbar_re"], p["s5_bbar_im"],
              p["s5_c_re"], p["s5_c_im"], p["s5_d"], p["s5_w_glu"], p["s5_b_glu"]]
    st = jax.ShapeDtypeStruct((nb, S5_WIDTH), F32)
    return pl.pallas_call(
        functools.partial(_s5_kernel, nb, tm),
        grid=(m // tm,),
        in_specs=[pl.BlockSpec((tm, S5_DIM), lambda i: (i, 2))] + [_const_spec(a.shape) for a in consts],
        out_specs=[pl.BlockSpec((tm, S5_DIM), lambda i: (i, 0)), _const_spec((nb, S5_WIDTH)),
                   _const_spec((nb, S5_WIDTH))],
        out_shape=[jax.ShapeDtypeStruct((m, S5_DIM), F32), st, st],
        scratch_shapes=[pltpu.VMEM((tm, S5_WIDTH), F32), pltpu.VMEM((tm, S5_WIDTH), F32),
                        pltpu.VMEM((nb, S5_WIDTH), F32), pltpu.VMEM((nb, S5_WIDTH), F32)],
        compiler_params=_params("arbitrary"),
        name="s5",
    )(zrest, *consts)


def _mix_out_kernel(x_ref, ya_ref, yb_ref, yc_ref, wa_ref, wb_ref, wc_ref, gq_ref, wq_ref, x1_ref, q_ref):
    ya = jnp.concatenate([ya_ref[p] for p in range(N_PAIRS)], axis=-1)
    x1 = (x_ref[...] + _bdot(ya, wa_ref[...]) + _bdot(yb_ref[...], wb_ref[...])
          + _bdot(yc_ref[...], wc_ref[...]))
    q = _bdot(_rms(x1, gq_ref[...]), wq_ref[...])
    for cb in range(LANE_BLOCKS):
        cols = slice(cb * LANES, (cb + 1) * LANES)
        x1_ref[cb] = x1[:, cols]
        q_ref[cb] = q[:, cols]


def _mix_out(x, ya, yb, yc, p):
    m = x.shape[0]
    tm = _tile(m, 8, 512)
    row = lambda w: pl.BlockSpec((tm, w), lambda i: (i, 0))
    split = pl.BlockSpec((LANE_BLOCKS, tm, LANES), lambda i: (0, i, 0))
    consts = [p["w_out_a"], p["w_out_b"], p["w_out_c"], p["norm_mem_q"], p["mem_wq"]]
    out = jax.ShapeDtypeStruct((LANE_BLOCKS, m, LANES), F32)
    return pl.pallas_call(
        _mix_out_kernel,
        grid=(m // tm,),
        in_specs=[row(D_MODEL), pl.BlockSpec((N_PAIRS, tm, HEAD_PAIR), lambda i: (0, i, 0)), row(LRU_DIM),
                  row(S5_DIM)] + [_const_spec(a.shape) for a in consts],
        out_specs=[split, split],
        out_shape=[out, out],
        compiler_params=_params("parallel"),
        name="mix_out",
    )(x, ya, yb, yc, *consts)


def _load_seq_rows(ref, rows):
    return jnp.concatenate([ref[cb, rows, :] for cb in range(LANE_BLOCKS)], axis=-1)


def _store_seq_rows(ref, rows, val):
    for cb in range(LANE_BLOCKS):
        ref[cb, rows, :] = val[:, cb * LANES:(cb + 1) * LANES]


def _softmax_rows(s):
    e = jnp.exp(s - jnp.max(s, axis=-1, keepdims=True))
    return e / jnp.sum(e, axis=-1, keepdims=True)


def _attn_kernel(tl, n_seq, stride, q_ref, x_ref, mk_ref, mv_ref, wo_ref, o_ref):
    scale = MEM_HEAD_DIM ** -0.5
    first_seq = pl.program_id(0) * n_seq
    head_cols = [slice(hh * MEM_HEAD_DIM, (hh + 1) * MEM_HEAD_DIM) for hh in range(MEM_HEADS)]

    def one_seq(j):
        rows = pl.ds(first_seq + j, tl, stride=stride)
        q = _load_seq_rows(q_ref, rows)
        scores = [_bdot_nt(q[:, cols], mk_ref[j, :, cols]) * scale for cols in head_cols]
        yield
        heads = [_bdot(_softmax_rows(s), mv_ref[j, :, cols]) for s, cols in zip(scores, head_cols)]
        yield
        out = _load_seq_rows(x_ref, rows) + _bdot(jnp.concatenate(heads, axis=-1), wo_ref[...])
        yield
        _store_seq_rows(o_ref, rows, out)

    for j0 in range(0, n_seq, ATTN_LOCKSTEP):
        _run_lockstep([one_seq(j) for j in range(j0, min(j0 + ATTN_LOCKSTEP, n_seq))])


def _attn_cache_kernel(tl, n_seq, stride, q_ref, x_ref, mk_ref, mv_ref, wo_ref, o_ref):
    scale = MEM_HEAD_DIM ** -0.5
    first_seq = pl.program_id(0) * n_seq
    qrow = lax.broadcasted_iota(jnp.int32, (MEM_HEADS * tl, N_MEM * MEM_HEADS), 0)
    kcol = lax.broadcasted_iota(jnp.int32, (MEM_HEADS * tl, N_MEM * MEM_HEADS), 1)
    q_head = jnp.zeros_like(qrow)
    for hh in range(1, MEM_HEADS):
        q_head = q_head + (qrow >= hh * tl).astype(jnp.int32)
    same_head = q_head == jnp.bitwise_and(kcol, MEM_HEADS - 1)

    def one_seq(j):
        rows = pl.ds(first_seq + j, tl, stride=stride)
        q = _load_seq_rows(q_ref, rows)
        qh = jnp.concatenate([q[:, hh * MEM_HEAD_DIM:(hh + 1) * MEM_HEAD_DIM] for hh in range(MEM_HEADS)], axis=0)
        k2 = mk_ref[j].reshape(N_MEM * MEM_HEADS, MEM_HEAD_DIM)
        s = jnp.where(same_head, _bdot_nt(qh, k2) * scale, MASKED_SCORE)
        yield
        v2 = mv_ref[j].reshape(N_MEM * MEM_HEADS, MEM_HEAD_DIM)
        oh = _bdot(_softmax_rows(s), v2)
        yield
        o = jnp.concatenate([oh[hh * tl:(hh + 1) * tl, :] for hh in range(MEM_HEADS)], axis=-1)
        out = _load_seq_rows(x_ref, rows) + _bdot(o, wo_ref[...])
        yield
        _store_seq_rows(o_ref, rows, out)

    _run_lockstep([one_seq(j) for j in range(n_seq)])


MASKED_SCORE = -1e30
ATTN_SEQ_PER_STEP = 4
ATTN_ROWS = 128
ATTN_LOCKSTEP = 2


def _attn(q, x1, mk, mv, kv_offset, wo, nb, seq):
    tl = _tile(seq, 8, ATTN_ROWS)
    n_seq = nb if seq > tl else min(nb, ATTN_SEQ_PER_STEP)
    assert nb % n_seq == 0 and kv_offset % n_seq == 0 and MEM_HEADS & (MEM_HEADS - 1) == 0
    blk = pl.BlockSpec((LANE_BLOCKS, tl * nb, LANES), lambda g, l: (0, l, 0))
    kv_index = lambda g, l: (kv_offset // n_seq + g,) + (0,) * (mk.ndim - 1)
    kv = pl.BlockSpec((n_seq,) + mk.shape[1:], kv_index)
    body = _attn_kernel if mk.ndim == 3 else _attn_cache_kernel
    return pl.pallas_call(
        functools.partial(body, tl, n_seq, nb),
        grid=(nb // n_seq, seq // tl),
        in_specs=[blk, blk, kv, kv, _const_spec(wo.shape)],
        out_specs=blk,
        out_shape=jax.ShapeDtypeStruct((LANE_BLOCKS, seq * nb, LANES), F32),
        compiler_params=_params("arbitrary", "arbitrary"),
        name="mem_attn",
    )(q, x1, mk, mv, wo)


def _ffn_kernel(d_ff, final_norm, x_ref, g_ref, wup_ref, wdown_ref, *rest):
    o_ref = rest[-1]
    x = jnp.concatenate([x_ref[cb] for cb in range(LANE_BLOCKS)], axis=-1)
    h = _rms(x, g_ref[...]).astype(BF16)
    gt = jnp.dot(h, wup_ref[:, 0:d_ff], preferred_element_type=F32)
    up = jnp.dot(h, wup_ref[:, d_ff:2 * d_ff], preferred_element_type=F32)
    act = gt * _sigmoid(gt) * up
    out = x + _bdot(act, wdown_ref[...])
    o_ref[...] = _rms(out, rest[0][...]) if final_norm else out


def _ffn(x, p, g_final=None):
    m = x.shape[1]
    tm = _tile(m, 8, 256)
    d_ff = p["ffn_w_down"].shape[0]
    row = pl.BlockSpec((tm, D_MODEL), lambda i: (i, 0))
    consts = [p["norm_ffn"], p["ffn_w_up"], p["ffn_w_down"]] + ([] if g_final is None else [g_final])
    return pl.pallas_call(
        functools.partial(_ffn_kernel, d_ff, g_final is not None),
        grid=(m // tm,),
        in_specs=[pl.BlockSpec((LANE_BLOCKS, tm, LANES), lambda i: (0, i, 0))] + [_const_spec(a.shape) for a in consts],
        out_specs=row,
        out_shape=jax.ShapeDtypeStruct((m, D_MODEL), F32),
        compiler_params=_params("parallel"),
        name="ffn",
    )(x, *consts)


def _mem_project_kernel(x_ref, g_ref, wk_ref, wv_ref, k_ref, v_ref, kb_ref, vb_ref):
    m = _rms(x_ref[...], g_ref[...]).astype(BF16)
    k = jnp.dot(m, wk_ref[...], preferred_element_type=F32)
    v = jnp.dot(m, wv_ref[...], preferred_element_type=F32)
    k_ref[...] = k
    v_ref[...] = v
    kb_ref[...] = k.astype(BF16)
    vb_ref[...] = v.astype(BF16)


def _mem_project(mem, g, wk, wv):
    m = mem.shape[0]
    tm = _tile(m, 16, 512)
    row = pl.BlockSpec((tm, D_MODEL), lambda i: (i, 0))
    out = jax.ShapeDtypeStruct((m, D_MODEL), F32)
    outb = jax.ShapeDtypeStruct((m, D_MODEL), BF16)
    return pl.pallas_call(
        _mem_project_kernel, grid=(m // tm,),
        in_specs=[row, _const_spec(g.shape), _const_spec(wk.shape), _const_spec(wv.shape)],
        out_specs=[row] * 4, out_shape=[out, out, outb, outb], compiler_params=_params("parallel"),
        name="mem_project",
    )(mem, g, wk, wv)


def _block_diag(blocks):
    n, r, c = blocks.shape
    eye = jnp.eye(n, dtype=blocks.dtype)
    return (eye[:, None, :, None] * blocks[:, :, None, :]).reshape(n * r, n * c)


def _layer_params(w, l):
    row = lambda a: a.reshape(1, -1).astype(F32)
    bf = lambda a: a.astype(BF16)
    ones64 = jnp.ones((RWKV_HEADS, RWKV_HEAD, RWKV_HEAD), F32)
    p = {
        "norm_mix": row(w["norm_mix"][l]), "w_in": bf(w["w_in"][l]), "mu_rkv": row(w["mu_rkv"][l]),
        "mu_wag": w["mu_wag"][l], "w0": row(w["w0"][l]), "w1": bf(w["w1"][l]), "w2": bf(w["w2"][l]),
        "a0": row(w["a0"][l]), "a1": bf(w["a1"][l]), "a2": bf(w["a2"][l]),
        "g1": bf(w["g1"][l]), "g2": bf(w["g2"][l]), "k_k": row(w["k_k"][l]), "k_a": row(w["k_a"][l]),
        "ones_head": bf(_block_diag(ones64)),
        "mean_pair": bf(_block_diag(ones64[:2]) / RWKV_HEAD),
        "r_k": row(w["r_k"][l]), "ln_x_w": row(w["ln_x_w"][l]), "ln_x_b": row(w["ln_x_b"][l]),
        "conv_w": w["conv_w"][l], "conv_b": row(w["conv_b"][l]),
        "lru_wa": bf(_block_diag(w["lru_wa"][l])), "lru_ba": row(w["lru_ba"][l]),
        "lru_wi": bf(_block_diag(w["lru_wi"][l])), "lru_bi": row(w["lru_bi"][l]),
        "lru_lambda": row(w["lru_lambda"][l]),
        "s5_d": row(w["s5_d"][l]), "s5_w_glu": bf(w["s5_w_glu"][l]), "s5_b_glu": row(w["s5_b_glu"][l]),
        "w_out_a": bf(w["w_out"][l][0:RWKV_DIM]),
        "w_out_b": bf(w["w_out"][l][RWKV_DIM:RWKV_DIM + LRU_DIM]),
        "w_out_c": bf(w["w_out"][l][RWKV_DIM + LRU_DIM:]),
        "norm_mem_q": row(w["norm_mem_q"][l]), "mem_wq": bf(w["mem_wq"][l]), "mem_wo": bf(w["mem_wo"][l]),
        "norm_ffn": row(w["norm_ffn"][l]), "ffn_w_up": bf(w["ffn_w_up"][l]), "ffn_w_down": bf(w["ffn_w_down"][l]),
    }
    lam_re, lam_im = w["s5_a_re"][l].astype(F32), w["s5_a_im"][l].astype(F32)
    dt = jnp.exp(w["s5_log_dt"][l].astype(F32))[:, None]
    mag = jnp.exp(lam_re * dt)
    abar_re, abar_im = mag * jnp.cos(lam_im * dt), mag * jnp.sin(lam_im * dt)
    den = lam_re * lam_re + lam_im * lam_im
    q_re = ((abar_re - 1.0) * lam_re + abar_im * lam_im) / den
    q_im = (abar_im * lam_re - (abar_re - 1.0) * lam_im) / den
    b_re, b_im = w["s5_b_re"][l].astype(F32), w["s5_b_im"][l].astype(F32)
    bbar_re = q_re[..., None] * b_re - q_im[..., None] * b_im
    bbar_im = q_re[..., None] * b_im + q_im[..., None] * b_re
    p["s5_abar_re"] = abar_re.reshape(1, S5_WIDTH)
    p["s5_abar_im"] = abar_im.reshape(1, S5_WIDTH)
    p["s5_bbar_re"] = bf(_block_diag(jnp.swapaxes(bbar_re, 1, 2)))
    p["s5_bbar_im"] = bf(_block_diag(jnp.swapaxes(bbar_im, 1, 2)))
    p["s5_c_re"] = bf(_block_diag(jnp.swapaxes(w["s5_c_re"][l], 1, 2)))
    p["s5_c_im"] = bf(_block_diag(jnp.swapaxes(w["s5_c_im"][l], 1, 2)))
    vres = None
    if l > 0:
        vres = {"mu_v": row(w["mu_v"][l - 1]), "v0": row(w["v0"][l - 1]),
                "v1": bf(w["v1"][l - 1]), "v2": bf(w["v2"][l - 1])}
    return p, vres


def _time_major(a):
    return jnp.swapaxes(a, 0, 1).reshape((a.shape[0] * a.shape[1],) + a.shape[2:])


def _run_group(x, mem_k, mem_v, kv_stride, shift0, wkv0, conv0, lru0, re0, im0, reset_first, layers, norm_final):
    nb, seq, _ = x.shape
    xt = _time_major(x)
    outs = {k: [] for k in ("shift", "wkv", "conv", "lru", "re", "im")}
    v_first = None
    for l, (p, vres) in enumerate(layers):
        zrest, r, lw, k, v, na, bb, gate, h_last = _mix_in(xt, shift0[l], p, vres, v_first, nb)
        if l == 0:
            v_first = v
        ya, s_out = _wkv((r, lw, k, v, na, bb, gate), wkv0[l], p, nb, seq)
        yb, conv_new, lru_new = _lru(zrest, _time_major(conv0[l]), lru0[l], p, reset_first, nb)
        yc, re_new, im_new = _s5(zrest, re0[l].reshape(nb, S5_WIDTH), im0[l].reshape(nb, S5_WIDTH), p, nb)
        x1, q = _mix_out(xt, ya, yb, yc, p)
        x2 = _attn(q, x1, mem_k, mem_v, l * kv_stride, p["mem_wo"], nb, seq)
        xt = _ffn(x2, p, norm_final if l == len(layers) - 1 else None)
        outs["shift"].append(h_last)
        outs["wkv"].append(s_out)
        outs["conv"].append(jnp.swapaxes(conv_new.reshape(CONV_W - 1, nb, LRU_DIM), 0, 1))
        outs["lru"].append(lru_new)
        outs["re"].append(re_new.reshape(nb, S5_GROUPS, S5_STATE))
        outs["im"].append(im_new.reshape(nb, S5_GROUPS, S5_STATE))
    y = jnp.swapaxes(xt.reshape(seq, nb, D_MODEL), 0, 1)
    return (y,) + tuple(jnp.stack(outs[k]) for k in ("shift", "wkv", "conv", "lru", "re", "im"))


def kernel(x_prompt, x_sample, mem_prompt, state_shift, state_wkv, state_conv, state_lru, state_s5_re, state_s5_im, cache_mem_k, cache_mem_v, norm_mix, w_in, w_out, mu_rkv, mu_wag, mu_v, w0, w1, w2, a0, a1, a2, v0, v1, v2, g1, g2, k_k, k_a, r_k, ln_x_w, ln_x_b, conv_w, conv_b, lru_wa, lru_ba, lru_wi, lru_bi, lru_lambda, s5_a_re, s5_a_im, s5_log_dt, s5_b_re, s5_b_im, s5_c_re, s5_c_im, s5_d, s5_w_glu, s5_b_glu, norm_mem_q, norm_mem_kv, mem_wq, mem_wk, mem_wv, mem_wo, norm_ffn, ffn_w_up, ffn_w_down, norm_final):
    w = dict(norm_mix=norm_mix, w_in=w_in, w_out=w_out, mu_rkv=mu_rkv, mu_wag=mu_wag, mu_v=mu_v, w0=w0, w1=w1,
             w2=w2, a0=a0, a1=a1, a2=a2, v0=v0, v1=v1, v2=v2, g1=g1, g2=g2, k_k=k_k, k_a=k_a, r_k=r_k,
             ln_x_w=ln_x_w, ln_x_b=ln_x_b, conv_w=conv_w, conv_b=conv_b, lru_wa=lru_wa, lru_ba=lru_ba,
             lru_wi=lru_wi, lru_bi=lru_bi, lru_lambda=lru_lambda, s5_a_re=s5_a_re, s5_a_im=s5_a_im,
             s5_log_dt=s5_log_dt, s5_b_re=s5_b_re, s5_b_im=s5_b_im, s5_c_re=s5_c_re, s5_c_im=s5_c_im,
             s5_d=s5_d, s5_w_glu=s5_w_glu, s5_b_glu=s5_b_glu, norm_mem_q=norm_mem_q, mem_wq=mem_wq,
             mem_wo=mem_wo, norm_ffn=norm_ffn, ffn_w_up=ffn_w_up, ffn_w_down=ffn_w_down)
    depth = norm_mix.shape[0]
    layers = [_layer_params(w, l) for l in range(depth)]
    g_final = norm_final.reshape(1, D_MODEL)

    bp, n_mem, _ = mem_prompt.shape
    mem_flat = mem_prompt.reshape(bp * n_mem, D_MODEL)
    mem_kv = [_mem_project(mem_flat, norm_mem_kv[l].reshape(1, D_MODEL), mem_wk[l].astype(BF16),
                           mem_wv[l].astype(BF16)) for l in range(depth)]
    mem_k_p = jnp.stack([kv[0] for kv in mem_kv]).reshape(depth, bp, n_mem, MEM_HEADS, MEM_HEAD_DIM)
    mem_v_p = jnp.stack([kv[1] for kv in mem_kv]).reshape(depth, bp, n_mem, MEM_HEADS, MEM_HEAD_DIM)
    mem_k_b = jnp.stack([kv[2] for kv in mem_kv]).reshape(depth * bp, n_mem, D_MODEL)
    mem_v_b = jnp.stack([kv[3] for kv in mem_kv]).reshape(depth * bp, n_mem, D_MODEL)

    zeros = lambda *s: jnp.zeros((depth, bp) + s, F32)
    prompt = _run_group(
        x_prompt, mem_k_b, mem_v_b, bp,
        zeros(D_MODEL), zeros(RWKV_HEADS, RWKV_HEAD, RWKV_HEAD), zeros(CONV_W - 1, LRU_DIM), zeros(LRU_DIM),
        zeros(S5_GROUPS, S5_STATE), zeros(S5_GROUPS, S5_STATE), True, layers, g_final)
    bs = x_sample.shape[0]
    cache_shape = (depth * bs, n_mem, MEM_HEADS, MEM_HEAD_DIM)
    sample = _run_group(
        x_sample, cache_mem_k.reshape(cache_shape), cache_mem_v.reshape(cache_shape),
        bs, state_shift, state_wkv, state_conv, state_lru, state_s5_re, state_s5_im, False, layers, g_final)
    (y_p, sh_p, wkv_p, conv_p, lru_p, re_p, im_p) = prompt
    (y_s, sh_s, wkv_s, conv_s, lru_s, re_s, im_s) = sample
    return (y_p, y_s, sh_p, sh_s, wkv_p, wkv_s, conv_p, conv_s, lru_p, lru_s, re_p, re_s, im_p, im_s,
            mem_k_p, mem_v_p)
```

```python
import functools
import math

import jax
import jax.numpy as jnp
from jax import lax
from jax.experimental import pallas as pl
from jax.experimental.pallas import tpu as pltpu

F32 = jnp.float32
BF16 = jnp.bfloat16

D_MODEL = 1024
RWKV_DIM = 512
RWKV_HEAD = 64
RWKV_HEADS = 8
LANES = 128
LANE_BLOCKS = D_MODEL // LANES
HEAD_PAIR = 2 * RWKV_HEAD
N_PAIRS = RWKV_DIM // HEAD_PAIR
LRU_DIM = 256
LRU_BLOCKS = 4
CONV_W = 4
LRU_C = 8.0
S5_DIM = 256
S5_GROUP = 16
S5_GROUPS = 16
S5_STATE = 64
S5_WIDTH = S5_GROUPS * S5_STATE
R3 = 3 * RWKV_DIM
D_IN = R3 + 2 * LRU_DIM + S5_DIM
N_MEM = 256
MEM_HEADS = 4
MEM_HEAD_DIM = 256
RMS_EPS = 1e-6
GN_EPS = 64e-5
WKV_CHUNK = 64

V7X_VMEM_LIMIT = 56 * 1024 * 1024


def _params(*sem):
    return pltpu.CompilerParams(dimension_semantics=sem, vmem_limit_bytes=V7X_VMEM_LIMIT)


def _tile(m, mult, target):
    best = mult
    t = mult
    while t <= min(m, target):
        if m % t == 0:
            best = t
        t += mult
    assert m % best == 0, (m, mult, target)
    return best


def _const_spec(shape):
    nd = len(shape)
    return pl.BlockSpec(shape, lambda *_: (0,) * nd)


def _bdot(a, b):
    return jnp.dot(a.astype(BF16), b.astype(BF16), preferred_element_type=F32)


def _bdot_nt(a, b):
    return lax.dot_general(a.astype(BF16), b.astype(BF16), (((1,), (1,)), ((), ())),
                           preferred_element_type=F32)


def _bdot_tn(a, b):
    return lax.dot_general(a.astype(BF16), b.astype(BF16), (((0,), (0,)), ((), ())),
                           preferred_element_type=F32)


def _split2(x):
    hi = x.astype(BF16)
    lo = (x - hi.astype(F32)).astype(BF16)
    return hi, lo


def _dot_exact_rhs(x, w):
    n = x.shape[0]
    hi = x.astype(BF16).astype(F32)
    both = jnp.dot(jnp.concatenate([hi, x - hi], axis=0).astype(BF16), w, preferred_element_type=F32)
    return both[0:n, :] + both[n:2 * n, :]


def _run_lockstep(gens):
    results = {}
    while len(results) < len(gens):
        for i, gen in enumerate(gens):
            if i not in results:
                try:
                    next(gen)
                except StopIteration as stop:
                    results[i] = stop.value
    return [results[i] for i in range(len(gens))]


def _rms(x, g):
    return x * lax.rsqrt(jnp.mean(x * x, axis=-1, keepdims=True) + RMS_EPS) * g


def _softplus(x):
    return jnp.maximum(x, 0.0) + jnp.log1p(jnp.exp(-jnp.abs(x)))


def _sigmoid(x):
    return 1.0 / (1.0 + jnp.exp(-x))


def _gelu(x):
    c = math.sqrt(2.0 / math.pi)
    return x * (0.5 * (1.0 + jnp.tanh(c * (x + 0.044715 * (x * x * x)))))


def _mix_in_kernel(has_vres, seq_major_x, nb, tm, *refs):
    refs = list(refs)
    (x_ref, first_ref, g_ref, win_ref, murkv_ref, muwag_ref, w0_ref, w1_ref, w2_ref,
     a0_ref, a1_ref, a2_ref, g1_ref, g2_ref, kk_ref, ka_ref, ones_ref) = refs[:17]
    del refs[:17]
    if has_vres:
        muv_ref, v0_ref, v1_ref, v2_ref, vfirst_ref = refs[:5]
        del refs[:5]
    if seq_major_x:
        xt_ref = refs.pop(0)
    zrest_ref, r_ref, lw_ref, k_ref, v_ref, na_ref, bb_ref, gate_ref, hlast_ref, hbuf, zbuf = refs[:11]

    @pl.when(pl.program_id(0) == 0)
    def _():
        first = first_ref[...]
        hbuf[0:nb, :] = first
        zbuf[0:nb, :] = _bdot(first, win_ref[:, 0:R3])

    if seq_major_x:
        xs = refs[-1]
        for b in range(nb):
            _store_seq_rows(xs, pl.ds(b, tm // nb, stride=nb), x_ref[b])
        x = jnp.concatenate([xs[cb] for cb in range(LANE_BLOCKS)], axis=-1)
        xt_ref[...] = x
    else:
        x = x_ref[...]
    h = _rms(x, g_ref[...])
    hbuf[nb:nb + tm, :] = h
    hp = hbuf[0:tm, :]
    hbuf[0:nb, :] = hbuf[tm:tm + nb, :]
    hlast_ref[...] = h[tm - nb:tm, :]
    dh = hp - h
    xw = h + dh * muwag_ref[0:1, :]
    xa = h + dh * muwag_ref[1:2, :]
    xg = h + dh * muwag_ref[2:3, :]

    z = _bdot(h, win_ref[...])
    p_w = _bdot(xw, w1_ref[...])
    p_a = _bdot(xa, a1_ref[...])
    p_g = _bdot(xg, g1_ref[...])
    if has_vres:
        p_v = _bdot(h + dh * muv_ref[...], v1_ref[...])

    zrest_ref[...] = z[:, R3:D_IN]
    z_rkv = z[:, 0:R3]
    zbuf[nb:nb + tm, :] = z_rkv
    z_sh = zbuf[0:tm, :]
    zbuf[0:nb, :] = zbuf[tm:tm + nb, :]
    zmix = z_rkv + (z_sh - z_rkv) * murkv_ref[...]
    r = zmix[:, 0:RWKV_DIM]
    k = zmix[:, RWKV_DIM:2 * RWKV_DIM]
    v = zmix[:, 2 * RWKV_DIM:R3]
    kk = k * kk_ref[...]

    w_pre = w0_ref[...] + _bdot(jnp.tanh(p_w), w2_ref[...])
    a = _sigmoid(a0_ref[...] + _bdot(p_a, a2_ref[...]))
    gate = _bdot(_sigmoid(p_g), g2_ref[...])
    ss = _dot_exact_rhs(kk * kk, ones_ref[...])
    if has_vres:
        mix = _sigmoid(v0_ref[...] + _bdot(p_v, v2_ref[...]))
        v_first = jnp.concatenate([vfirst_ref[p] for p in range(N_PAIRS)], axis=-1)
        v = v + (v_first - v) * mix
    w_log = -_softplus(-w_pre) - 0.5
    kk = kk / jnp.maximum(jnp.sqrt(ss), 1e-12)
    outs = ((r_ref, r), (lw_ref, -jnp.exp(w_log)), (k_ref, k * (1.0 + (a - 1.0) * ka_ref[...])), (v_ref, v),
            (na_ref, -kk), (bb_ref, kk * a), (gate_ref, gate))
    for ref, val in outs:
        for p in range(N_PAIRS):
            ref[p] = val[:, p * HEAD_PAIR:(p + 1) * HEAD_PAIR]


def _mix_in(x, first, p, vres, v_first, nb, seq):
    m = nb * seq
    tm = _tile(m, max(nb, 8), 512)
    has_vres = vres is not None
    seq_major_x = x.ndim == 3
    row = lambda w: pl.BlockSpec((tm, w), lambda i: (i, 0))
    pair_row = pl.BlockSpec((N_PAIRS, tm, HEAD_PAIR), lambda i: (0, i, 0))
    ins = [x, first, p["norm_mix"], p["w_in"], p["mu_rkv"], p["mu_wag"], p["w0"], p["w1"], p["w2"],
           p["a0"], p["a1"], p["a2"], p["g1"], p["g2"], p["k_k"], p["k_a"], p["ones_head"]]
    x_spec = pl.BlockSpec((nb, tm // nb, D_MODEL), lambda i: (0, i, 0)) if seq_major_x else row(D_MODEL)
    specs = [x_spec] + [_const_spec(a.shape) for a in ins[1:]]
    if has_vres:
        extra = [vres["mu_v"], vres["v0"], vres["v1"], vres["v2"]]
        ins += extra + [v_first]
        specs += [_const_spec(a.shape) for a in extra] + [pair_row]
    wide = jax.ShapeDtypeStruct((N_PAIRS, m, HEAD_PAIR), F32)
    outs = [jax.ShapeDtypeStruct((m, D_IN - R3), F32)] + [wide] * 7 + [jax.ShapeDtypeStruct((nb, D_MODEL), F32)]
    out_specs = [row(D_IN - R3)] + [pair_row] * 7 + [_const_spec((nb, D_MODEL))]
    scratch = [pltpu.VMEM((tm + nb, D_MODEL), F32), pltpu.VMEM((tm + nb, R3), F32)]
    if seq_major_x:
        assert (tm // nb) % 8 == 0
        outs.insert(0, jax.ShapeDtypeStruct((m, D_MODEL), F32))
        out_specs.insert(0, row(D_MODEL))
        scratch.append(pltpu.VMEM((LANE_BLOCKS, tm, LANES), F32))
    return pl.pallas_call(
        functools.partial(_mix_in_kernel, has_vres, seq_major_x, nb, tm),
        grid=(m // tm,),
        in_specs=specs,
        out_specs=out_specs,
        out_shape=outs,
        scratch_shapes=scratch,
        compiler_params=_params("arbitrary"),
        name="mix_in",
    )(*ins)


def _wkv_kernel(c, n_seq, group, stride, r_ref, lw_ref, k_ref, v_ref, na_ref, bb_ref, gate_ref, s0_ref,
                rk_ref, lnw_ref, lnb_ref, tri_ref, mean_ref, s_all_ref,
                y_ref, sout_ref, s_scr):
    li = pl.program_id(1)

    @pl.when(li == 0)
    def _():
        zero = jnp.zeros((RWKV_HEAD, RWKV_HEAD), F32)
        for j in range(n_seq):
            for p in range(N_PAIRS):
                s_scr[j, p] = jnp.concatenate(
                    [jnp.concatenate([s0_ref[j, 2 * p], zero], axis=1),
                     jnp.concatenate([zero, s0_ref[j, 2 * p + 1]], axis=1)], axis=0)

    lane = lax.broadcasted_iota(jnp.int32, (1, HEAD_PAIR), 1)
    head_masks = (lane < RWKV_HEAD, lane >= RWKV_HEAD)
    trow = lax.broadcasted_iota(jnp.int32, (c, 2 * c), 0)
    tcol = lax.broadcasted_iota(jnp.int32, (c, 2 * c), 1)
    left_half = tcol < c
    scol = jnp.where(left_half, tcol, tcol - c)
    strict = scol < trow
    strict_left = strict & left_half
    strict_right = strict & (tcol >= c)
    incl = scol <= trow
    eye2 = (scol == trow).astype(F32)
    drow = lax.broadcasted_iota(jnp.int32, (2 * c, 2 * c), 0)
    dcol = lax.broadcasted_iota(jnp.int32, (2 * c, 2 * c), 1)
    diag_blocks = jnp.bitwise_xor(drow - c, dcol - c) >= 0
    brow = lax.broadcasted_iota(jnp.int32, (HEAD_PAIR, HEAD_PAIR), 0)
    bcol = lax.broadcasted_iota(jnp.int32, (HEAD_PAIR, HEAD_PAIR), 1)
    same_head = jnp.bitwise_xor(brow - RWKV_HEAD, bcol - RWKV_HEAD) >= 0
    n_double = max(int(math.log2(c)) - 1, 0)
    tri = tri_ref[...]
    zeros_c = jnp.zeros((c, HEAD_PAIR), F32)

    def pair_chunk(lw, r, k, v, na, bb, gate, sp, cols):
        lw_hi, lw_lo = _split2(lw)
        cl = jnp.dot(tri, lw_hi, preferred_element_type=F32) + jnp.dot(tri, lw_lo, preferred_element_type=F32)
        yield
        e_in = jnp.exp(cl)
        e_neg = jnp.exp(-cl)
        g_last = jnp.exp(jnp.sum(lw, axis=0, keepdims=True))
        at = na * jnp.exp(cl - lw)
        rt = r * e_in
        bt = bb * e_neg
        kt = k * e_neg
        ar = jnp.concatenate([at, rt], axis=0)
        bk = jnp.concatenate([bt, kt], axis=0)
        a_s = _bdot_nt(ar, sp)
        g0 = _bdot_nt(jnp.where(head_masks[0], ar, 0.0), bk)
        g1 = _bdot_nt(jnp.where(head_masks[1], ar, 0.0), jnp.concatenate([kt, bt], axis=0))
        yield
        ga0, ga1 = g0[0:c, :], g1[0:c, :]
        low = jnp.where(strict, jnp.where(left_half, ga0, ga1), 0.0)
        ak0 = _bdot(jnp.where(strict_right, ga0, 0.0), jnp.concatenate([zeros_c, v], axis=0))
        ak1 = _bdot(jnp.where(strict_left, ga1, 0.0), jnp.concatenate([v, zeros_c], axis=0))
        rhs = jnp.concatenate([jnp.where(head_masks[0], a_s[0:c, :] + ak0, 0.0),
                               jnp.where(head_masks[1], a_s[0:c, :] + ak1, 0.0)], axis=0)
        blockdiag = lambda x: jnp.where(diag_blocks, jnp.concatenate([x, x], axis=0), 0.0)
        t_inv = eye2 + low
        pw = low
        if n_double > 0:
            pw = _bdot(pw, blockdiag(pw))
            yield
        for level in range(n_double):
            if level == n_double - 1:
                t_inv = t_inv + _bdot(t_inv, blockdiag(pw))
            else:
                both = _bdot(jnp.concatenate([t_inv, pw], axis=0), blockdiag(pw))
                t_inv = t_inv + both[0:c, :]
                pw = both[c:2 * c, :]
            yield
        u = _bdot(t_inv, rhs)
        yield
        uv = jnp.concatenate([u, v], axis=0)
        y = a_s[c:2 * c, :] + jnp.where(
            head_masks[0], _bdot(jnp.where(incl, g0[c:2 * c, :], 0.0), uv),
            _bdot(jnp.where(incl, g1[c:2 * c, :], 0.0), jnp.concatenate([v, u], axis=0)))
        upd = _bdot_tn(uv, bk * g_last)
        s_new = sp * g_last + jnp.where(same_head, upd, 0.0)
        yield
        stats = _dot_exact_rhs(jnp.concatenate([y, r * k * rk_ref[:, cols]], axis=0), mean_ref[...])
        mu = stats[0:c, :]
        bonus = stats[c:2 * c, :] * float(RWKV_HEAD) * v
        yield
        d = y - mu
        var = _dot_exact_rhs(d * d, mean_ref[...])
        yield
        yn = d * lax.rsqrt(var + GN_EPS) * lnw_ref[:, cols] + lnb_ref[:, cols]
        return (yn + bonus) * gate, s_new

    first_seq = pl.program_id(0) * n_seq

    def seq_group(gi, carry):
        units = []
        for jj in range(group):
            j = gi * group + jj
            rows = pl.ds(first_seq + j, c, stride=stride)
            for p in range(N_PAIRS):
                cols = slice(p * HEAD_PAIR, (p + 1) * HEAD_PAIR)
                vals = [ref[p, rows, :] for ref in (lw_ref, r_ref, k_ref, v_ref, na_ref, bb_ref, gate_ref)]
                units.append((j, p, rows, pair_chunk(*vals, s_scr[j, p], cols)))
        results = _run_lockstep([gen for (_, _, _, gen) in units])
        for (j, p, rows, _), (y, s_new) in zip(units, results):
            y_ref[p, rows, :] = y
            s_scr[j, p] = s_new
        return carry

    lax.fori_loop(0, n_seq // group, seq_group, 0)

    @pl.when(li == pl.num_programs(1) - 1)
    def _():
        for j in range(n_seq):
            for p in range(N_PAIRS):
                sp = s_scr[j, p]
                sout_ref[j, 2 * p] = sp[0:RWKV_HEAD, 0:RWKV_HEAD]
                sout_ref[j, 2 * p + 1] = sp[RWKV_HEAD:HEAD_PAIR, RWKV_HEAD:HEAD_PAIR]


WKV_SEQ_PER_STEP = 8
WKV_SEQ_GROUP = 8


def _wkv(arrs, s0, s_all, layer, p, nb, seq):
    c = min(WKV_CHUNK, seq)
    n_seq = nb if seq > c else min(nb, WKV_SEQ_PER_STEP)
    assert seq % c == 0 and c % 8 == 0 and nb % n_seq == 0 and n_seq % WKV_SEQ_GROUP == 0
    blk = pl.BlockSpec((N_PAIRS, c * nb, HEAD_PAIR), lambda g, l: (0, l, 0))
    tri = jnp.tril(jnp.ones((c, c), F32)).astype(BF16)
    st_spec = pl.BlockSpec((None, n_seq, RWKV_HEADS, RWKV_HEAD, RWKV_HEAD), lambda g, l: (layer, g, 0, 0, 0))
    consts = [p["r_k"], p["ln_x_w"], p["ln_x_b"], tri, p["mean_pair"]]
    n_in = 8 + len(consts)
    return pl.pallas_call(
        functools.partial(_wkv_kernel, c, n_seq, WKV_SEQ_GROUP, nb),
        grid=(nb // n_seq, seq // c),
        in_specs=[blk] * 7 + [st_spec] + [_const_spec(a.shape) for a in consts] + [pl.BlockSpec(memory_space=pl.ANY)],
        out_specs=[blk, st_spec],
        out_shape=[jax.ShapeDtypeStruct((N_PAIRS, seq * nb, HEAD_PAIR), F32), jax.ShapeDtypeStruct(s_all.shape, F32)],
        input_output_aliases={n_in: 1},
        scratch_shapes=[pltpu.VMEM((n_seq, N_PAIRS, HEAD_PAIR, HEAD_PAIR), F32)],
        compiler_params=_params("arbitrary", "arbitrary"),
        name="wkv",
    )(*arrs, s0, *consts, s_all)


def _lru_kernel(reset_first, nb, tm, zx_ref, zg_ref, conv0_ref, h0_ref, cw_ref, cb_ref, wa_ref, ba_ref,
                wi_ref, bi_ref, lam_ref, y_ref, conv_out_ref, h_out_ref, xbuf, abuf, bbuf, hcar):
    i = pl.program_id(0)
    hist = (CONV_W - 1) * nb

    @pl.when(i == 0)
    def _():
        xbuf[0:hist, :] = conv0_ref[...]
        hcar[...] = h0_ref[...]

    xbuf[hist:hist + tm, :] = zx_ref[...]
    xc = cb_ref[...]
    for j in range(CONV_W):
        xc = xc + xbuf[j * nb:j * nb + tm, :] * cw_ref[j:j + 1, :]
    conv_out_ref[...] = xbuf[tm:tm + hist, :]
    xbuf[0:hist, :] = xbuf[tm:tm + hist, :]

    gate_a = _sigmoid(_bdot(xc, wa_ref[...]) + ba_ref[...])
    gate_i = _sigmoid(_bdot(xc, wi_ref[...]) + bi_ref[...])
    log_a = -LRU_C * gate_a * _softplus(-lam_ref[...])
    a_sq = jnp.exp(2.0 * log_a)
    mult = jnp.sqrt(-jnp.tanh(log_a) * (a_sq + 1.0))
    if reset_first:
        row = lax.broadcasted_iota(jnp.int32, (tm, 1), 0)
        mult = jnp.where(row < jnp.where(i == 0, nb, 0), 1.0, mult)
    abuf[...] = jnp.exp(log_a)
    bbuf[...] = mult * gate_i * xc

    def step(t, h):
        rows = pl.ds(pl.multiple_of(t * nb, nb), nb)
        h = abuf[rows, :] * h + bbuf[rows, :]
        bbuf[rows, :] = h
        return h

    h = lax.fori_loop(0, tm // nb, step, hcar[...])
    hcar[...] = h
    h_out_ref[...] = h
    y_ref[...] = bbuf[...] * _gelu(zg_ref[...])


def _lru(zrest, conv0, h0, p, reset_first, nb):
    m = zrest.shape[0]
    tm = _tile(m, max(nb, 8), 512)
    hist = (CONV_W - 1) * nb
    consts = [conv0, h0, p["conv_w"], p["conv_b"], p["lru_wa"], p["lru_ba"], p["lru_wi"], p["lru_bi"],
              p["lru_lambda"]]
    return pl.pallas_call(
        functools.partial(_lru_kernel, reset_first, nb, tm),
        grid=(m // tm,),
        in_specs=[pl.BlockSpec((tm, LRU_DIM), lambda i: (i, 0)),
                  pl.BlockSpec((tm, LRU_DIM), lambda i: (i, 1))] + [_const_spec(a.shape) for a in consts],
        out_specs=[pl.BlockSpec((tm, LRU_DIM), lambda i: (i, 0)), _const_spec((hist, LRU_DIM)),
                   _const_spec((nb, LRU_DIM))],
        out_shape=[jax.ShapeDtypeStruct((m, LRU_DIM), F32), jax.ShapeDtypeStruct((hist, LRU_DIM), F32),
                   jax.ShapeDtypeStruct((nb, LRU_DIM), F32)],
        scratch_shapes=[pltpu.VMEM((tm + hist, LRU_DIM), F32), pltpu.VMEM((tm, LRU_DIM), F32),
                        pltpu.VMEM((tm, LRU_DIM), F32), pltpu.VMEM((nb, LRU_DIM), F32)],
        compiler_params=_params("arbitrary"),
        name="rg_lru",
    )(zrest, zrest, *consts)


def _s5_kernel(nb, tm, u_ref, re0_ref, im0_ref, are_ref, aim_ref, bre_ref, bim_ref, cre_ref, cim_ref,
               d_ref, wglu_ref, bglu_ref, y_ref, re_out_ref, im_out_ref, xre, xim, car_re, car_im):
    @pl.when(pl.program_id(0) == 0)
    def _():
        car_re[...] = re0_ref[...]
        car_im[...] = im0_ref[...]

    u = u_ref[...]
    xre[...] = _bdot(u, bre_ref[...])
    xim[...] = _bdot(u, bim_ref[...])
    a_re = jnp.broadcast_to(are_ref[...], (nb, S5_WIDTH))
    a_im = jnp.broadcast_to(aim_ref[...], (nb, S5_WIDTH))

    def step(t, carry):
        s_re, s_im = carry
        rows = pl.ds(pl.multiple_of(t * nb, nb), nb)
        n_re = a_re * s_re - a_im * s_im + xre[rows, :]
        n_im = a_re * s_im + a_im * s_re + xim[rows, :]
        xre[rows, :] = n_re
        xim[rows, :] = n_im
        return n_re, n_im

    s_re, s_im = lax.fori_loop(0, tm // nb, step, (car_re[...], car_im[...]))
    car_re[...] = s_re
    car_im[...] = s_im
    re_out_ref[...] = s_re
    im_out_ref[...] = s_im

    y = _bdot(xre[...], cre_ref[...]) - _bdot(xim[...], cim_ref[...]) + d_ref[...] * u
    o = _bdot(_gelu(y), wglu_ref[...]) + bglu_ref[...]
    y_ref[...] = o[:, 0:S5_DIM] * _sigmoid(o[:, S5_DIM:2 * S5_DIM])


def _s5(zrest, re0, im0, p, nb):
    m = zrest.shape[0]
    tm = _tile(m, max(nb, 8), 512)
    consts = [re0, im0, p["s5_abar_re"], p["s5_abar_im"], p["s5_bbar_re"], p["s5_bbar_im"],
              p["s5_c_re"], p["s5_c_im"], p["s5_d"], p["s5_w_glu"], p["s5_b_glu"]]
    st = jax.ShapeDtypeStruct((nb, S5_WIDTH), F32)
    return pl.pallas_call(
        functools.partial(_s5_kernel, nb, tm),
        grid=(m // tm,),
        in_specs=[pl.BlockSpec((tm, S5_DIM), lambda i: (i, 2))] + [_const_spec(a.shape) for a in consts],
        out_specs=[pl.BlockSpec((tm, S5_DIM), lambda i: (i, 0)), _const_spec((nb, S5_WIDTH)),
                   _const_spec((nb, S5_WIDTH))],
        out_shape=[jax.ShapeDtypeStruct((m, S5_DIM), F32), st, st],
        scratch_shapes=[pltpu.VMEM((tm, S5_WIDTH), F32), pltpu.VMEM((tm, S5_WIDTH), F32),
                        pltpu.VMEM((nb, S5_WIDTH), F32), pltpu.VMEM((nb, S5_WIDTH), F32)],
        compiler_params=_params("arbitrary"),
        name="s5",
    )(zrest, *consts)


def _mix_out_kernel(x_ref, ya_ref, yb_ref, yc_ref, wa_ref, wb_ref, wc_ref, gq_ref, wq_ref, x1_ref, q_ref):
    ya = jnp.concatenate([ya_ref[p] for p in range(N_PAIRS)], axis=-1)
    x1 = (x_ref[...] + _bdot(ya, wa_ref[...]) + _bdot(yb_ref[...], wb_ref[...])
          + _bdot(yc_ref[...], wc_ref[...]))
    q = _bdot(_rms(x1, gq_ref[...]), wq_ref[...])
    for cb in range(LANE_BLOCKS):
        cols = slice(cb * LANES, (cb + 1) * LANES)
        x1_ref[cb] = x1[:, cols]
        q_ref[cb] = q[:, cols]


def _mix_out(x, ya, yb, yc, p):
    m = x.shape[0]
    tm = _tile(m, 8, 512)
    row = lambda w: pl.BlockSpec((tm, w), lambda i: (i, 0))
    split = pl.BlockSpec((LANE_BLOCKS, tm, LANES), lambda i: (0, i, 0))
    consts = [p["w_out_a"], p["w_out_b"], p["w_out_c"], p["norm_mem_q"], p["mem_wq"]]
    out = jax.ShapeDtypeStruct((LANE_BLOCKS, m, LANES), F32)
    return pl.pallas_call(
        _mix_out_kernel,
        grid=(m // tm,),
        in_specs=[row(D_MODEL), pl.BlockSpec((N_PAIRS, tm, HEAD_PAIR), lambda i: (0, i, 0)), row(LRU_DIM),
                  row(S5_DIM)] + [_const_spec(a.shape) for a in consts],
        out_specs=[split, split],
        out_shape=[out, out],
        compiler_params=_params("parallel"),
        name="mix_out",
    )(x, ya, yb, yc, *consts)


def _load_seq_rows(ref, rows):
    return jnp.concatenate([ref[cb, rows, :] for cb in range(LANE_BLOCKS)], axis=-1)


def _store_seq_rows(ref, rows, val):
    for cb in range(LANE_BLOCKS):
        ref[cb, rows, :] = val[:, cb * LANES:(cb + 1) * LANES]


def _softmax_rows(s):
    e = jnp.exp(s - jnp.max(s, axis=-1, keepdims=True))
    return e / jnp.sum(e, axis=-1, keepdims=True)


def _attn_kernel(tl, n_seq, stride, q_ref, x_ref, mk_ref, mv_ref, wo_ref, o_ref):
    scale = MEM_HEAD_DIM ** -0.5
    first_seq = pl.program_id(0) * n_seq
    head_cols = [slice(hh * MEM_HEAD_DIM, (hh + 1) * MEM_HEAD_DIM) for hh in range(MEM_HEADS)]

    def one_seq(j):
        rows = pl.ds(first_seq + j, tl, stride=stride)
        q = _load_seq_rows(q_ref, rows)
        scores = [_bdot_nt(q[:, cols], mk_ref[j, :, cols]) * scale for cols in head_cols]
        yield
        heads = [_bdot(_softmax_rows(s), mv_ref[j, :, cols]) for s, cols in zip(scores, head_cols)]
        yield
        out = _load_seq_rows(x_ref, rows) + _bdot(jnp.concatenate(heads, axis=-1), wo_ref[...])
        yield
        _store_seq_rows(o_ref, rows, out)

    for j0 in range(0, n_seq, ATTN_LOCKSTEP):
        _run_lockstep([one_seq(j) for j in range(j0, min(j0 + ATTN_LOCKSTEP, n_seq))])


def _attn_cache_kernel(tl, n_seq, stride, q_ref, x_ref, mk_ref, mv_ref, wo_ref, o_ref):
    scale = MEM_HEAD_DIM ** -0.5
    first_seq = pl.program_id(0) * n_seq
    qrow = lax.broadcasted_iota(jnp.int32, (MEM_HEADS * tl, N_MEM * MEM_HEADS), 0)
    kcol = lax.broadcasted_iota(jnp.int32, (MEM_HEADS * tl, N_MEM * MEM_HEADS), 1)
    q_head = jnp.zeros_like(qrow)
    for hh in range(1, MEM_HEADS):
        q_head = q_head + (qrow >= hh * tl).astype(jnp.int32)
    same_head = q_head == jnp.bitwise_and(kcol, MEM_HEADS - 1)

    def one_seq(j):
        rows = pl.ds(first_seq + j, tl, stride=stride)
        q = _load_seq_rows(q_ref, rows)
        qh = jnp.concatenate([q[:, hh * MEM_HEAD_DIM:(hh + 1) * MEM_HEAD_DIM] for hh in range(MEM_HEADS)], axis=0)
        k2 = mk_ref[j].reshape(N_MEM * MEM_HEADS, MEM_HEAD_DIM)
        s = jnp.where(same_head, _bdot_nt(qh, k2) * scale, MASKED_SCORE)
        yield
        v2 = mv_ref[j].reshape(N_MEM * MEM_HEADS, MEM_HEAD_DIM)
        oh = _bdot(_softmax_rows(s), v2)
        yield
        o = jnp.concatenate([oh[hh * tl:(hh + 1) * tl, :] for hh in range(MEM_HEADS)], axis=-1)
        out = _load_seq_rows(x_ref, rows) + _bdot(o, wo_ref[...])
        yield
        _store_seq_rows(o_ref, rows, out)

    _run_lockstep([one_seq(j) for j in range(n_seq)])


MASKED_SCORE = -1e30
ATTN_SEQ_PER_STEP = 4
ATTN_ROWS = 128
ATTN_LOCKSTEP = 2


def _attn(q, x1, mk, mv, kv_offset, wo, nb, seq):
    tl = _tile(seq, 8, ATTN_ROWS)
    n_seq = nb if seq > tl else min(nb, ATTN_SEQ_PER_STEP)
    assert nb % n_seq == 0 and kv_offset % n_seq == 0 and MEM_HEADS & (MEM_HEADS - 1) == 0
    blk = pl.BlockSpec((LANE_BLOCKS, tl * nb, LANES), lambda g, l: (0, l, 0))
    kv_index = lambda g, l: (kv_offset // n_seq + g,) + (0,) * (mk.ndim - 1)
    kv = pl.BlockSpec((n_seq,) + mk.shape[1:], kv_index)
    body = _attn_kernel if mk.ndim == 3 else _attn_cache_kernel
    return pl.pallas_call(
        functools.partial(body, tl, n_seq, nb),
        grid=(nb // n_seq, seq // tl),
        in_specs=[blk, blk, kv, kv, _const_spec(wo.shape)],
        out_specs=blk,
        out_shape=jax.ShapeDtypeStruct((LANE_BLOCKS, seq * nb, LANES), F32),
        compiler_params=_params("arbitrary", "arbitrary"),
        name="mem_attn",
    )(q, x1, mk, mv, wo)


def _ffn_kernel(d_ff, final_norm, n_seq, x_ref, g_ref, wup_ref, wdown_ref, *rest):
    x = jnp.concatenate([x_ref[cb] for cb in range(LANE_BLOCKS)], axis=-1)
    h = _rms(x, g_ref[...]).astype(BF16)
    gt = jnp.dot(h, wup_ref[:, 0:d_ff], preferred_element_type=F32)
    up = jnp.dot(h, wup_ref[:, d_ff:2 * d_ff], preferred_element_type=F32)
    act = gt * _sigmoid(gt) * up
    out = x + _bdot(act, wdown_ref[...])
    if final_norm:
        out = _rms(out, rest[0][...])
    if n_seq:
        o_ref, os = rest[-2:]
        for cb in range(LANE_BLOCKS):
            os[cb] = out[:, cb * LANES:(cb + 1) * LANES]
        for b in range(n_seq):
            o_ref[b] = _load_seq_rows(os, pl.ds(b, out.shape[0] // n_seq, stride=n_seq))
    else:
        rest[-1][...] = out


def _ffn(x, p, g_final=None, seq_major_out=None):
    m = x.shape[1]
    tm = _tile(m, 8, 256)
    d_ff = p["ffn_w_down"].shape[0]
    consts = [p["norm_ffn"], p["ffn_w_up"], p["ffn_w_down"]] + ([] if g_final is None else [g_final])
    if seq_major_out is None:
        n_seq, scratch = 0, []
        out_spec = pl.BlockSpec((tm, D_MODEL), lambda i: (i, 0))
        out_shape = jax.ShapeDtypeStruct((m, D_MODEL), F32)
    else:
        n_seq, seq = seq_major_out
        assert tm % n_seq == 0 and (tm // n_seq) % 8 == 0 and n_seq * seq == m
        scratch = [pltpu.VMEM((LANE_BLOCKS, tm, LANES), F32)]
        out_spec = pl.BlockSpec((n_seq, tm // n_seq, D_MODEL), lambda i: (0, i, 0))
        out_shape = jax.ShapeDtypeStruct((n_seq, seq, D_MODEL), F32)
    return pl.pallas_call(
        functools.partial(_ffn_kernel, d_ff, g_final is not None, n_seq),
        grid=(m // tm,),
        in_specs=[pl.BlockSpec((LANE_BLOCKS, tm, LANES), lambda i: (0, i, 0))] + [_const_spec(a.shape) for a in consts],
        out_specs=out_spec,
        out_shape=out_shape,
        scratch_shapes=scratch,
        compiler_params=_params("parallel"),
        name="ffn",
    )(x, *consts)


def _mem_project_kernel(x_ref, g_ref, wk_ref, wv_ref, k_ref, v_ref, kb_ref, vb_ref):
    m = _rms(x_ref[...], g_ref[...]).astype(BF16)
    k = jnp.dot(m, wk_ref[...], preferred_element_type=F32)
    v = jnp.dot(m, wv_ref[...], preferred_element_type=F32)
    k_ref[...] = k
    v_ref[...] = v
    kb_ref[...] = k.astype(BF16)
    vb_ref[...] = v.astype(BF16)


def _mem_project(mem, g, wk, wv):
    m = mem.shape[0]
    tm = _tile(m, 16, 512)
    row = pl.BlockSpec((tm, D_MODEL), lambda i: (i, 0))
    out = jax.ShapeDtypeStruct((m, D_MODEL), F32)
    outb = jax.ShapeDtypeStruct((m, D_MODEL), BF16)
    return pl.pallas_call(
        _mem_project_kernel, grid=(m // tm,),
        in_specs=[row, _const_spec(g.shape), _const_spec(wk.shape), _const_spec(wv.shape)],
        out_specs=[row] * 4, out_shape=[out, out, outb, outb], compiler_params=_params("parallel"),
        name="mem_project",
    )(mem, g, wk, wv)


def _block_diag(blocks):
    n, r, c = blocks.shape
    eye = jnp.eye(n, dtype=blocks.dtype)
    return (eye[:, None, :, None] * blocks[:, :, None, :]).reshape(n * r, n * c)


def _layer_params(w, l):
    row = lambda a: a.reshape(1, -1).astype(F32)
    bf = lambda a: a.astype(BF16)
    ones64 = jnp.ones((RWKV_HEADS, RWKV_HEAD, RWKV_HEAD), F32)
    p = {
        "norm_mix": row(w["norm_mix"][l]), "w_in": bf(w["w_in"][l]), "mu_rkv": row(w["mu_rkv"][l]),
        "mu_wag": w["mu_wag"][l], "w0": row(w["w0"][l]), "w1": bf(w["w1"][l]), "w2": bf(w["w2"][l]),
        "a0": row(w["a0"][l]), "a1": bf(w["a1"][l]), "a2": bf(w["a2"][l]),
        "g1": bf(w["g1"][l]), "g2": bf(w["g2"][l]), "k_k": row(w["k_k"][l]), "k_a": row(w["k_a"][l]),
        "ones_head": bf(_block_diag(ones64)),
        "mean_pair": bf(_block_diag(ones64[:2]) / RWKV_HEAD),
        "r_k": row(w["r_k"][l]), "ln_x_w": row(w["ln_x_w"][l]), "ln_x_b": row(w["ln_x_b"][l]),
        "conv_w": w["conv_w"][l], "conv_b": row(w["conv_b"][l]),
        "lru_wa": bf(_block_diag(w["lru_wa"][l])), "lru_ba": row(w["lru_ba"][l]),
        "lru_wi": bf(_block_diag(w["lru_wi"][l])), "lru_bi": row(w["lru_bi"][l]),
        "lru_lambda": row(w["lru_lambda"][l]),
        "s5_d": row(w["s5_d"][l]), "s5_w_glu": bf(w["s5_w_glu"][l]), "s5_b_glu": row(w["s5_b_glu"][l]),
        "w_out_a": bf(w["w_out"][l][0:RWKV_DIM]),
        "w_out_b": bf(w["w_out"][l][RWKV_DIM:RWKV_DIM + LRU_DIM]),
        "w_out_c": bf(w["w_out"][l][RWKV_DIM + LRU_DIM:]),
        "norm_mem_q": row(w["norm_mem_q"][l]), "mem_wq": bf(w["mem_wq"][l]), "mem_wo": bf(w["mem_wo"][l]),
        "norm_ffn": row(w["norm_ffn"][l]), "ffn_w_up": bf(w["ffn_w_up"][l]), "ffn_w_down": bf(w["ffn_w_down"][l]),
    }
    lam_re, lam_im = w["s5_a_re"][l].astype(F32), w["s5_a_im"][l].astype(F32)
    dt = jnp.exp(w["s5_log_dt"][l].astype(F32))[:, None]
    mag = jnp.exp(lam_re * dt)
    abar_re, abar_im = mag * jnp.cos(lam_im * dt), mag * jnp.sin(lam_im * dt)
    den = lam_re * lam_re + lam_im * lam_im
    q_re = ((abar_re - 1.0) * lam_re + abar_im * lam_im) / den
    q_im = (abar_im * lam_re - (abar_re - 1.0) * lam_im) / den
    b_re, b_im = w["s5_b_re"][l].astype(F32), w["s5_b_im"][l].astype(F32)
    bbar_re = q_re[..., None] * b_re - q_im[..., None] * b_im
    bbar_im = q_re[..., None] * b_im + q_im[..., None] * b_re
    p["s5_abar_re"] = abar_re.reshape(1, S5_WIDTH)
    p["s5_abar_im"] = abar_im.reshape(1, S5_WIDTH)
    p["s5_bbar_re"] = bf(_block_diag(jnp.swapaxes(bbar_re, 1, 2)))
    p["s5_bbar_im"] = bf(_block_diag(jnp.swapaxes(bbar_im, 1, 2)))
    p["s5_c_re"] = bf(_block_diag(jnp.swapaxes(w["s5_c_re"][l], 1, 2)))
    p["s5_c_im"] = bf(_block_diag(jnp.swapaxes(w["s5_c_im"][l], 1, 2)))
    vres = None
    if l > 0:
        vres = {"mu_v": row(w["mu_v"][l - 1]), "v0": row(w["v0"][l - 1]),
                "v1": bf(w["v1"][l - 1]), "v2": bf(w["v2"][l - 1])}
    return p, vres


def _time_major(a):
    return jnp.swapaxes(a, 0, 1).reshape((a.shape[0] * a.shape[1],) + a.shape[2:])


def _run_group(x, mem_k, mem_v, kv_stride, shift0, wkv0, conv0, lru0, re0, im0, reset_first, layers, norm_final,
               relayout_in_kernel):
    nb, seq, _ = x.shape
    xt = x if relayout_in_kernel else _time_major(x)
    outs = {k: [] for k in ("shift", "conv", "lru", "re", "im")}
    wkv_all = jnp.zeros_like(wkv0)
    v_first = None
    for l, (p, vres) in enumerate(layers):
        last = l == len(layers) - 1
        mixed = _mix_in(xt, shift0[l], p, vres, v_first, nb, seq)
        if xt.ndim == 3:
            xt, mixed = mixed[0], mixed[1:]
        zrest, r, lw, k, v, na, bb, gate, h_last = mixed
        if l == 0:
            v_first = v
        ya, wkv_all = _wkv((r, lw, k, v, na, bb, gate), wkv0, wkv_all, l, p, nb, seq)
        yb, conv_new, lru_new = _lru(zrest, _time_major(conv0[l]), lru0[l], p, reset_first, nb)
        yc, re_new, im_new = _s5(zrest, re0[l].reshape(nb, S5_WIDTH), im0[l].reshape(nb, S5_WIDTH), p, nb)
        x1, q = _mix_out(xt, ya, yb, yc, p)
        x2 = _attn(q, x1, mem_k, mem_v, l * kv_stride, p["mem_wo"], nb, seq)
        xt = _ffn(x2, p, norm_final if last else None, (nb, seq) if last and relayout_in_kernel else None)
        outs["shift"].append(h_last)
        outs["conv"].append(jnp.swapaxes(conv_new.reshape(CONV_W - 1, nb, LRU_DIM), 0, 1))
        outs["lru"].append(lru_new)
        outs["re"].append(re_new.reshape(nb, S5_GROUPS, S5_STATE))
        outs["im"].append(im_new.reshape(nb, S5_GROUPS, S5_STATE))
    y = xt if relayout_in_kernel else jnp.swapaxes(xt.reshape(seq, nb, D_MODEL), 0, 1)
    st = {k: jnp.stack(v) for k, v in outs.items()}
    return (y, st["shift"], wkv_all, st["conv"], st["lru"], st["re"], st["im"])


def kernel(x_prompt, x_sample, mem_prompt, state_shift, state_wkv, state_conv, state_lru, state_s5_re, state_s5_im, cache_mem_k, cache_mem_v, norm_mix, w_in, w_out, mu_rkv, mu_wag, mu_v, w0, w1, w2, a0, a1, a2, v0, v1, v2, g1, g2, k_k, k_a, r_k, ln_x_w, ln_x_b, conv_w, conv_b, lru_wa, lru_ba, lru_wi, lru_bi, lru_lambda, s5_a_re, s5_a_im, s5_log_dt, s5_b_re, s5_b_im, s5_c_re, s5_c_im, s5_d, s5_w_glu, s5_b_glu, norm_mem_q, norm_mem_kv, mem_wq, mem_wk, mem_wv, mem_wo, norm_ffn, ffn_w_up, ffn_w_down, norm_final):
    w = dict(norm_mix=norm_mix, w_in=w_in, w_out=w_out, mu_rkv=mu_rkv, mu_wag=mu_wag, mu_v=mu_v, w0=w0, w1=w1,
             w2=w2, a0=a0, a1=a1, a2=a2, v0=v0, v1=v1, v2=v2, g1=g1, g2=g2, k_k=k_k, k_a=k_a, r_k=r_k,
             ln_x_w=ln_x_w, ln_x_b=ln_x_b, conv_w=conv_w, conv_b=conv_b, lru_wa=lru_wa, lru_ba=lru_ba,
             lru_wi=lru_wi, lru_bi=lru_bi, lru_lambda=lru_lambda, s5_a_re=s5_a_re, s5_a_im=s5_a_im,
             s5_log_dt=s5_log_dt, s5_b_re=s5_b_re, s5_b_im=s5_b_im, s5_c_re=s5_c_re, s5_c_im=s5_c_im,
             s5_d=s5_d, s5_w_glu=s5_w_glu, s5_b_glu=s5_b_glu, norm_mem_q=norm_mem_q, mem_wq=mem_wq,
             mem_wo=mem_wo, norm_ffn=norm_ffn, ffn_w_up=ffn_w_up, ffn_w_down=ffn_w_down)
    depth = norm_mix.shape[0]
    layers = [_layer_params(w, l) for l in range(depth)]
    g_final = norm_final.reshape(1, D_MODEL)

    bp, n_mem, _ = mem_prompt.shape
    mem_flat = mem_prompt.reshape(bp * n_mem, D_MODEL)
    mem_kv = [_mem_project(mem_flat, norm_mem_kv[l].reshape(1, D_MODEL), mem_wk[l].astype(BF16),
                           mem_wv[l].astype(BF16)) for l in range(depth)]
    mem_k_p = jnp.stack([kv[0] for kv in mem_kv]).reshape(depth, bp, n_mem, MEM_HEADS, MEM_HEAD_DIM)
    mem_v_p = jnp.stack([kv[1] for kv in mem_kv]).reshape(depth, bp, n_mem, MEM_HEADS, MEM_HEAD_DIM)
    mem_k_b = jnp.stack([kv[2] for kv in mem_kv]).reshape(depth * bp, n_mem, D_MODEL)
    mem_v_b = jnp.stack([kv[3] for kv in mem_kv]).reshape(depth * bp, n_mem, D_MODEL)

    zeros = lambda *s: jnp.zeros((depth, bp) + s, F32)
    prompt = _run_group(
        x_prompt, mem_k_b, mem_v_b, bp,
        zeros(D_MODEL), zeros(RWKV_HEADS, RWKV_HEAD, RWKV_HEAD), zeros(CONV_W - 1, LRU_DIM), zeros(LRU_DIM),
        zeros(S5_GROUPS, S5_STATE), zeros(S5_GROUPS, S5_STATE), True, layers, g_final, True)
    bs = x_sample.shape[0]
    cache_shape = (depth * bs, n_mem, MEM_HEADS, MEM_HEAD_DIM)
    sample = _run_group(
        x_sample, cache_mem_k.reshape(cache_shape), cache_mem_v.reshape(cache_shape),
        bs, state_shift, state_wkv, state_conv, state_lru, state_s5_re, state_s5_im, False, layers, g_final, False)
    (y_p, sh_p, wkv_p, conv_p, lru_p, re_p, im_p) = prompt
    (y_s, sh_s, wkv_s, conv_s, lru_s, re_s, im_s) = sample
    return (y_p, y_s, sh_p, sh_s, wkv_p, wkv_s, conv_p, conv_s, lru_p, lru_s, re_p, re_s, im_p, im_s,
            mem_k_p, mem_v_p)
```

```python
import functools
import math

import jax
import jax.numpy as jnp
from jax import lax
from jax.experimental import pallas as pl
from jax.experimental.pallas import tpu as pltpu

F32 = jnp.float32
BF16 = jnp.bfloat16

D_MODEL = 1024
RWKV_DIM = 512
RWKV_HEAD = 64
RWKV_HEADS = 8
LANES = 128
LANE_BLOCKS = D_MODEL // LANES
HEAD_PAIR = 2 * RWKV_HEAD
N_PAIRS = RWKV_DIM // HEAD_PAIR
LRU_DIM = 256
LRU_BLOCKS = 4
CONV_W = 4
LRU_C = 8.0
S5_DIM = 256
S5_GROUP = 16
S5_GROUPS = 16
S5_STATE = 64
S5_WIDTH = S5_GROUPS * S5_STATE
R3 = 3 * RWKV_DIM
D_IN = R3 + 2 * LRU_DIM + S5_DIM
N_MEM = 256
MEM_HEADS = 4
MEM_HEAD_DIM = 256
RMS_EPS = 1e-6
GN_EPS = 64e-5
WKV_CHUNK = 64

V7X_VMEM_LIMIT = 56 * 1024 * 1024


def _params(*sem):
    return pltpu.CompilerParams(dimension_semantics=sem, vmem_limit_bytes=V7X_VMEM_LIMIT)


def _tile(m, mult, target):
    best = mult
    t = mult
    while t <= min(m, target):
        if m % t == 0:
            best = t
        t += mult
    assert m % best == 0, (m, mult, target)
    return best


def _const_spec(shape):
    nd = len(shape)
    return pl.BlockSpec(shape, lambda *_: (0,) * nd)


class _Layer:
    def __init__(self, arr, layer):
        self.arr, self.layer, self.shape = arr, layer, arr.shape[1:]


def _spec(a):
    if isinstance(a, _Layer):
        nd = len(a.shape)
        return pl.BlockSpec((None,) + a.shape, lambda *_: (a.layer,) + (0,) * nd)
    return _const_spec(a.shape)


def _arrays(operands):
    return [a.arr if isinstance(a, _Layer) else a for a in operands]


def _bdot(a, b):
    return jnp.dot(a.astype(BF16), b.astype(BF16), preferred_element_type=F32)


def _bdot_nt(a, b):
    return lax.dot_general(a.astype(BF16), b.astype(BF16), (((1,), (1,)), ((), ())),
                           preferred_element_type=F32)


def _bdot_tn(a, b):
    return lax.dot_general(a.astype(BF16), b.astype(BF16), (((0,), (0,)), ((), ())),
                           preferred_element_type=F32)


def _split2(x):
    hi = x.astype(BF16)
    lo = (x - hi.astype(F32)).astype(BF16)
    return hi, lo


def _dot_exact_rhs(x, w):
    n = x.shape[0]
    hi = x.astype(BF16).astype(F32)
    both = jnp.dot(jnp.concatenate([hi, x - hi], axis=0).astype(BF16), w, preferred_element_type=F32)
    return both[0:n, :] + both[n:2 * n, :]


def _run_lockstep(gens):
    results = {}
    while len(results) < len(gens):
        for i, gen in enumerate(gens):
            if i not in results:
                try:
                    next(gen)
                except StopIteration as stop:
                    results[i] = stop.value
    return [results[i] for i in range(len(gens))]


def _rms(x, g):
    return x * lax.rsqrt(jnp.mean(x * x, axis=-1, keepdims=True) + RMS_EPS) * g


def _softplus(x):
    return jnp.maximum(x, 0.0) + jnp.log1p(jnp.exp(-jnp.abs(x)))


def _sigmoid(x):
    return 1.0 / (1.0 + jnp.exp(-x))


def _gelu(x):
    c = math.sqrt(2.0 / math.pi)
    return x * (0.5 * (1.0 + jnp.tanh(c * (x + 0.044715 * (x * x * x)))))


def _mix_in_kernel(has_vres, seq_major_x, nb, tm, *refs):
    refs = list(refs)
    (x_ref, first_ref, g_ref, win_ref, murkv_ref, muwag_ref, w0_ref, w1_ref, w2_ref,
     a0_ref, a1_ref, a2_ref, g1_ref, g2_ref, kk_ref, ka_ref, ones_ref) = refs[:17]
    del refs[:17]
    if has_vres:
        muv_ref, v0_ref, v1_ref, v2_ref, vfirst_ref = refs[:5]
        del refs[:5]
    if seq_major_x:
        xt_ref = refs.pop(0)
    zrest_ref, r_ref, lw_ref, k_ref, v_ref, na_ref, bb_ref, gate_ref, hlast_ref, hbuf, zbuf = refs[:11]

    @pl.when(pl.program_id(0) == 0)
    def _():
        first = first_ref[...]
        hbuf[0:nb, :] = first
        zbuf[0:nb, :] = _bdot(first, win_ref[:, 0:R3])

    if seq_major_x:
        xs = refs[-1]
        for b in range(nb):
            _store_seq_rows(xs, pl.ds(b, tm // nb, stride=nb), x_ref[b])
        x = jnp.concatenate([xs[cb] for cb in range(LANE_BLOCKS)], axis=-1)
        xt_ref[...] = x
    else:
        x = x_ref[...]
    h = _rms(x, g_ref[...])
    hbuf[nb:nb + tm, :] = h
    hp = hbuf[0:tm, :]
    hbuf[0:nb, :] = hbuf[tm:tm + nb, :]
    hlast_ref[...] = h[tm - nb:tm, :]
    dh = hp - h
    xw = h + dh * muwag_ref[0:1, :]
    xa = h + dh * muwag_ref[1:2, :]
    xg = h + dh * muwag_ref[2:3, :]

    z = _bdot(h, win_ref[...])
    p_w = _bdot(xw, w1_ref[...])
    p_a = _bdot(xa, a1_ref[...])
    p_g = _bdot(xg, g1_ref[...])
    if has_vres:
        p_v = _bdot(h + dh * muv_ref[...], v1_ref[...])

    zrest_ref[...] = z[:, R3:D_IN]
    z_rkv = z[:, 0:R3]
    zbuf[nb:nb + tm, :] = z_rkv
    z_sh = zbuf[0:tm, :]
    zbuf[0:nb, :] = zbuf[tm:tm + nb, :]
    zmix = z_rkv + (z_sh - z_rkv) * murkv_ref[...]
    r = zmix[:, 0:RWKV_DIM]
    k = zmix[:, RWKV_DIM:2 * RWKV_DIM]
    v = zmix[:, 2 * RWKV_DIM:R3]
    kk = k * kk_ref[...]

    w_pre = w0_ref[...] + _bdot(jnp.tanh(p_w), w2_ref[...])
    a = _sigmoid(a0_ref[...] + _bdot(p_a, a2_ref[...]))
    gate = _bdot(_sigmoid(p_g), g2_ref[...])
    ss = _dot_exact_rhs(kk * kk, ones_ref[...])
    if has_vres:
        mix = _sigmoid(v0_ref[...] + _bdot(p_v, v2_ref[...]))
        v_first = jnp.concatenate([vfirst_ref[p] for p in range(N_PAIRS)], axis=-1)
        v = v + (v_first - v) * mix
    w_log = -_softplus(-w_pre) - 0.5
    kk = kk / jnp.maximum(jnp.sqrt(ss), 1e-12)
    outs = ((r_ref, r), (lw_ref, -jnp.exp(w_log)), (k_ref, k * (1.0 + (a - 1.0) * ka_ref[...])), (v_ref, v),
            (na_ref, -kk), (bb_ref, kk * a), (gate_ref, gate))
    for ref, val in outs:
        for p in range(N_PAIRS):
            ref[p] = val[:, p * HEAD_PAIR:(p + 1) * HEAD_PAIR]


def _mix_in(x, first, p, vres, v_first, nb, seq):
    m = nb * seq
    tm = _tile(m, max(nb, 8), 512)
    has_vres = vres is not None
    seq_major_x = x.ndim == 3
    row = lambda w: pl.BlockSpec((tm, w), lambda i: (i, 0))
    pair_row = pl.BlockSpec((N_PAIRS, tm, HEAD_PAIR), lambda i: (0, i, 0))
    ins = [x, first, p["norm_mix"], p["w_in"], p["mu_rkv"], p["mu_wag"], p["w0"], p["w1"], p["w2"],
           p["a0"], p["a1"], p["a2"], p["g1"], p["g2"], p["k_k"], p["k_a"], p["ones_head"]]
    x_spec = pl.BlockSpec((nb, tm // nb, D_MODEL), lambda i: (0, i, 0)) if seq_major_x else row(D_MODEL)
    specs = [x_spec] + [_spec(a) for a in ins[1:]]
    if has_vres:
        extra = [vres["mu_v"], vres["v0"], vres["v1"], vres["v2"]]
        ins += extra + [v_first]
        specs += [_spec(a) for a in extra] + [pair_row]
    wide = jax.ShapeDtypeStruct((N_PAIRS, m, HEAD_PAIR), F32)
    outs = [jax.ShapeDtypeStruct((m, D_IN - R3), F32)] + [wide] * 7 + [jax.ShapeDtypeStruct((nb, D_MODEL), F32)]
    out_specs = [row(D_IN - R3)] + [pair_row] * 7 + [_const_spec((nb, D_MODEL))]
    scratch = [pltpu.VMEM((tm + nb, D_MODEL), F32), pltpu.VMEM((tm + nb, R3), F32)]
    if seq_major_x:
        assert (tm // nb) % 8 == 0
        outs.insert(0, jax.ShapeDtypeStruct((m, D_MODEL), F32))
        out_specs.insert(0, row(D_MODEL))
        scratch.append(pltpu.VMEM((LANE_BLOCKS, tm, LANES), F32))
    return pl.pallas_call(
        functools.partial(_mix_in_kernel, has_vres, seq_major_x, nb, tm),
        grid=(m // tm,),
        in_specs=specs,
        out_specs=out_specs,
        out_shape=outs,
        scratch_shapes=scratch,
        compiler_params=_params("arbitrary"),
        name="mix_in",
    )(*_arrays(ins))


def _wkv_kernel(c, n_seq, group, stride, r_ref, lw_ref, k_ref, v_ref, na_ref, bb_ref, gate_ref, s0_ref,
                rk_ref, lnw_ref, lnb_ref, tri_ref, mean_ref, s_all_ref,
                y_ref, sout_ref, s_scr):
    li = pl.program_id(1)

    @pl.when(li == 0)
    def _():
        zero = jnp.zeros((RWKV_HEAD, RWKV_HEAD), F32)
        for j in range(n_seq):
            for p in range(N_PAIRS):
                s_scr[j, p] = jnp.concatenate(
                    [jnp.concatenate([s0_ref[j, 2 * p], zero], axis=1),
                     jnp.concatenate([zero, s0_ref[j, 2 * p + 1]], axis=1)], axis=0)

    lane = lax.broadcasted_iota(jnp.int32, (1, HEAD_PAIR), 1)
    head_masks = (lane < RWKV_HEAD, lane >= RWKV_HEAD)
    trow = lax.broadcasted_iota(jnp.int32, (c, 2 * c), 0)
    tcol = lax.broadcasted_iota(jnp.int32, (c, 2 * c), 1)
    left_half = tcol < c
    scol = jnp.where(left_half, tcol, tcol - c)
    strict = scol < trow
    strict_left = strict & left_half
    strict_right = strict & (tcol >= c)
    incl = scol <= trow
    eye2 = (scol == trow).astype(F32)
    drow = lax.broadcasted_iota(jnp.int32, (2 * c, 2 * c), 0)
    dcol = lax.broadcasted_iota(jnp.int32, (2 * c, 2 * c), 1)
    diag_blocks = jnp.bitwise_xor(drow - c, dcol - c) >= 0
    brow = lax.broadcasted_iota(jnp.int32, (HEAD_PAIR, HEAD_PAIR), 0)
    bcol = lax.broadcasted_iota(jnp.int32, (HEAD_PAIR, HEAD_PAIR), 1)
    same_head = jnp.bitwise_xor(brow - RWKV_HEAD, bcol - RWKV_HEAD) >= 0
    n_double = max(int(math.log2(c)) - 1, 0)
    tri = tri_ref[...]
    zeros_c = jnp.zeros((c, HEAD_PAIR), F32)

    def pair_chunk(lw, r, k, v, na, bb, gate, sp, cols):
        lw_hi, lw_lo = _split2(lw)
        cl = jnp.dot(tri, lw_hi, preferred_element_type=F32) + jnp.dot(tri, lw_lo, preferred_element_type=F32)
        yield
        e_in = jnp.exp(cl)
        e_neg = jnp.exp(-cl)
        g_last = jnp.exp(jnp.sum(lw, axis=0, keepdims=True))
        at = na * jnp.exp(cl - lw)
        rt = r * e_in
        bt = bb * e_neg
        kt = k * e_neg
        ar = jnp.concatenate([at, rt], axis=0)
        bk = jnp.concatenate([bt, kt], axis=0)
        a_s = _bdot_nt(ar, sp)
        g0 = _bdot_nt(jnp.where(head_masks[0], ar, 0.0), bk)
        g1 = _bdot_nt(jnp.where(head_masks[1], ar, 0.0), jnp.concatenate([kt, bt], axis=0))
        yield
        ga0, ga1 = g0[0:c, :], g1[0:c, :]
        low = jnp.where(strict, jnp.where(left_half, ga0, ga1), 0.0)
        ak0 = _bdot(jnp.where(strict_right, ga0, 0.0), jnp.concatenate([zeros_c, v], axis=0))
        ak1 = _bdot(jnp.where(strict_left, ga1, 0.0), jnp.concatenate([v, zeros_c], axis=0))
        rhs = jnp.concatenate([jnp.where(head_masks[0], a_s[0:c, :] + ak0, 0.0),
                               jnp.where(head_masks[1], a_s[0:c, :] + ak1, 0.0)], axis=0)
        blockdiag = lambda x: jnp.where(diag_blocks, jnp.concatenate([x, x], axis=0), 0.0)
        t_inv = eye2 + low
        pw = low
        if n_double > 0:
            pw = _bdot(pw, blockdiag(pw))
            yield
        for level in range(n_double):
            if level == n_double - 1:
                t_inv = t_inv + _bdot(t_inv, blockdiag(pw))
            else:
                both = _bdot(jnp.concatenate([t_inv, pw], axis=0), blockdiag(pw))
                t_inv = t_inv + both[0:c, :]
                pw = both[c:2 * c, :]
            yield
        u = _bdot(t_inv, rhs)
        yield
        uv = jnp.concatenate([u, v], axis=0)
        y = a_s[c:2 * c, :] + jnp.where(
            head_masks[0], _bdot(jnp.where(incl, g0[c:2 * c, :], 0.0), uv),
            _bdot(jnp.where(incl, g1[c:2 * c, :], 0.0), jnp.concatenate([v, u], axis=0)))
        upd = _bdot_tn(uv, bk * g_last)
        s_new = sp * g_last + jnp.where(same_head, upd, 0.0)
        yield
        stats = _dot_exact_rhs(jnp.concatenate([y, r * k * rk_ref[:, cols]], axis=0), mean_ref[...])
        mu = stats[0:c, :]
        bonus = stats[c:2 * c, :] * float(RWKV_HEAD) * v
        yield
        d = y - mu
        var = _dot_exact_rhs(d * d, mean_ref[...])
        yield
        yn = d * lax.rsqrt(var + GN_EPS) * lnw_ref[:, cols] + lnb_ref[:, cols]
        return (yn + bonus) * gate, s_new

    first_seq = pl.program_id(0) * n_seq

    def seq_group(gi, carry):
        units = []
        for jj in range(group):
            j = gi * group + jj
            rows = pl.ds(first_seq + j, c, stride=stride)
            for p in range(N_PAIRS):
                cols = slice(p * HEAD_PAIR, (p + 1) * HEAD_PAIR)
                vals = [ref[p, rows, :] for ref in (lw_ref, r_ref, k_ref, v_ref, na_ref, bb_ref, gate_ref)]
                units.append((j, p, rows, pair_chunk(*vals, s_scr[j, p], cols)))
        results = _run_lockstep([gen for (_, _, _, gen) in units])
        for (j, p, rows, _), (y, s_new) in zip(units, results):
            y_ref[p, rows, :] = y
            s_scr[j, p] = s_new
        return carry

    lax.fori_loop(0, n_seq // group, seq_group, 0)

    @pl.when(li == pl.num_programs(1) - 1)
    def _():
        for j in range(n_seq):
            for p in range(N_PAIRS):
                sp = s_scr[j, p]
                sout_ref[j, 2 * p] = sp[0:RWKV_HEAD, 0:RWKV_HEAD]
                sout_ref[j, 2 * p + 1] = sp[RWKV_HEAD:HEAD_PAIR, RWKV_HEAD:HEAD_PAIR]


WKV_SEQ_PER_STEP = 8
WKV_SEQ_GROUP = 8


def _wkv(arrs, s0, s_all, layer, p, nb, seq):
    c = min(WKV_CHUNK, seq)
    n_seq = nb if seq > c else min(nb, WKV_SEQ_PER_STEP)
    assert seq % c == 0 and c % 8 == 0 and nb % n_seq == 0 and n_seq % WKV_SEQ_GROUP == 0
    blk = pl.BlockSpec((N_PAIRS, c * nb, HEAD_PAIR), lambda g, l: (0, l, 0))
    tri = jnp.tril(jnp.ones((c, c), F32)).astype(BF16)
    st_spec = pl.BlockSpec((None, n_seq, RWKV_HEADS, RWKV_HEAD, RWKV_HEAD), lambda g, l: (layer, g, 0, 0, 0))
    consts = [p["r_k"], p["ln_x_w"], p["ln_x_b"], tri, p["mean_pair"]]
    n_in = 8 + len(consts)
    return pl.pallas_call(
        functools.partial(_wkv_kernel, c, n_seq, WKV_SEQ_GROUP, nb),
        grid=(nb // n_seq, seq // c),
        in_specs=[blk] * 7 + [st_spec] + [_spec(a) for a in consts] + [pl.BlockSpec(memory_space=pl.ANY)],
        out_specs=[blk, st_spec],
        out_shape=[jax.ShapeDtypeStruct((N_PAIRS, seq * nb, HEAD_PAIR), F32), jax.ShapeDtypeStruct(s_all.shape, F32)],
        input_output_aliases={n_in: 1},
        scratch_shapes=[pltpu.VMEM((n_seq, N_PAIRS, HEAD_PAIR, HEAD_PAIR), F32)],
        compiler_params=_params("arbitrary", "arbitrary"),
        name="wkv",
    )(*arrs, s0, *_arrays(consts), s_all)


def _lru_kernel(reset_first, nb, tm, zx_ref, zg_ref, conv0_ref, h0_ref, cw_ref, cb_ref, wa_ref, ba_ref,
                wi_ref, bi_ref, lam_ref, y_ref, conv_out_ref, h_out_ref, xbuf, abuf, bbuf, hcar):
    i = pl.program_id(0)
    hist = (CONV_W - 1) * nb

    @pl.when(i == 0)
    def _():
        xbuf[0:hist, :] = conv0_ref[...]
        hcar[...] = h0_ref[...]

    xbuf[hist:hist + tm, :] = zx_ref[...]
    xc = cb_ref[...]
    for j in range(CONV_W):
        xc = xc + xbuf[j * nb:j * nb + tm, :] * cw_ref[j:j + 1, :]
    conv_out_ref[...] = xbuf[tm:tm + hist, :]
    xbuf[0:hist, :] = xbuf[tm:tm + hist, :]

    gate_a = _sigmoid(_bdot(xc, wa_ref[...]) + ba_ref[...])
    gate_i = _sigmoid(_bdot(xc, wi_ref[...]) + bi_ref[...])
    log_a = -LRU_C * gate_a * _softplus(-lam_ref[...])
    a_sq = jnp.exp(2.0 * log_a)
    mult = jnp.sqrt(-jnp.tanh(log_a) * (a_sq + 1.0))
    if reset_first:
        row = lax.broadcasted_iota(jnp.int32, (tm, 1), 0)
        mult = jnp.where(row < jnp.where(i == 0, nb, 0), 1.0, mult)
    abuf[...] = jnp.exp(log_a)
    bbuf[...] = mult * gate_i * xc

    def step(t, h):
        rows = pl.ds(pl.multiple_of(t * nb, nb), nb)
        h = abuf[rows, :] * h + bbuf[rows, :]
        bbuf[rows, :] = h
        return h

    h = lax.fori_loop(0, tm // nb, step, hcar[...])
    hcar[...] = h
    h_out_ref[...] = h
    y_ref[...] = bbuf[...] * _gelu(zg_ref[...])


def _lru(zrest, conv0, h0, p, reset_first, nb):
    m = zrest.shape[0]
    tm = _tile(m, max(nb, 8), 512)
    hist = (CONV_W - 1) * nb
    consts = [conv0, h0, p["conv_w"], p["conv_b"], p["lru_wa"], p["lru_ba"], p["lru_wi"], p["lru_bi"],
              p["lru_lambda"]]
    return pl.pallas_call(
        functools.partial(_lru_kernel, reset_first, nb, tm),
        grid=(m // tm,),
        in_specs=[pl.BlockSpec((tm, LRU_DIM), lambda i: (i, 0)),
                  pl.BlockSpec((tm, LRU_DIM), lambda i: (i, 1))] + [_spec(a) for a in consts],
        out_specs=[pl.BlockSpec((tm, LRU_DIM), lambda i: (i, 0)), _const_spec((hist, LRU_DIM)),
                   _const_spec((nb, LRU_DIM))],
        out_shape=[jax.ShapeDtypeStruct((m, LRU_DIM), F32), jax.ShapeDtypeStruct((hist, LRU_DIM), F32),
                   jax.ShapeDtypeStruct((nb, LRU_DIM), F32)],
        scratch_shapes=[pltpu.VMEM((tm + hist, LRU_DIM), F32), pltpu.VMEM((tm, LRU_DIM), F32),
                        pltpu.VMEM((tm, LRU_DIM), F32), pltpu.VMEM((nb, LRU_DIM), F32)],
        compiler_params=_params("arbitrary"),
        name="rg_lru",
    )(zrest, zrest, *_arrays(consts))


def _s5_kernel(nb, tm, u_ref, re0_ref, im0_ref, are_ref, aim_ref, bre_ref, bim_ref, cre_ref, cim_ref,
               d_ref, wglu_ref, bglu_ref, y_ref, re_out_ref, im_out_ref, xre, xim, car_re, car_im):
    @pl.when(pl.program_id(0) == 0)
    def _():
        car_re[...] = re0_ref[...]
        car_im[...] = im0_ref[...]

    u = u_ref[...]
    xre[...] = _bdot(u, bre_ref[...])
    xim[...] = _bdot(u, bim_ref[...])
    a_re = jnp.broadcast_to(are_ref[...], (nb, S5_WIDTH))
    a_im = jnp.broadcast_to(aim_ref[...], (nb, S5_WIDTH))

    def step(t, carry):
        s_re, s_im = carry
        rows = pl.ds(pl.multiple_of(t * nb, nb), nb)
        n_re = a_re * s_re - a_im * s_im + xre[rows, :]
        n_im = a_re * s_im + a_im * s_re + xim[rows, :]
        xre[rows, :] = n_re
        xim[rows, :] = n_im
        return n_re, n_im

    s_re, s_im = lax.fori_loop(0, tm // nb, step, (car_re[...], car_im[...]), unroll=min(4, tm // nb))
    car_re[...] = s_re
    car_im[...] = s_im
    re_out_ref[...] = s_re
    im_out_ref[...] = s_im

    y = _bdot(xre[...], cre_ref[...]) - _bdot(xim[...], cim_ref[...]) + d_ref[...] * u
    o = _bdot(_gelu(y), wglu_ref[...]) + bglu_ref[...]
    y_ref[...] = o[:, 0:S5_DIM] * _sigmoid(o[:, S5_DIM:2 * S5_DIM])


def _s5(zrest, re0, im0, p, nb):
    m = zrest.shape[0]
    tm = _tile(m, max(nb, 8), 512)
    consts = [re0, im0, p["s5_abar_re"], p["s5_abar_im"], p["s5_bbar_re"], p["s5_bbar_im"],
              p["s5_c_re"], p["s5_c_im"], p["s5_d"], p["s5_w_glu"], p["s5_b_glu"]]
    st = jax.ShapeDtypeStruct((nb, S5_WIDTH), F32)
    return pl.pallas_call(
        functools.partial(_s5_kernel, nb, tm),
        grid=(m // tm,),
        in_specs=[pl.BlockSpec((tm, S5_DIM), lambda i: (i, 2))] + [_spec(a) for a in consts],
        out_specs=[pl.BlockSpec((tm, S5_DIM), lambda i: (i, 0)), _const_spec((nb, S5_WIDTH)),
                   _const_spec((nb, S5_WIDTH))],
        out_shape=[jax.ShapeDtypeStruct((m, S5_DIM), F32), st, st],
        scratch_shapes=[pltpu.VMEM((tm, S5_WIDTH), F32), pltpu.VMEM((tm, S5_WIDTH), F32),
                        pltpu.VMEM((nb, S5_WIDTH), F32), pltpu.VMEM((nb, S5_WIDTH), F32)],
        compiler_params=_params("arbitrary"),
        name="s5",
    )(zrest, *_arrays(consts))


def _mix_out_kernel(x_ref, ya_ref, yb_ref, yc_ref, wa_ref, wb_ref, wc_ref, gq_ref, wq_ref, x1_ref, q_ref):
    ya = jnp.concatenate([ya_ref[p] for p in range(N_PAIRS)], axis=-1)
    x1 = (x_ref[...] + _bdot(ya, wa_ref[...]) + _bdot(yb_ref[...], wb_ref[...])
          + _bdot(yc_ref[...], wc_ref[...]))
    q = _bdot(_rms(x1, gq_ref[...]), wq_ref[...])
    for cb in range(LANE_BLOCKS):
        cols = slice(cb * LANES, (cb + 1) * LANES)
        x1_ref[cb] = x1[:, cols]
        q_ref[cb] = q[:, cols]


def _mix_out(x, ya, yb, yc, p):
    m = x.shape[0]
    tm = _tile(m, 8, 512)
    row = lambda w: pl.BlockSpec((tm, w), lambda i: (i, 0))
    split = pl.BlockSpec((LANE_BLOCKS, tm, LANES), lambda i: (0, i, 0))
    consts = [p["w_out_a"], p["w_out_b"], p["w_out_c"], p["norm_mem_q"], p["mem_wq"]]
    out = jax.ShapeDtypeStruct((LANE_BLOCKS, m, LANES), F32)
    return pl.pallas_call(
        _mix_out_kernel,
        grid=(m // tm,),
        in_specs=[row(D_MODEL), pl.BlockSpec((N_PAIRS, tm, HEAD_PAIR), lambda i: (0, i, 0)), row(LRU_DIM),
                  row(S5_DIM)] + [_spec(a) for a in consts],
        out_specs=[split, split],
        out_shape=[out, out],
        compiler_params=_params("parallel"),
        name="mix_out",
    )(x, ya, yb, yc, *_arrays(consts))


def _load_seq_rows(ref, rows):
    return jnp.concatenate([ref[cb, rows, :] for cb in range(LANE_BLOCKS)], axis=-1)


def _store_seq_rows(ref, rows, val):
    for cb in range(LANE_BLOCKS):
        ref[cb, rows, :] = val[:, cb * LANES:(cb + 1) * LANES]


def _softmax_rows(s):
    e = jnp.exp(s - jnp.max(s, axis=-1, keepdims=True))
    return e / jnp.sum(e, axis=-1, keepdims=True)


def _attn_kernel(tl, n_seq, stride, q_ref, x_ref, mk_ref, mv_ref, wo_ref, o_ref):
    scale = MEM_HEAD_DIM ** -0.5
    first_seq = pl.program_id(0) * n_seq
    head_cols = [slice(hh * MEM_HEAD_DIM, (hh + 1) * MEM_HEAD_DIM) for hh in range(MEM_HEADS)]

    def one_seq(j):
        rows = pl.ds(first_seq + j, tl, stride=stride)
        q = _load_seq_rows(q_ref, rows)
        scores = [_bdot_nt(q[:, cols], mk_ref[j, :, cols]) * scale for cols in head_cols]
        yield
        heads = [_bdot(_softmax_rows(s), mv_ref[j, :, cols]) for s, cols in zip(scores, head_cols)]
        yield
        out = _load_seq_rows(x_ref, rows) + _bdot(jnp.concatenate(heads, axis=-1), wo_ref[...])
        yield
        _store_seq_rows(o_ref, rows, out)

    for j0 in range(0, n_seq, ATTN_LOCKSTEP):
        _run_lockstep([one_seq(j) for j in range(j0, min(j0 + ATTN_LOCKSTEP, n_seq))])


def _attn_cache_kernel(tl, n_seq, stride, q_ref, x_ref, mk_ref, mv_ref, wo_ref, o_ref):
    scale = MEM_HEAD_DIM ** -0.5
    first_seq = pl.program_id(0) * n_seq
    qrow = lax.broadcasted_iota(jnp.int32, (MEM_HEADS * tl, N_MEM * MEM_HEADS), 0)
    kcol = lax.broadcasted_iota(jnp.int32, (MEM_HEADS * tl, N_MEM * MEM_HEADS), 1)
    q_head = jnp.zeros_like(qrow)
    for hh in range(1, MEM_HEADS):
        q_head = q_head + (qrow >= hh * tl).astype(jnp.int32)
    same_head = q_head == jnp.bitwise_and(kcol, MEM_HEADS - 1)

    def one_seq(j):
        rows = pl.ds(first_seq + j, tl, stride=stride)
        q = _load_seq_rows(q_ref, rows)
        qh = jnp.concatenate([q[:, hh * MEM_HEAD_DIM:(hh + 1) * MEM_HEAD_DIM] for hh in range(MEM_HEADS)], axis=0)
        k2 = mk_ref[j].reshape(N_MEM * MEM_HEADS, MEM_HEAD_DIM)
        s = jnp.where(same_head, _bdot_nt(qh, k2) * scale, MASKED_SCORE)
        yield
        v2 = mv_ref[j].reshape(N_MEM * MEM_HEADS, MEM_HEAD_DIM)
        oh = _bdot(_softmax_rows(s), v2)
        yield
        o = jnp.concatenate([oh[hh * tl:(hh + 1) * tl, :] for hh in range(MEM_HEADS)], axis=-1)
        out = _load_seq_rows(x_ref, rows) + _bdot(o, wo_ref[...])
        yield
        _store_seq_rows(o_ref, rows, out)

    _run_lockstep([one_seq(j) for j in range(n_seq)])


MASKED_SCORE = -1e30
ATTN_SEQ_PER_STEP = 4
ATTN_ROWS = 128
ATTN_LOCKSTEP = 4


def _attn(q, x1, mk, mv, kv_offset, wo, nb, seq):
    tl = _tile(seq, 8, ATTN_ROWS)
    n_seq = nb if seq > tl else min(nb, ATTN_SEQ_PER_STEP)
    assert nb % n_seq == 0 and kv_offset % n_seq == 0 and MEM_HEADS & (MEM_HEADS - 1) == 0
    blk = pl.BlockSpec((LANE_BLOCKS, tl * nb, LANES), lambda g, l: (0, l, 0))
    kv_index = lambda g, l: (kv_offset // n_seq + g,) + (0,) * (mk.ndim - 1)
    kv = pl.BlockSpec((n_seq,) + mk.shape[1:], kv_index)
    body = _attn_kernel if mk.ndim == 3 else _attn_cache_kernel
    return pl.pallas_call(
        functools.partial(body, tl, n_seq, nb),
        grid=(nb // n_seq, seq // tl),
        in_specs=[blk, blk, kv, kv, _spec(wo)],
        out_specs=blk,
        out_shape=jax.ShapeDtypeStruct((LANE_BLOCKS, seq * nb, LANES), F32),
        compiler_params=_params("arbitrary", "arbitrary"),
        name="mem_attn",
    )(q, x1, mk, mv, *_arrays([wo]))


def _ffn_kernel(d_ff, final_norm, n_seq, x_ref, g_ref, wup_ref, wdown_ref, *rest):
    x = jnp.concatenate([x_ref[cb] for cb in range(LANE_BLOCKS)], axis=-1)
    h = _rms(x, g_ref[...]).astype(BF16)
    gt = jnp.dot(h, wup_ref[:, 0:d_ff], preferred_element_type=F32)
    up = jnp.dot(h, wup_ref[:, d_ff:2 * d_ff], preferred_element_type=F32)
    act = gt * _sigmoid(gt) * up
    out = x + _bdot(act, wdown_ref[...])
    if final_norm:
        out = _rms(out, rest[0][...])
    if n_seq:
        o_ref, os = rest[-2:]
        for cb in range(LANE_BLOCKS):
            os[cb] = out[:, cb * LANES:(cb + 1) * LANES]
        for b in range(n_seq):
            o_ref[b] = _load_seq_rows(os, pl.ds(b, out.shape[0] // n_seq, stride=n_seq))
    else:
        rest[-1][...] = out


def _ffn(x, p, g_final=None, seq_major_out=None):
    m = x.shape[1]
    tm = _tile(m, 8, 512)
    d_ff = p["ffn_w_down"].shape[0]
    consts = [p["norm_ffn"], p["ffn_w_up"], p["ffn_w_down"]] + ([] if g_final is None else [g_final])
    if seq_major_out is None:
        n_seq, scratch = 0, []
        out_spec = pl.BlockSpec((tm, D_MODEL), lambda i: (i, 0))
        out_shape = jax.ShapeDtypeStruct((m, D_MODEL), F32)
    else:
        n_seq, seq = seq_major_out
        assert tm % n_seq == 0 and (tm // n_seq) % 8 == 0 and n_seq * seq == m
        scratch = [pltpu.VMEM((LANE_BLOCKS, tm, LANES), F32)]
        out_spec = pl.BlockSpec((n_seq, tm // n_seq, D_MODEL), lambda i: (0, i, 0))
        out_shape = jax.ShapeDtypeStruct((n_seq, seq, D_MODEL), F32)
    return pl.pallas_call(
        functools.partial(_ffn_kernel, d_ff, g_final is not None, n_seq),
        grid=(m // tm,),
        in_specs=[pl.BlockSpec((LANE_BLOCKS, tm, LANES), lambda i: (0, i, 0))] + [_spec(a) for a in consts],
        out_specs=out_spec,
        out_shape=out_shape,
        scratch_shapes=scratch,
        compiler_params=_params("parallel"),
        name="ffn",
    )(x, *_arrays(consts))


def _mem_project_kernel(x_ref, g_ref, wk_ref, wv_ref, k_ref, v_ref, kb_ref, vb_ref):
    m = _rms(x_ref[...], g_ref[...]).astype(BF16)
    k = jnp.dot(m, wk_ref[...], preferred_element_type=F32)
    v = jnp.dot(m, wv_ref[...], preferred_element_type=F32)
    k_ref[...] = k
    v_ref[...] = v
    kb_ref[...] = k.astype(BF16)
    vb_ref[...] = v.astype(BF16)


def _mem_project(mem, g, wk, wv):
    m = mem.shape[0]
    tm = _tile(m, 16, 512)
    row = pl.BlockSpec((tm, D_MODEL), lambda i: (i, 0))
    out = jax.ShapeDtypeStruct((m, D_MODEL), F32)
    outb = jax.ShapeDtypeStruct((m, D_MODEL), BF16)
    return pl.pallas_call(
        _mem_project_kernel, grid=(m // tm,),
        in_specs=[row, _spec(g), _spec(wk), _spec(wv)],
        out_specs=[row] * 4, out_shape=[out, out, outb, outb], compiler_params=_params("parallel"),
        name="mem_project",
    )(mem, *_arrays([g, wk, wv]))


def _block_diag(blocks):
    n, r, c = blocks.shape[-3:]
    eye = jnp.eye(n, dtype=blocks.dtype)
    out = eye[:, None, :, None] * blocks[..., :, :, None, :]
    return out.reshape(blocks.shape[:-3] + (n * r, n * c))


def _stack_params(w):
    depth = w["norm_mix"].shape[0]
    row = lambda a: a.reshape(a.shape[0], 1, -1).astype(F32)
    bf = lambda a: a.astype(BF16)
    t = lambda a: jnp.swapaxes(a, -1, -2)
    p = {k: row(w[k]) for k in ("norm_mix", "mu_rkv", "w0", "a0", "k_k", "k_a", "r_k", "ln_x_w", "ln_x_b", "conv_b",
                                "lru_ba", "lru_bi", "lru_lambda", "s5_d", "s5_b_glu", "norm_mem_q", "norm_mem_kv",
                                "norm_ffn", "mu_v", "v0")}
    p.update({k: bf(w[k]) for k in ("w_in", "w1", "w2", "a1", "a2", "g1", "g2", "s5_w_glu", "mem_wq", "mem_wk",
                                    "mem_wv", "mem_wo", "ffn_w_up", "ffn_w_down", "v1", "v2")})
    p["mu_wag"], p["conv_w"] = w["mu_wag"], w["conv_w"]
    p["lru_wa"], p["lru_wi"] = bf(_block_diag(w["lru_wa"])), bf(_block_diag(w["lru_wi"]))
    w_out = bf(w["w_out"])
    p["w_out_a"] = w_out[:, 0:RWKV_DIM]
    p["w_out_b"] = w_out[:, RWKV_DIM:RWKV_DIM + LRU_DIM]
    p["w_out_c"] = w_out[:, RWKV_DIM + LRU_DIM:]
    lam_re, lam_im = w["s5_a_re"].astype(F32), w["s5_a_im"].astype(F32)
    dt = jnp.exp(w["s5_log_dt"].astype(F32))[..., None]
    mag = jnp.exp(lam_re * dt)
    abar_re, abar_im = mag * jnp.cos(lam_im * dt), mag * jnp.sin(lam_im * dt)
    den = lam_re * lam_re + lam_im * lam_im
    q_re = ((abar_re - 1.0) * lam_re + abar_im * lam_im) / den
    q_im = (abar_im * lam_re - (abar_re - 1.0) * lam_im) / den
    b_re, b_im = w["s5_b_re"].astype(F32), w["s5_b_im"].astype(F32)
    bbar_re = q_re[..., None] * b_re - q_im[..., None] * b_im
    bbar_im = q_re[..., None] * b_im + q_im[..., None] * b_re
    p["s5_abar_re"] = abar_re.reshape(depth, 1, S5_WIDTH)
    p["s5_abar_im"] = abar_im.reshape(depth, 1, S5_WIDTH)
    p["s5_bbar_re"] = bf(_block_diag(t(bbar_re)))
    p["s5_bbar_im"] = bf(_block_diag(t(bbar_im)))
    p["s5_c_re"] = bf(_block_diag(t(w["s5_c_re"])))
    p["s5_c_im"] = bf(_block_diag(t(w["s5_c_im"])))
    return p


def _layer_params(stacked, l, consts):
    vres_keys = ("mu_v", "v0", "v1", "v2")
    p = {k: _Layer(a, l) for k, a in stacked.items() if k not in vres_keys}
    p.update(consts)
    vres = {k: _Layer(stacked[k], l - 1) for k in vres_keys} if l > 0 else None
    return p, vres


def _time_major(a):
    return jnp.swapaxes(a, 0, 1).reshape((a.shape[0] * a.shape[1],) + a.shape[2:])


def _run_group(x, mem_k, mem_v, kv_stride, shift0, wkv0, conv0, lru0, re0, im0, reset_first, layers, norm_final,
               relayout_in_kernel):
    nb, seq, _ = x.shape
    xt = x if relayout_in_kernel else _time_major(x)
    outs = {k: [] for k in ("shift", "conv", "lru", "re", "im")}
    wkv_all = jnp.zeros_like(wkv0)
    v_first = None
    for l, (p, vres) in enumerate(layers):
        last = l == len(layers) - 1
        mixed = _mix_in(xt, _Layer(shift0, l), p, vres, v_first, nb, seq)
        if xt.ndim == 3:
            xt, mixed = mixed[0], mixed[1:]
        zrest, r, lw, k, v, na, bb, gate, h_last = mixed
        if l == 0:
            v_first = v
        ya, wkv_all = _wkv((r, lw, k, v, na, bb, gate), wkv0, wkv_all, l, p, nb, seq)
        yb, conv_new, lru_new = _lru(zrest, _time_major(conv0[l]), _Layer(lru0, l), p, reset_first, nb)
        yc, re_new, im_new = _s5(zrest, _Layer(re0.reshape(-1, nb, S5_WIDTH), l),
                                 _Layer(im0.reshape(-1, nb, S5_WIDTH), l), p, nb)
        x1, q = _mix_out(xt, ya, yb, yc, p)
        x2 = _attn(q, x1, mem_k, mem_v, l * kv_stride, p["mem_wo"], nb, seq)
        xt = _ffn(x2, p, norm_final if last else None, (nb, seq) if last and relayout_in_kernel else None)
        outs["shift"].append(h_last)
        outs["conv"].append(jnp.swapaxes(conv_new.reshape(CONV_W - 1, nb, LRU_DIM), 0, 1))
        outs["lru"].append(lru_new)
        outs["re"].append(re_new.reshape(nb, S5_GROUPS, S5_STATE))
        outs["im"].append(im_new.reshape(nb, S5_GROUPS, S5_STATE))
    y = xt if relayout_in_kernel else jnp.swapaxes(xt.reshape(seq, nb, D_MODEL), 0, 1)
    st = {k: jnp.stack(v) for k, v in outs.items()}
    return (y, st["shift"], wkv_all, st["conv"], st["lru"], st["re"], st["im"])


def kernel(x_prompt, x_sample, mem_prompt, state_shift, state_wkv, state_conv, state_lru, state_s5_re, state_s5_im, cache_mem_k, cache_mem_v, norm_mix, w_in, w_out, mu_rkv, mu_wag, mu_v, w0, w1, w2, a0, a1, a2, v0, v1, v2, g1, g2, k_k, k_a, r_k, ln_x_w, ln_x_b, conv_w, conv_b, lru_wa, lru_ba, lru_wi, lru_bi, lru_lambda, s5_a_re, s5_a_im, s5_log_dt, s5_b_re, s5_b_im, s5_c_re, s5_c_im, s5_d, s5_w_glu, s5_b_glu, norm_mem_q, norm_mem_kv, mem_wq, mem_wk, mem_wv, mem_wo, norm_ffn, ffn_w_up, ffn_w_down, norm_final):
    w = dict(norm_mix=norm_mix, w_in=w_in, w_out=w_out, mu_rkv=mu_rkv, mu_wag=mu_wag, mu_v=mu_v, w0=w0, w1=w1,
             w2=w2, a0=a0, a1=a1, a2=a2, v0=v0, v1=v1, v2=v2, g1=g1, g2=g2, k_k=k_k, k_a=k_a, r_k=r_k,
             ln_x_w=ln_x_w, ln_x_b=ln_x_b, conv_w=conv_w, conv_b=conv_b, lru_wa=lru_wa, lru_ba=lru_ba,
             lru_wi=lru_wi, lru_bi=lru_bi, lru_lambda=lru_lambda, s5_a_re=s5_a_re, s5_a_im=s5_a_im,
             s5_log_dt=s5_log_dt, s5_b_re=s5_b_re, s5_b_im=s5_b_im, s5_c_re=s5_c_re, s5_c_im=s5_c_im,
             s5_d=s5_d, s5_w_glu=s5_w_glu, s5_b_glu=s5_b_glu, norm_mem_q=norm_mem_q, norm_mem_kv=norm_mem_kv,
             mem_wq=mem_wq, mem_wk=mem_wk, mem_wv=mem_wv, mem_wo=mem_wo, norm_ffn=norm_ffn, ffn_w_up=ffn_w_up,
             ffn_w_down=ffn_w_down)
    depth = norm_mix.shape[0]
    ones64 = jnp.ones((RWKV_HEADS, RWKV_HEAD, RWKV_HEAD), F32)
    consts = {"ones_head": _block_diag(ones64).astype(BF16),
              "mean_pair": (_block_diag(ones64[:2]) / RWKV_HEAD).astype(BF16)}
    stacked = _stack_params(w)
    layers = [_layer_params(stacked, l, consts) for l in range(depth)]
    g_final = norm_final.reshape(1, D_MODEL)

    bp, n_mem, _ = mem_prompt.shape
    mem_flat = mem_prompt.reshape(bp * n_mem, D_MODEL)
    mem_kv = [_mem_project(mem_flat, p["norm_mem_kv"], p["mem_wk"], p["mem_wv"]) for p, _ in layers]
    mem_k_p = jnp.stack([kv[0] for kv in mem_kv]).reshape(depth, bp, n_mem, MEM_HEADS, MEM_HEAD_DIM)
    mem_v_p = jnp.stack([kv[1] for kv in mem_kv]).reshape(depth, bp, n_mem, MEM_HEADS, MEM_HEAD_DIM)
    mem_k_b = jnp.stack([kv[2] for kv in mem_kv]).reshape(depth * bp, n_mem, D_MODEL)
    mem_v_b = jnp.stack([kv[3] for kv in mem_kv]).reshape(depth * bp, n_mem, D_MODEL)

    zeros = lambda *s: jnp.zeros((depth, bp) + s, F32)
    prompt = _run_group(
        x_prompt, mem_k_b, mem_v_b, bp,
        zeros(D_MODEL), zeros(RWKV_HEADS, RWKV_HEAD, RWKV_HEAD), zeros(CONV_W - 1, LRU_DIM), zeros(LRU_DIM),
        zeros(S5_GROUPS, S5_STATE), zeros(S5_GROUPS, S5_STATE), True, layers, g_final, True)
    bs = x_sample.shape[0]
    cache_shape = (depth * bs, n_mem, MEM_HEADS, MEM_HEAD_DIM)
    sample = _run_group(
        x_sample, cache_mem_k.reshape(cache_shape), cache_mem_v.reshape(cache_shape),
        bs, state_shift, state_wkv, state_conv, state_lru, state_s5_re, state_s5_im, False, layers, g_final, False)
    (y_p, sh_p, wkv_p, conv_p, lru_p, re_p, im_p) = prompt
    (y_s, sh_s, wkv_s, conv_s, lru_s, re_s, im_s) = sample
    return (y_p, y_s, sh_p, sh_s, wkv_p, wkv_s, conv_p, conv_s, lru_p, lru_s, re_p, re_s, im_p, im_s,
            mem_k_p, mem_v_p)
```

```python
import functools
import math

import jax
import jax.numpy as jnp
from jax import lax
from jax.experimental import pallas as pl
from jax.experimental.pallas import tpu as pltpu

F32 = jnp.float32
BF16 = jnp.bfloat16

D_MODEL = 1024
RWKV_DIM = 512
RWKV_HEAD = 64
RWKV_HEADS = 8
LANES = 128
LANE_BLOCKS = D_MODEL // LANES
HEAD_PAIR = 2 * RWKV_HEAD
N_PAIRS = RWKV_DIM // HEAD_PAIR
LRU_DIM = 256
LRU_BLOCKS = 4
CONV_W = 4
LRU_C = 8.0
S5_DIM = 256
S5_GROUP = 16
S5_GROUPS = 16
S5_STATE = 64
S5_WIDTH = S5_GROUPS * S5_STATE
R3 = 3 * RWKV_DIM
D_IN = R3 + 2 * LRU_DIM + S5_DIM
N_MEM = 256
MEM_HEADS = 4
MEM_HEAD_DIM = 256
RMS_EPS = 1e-6
GN_EPS = 64e-5
WKV_CHUNK = 64

V7X_VMEM_LIMIT = 56 * 1024 * 1024


def _params(*sem):
    return pltpu.CompilerParams(dimension_semantics=sem, vmem_limit_bytes=V7X_VMEM_LIMIT)


def _tile(m, mult, target):
    best = mult
    t = mult
    while t <= min(m, target):
        if m % t == 0:
            best = t
        t += mult
    assert m % best == 0, (m, mult, target)
    return best


def _const_spec(shape):
    nd = len(shape)
    return pl.BlockSpec(shape, lambda *_: (0,) * nd)


class _Layer:
    def __init__(self, arr, layer):
        self.arr, self.layer, self.shape = arr, layer, arr.shape[1:]


def _spec(a):
    if isinstance(a, _Layer):
        nd = len(a.shape)
        return pl.BlockSpec((None,) + a.shape, lambda *_: (a.layer,) + (0,) * nd)
    return _const_spec(a.shape)


def _arrays(operands):
    return [a.arr if isinstance(a, _Layer) else a for a in operands]


def _bdot(a, b):
    return jnp.dot(a.astype(BF16), b.astype(BF16), preferred_element_type=F32)


def _bdot_nt(a, b):
    return lax.dot_general(a.astype(BF16), b.astype(BF16), (((1,), (1,)), ((), ())),
                           preferred_element_type=F32)


def _bdot_tn(a, b):
    return lax.dot_general(a.astype(BF16), b.astype(BF16), (((0,), (0,)), ((), ())),
                           preferred_element_type=F32)


def _split2(x):
    hi = x.astype(BF16)
    lo = (x - hi.astype(F32)).astype(BF16)
    return hi, lo


def _dot_exact_rhs(x, w):
    n = x.shape[0]
    hi = x.astype(BF16).astype(F32)
    both = jnp.dot(jnp.concatenate([hi, x - hi], axis=0).astype(BF16), w, preferred_element_type=F32)
    return both[0:n, :] + both[n:2 * n, :]


def _run_lockstep(gens):
    results = {}
    while len(results) < len(gens):
        for i, gen in enumerate(gens):
            if i not in results:
                try:
                    next(gen)
                except StopIteration as stop:
                    results[i] = stop.value
    return [results[i] for i in range(len(gens))]


def _rms(x, g):
    return x * lax.rsqrt(jnp.mean(x * x, axis=-1, keepdims=True) + RMS_EPS) * g


def _softplus(x):
    return jnp.maximum(x, 0.0) + jnp.log1p(jnp.exp(-jnp.abs(x)))


def _sigmoid(x):
    return 1.0 / (1.0 + jnp.exp(-x))


def _gelu(x):
    c = math.sqrt(2.0 / math.pi)
    return x * (0.5 * (1.0 + jnp.tanh(c * (x + 0.044715 * (x * x * x)))))


def _mix_in_kernel(has_vres, seq_major_x, nb, tm, *refs):
    refs = list(refs)
    (x_ref, first_ref, g_ref, win_ref, murkv_ref, muwag_ref, w0_ref, w1_ref, w2_ref,
     a0_ref, a1_ref, a2_ref, g1_ref, g2_ref, kk_ref, ka_ref, ones_ref) = refs[:17]
    del refs[:17]
    if has_vres:
        muv_ref, v0_ref, v1_ref, v2_ref, vfirst_ref = refs[:5]
        del refs[:5]
    if seq_major_x:
        xt_ref = refs.pop(0)
    zrest_ref, r_ref, lw_ref, k_ref, v_ref, na_ref, bb_ref, gate_ref, hlast_ref, hbuf, zbuf = refs[:11]

    @pl.when(pl.program_id(0) == 0)
    def _():
        first = first_ref[...]
        hbuf[0:nb, :] = first
        zbuf[0:nb, :] = _bdot(first, win_ref[:, 0:R3])

    if seq_major_x:
        xs = refs[-1]
        for b in range(nb):
            _store_seq_rows(xs, pl.ds(b, tm // nb, stride=nb), x_ref[b])
        x = jnp.concatenate([xs[cb] for cb in range(LANE_BLOCKS)], axis=-1)
        xt_ref[...] = x
    else:
        x = x_ref[...]
    h_all = _rms(x, g_ref[...])
    hbuf[nb:nb + tm, :] = h_all
    hlast_ref[...] = h_all[tm - nb:tm, :]

    def part(lo, n):
        h = hbuf[nb + lo:nb + lo + n, :]
        dh = hbuf[lo:lo + n, :] - h
        z = _bdot(h, win_ref[...])
        p_w = _bdot(h + dh * muwag_ref[0:1, :], w1_ref[...])
        p_a = _bdot(h + dh * muwag_ref[1:2, :], a1_ref[...])
        p_g = _bdot(h + dh * muwag_ref[2:3, :], g1_ref[...])
        if has_vres:
            p_v = _bdot(h + dh * muv_ref[...], v1_ref[...])
        zrest_ref[lo:lo + n, :] = z[:, R3:D_IN]
        z_rkv = z[:, 0:R3]
        zbuf[nb + lo:nb + lo + n, :] = z_rkv
        yield
        zmix = z_rkv + (zbuf[lo:lo + n, :] - z_rkv) * murkv_ref[...]
        r = zmix[:, 0:RWKV_DIM]
        k = zmix[:, RWKV_DIM:2 * RWKV_DIM]
        v = zmix[:, 2 * RWKV_DIM:R3]
        kk = k * kk_ref[...]
        w_pre = w0_ref[...] + _bdot(jnp.tanh(p_w), w2_ref[...])
        a = _sigmoid(a0_ref[...] + _bdot(p_a, a2_ref[...]))
        gate = _bdot(_sigmoid(p_g), g2_ref[...])
        ss = _bdot(kk * kk, ones_ref[...])
        if has_vres:
            mix = _sigmoid(v0_ref[...] + _bdot(p_v, v2_ref[...]))
            v_first = jnp.concatenate([vfirst_ref[p, lo:lo + n, :] for p in range(N_PAIRS)], axis=-1)
            v = v + (v_first - v) * mix
        yield
        w_log = -_softplus(-w_pre) - 0.5
        kk = kk / jnp.maximum(jnp.sqrt(ss), 1e-12)
        outs = ((r_ref, r), (lw_ref, -jnp.exp(w_log)), (k_ref, k * (1.0 + (a - 1.0) * ka_ref[...])), (v_ref, v),
                (na_ref, -kk), (bb_ref, kk * a), (gate_ref, gate))
        for ref, val in outs:
            for p in range(N_PAIRS):
                ref[p, lo:lo + n, :] = val[:, p * HEAD_PAIR:(p + 1) * HEAD_PAIR]

    n_part = tm // MIX_IN_PARTS
    _run_lockstep([part(i * n_part, n_part) for i in range(MIX_IN_PARTS)])
    hbuf[0:nb, :] = hbuf[tm:tm + nb, :]
    zbuf[0:nb, :] = zbuf[tm:tm + nb, :]


MIX_IN_PARTS = 2


def _mix_in(x, first, p, vres, v_first, nb, seq):
    m = nb * seq
    tm = _tile(m, max(nb, 8) * MIX_IN_PARTS, 512)
    has_vres = vres is not None
    seq_major_x = x.ndim == 3
    row = lambda w: pl.BlockSpec((tm, w), lambda i: (i, 0))
    pair_row = pl.BlockSpec((N_PAIRS, tm, HEAD_PAIR), lambda i: (0, i, 0))
    ins = [x, first, p["norm_mix"], p["w_in"], p["mu_rkv"], p["mu_wag"], p["w0"], p["w1"], p["w2"],
           p["a0"], p["a1"], p["a2"], p["g1"], p["g2"], p["k_k"], p["k_a"], p["ones_head"]]
    x_spec = pl.BlockSpec((nb, tm // nb, D_MODEL), lambda i: (0, i, 0)) if seq_major_x else row(D_MODEL)
    specs = [x_spec] + [_spec(a) for a in ins[1:]]
    if has_vres:
        extra = [vres["mu_v"], vres["v0"], vres["v1"], vres["v2"]]
        ins += extra + [v_first]
        specs += [_spec(a) for a in extra] + [pair_row]
    wide = jax.ShapeDtypeStruct((N_PAIRS, m, HEAD_PAIR), F32)
    outs = [jax.ShapeDtypeStruct((m, D_IN - R3), F32)] + [wide] * 7 + [jax.ShapeDtypeStruct((nb, D_MODEL), F32)]
    out_specs = [row(D_IN - R3)] + [pair_row] * 7 + [_const_spec((nb, D_MODEL))]
    scratch = [pltpu.VMEM((tm + nb, D_MODEL), F32), pltpu.VMEM((tm + nb, R3), F32)]
    if seq_major_x:
        assert (tm // nb) % 8 == 0
        outs.insert(0, jax.ShapeDtypeStruct((m, D_MODEL), F32))
        out_specs.insert(0, row(D_MODEL))
        scratch.append(pltpu.VMEM((LANE_BLOCKS, tm, LANES), F32))
    return pl.pallas_call(
        functools.partial(_mix_in_kernel, has_vres, seq_major_x, nb, tm),
        grid=(m // tm,),
        in_specs=specs,
        out_specs=out_specs,
        out_shape=outs,
        scratch_shapes=scratch,
        compiler_params=_params("arbitrary"),
        name="mix_in",
    )(*_arrays(ins))


def _wkv_kernel(c, n_seq, group, stride, r_ref, lw_ref, k_ref, v_ref, na_ref, bb_ref, gate_ref, s0_ref,
                rk_ref, lnw_ref, lnb_ref, tri_ref, mean_ref,
                y_ref, sout_ref, s_scr):
    li = pl.program_id(1)

    @pl.when(li == 0)
    def _():
        zero = jnp.zeros((RWKV_HEAD, RWKV_HEAD), F32)
        for j in range(n_seq):
            for p in range(N_PAIRS):
                s_scr[j, p] = jnp.concatenate(
                    [jnp.concatenate([s0_ref[j, 2 * p], zero], axis=1),
                     jnp.concatenate([zero, s0_ref[j, 2 * p + 1]], axis=1)], axis=0)

    lane = lax.broadcasted_iota(jnp.int32, (1, HEAD_PAIR), 1)
    head_masks = (lane < RWKV_HEAD, lane >= RWKV_HEAD)
    trow = lax.broadcasted_iota(jnp.int32, (c, 2 * c), 0)
    tcol = lax.broadcasted_iota(jnp.int32, (c, 2 * c), 1)
    left_half = tcol < c
    scol = jnp.where(left_half, tcol, tcol - c)
    strict = scol < trow
    strict_left = strict & left_half
    strict_right = strict & (tcol >= c)
    incl = scol <= trow
    eye2 = (scol == trow).astype(F32)
    drow = lax.broadcasted_iota(jnp.int32, (2 * c, 2 * c), 0)
    dcol = lax.broadcasted_iota(jnp.int32, (2 * c, 2 * c), 1)
    diag_blocks = jnp.bitwise_xor(drow - c, dcol - c) >= 0
    brow = lax.broadcasted_iota(jnp.int32, (HEAD_PAIR, HEAD_PAIR), 0)
    bcol = lax.broadcasted_iota(jnp.int32, (HEAD_PAIR, HEAD_PAIR), 1)
    same_head = jnp.bitwise_xor(brow - RWKV_HEAD, bcol - RWKV_HEAD) >= 0
    n_double = max(int(math.log2(c)) - 1, 0)
    tri = tri_ref[...]
    zeros_c = jnp.zeros((c, HEAD_PAIR), F32)

    def pair_chunk(lw, r, k, v, na, bb, gate, sp, cols):
        lw_hi, lw_lo = _split2(lw)
        cl = jnp.dot(tri, lw_hi, preferred_element_type=F32) + jnp.dot(tri, lw_lo, preferred_element_type=F32)
        yield
        e_in = jnp.exp(cl)
        e_neg = jnp.exp(-cl)
        g_last = jnp.exp(jnp.sum(lw, axis=0, keepdims=True))
        at = na * jnp.exp(cl - lw)
        rt = r * e_in
        bt = bb * e_neg
        kt = k * e_neg
        ar = jnp.concatenate([at, rt], axis=0)
        bk = jnp.concatenate([bt, kt], axis=0)
        a_s = _bdot_nt(ar, sp)
        g0 = _bdot_nt(jnp.where(head_masks[0], ar, 0.0), bk)
        g1 = _bdot_nt(jnp.where(head_masks[1], ar, 0.0), jnp.concatenate([kt, bt], axis=0))
        yield
        ga0, ga1 = g0[0:c, :], g1[0:c, :]
        low = jnp.where(strict, jnp.where(left_half, ga0, ga1), 0.0)
        ak0 = _bdot(jnp.where(strict_right, ga0, 0.0), jnp.concatenate([zeros_c, v], axis=0))
        ak1 = _bdot(jnp.where(strict_left, ga1, 0.0), jnp.concatenate([v, zeros_c], axis=0))
        rhs = jnp.concatenate([jnp.where(head_masks[0], a_s[0:c, :] + ak0, 0.0),
                               jnp.where(head_masks[1], a_s[0:c, :] + ak1, 0.0)], axis=0)
        blockdiag = lambda x: jnp.where(diag_blocks, jnp.concatenate([x, x], axis=0), 0.0)
        t_inv = eye2 + low
        pw = low
        if n_double > 0:
            pw = _bdot(pw, blockdiag(pw))
            yield
        for level in range(n_double):
            if level == n_double - 1:
                t_inv = t_inv + _bdot(t_inv, blockdiag(pw))
            else:
                both = _bdot(jnp.concatenate([t_inv, pw], axis=0), blockdiag(pw))
                t_inv = t_inv + both[0:c, :]
                pw = both[c:2 * c, :]
            yield
        u = _bdot(t_inv, rhs)
        yield
        uv = jnp.concatenate([u, v], axis=0)
        y = a_s[c:2 * c, :] + jnp.where(
            head_masks[0], _bdot(jnp.where(incl, g0[c:2 * c, :], 0.0), uv),
            _bdot(jnp.where(incl, g1[c:2 * c, :], 0.0), jnp.concatenate([v, u], axis=0)))
        upd = _bdot_tn(uv, bk * g_last)
        s_new = sp * g_last + jnp.where(same_head, upd, 0.0)
        yield
        stats = _dot_exact_rhs(jnp.concatenate([y, r * k * rk_ref[:, cols]], axis=0), mean_ref[...])
        mu = stats[0:c, :]
        bonus = stats[c:2 * c, :] * float(RWKV_HEAD) * v
        yield
        d = y - mu
        var = _bdot(d * d, mean_ref[...])
        yield
        yn = d * lax.rsqrt(var + GN_EPS) * lnw_ref[:, cols] + lnb_ref[:, cols]
        return (yn + bonus) * gate, s_new

    first_seq = pl.program_id(0) * n_seq

    def seq_group(gi, carry):
        units = []
        for jj in range(group):
            j = gi * group + jj
            rows = pl.ds(first_seq + j, c, stride=stride)
            for p in range(N_PAIRS):
                cols = slice(p * HEAD_PAIR, (p + 1) * HEAD_PAIR)
                vals = [ref[p, rows, :] for ref in (lw_ref, r_ref, k_ref, v_ref, na_ref, bb_ref, gate_ref)]
                units.append((j, p, rows, pair_chunk(*vals, s_scr[j, p], cols)))
        results = _run_lockstep([gen for (_, _, _, gen) in units])
        for (j, p, rows, _), (y, s_new) in zip(units, results):
            y_ref[p, rows, :] = y
            s_scr[j, p] = s_new
        return carry

    lax.fori_loop(0, n_seq // group, seq_group, 0)

    @pl.when(li == pl.num_programs(1) - 1)
    def _():
        for j in range(n_seq):
            for p in range(N_PAIRS):
                sp = s_scr[j, p]
                sout_ref[j, 2 * p] = sp[0:RWKV_HEAD, 0:RWKV_HEAD]
                sout_ref[j, 2 * p + 1] = sp[RWKV_HEAD:HEAD_PAIR, RWKV_HEAD:HEAD_PAIR]


WKV_SEQ_PER_STEP = 8
WKV_SEQ_GROUP = 8


def _wkv(arrs, s_all, layer, p, nb, seq):
    c = min(WKV_CHUNK, seq)
    n_seq = nb if seq > c else min(nb, WKV_SEQ_PER_STEP)
    assert seq % c == 0 and c % 8 == 0 and nb % n_seq == 0 and n_seq % WKV_SEQ_GROUP == 0
    blk = pl.BlockSpec((N_PAIRS, c * nb, HEAD_PAIR), lambda g, l: (0, l, 0))
    tri = jnp.tril(jnp.ones((c, c), F32)).astype(BF16)
    st_spec = pl.BlockSpec((None, n_seq, RWKV_HEADS, RWKV_HEAD, RWKV_HEAD), lambda g, l: (layer, g, 0, 0, 0))
    consts = [p["r_k"], p["ln_x_w"], p["ln_x_b"], tri, p["mean_pair"]]
    return pl.pallas_call(
        functools.partial(_wkv_kernel, c, n_seq, WKV_SEQ_GROUP, nb),
        grid=(nb // n_seq, seq // c),
        in_specs=[blk] * 7 + [st_spec] + [_spec(a) for a in consts],
        out_specs=[blk, st_spec],
        out_shape=[jax.ShapeDtypeStruct((N_PAIRS, seq * nb, HEAD_PAIR), F32), jax.ShapeDtypeStruct(s_all.shape, F32)],
        input_output_aliases={7: 1},
        scratch_shapes=[pltpu.VMEM((n_seq, N_PAIRS, HEAD_PAIR, HEAD_PAIR), F32)],
        compiler_params=_params("arbitrary", "arbitrary"),
        name="wkv",
    )(*arrs, s_all, *_arrays(consts))


def _lru_kernel(reset_first, nb, tm, zx_ref, zg_ref, conv0_ref, h0_ref, cw_ref, cb_ref, wa_ref, ba_ref,
                wi_ref, bi_ref, lam_ref, y_ref, conv_out_ref, h_out_ref, xbuf, abuf, bbuf, hcar):
    i = pl.program_id(0)
    hist = (CONV_W - 1) * nb

    @pl.when(i == 0)
    def _():
        xbuf[0:hist, :] = conv0_ref[...]
        hcar[...] = h0_ref[...]

    xbuf[hist:hist + tm, :] = zx_ref[...]
    xc = cb_ref[...]
    for j in range(CONV_W):
        xc = xc + xbuf[j * nb:j * nb + tm, :] * cw_ref[j:j + 1, :]
    conv_out_ref[...] = xbuf[tm:tm + hist, :]
    xbuf[0:hist, :] = xbuf[tm:tm + hist, :]

    gate_a = _sigmoid(_bdot(xc, wa_ref[...]) + ba_ref[...])
    gate_i = _sigmoid(_bdot(xc, wi_ref[...]) + bi_ref[...])
    log_a = -LRU_C * gate_a * _softplus(-lam_ref[...])
    a_sq = jnp.exp(2.0 * log_a)
    mult = jnp.sqrt(-jnp.tanh(log_a) * (a_sq + 1.0))
    if reset_first:
        row = lax.broadcasted_iota(jnp.int32, (tm, 1), 0)
        mult = jnp.where(row < jnp.where(i == 0, nb, 0), 1.0, mult)
    abuf[...] = jnp.exp(log_a)
    bbuf[...] = mult * gate_i * xc

    def step(t, h):
        rows = pl.ds(pl.multiple_of(t * nb, nb), nb)
        h = abuf[rows, :] * h + bbuf[rows, :]
        bbuf[rows, :] = h
        return h

    h = lax.fori_loop(0, tm // nb, step, hcar[...])
    hcar[...] = h
    h_out_ref[...] = h
    y_ref[...] = bbuf[...] * _gelu(zg_ref[...])


def _lru(zrest, conv0, h0, p, reset_first, nb):
    m = zrest.shape[0]
    tm = _tile(m, max(nb, 8), 512)
    hist = (CONV_W - 1) * nb
    consts = [conv0, h0, p["conv_w"], p["conv_b"], p["lru_wa"], p["lru_ba"], p["lru_wi"], p["lru_bi"],
              p["lru_lambda"]]
    return pl.pallas_call(
        functools.partial(_lru_kernel, reset_first, nb, tm),
        grid=(m // tm,),
        in_specs=[pl.BlockSpec((tm, LRU_DIM), lambda i: (i, 0)),
                  pl.BlockSpec((tm, LRU_DIM), lambda i: (i, 1))] + [_spec(a) for a in consts],
        out_specs=[pl.BlockSpec((tm, LRU_DIM), lambda i: (i, 0)), _const_spec((hist, LRU_DIM)),
                   _const_spec((nb, LRU_DIM))],
        out_shape=[jax.ShapeDtypeStruct((m, LRU_DIM), F32), jax.ShapeDtypeStruct((hist, LRU_DIM), F32),
                   jax.ShapeDtypeStruct((nb, LRU_DIM), F32)],
        scratch_shapes=[pltpu.VMEM((tm + hist, LRU_DIM), F32), pltpu.VMEM((tm, LRU_DIM), F32),
                        pltpu.VMEM((tm, LRU_DIM), F32), pltpu.VMEM((nb, LRU_DIM), F32)],
        compiler_params=_params("arbitrary"),
        name="rg_lru",
    )(zrest, zrest, *_arrays(consts))


def _s5_kernel(nb, tm, u_ref, re0_ref, im0_ref, are_ref, aim_ref, bre_ref, bim_ref, cre_ref, cim_ref,
               d_ref, wglu_ref, bglu_ref, y_ref, re_out_ref, im_out_ref, xre, xim, car_re, car_im):
    @pl.when(pl.program_id(0) == 0)
    def _():
        car_re[...] = re0_ref[...]
        car_im[...] = im0_ref[...]

    u = u_ref[...]
    xre[...] = _bdot(u, bre_ref[...])
    xim[...] = _bdot(u, bim_ref[...])
    a_re = jnp.broadcast_to(are_ref[...], (nb, S5_WIDTH))
    a_im = jnp.broadcast_to(aim_ref[...], (nb, S5_WIDTH))

    def step(t, carry):
        s_re, s_im = carry
        rows = pl.ds(pl.multiple_of(t * nb, nb), nb)
        n_re = a_re * s_re - a_im * s_im + xre[rows, :]
        n_im = a_re * s_im + a_im * s_re + xim[rows, :]
        xre[rows, :] = n_re
        xim[rows, :] = n_im
        return n_re, n_im

    s_re, s_im = lax.fori_loop(0, tm // nb, step, (car_re[...], car_im[...]), unroll=min(4, tm // nb))
    car_re[...] = s_re
    car_im[...] = s_im
    re_out_ref[...] = s_re
    im_out_ref[...] = s_im

    y = _bdot(xre[...], cre_ref[...]) - _bdot(xim[...], cim_ref[...]) + d_ref[...] * u
    o = _bdot(_gelu(y), wglu_ref[...]) + bglu_ref[...]
    y_ref[...] = o[:, 0:S5_DIM] * _sigmoid(o[:, S5_DIM:2 * S5_DIM])


def _s5(zrest, re0, im0, p, nb):
    m = zrest.shape[0]
    tm = _tile(m, max(nb, 8), 512)
    consts = [re0, im0, p["s5_abar_re"], p["s5_abar_im"], p["s5_bbar_re"], p["s5_bbar_im"],
              p["s5_c_re"], p["s5_c_im"], p["s5_d"], p["s5_w_glu"], p["s5_b_glu"]]
    st = jax.ShapeDtypeStruct((nb, S5_WIDTH), F32)
    return pl.pallas_call(
        functools.partial(_s5_kernel, nb, tm),
        grid=(m // tm,),
        in_specs=[pl.BlockSpec((tm, S5_DIM), lambda i: (i, 2))] + [_spec(a) for a in consts],
        out_specs=[pl.BlockSpec((tm, S5_DIM), lambda i: (i, 0)), _const_spec((nb, S5_WIDTH)),
                   _const_spec((nb, S5_WIDTH))],
        out_shape=[jax.ShapeDtypeStruct((m, S5_DIM), F32), st, st],
        scratch_shapes=[pltpu.VMEM((tm, S5_WIDTH), F32), pltpu.VMEM((tm, S5_WIDTH), F32),
                        pltpu.VMEM((nb, S5_WIDTH), F32), pltpu.VMEM((nb, S5_WIDTH), F32)],
        compiler_params=_params("arbitrary"),
        name="s5",
    )(zrest, *_arrays(consts))


def _mix_out_kernel(x_ref, ya_ref, yb_ref, yc_ref, wa_ref, wb_ref, wc_ref, gq_ref, wq_ref, x1_ref, q_ref):
    ya = jnp.concatenate([ya_ref[p] for p in range(N_PAIRS)], axis=-1)
    x1 = (x_ref[...] + _bdot(ya, wa_ref[...]) + _bdot(yb_ref[...], wb_ref[...])
          + _bdot(yc_ref[...], wc_ref[...]))
    q = _bdot(_rms(x1, gq_ref[...]), wq_ref[...])
    for cb in range(LANE_BLOCKS):
        cols = slice(cb * LANES, (cb + 1) * LANES)
        x1_ref[cb] = x1[:, cols]
        q_ref[cb] = q[:, cols]


def _mix_out(x, ya, yb, yc, p):
    m = x.shape[0]
    tm = _tile(m, 8, 512)
    row = lambda w: pl.BlockSpec((tm, w), lambda i: (i, 0))
    split = pl.BlockSpec((LANE_BLOCKS, tm, LANES), lambda i: (0, i, 0))
    consts = [p["w_out_a"], p["w_out_b"], p["w_out_c"], p["norm_mem_q"], p["mem_wq"]]
    out = jax.ShapeDtypeStruct((LANE_BLOCKS, m, LANES), F32)
    return pl.pallas_call(
        _mix_out_kernel,
        grid=(m // tm,),
        in_specs=[row(D_MODEL), pl.BlockSpec((N_PAIRS, tm, HEAD_PAIR), lambda i: (0, i, 0)), row(LRU_DIM),
                  row(S5_DIM)] + [_spec(a) for a in consts],
        out_specs=[split, split],
        out_shape=[out, out],
        compiler_params=_params("parallel"),
        name="mix_out",
    )(x, ya, yb, yc, *_arrays(consts))


def _load_seq_rows(ref, rows):
    return jnp.concatenate([ref[cb, rows, :] for cb in range(LANE_BLOCKS)], axis=-1)


def _store_seq_rows(ref, rows, val):
    for cb in range(LANE_BLOCKS):
        ref[cb, rows, :] = val[:, cb * LANES:(cb + 1) * LANES]


def _softmax_rows(s):
    e = jnp.exp(s - jnp.max(s, axis=-1, keepdims=True))
    return e / jnp.sum(e, axis=-1, keepdims=True)


def _attn_kernel(tl, n_seq, stride, q_ref, x_ref, mk_ref, mv_ref, wo_ref, o_ref):
    scale = MEM_HEAD_DIM ** -0.5
    first_seq = pl.program_id(0) * n_seq
    head_cols = [slice(hh * MEM_HEAD_DIM, (hh + 1) * MEM_HEAD_DIM) for hh in range(MEM_HEADS)]

    def one_seq(j):
        rows = pl.ds(first_seq + j, tl, stride=stride)
        q = _load_seq_rows(q_ref, rows)
        scores = [_bdot_nt(q[:, cols], mk_ref[j, :, cols]) * scale for cols in head_cols]
        yield
        heads = [_bdot(_softmax_rows(s), mv_ref[j, :, cols]) for s, cols in zip(scores, head_cols)]
        yield
        out = _load_seq_rows(x_ref, rows) + _bdot(jnp.concatenate(heads, axis=-1), wo_ref[...])
        yield
        _store_seq_rows(o_ref, rows, out)

    for j0 in range(0, n_seq, ATTN_LOCKSTEP):
        _run_lockstep([one_seq(j) for j in range(j0, min(j0 + ATTN_LOCKSTEP, n_seq))])


def _attn_cache_kernel(tl, n_seq, stride, q_ref, x_ref, mk_ref, mv_ref, wo_ref, o_ref):
    scale = MEM_HEAD_DIM ** -0.5
    first_seq = pl.program_id(0) * n_seq
    qrow = lax.broadcasted_iota(jnp.int32, (MEM_HEADS * tl, N_MEM * MEM_HEADS), 0)
    kcol = lax.broadcasted_iota(jnp.int32, (MEM_HEADS * tl, N_MEM * MEM_HEADS), 1)
    q_head = jnp.zeros_like(qrow)
    for hh in range(1, MEM_HEADS):
        q_head = q_head + (qrow >= hh * tl).astype(jnp.int32)
    same_head = q_head == jnp.bitwise_and(kcol, MEM_HEADS - 1)

    def one_seq(j):
        rows = pl.ds(first_seq + j, tl, stride=stride)
        q = _load_seq_rows(q_ref, rows)
        qh = jnp.concatenate([q[:, hh * MEM_HEAD_DIM:(hh + 1) * MEM_HEAD_DIM] for hh in range(MEM_HEADS)], axis=0)
        k2 = mk_ref[j].reshape(N_MEM * MEM_HEADS, MEM_HEAD_DIM)
        s = jnp.where(same_head, _bdot_nt(qh, k2) * scale, MASKED_SCORE)
        yield
        v2 = mv_ref[j].reshape(N_MEM * MEM_HEADS, MEM_HEAD_DIM)
        oh = _bdot(_softmax_rows(s), v2)
        yield
        o = jnp.concatenate([oh[hh * tl:(hh + 1) * tl, :] for hh in range(MEM_HEADS)], axis=-1)
        out = _load_seq_rows(x_ref, rows) + _bdot(o, wo_ref[...])
        yield
        _store_seq_rows(o_ref, rows, out)

    _run_lockstep([one_seq(j) for j in range(n_seq)])


MASKED_SCORE = -1e30
ATTN_SEQ_PER_STEP = 4
ATTN_ROWS = 128
ATTN_LOCKSTEP = 4


def _attn(q, x1, mk, mv, kv_offset, wo, nb, seq):
    tl = _tile(seq, 8, ATTN_ROWS)
    n_seq = nb if seq > tl else min(nb, ATTN_SEQ_PER_STEP)
    assert nb % n_seq == 0 and kv_offset % n_seq == 0 and MEM_HEADS & (MEM_HEADS - 1) == 0
    blk = pl.BlockSpec((LANE_BLOCKS, tl * nb, LANES), lambda g, l: (0, l, 0))
    kv_index = lambda g, l: (kv_offset // n_seq + g,) + (0,) * (mk.ndim - 1)
    kv = pl.BlockSpec((n_seq,) + mk.shape[1:], kv_index)
    body = _attn_kernel if mk.ndim == 3 else _attn_cache_kernel
    return pl.pallas_call(
        functools.partial(body, tl, n_seq, nb),
        grid=(nb // n_seq, seq // tl),
        in_specs=[blk, blk, kv, kv, _spec(wo)],
        out_specs=blk,
        out_shape=jax.ShapeDtypeStruct((LANE_BLOCKS, seq * nb, LANES), F32),
        compiler_params=_params("arbitrary", "arbitrary"),
        name="mem_attn",
    )(q, x1, mk, mv, *_arrays([wo]))


def _ffn_kernel(d_ff, final_norm, n_seq, x_ref, g_ref, wup_ref, wdown_ref, *rest):
    x = jnp.concatenate([x_ref[cb] for cb in range(LANE_BLOCKS)], axis=-1)
    h = _rms(x, g_ref[...]).astype(BF16)
    gt = jnp.dot(h, wup_ref[:, 0:d_ff], preferred_element_type=F32)
    up = jnp.dot(h, wup_ref[:, d_ff:2 * d_ff], preferred_element_type=F32)
    act = gt * _sigmoid(gt) * up
    out = x + _bdot(act, wdown_ref[...])
    if final_norm:
        out = _rms(out, rest[0][...])
    if n_seq:
        o_ref, os = rest[-2:]
        for cb in range(LANE_BLOCKS):
            os[cb] = out[:, cb * LANES:(cb + 1) * LANES]
        for b in range(n_seq):
            o_ref[b] = _load_seq_rows(os, pl.ds(b, out.shape[0] // n_seq, stride=n_seq))
    else:
        rest[-1][...] = out


def _ffn(x, p, g_final=None, seq_major_out=None):
    m = x.shape[1]
    tm = _tile(m, 8, 512)
    d_ff = p["ffn_w_down"].shape[0]
    consts = [p["norm_ffn"], p["ffn_w_up"], p["ffn_w_down"]] + ([] if g_final is None else [g_final])
    if seq_major_out is None:
        n_seq, scratch = 0, []
        out_spec = pl.BlockSpec((tm, D_MODEL), lambda i: (i, 0))
        out_shape = jax.ShapeDtypeStruct((m, D_MODEL), F32)
    else:
        n_seq, seq = seq_major_out
        assert tm % n_seq == 0 and (tm // n_seq) % 8 == 0 and n_seq * seq == m
        scratch = [pltpu.VMEM((LANE_BLOCKS, tm, LANES), F32)]
        out_spec = pl.BlockSpec((n_seq, tm // n_seq, D_MODEL), lambda i: (0, i, 0))
        out_shape = jax.ShapeDtypeStruct((n_seq, seq, D_MODEL), F32)
    return pl.pallas_call(
        functools.partial(_ffn_kernel, d_ff, g_final is not None, n_seq),
        grid=(m // tm,),
        in_specs=[pl.BlockSpec((LANE_BLOCKS, tm, LANES), lambda i: (0, i, 0))] + [_spec(a) for a in consts],
        out_specs=out_spec,
        out_shape=out_shape,
        scratch_shapes=scratch,
        compiler_params=_params("parallel"),
        name="ffn",
    )(x, *_arrays(consts))


def _mem_project_kernel(x_ref, g_ref, wk_ref, wv_ref, k_ref, v_ref, kb_ref, vb_ref):
    m = _rms(x_ref[...], g_ref[...]).astype(BF16)
    k = jnp.dot(m, wk_ref[...], preferred_element_type=F32)
    v = jnp.dot(m, wv_ref[...], preferred_element_type=F32)
    k_ref[...] = k
    v_ref[...] = v
    kb_ref[...] = k.astype(BF16)
    vb_ref[...] = v.astype(BF16)


def _mem_project(mem, g, wk, wv):
    m = mem.shape[0]
    tm = _tile(m, 16, 512)
    row = pl.BlockSpec((tm, D_MODEL), lambda i: (i, 0))
    out = jax.ShapeDtypeStruct((m, D_MODEL), F32)
    outb = jax.ShapeDtypeStruct((m, D_MODEL), BF16)
    return pl.pallas_call(
        _mem_project_kernel, grid=(m // tm,),
        in_specs=[row, _spec(g), _spec(wk), _spec(wv)],
        out_specs=[row] * 4, out_shape=[out, out, outb, outb], compiler_params=_params("parallel"),
        name="mem_project",
    )(mem, *_arrays([g, wk, wv]))


def _block_diag(blocks):
    n, r, c = blocks.shape[-3:]
    eye = jnp.eye(n, dtype=blocks.dtype)
    out = eye[:, None, :, None] * blocks[..., :, :, None, :]
    return out.reshape(blocks.shape[:-3] + (n * r, n * c))


def _stack_params(w):
    depth = w["norm_mix"].shape[0]
    row = lambda a: a.reshape(a.shape[0], 1, -1).astype(F32)
    bf = lambda a: a.astype(BF16)
    t = lambda a: jnp.swapaxes(a, -1, -2)
    p = {k: row(w[k]) for k in ("norm_mix", "mu_rkv", "w0", "a0", "k_k", "k_a", "r_k", "ln_x_w", "ln_x_b", "conv_b",
                                "lru_ba", "lru_bi", "lru_lambda", "s5_d", "s5_b_glu", "norm_mem_q", "norm_mem_kv",
                                "norm_ffn", "mu_v", "v0")}
    p.update({k: bf(w[k]) for k in ("w_in", "w1", "w2", "a1", "a2", "g1", "g2", "s5_w_glu", "mem_wq", "mem_wk",
                                    "mem_wv", "mem_wo", "ffn_w_up", "ffn_w_down", "v1", "v2")})
    p["mu_wag"], p["conv_w"] = w["mu_wag"], w["conv_w"]
    p["lru_wa"], p["lru_wi"] = bf(_block_diag(w["lru_wa"])), bf(_block_diag(w["lru_wi"]))
    w_out = bf(w["w_out"])
    p["w_out_a"] = w_out[:, 0:RWKV_DIM]
    p["w_out_b"] = w_out[:, RWKV_DIM:RWKV_DIM + LRU_DIM]
    p["w_out_c"] = w_out[:, RWKV_DIM + LRU_DIM:]
    lam_re, lam_im = w["s5_a_re"].astype(F32), w["s5_a_im"].astype(F32)
    dt = jnp.exp(w["s5_log_dt"].astype(F32))[..., None]
    mag = jnp.exp(lam_re * dt)
    abar_re, abar_im = mag * jnp.cos(lam_im * dt), mag * jnp.sin(lam_im * dt)
    den = lam_re * lam_re + lam_im * lam_im
    q_re = ((abar_re - 1.0) * lam_re + abar_im * lam_im) / den
    q_im = (abar_im * lam_re - (abar_re - 1.0) * lam_im) / den
    b_re, b_im = w["s5_b_re"].astype(F32), w["s5_b_im"].astype(F32)
    bbar_re = q_re[..., None] * b_re - q_im[..., None] * b_im
    bbar_im = q_re[..., None] * b_im + q_im[..., None] * b_re
    p["s5_abar_re"] = abar_re.reshape(depth, 1, S5_WIDTH)
    p["s5_abar_im"] = abar_im.reshape(depth, 1, S5_WIDTH)
    p["s5_bbar_re"] = bf(_block_diag(t(bbar_re)))
    p["s5_bbar_im"] = bf(_block_diag(t(bbar_im)))
    p["s5_c_re"] = bf(_block_diag(t(w["s5_c_re"])))
    p["s5_c_im"] = bf(_block_diag(t(w["s5_c_im"])))
    return p


def _layer_params(stacked, l, consts):
    vres_keys = ("mu_v", "v0", "v1", "v2")
    p = {k: _Layer(a, l) for k, a in stacked.items() if k not in vres_keys}
    p.update(consts)
    vres = {k: _Layer(stacked[k], l - 1) for k in vres_keys} if l > 0 else None
    return p, vres


def _time_major(a):
    return jnp.swapaxes(a, 0, 1).reshape((a.shape[0] * a.shape[1],) + a.shape[2:])


def _run_group(x, mem_k, mem_v, kv_stride, shift0, wkv0, conv0, lru0, re0, im0, reset_first, layers, norm_final,
               relayout_in_kernel):
    nb, seq, _ = x.shape
    xt = x if relayout_in_kernel else _time_major(x)
    outs = {k: [] for k in ("shift", "conv", "lru", "re", "im")}
    wkv_all = wkv0
    v_first = None
    for l, (p, vres) in enumerate(layers):
        last = l == len(layers) - 1
        mixed = _mix_in(xt, _Layer(shift0, l), p, vres, v_first, nb, seq)
        if xt.ndim == 3:
            xt, mixed = mixed[0], mixed[1:]
        zrest, r, lw, k, v, na, bb, gate, h_last = mixed
        if l == 0:
            v_first = v
        ya, wkv_all = _wkv((r, lw, k, v, na, bb, gate), wkv_all, l, p, nb, seq)
        yb, conv_new, lru_new = _lru(zrest, _time_major(conv0[l]), _Layer(lru0, l), p, reset_first, nb)
        yc, re_new, im_new = _s5(zrest, _Layer(re0.reshape(-1, nb, S5_WIDTH), l),
                                 _Layer(im0.reshape(-1, nb, S5_WIDTH), l), p, nb)
        x1, q = _mix_out(xt, ya, yb, yc, p)
        x2 = _attn(q, x1, mem_k, mem_v, l * kv_stride, p["mem_wo"], nb, seq)
        xt = _ffn(x2, p, norm_final if last else None, (nb, seq) if last and relayout_in_kernel else None)
        outs["shift"].append(h_last)
        outs["conv"].append(jnp.swapaxes(conv_new.reshape(CONV_W - 1, nb, LRU_DIM), 0, 1))
        outs["lru"].append(lru_new)
        outs["re"].append(re_new.reshape(nb, S5_GROUPS, S5_STATE))
        outs["im"].append(im_new.reshape(nb, S5_GROUPS, S5_STATE))
    y = xt if relayout_in_kernel else jnp.swapaxes(xt.reshape(seq, nb, D_MODEL), 0, 1)
    st = {k: jnp.stack(v) for k, v in outs.items()}
    return (y, st["shift"], wkv_all, st["conv"], st["lru"], st["re"], st["im"])


def kernel(x_prompt, x_sample, mem_prompt, state_shift, state_wkv, state_conv, state_lru, state_s5_re, state_s5_im, cache_mem_k, cache_mem_v, norm_mix, w_in, w_out, mu_rkv, mu_wag, mu_v, w0, w1, w2, a0, a1, a2, v0, v1, v2, g1, g2, k_k, k_a, r_k, ln_x_w, ln_x_b, conv_w, conv_b, lru_wa, lru_ba, lru_wi, lru_bi, lru_lambda, s5_a_re, s5_a_im, s5_log_dt, s5_b_re, s5_b_im, s5_c_re, s5_c_im, s5_d, s5_w_glu, s5_b_glu, norm_mem_q, norm_mem_kv, mem_wq, mem_wk, mem_wv, mem_wo, norm_ffn, ffn_w_up, ffn_w_down, norm_final):
    w = dict(norm_mix=norm_mix, w_in=w_in, w_out=w_out, mu_rkv=mu_rkv, mu_wag=mu_wag, mu_v=mu_v, w0=w0, w1=w1,
             w2=w2, a0=a0, a1=a1, a2=a2, v0=v0, v1=v1, v2=v2, g1=g1, g2=g2, k_k=k_k, k_a=k_a, r_k=r_k,
             ln_x_w=ln_x_w, ln_x_b=ln_x_b, conv_w=conv_w, conv_b=conv_b, lru_wa=lru_wa, lru_ba=lru_ba,
             lru_wi=lru_wi, lru_bi=lru_bi, lru_lambda=lru_lambda, s5_a_re=s5_a_re, s5_a_im=s5_a_im,
             s5_log_dt=s5_log_dt, s5_b_re=s5_b_re, s5_b_im=s5_b_im, s5_c_re=s5_c_re, s5_c_im=s5_c_im,
             s5_d=s5_d, s5_w_glu=s5_w_glu, s5_b_glu=s5_b_glu, norm_mem_q=norm_mem_q, norm_mem_kv=norm_mem_kv,
             mem_wq=mem_wq, mem_wk=mem_wk, mem_wv=mem_wv, mem_wo=mem_wo, norm_ffn=norm_ffn, ffn_w_up=ffn_w_up,
             ffn_w_down=ffn_w_down)
    depth = norm_mix.shape[0]
    ones64 = jnp.ones((RWKV_HEADS, RWKV_HEAD, RWKV_HEAD), F32)
    consts = {"ones_head": _block_diag(ones64).astype(BF16),
              "mean_pair": (_block_diag(ones64[:2]) / RWKV_HEAD).astype(BF16)}
    stacked = _stack_params(w)
    layers = [_layer_params(stacked, l, consts) for l in range(depth)]
    g_final = norm_final.reshape(1, D_MODEL)

    bp, n_mem, _ = mem_prompt.shape
    mem_flat = mem_prompt.reshape(bp * n_mem, D_MODEL)
    mem_kv = [_mem_project(mem_flat, p["norm_mem_kv"], p["mem_wk"], p["mem_wv"]) for p, _ in layers]
    mem_k_p = jnp.stack([kv[0] for kv in mem_kv]).reshape(depth, bp, n_mem, MEM_HEADS, MEM_HEAD_DIM)
    mem_v_p = jnp.stack([kv[1] for kv in mem_kv]).reshape(depth, bp, n_mem, MEM_HEADS, MEM_HEAD_DIM)
    mem_k_b = jnp.stack([kv[2] for kv in mem_kv]).reshape(depth * bp, n_mem, D_MODEL)
    mem_v_b = jnp.stack([kv[3] for kv in mem_kv]).reshape(depth * bp, n_mem, D_MODEL)

    zeros = lambda *s: jnp.zeros((depth, bp) + s, F32)
    prompt = _run_group(
        x_prompt, mem_k_b, mem_v_b, bp,
        zeros(D_MODEL), zeros(RWKV_HEADS, RWKV_HEAD, RWKV_HEAD), zeros(CONV_W - 1, LRU_DIM), zeros(LRU_DIM),
        zeros(S5_GROUPS, S5_STATE), zeros(S5_GROUPS, S5_STATE), True, layers, g_final, True)
    bs = x_sample.shape[0]
    cache_shape = (depth * bs, n_mem, MEM_HEADS, MEM_HEAD_DIM)
    sample = _run_group(
        x_sample, cache_mem_k.reshape(cache_shape), cache_mem_v.reshape(cache_shape),
        bs, state_shift, state_wkv, state_conv, state_lru, state_s5_re, state_s5_im, False, layers, g_final, False)
    (y_p, sh_p, wkv_p, conv_p, lru_p, re_p, im_p) = prompt
    (y_s, sh_s, wkv_s, conv_s, lru_s, re_s, im_s) = sample
    return (y_p, y_s, sh_p, sh_s, wkv_p, wkv_s, conv_p, conv_s, lru_p, lru_s, re_p, re_s, im_p, im_s,
            mem_k_p, mem_v_p)
```

```python
import functools
import math

import jax
import jax.numpy as jnp
from jax import lax
from jax.experimental import pallas as pl
from jax.experimental.pallas import tpu as pltpu

F32 = jnp.float32
BF16 = jnp.bfloat16

D_MODEL = 1024
RWKV_DIM = 512
RWKV_HEAD = 64
RWKV_HEADS = 8
LANES = 128
LANE_BLOCKS = D_MODEL // LANES
HEAD_PAIR = 2 * RWKV_HEAD
N_PAIRS = RWKV_DIM // HEAD_PAIR
LRU_DIM = 256
LRU_BLOCKS = 4
CONV_W = 4
LRU_C = 8.0
S5_DIM = 256
S5_GROUP = 16
S5_GROUPS = 16
S5_STATE = 64
S5_WIDTH = S5_GROUPS * S5_STATE
R3 = 3 * RWKV_DIM
D_IN = R3 + 2 * LRU_DIM + S5_DIM
N_MEM = 256
MEM_HEADS = 4
MEM_HEAD_DIM = 256
RMS_EPS = 1e-6
GN_EPS = 64e-5
WKV_CHUNK = 64

V7X_VMEM_LIMIT = 56 * 1024 * 1024


def _params(*sem):
    return pltpu.CompilerParams(dimension_semantics=sem, vmem_limit_bytes=V7X_VMEM_LIMIT)


def _tile(m, mult, target):
    best = mult
    t = mult
    while t <= min(m, target):
        if m % t == 0:
            best = t
        t += mult
    assert m % best == 0, (m, mult, target)
    return best


def _const_spec(shape):
    nd = len(shape)
    return pl.BlockSpec(shape, lambda *_: (0,) * nd)


class _Layer:
    def __init__(self, arr, layer):
        self.arr, self.layer, self.shape = arr, layer, arr.shape[1:]


def _spec(a):
    if isinstance(a, _Layer):
        nd = len(a.shape)
        return pl.BlockSpec((None,) + a.shape, lambda *_: (a.layer,) + (0,) * nd)
    return _const_spec(a.shape)


def _arrays(operands):
    return [a.arr if isinstance(a, _Layer) else a for a in operands]


def _bdot(a, b):
    return jnp.dot(a.astype(BF16), b.astype(BF16), preferred_element_type=F32)


def _bdot_nt(a, b):
    return lax.dot_general(a.astype(BF16), b.astype(BF16), (((1,), (1,)), ((), ())),
                           preferred_element_type=F32)


def _bdot_tn(a, b):
    return lax.dot_general(a.astype(BF16), b.astype(BF16), (((0,), (0,)), ((), ())),
                           preferred_element_type=F32)


def _split2(x):
    hi = x.astype(BF16)
    lo = (x - hi.astype(F32)).astype(BF16)
    return hi, lo


def _dot_exact_rhs(x, w):
    n = x.shape[0]
    hi = x.astype(BF16).astype(F32)
    both = jnp.dot(jnp.concatenate([hi, x - hi], axis=0).astype(BF16), w, preferred_element_type=F32)
    return both[0:n, :] + both[n:2 * n, :]


def _run_lockstep(gens):
    results = {}
    while len(results) < len(gens):
        for i, gen in enumerate(gens):
            if i not in results:
                try:
                    next(gen)
                except StopIteration as stop:
                    results[i] = stop.value
    return [results[i] for i in range(len(gens))]


def _rms(x, g):
    return x * lax.rsqrt(jnp.mean(x * x, axis=-1, keepdims=True) + RMS_EPS) * g


def _softplus(x):
    return jnp.maximum(x, 0.0) + jnp.log1p(jnp.exp(-jnp.abs(x)))


def _sigmoid(x):
    return 1.0 / (1.0 + jnp.exp(-x))


def _gelu(x):
    c = math.sqrt(2.0 / math.pi)
    return x * (0.5 * (1.0 + jnp.tanh(c * (x + 0.044715 * (x * x * x)))))


def _mix_in_kernel(has_vres, seq_major_x, nb, tm, *refs):
    refs = list(refs)
    (x_ref, first_ref, g_ref, win_ref, murkv_ref, muwag_ref, w0_ref, w1_ref, w2_ref,
     a0_ref, a1_ref, a2_ref, g1_ref, g2_ref, kk_ref, ka_ref, ones_ref) = refs[:17]
    del refs[:17]
    if has_vres:
        muv_ref, v0_ref, v1_ref, v2_ref, vfirst_ref = refs[:5]
        del refs[:5]
    if seq_major_x:
        xt_ref = refs.pop(0)
    zrest_ref, r_ref, lw_ref, k_ref, v_ref, na_ref, bb_ref, gate_ref, hlast_ref, hbuf, zbuf = refs[:11]

    @pl.when(pl.program_id(0) == 0)
    def _():
        first = first_ref[...]
        hbuf[0:nb, :] = first
        zbuf[0:nb, :] = _bdot(first, win_ref[:, 0:R3])

    if seq_major_x:
        xs = refs[-1]
        for b in range(nb):
            _store_seq_rows(xs, pl.ds(b, tm // nb, stride=nb), x_ref[b])
        x = jnp.concatenate([xs[cb] for cb in range(LANE_BLOCKS)], axis=-1)
        xt_ref[...] = x
    else:
        x = x_ref[...]
    h_all = _rms(x, g_ref[...])
    hbuf[nb:nb + tm, :] = h_all
    hlast_ref[...] = h_all[tm - nb:tm, :]

    def part(lo, n):
        h = hbuf[nb + lo:nb + lo + n, :]
        dh = hbuf[lo:lo + n, :] - h
        z = _bdot(h, win_ref[...])
        p_w = _bdot(h + dh * muwag_ref[0:1, :], w1_ref[...])
        p_a = _bdot(h + dh * muwag_ref[1:2, :], a1_ref[...])
        p_g = _bdot(h + dh * muwag_ref[2:3, :], g1_ref[...])
        if has_vres:
            p_v = _bdot(h + dh * muv_ref[...], v1_ref[...])
        zrest_ref[lo:lo + n, :] = z[:, R3:D_IN]
        z_rkv = z[:, 0:R3]
        zbuf[nb + lo:nb + lo + n, :] = z_rkv
        yield
        zmix = z_rkv + (zbuf[lo:lo + n, :] - z_rkv) * murkv_ref[...]
        r = zmix[:, 0:RWKV_DIM]
        k = zmix[:, RWKV_DIM:2 * RWKV_DIM]
        v = zmix[:, 2 * RWKV_DIM:R3]
        kk = k * kk_ref[...]
        w_pre = w0_ref[...] + _bdot(jnp.tanh(p_w), w2_ref[...])
        a = _sigmoid(a0_ref[...] + _bdot(p_a, a2_ref[...]))
        gate = _bdot(_sigmoid(p_g), g2_ref[...])
        ss = _bdot(kk * kk, ones_ref[...])
        if has_vres:
            mix = _sigmoid(v0_ref[...] + _bdot(p_v, v2_ref[...]))
            v_first = jnp.concatenate([vfirst_ref[p, lo:lo + n, :] for p in range(N_PAIRS)], axis=-1)
            v = v + (v_first - v) * mix
        yield
        w_log = -_softplus(-w_pre) - 0.5
        kk = kk / jnp.maximum(jnp.sqrt(ss), 1e-12)
        outs = ((r_ref, r), (lw_ref, -jnp.exp(w_log)), (k_ref, k * (1.0 + (a - 1.0) * ka_ref[...])), (v_ref, v),
                (na_ref, -kk), (bb_ref, kk * a), (gate_ref, gate))
        for ref, val in outs:
            for p in range(N_PAIRS):
                ref[p, lo:lo + n, :] = val[:, p * HEAD_PAIR:(p + 1) * HEAD_PAIR]

    n_part = tm // MIX_IN_PARTS
    _run_lockstep([part(i * n_part, n_part) for i in range(MIX_IN_PARTS)])
    hbuf[0:nb, :] = hbuf[tm:tm + nb, :]
    zbuf[0:nb, :] = zbuf[tm:tm + nb, :]


MIX_IN_PARTS = 2


def _mix_in(x, first, p, vres, v_first, nb, seq):
    m = nb * seq
    tm = _tile(m, max(nb, 8) * MIX_IN_PARTS, 512)
    has_vres = vres is not None
    seq_major_x = x.ndim == 3
    row = lambda w: pl.BlockSpec((tm, w), lambda i: (i, 0))
    pair_row = pl.BlockSpec((N_PAIRS, tm, HEAD_PAIR), lambda i: (0, i, 0))
    ins = [x, first, p["norm_mix"], p["w_in"], p["mu_rkv"], p["mu_wag"], p["w0"], p["w1"], p["w2"],
           p["a0"], p["a1"], p["a2"], p["g1"], p["g2"], p["k_k"], p["k_a"], p["ones_head"]]
    x_spec = pl.BlockSpec((nb, tm // nb, D_MODEL), lambda i: (0, i, 0)) if seq_major_x else row(D_MODEL)
    specs = [x_spec] + [_spec(a) for a in ins[1:]]
    if has_vres:
        extra = [vres["mu_v"], vres["v0"], vres["v1"], vres["v2"]]
        ins += extra + [v_first]
        specs += [_spec(a) for a in extra] + [pair_row]
    wide = jax.ShapeDtypeStruct((N_PAIRS, m, HEAD_PAIR), F32)
    outs = [jax.ShapeDtypeStruct((m, D_IN - R3), F32)] + [wide] * 7 + [jax.ShapeDtypeStruct((nb, D_MODEL), F32)]
    out_specs = [row(D_IN - R3)] + [pair_row] * 7 + [_const_spec((nb, D_MODEL))]
    scratch = [pltpu.VMEM((tm + nb, D_MODEL), F32), pltpu.VMEM((tm + nb, R3), F32)]
    if seq_major_x:
        assert (tm // nb) % 8 == 0
        outs.insert(0, jax.ShapeDtypeStruct((m, D_MODEL), F32))
        out_specs.insert(0, row(D_MODEL))
        scratch.append(pltpu.VMEM((LANE_BLOCKS, tm, LANES), F32))
    return pl.pallas_call(
        functools.partial(_mix_in_kernel, has_vres, seq_major_x, nb, tm),
        grid=(m // tm,),
        in_specs=specs,
        out_specs=out_specs,
        out_shape=outs,
        scratch_shapes=scratch,
        compiler_params=_params("arbitrary"),
        name="mix_in",
    )(*_arrays(ins))


def _wkv_kernel(c, n_seq, group, stride, r_ref, lw_ref, k_ref, v_ref, na_ref, bb_ref, gate_ref, s0_ref,
                rk_ref, lnw_ref, lnb_ref, tri_ref, mean_ref,
                y_ref, sout_ref, s_scr):
    li = pl.program_id(1)

    @pl.when(li == 0)
    def _():
        zero = jnp.zeros((RWKV_HEAD, RWKV_HEAD), F32)
        for j in range(n_seq):
            for p in range(N_PAIRS):
                s_scr[j, p] = jnp.concatenate(
                    [jnp.concatenate([s0_ref[j, 2 * p], zero], axis=1),
                     jnp.concatenate([zero, s0_ref[j, 2 * p + 1]], axis=1)], axis=0)

    lane = lax.broadcasted_iota(jnp.int32, (1, HEAD_PAIR), 1)
    head_masks = (lane < RWKV_HEAD, lane >= RWKV_HEAD)
    trow = lax.broadcasted_iota(jnp.int32, (c, 2 * c), 0)
    tcol = lax.broadcasted_iota(jnp.int32, (c, 2 * c), 1)
    left_half = tcol < c
    scol = jnp.where(left_half, tcol, tcol - c)
    strict = scol < trow
    strict_left = strict & left_half
    strict_right = strict & (tcol >= c)
    incl = scol <= trow
    eye2 = (scol == trow).astype(F32)
    drow = lax.broadcasted_iota(jnp.int32, (2 * c, 2 * c), 0)
    dcol = lax.broadcasted_iota(jnp.int32, (2 * c, 2 * c), 1)
    diag_blocks = jnp.bitwise_xor(drow - c, dcol - c) >= 0
    brow = lax.broadcasted_iota(jnp.int32, (HEAD_PAIR, HEAD_PAIR), 0)
    bcol = lax.broadcasted_iota(jnp.int32, (HEAD_PAIR, HEAD_PAIR), 1)
    same_head = jnp.bitwise_xor(brow - RWKV_HEAD, bcol - RWKV_HEAD) >= 0
    n_double = max(int(math.log2(c)) - 1, 0)
    tri = tri_ref[...]
    zeros_c = jnp.zeros((c, HEAD_PAIR), F32)

    def pair_chunk(lw, r, k, v, na, bb, gate, sp, cols):
        lw_hi, lw_lo = _split2(lw)
        cl = jnp.dot(tri, lw_hi, preferred_element_type=F32) + jnp.dot(tri, lw_lo, preferred_element_type=F32)
        yield
        e_in = jnp.exp(cl)
        e_neg = jnp.exp(-cl)
        g_last = jnp.exp(jnp.sum(lw, axis=0, keepdims=True))
        at = na * jnp.exp(cl - lw)
        rt = r * e_in
        bt = bb * e_neg
        kt = k * e_neg
        ar = jnp.concatenate([at, rt], axis=0)
        bk = jnp.concatenate([bt, kt], axis=0)
        a_s = _bdot_nt(ar, sp)
        g0 = _bdot_nt(jnp.where(head_masks[0], ar, 0.0), bk)
        g1 = _bdot_nt(jnp.where(head_masks[1], ar, 0.0), jnp.concatenate([kt, bt], axis=0))
        yield
        ga0, ga1 = g0[0:c, :], g1[0:c, :]
        low = jnp.where(strict, jnp.where(left_half, ga0, ga1), 0.0)
        ak0 = _bdot(jnp.where(strict_right, ga0, 0.0), jnp.concatenate([zeros_c, v], axis=0))
        ak1 = _bdot(jnp.where(strict_left, ga1, 0.0), jnp.concatenate([v, zeros_c], axis=0))
        rhs = jnp.concatenate([jnp.where(head_masks[0], a_s[0:c, :] + ak0, 0.0),
                               jnp.where(head_masks[1], a_s[0:c, :] + ak1, 0.0)], axis=0)
        blockdiag = lambda x: jnp.where(diag_blocks, jnp.concatenate([x, x], axis=0), 0.0)
        t_inv = eye2 + low
        pw = low
        if n_double > 0:
            pw = _bdot(pw, blockdiag(pw))
            yield
        for level in range(n_double):
            if level == n_double - 1:
                t_inv = t_inv + _bdot(t_inv, blockdiag(pw))
            else:
                both = _bdot(jnp.concatenate([t_inv, pw], axis=0), blockdiag(pw))
                t_inv = t_inv + both[0:c, :]
                pw = both[c:2 * c, :]
            yield
        u = _bdot(t_inv, rhs)
        yield
        uv = jnp.concatenate([u, v], axis=0)
        y = a_s[c:2 * c, :] + jnp.where(
            head_masks[0], _bdot(jnp.where(incl, g0[c:2 * c, :], 0.0), uv),
            _bdot(jnp.where(incl, g1[c:2 * c, :], 0.0), jnp.concatenate([v, u], axis=0)))
        upd = _bdot_tn(uv, bk * g_last)
        s_new = sp * g_last + jnp.where(same_head, upd, 0.0)
        yield
        stats = _dot_exact_rhs(jnp.concatenate([y, r * k * rk_ref[:, cols]], axis=0), mean_ref[...])
        mu = stats[0:c, :]
        bonus = stats[c:2 * c, :] * float(RWKV_HEAD) * v
        yield
        d = y - mu
        var = _bdot(d * d, mean_ref[...])
        yield
        yn = d * lax.rsqrt(var + GN_EPS) * lnw_ref[:, cols] + lnb_ref[:, cols]
        return (yn + bonus) * gate, s_new

    first_seq = pl.program_id(0) * n_seq

    def seq_group(gi, carry):
        units = []
        for jj in range(group):
            j = gi * group + jj
            rows = pl.ds(first_seq + j, c, stride=stride)
            for p in range(N_PAIRS):
                cols = slice(p * HEAD_PAIR, (p + 1) * HEAD_PAIR)
                vals = [ref[p, rows, :] for ref in (lw_ref, r_ref, k_ref, v_ref, na_ref, bb_ref, gate_ref)]
                units.append((j, p, rows, pair_chunk(*vals, s_scr[j, p], cols)))
        results = _run_lockstep([gen for (_, _, _, gen) in units])
        for (j, p, rows, _), (y, s_new) in zip(units, results):
            y_ref[p, rows, :] = y
            s_scr[j, p] = s_new
        return carry

    lax.fori_loop(0, n_seq // group, seq_group, 0)

    @pl.when(li == pl.num_programs(1) - 1)
    def _():
        for j in range(n_seq):
            for p in range(N_PAIRS):
                sp = s_scr[j, p]
                sout_ref[j, 2 * p] = sp[0:RWKV_HEAD, 0:RWKV_HEAD]
                sout_ref[j, 2 * p + 1] = sp[RWKV_HEAD:HEAD_PAIR, RWKV_HEAD:HEAD_PAIR]


WKV_SEQ_PER_STEP = 8
WKV_SEQ_GROUP = 8


def _wkv(arrs, s_all, layer, p, nb, seq):
    c = min(WKV_CHUNK, seq)
    n_seq = nb if seq > c else min(nb, WKV_SEQ_PER_STEP)
    assert seq % c == 0 and c % 8 == 0 and nb % n_seq == 0 and n_seq % WKV_SEQ_GROUP == 0
    blk = pl.BlockSpec((N_PAIRS, c * nb, HEAD_PAIR), lambda g, l: (0, l, 0))
    tri = jnp.tril(jnp.ones((c, c), F32)).astype(BF16)
    st_spec = pl.BlockSpec((None, n_seq, RWKV_HEADS, RWKV_HEAD, RWKV_HEAD), lambda g, l: (layer, g, 0, 0, 0))
    consts = [p["r_k"], p["ln_x_w"], p["ln_x_b"], tri, p["mean_pair"]]
    return pl.pallas_call(
        functools.partial(_wkv_kernel, c, n_seq, WKV_SEQ_GROUP, nb),
        grid=(nb // n_seq, seq // c),
        in_specs=[blk] * 7 + [st_spec] + [_spec(a) for a in consts],
        out_specs=[blk, st_spec],
        out_shape=[jax.ShapeDtypeStruct((N_PAIRS, seq * nb, HEAD_PAIR), F32), jax.ShapeDtypeStruct(s_all.shape, F32)],
        input_output_aliases={7: 1},
        scratch_shapes=[pltpu.VMEM((n_seq, N_PAIRS, HEAD_PAIR, HEAD_PAIR), F32)],
        compiler_params=_params("arbitrary", "arbitrary"),
        name="wkv",
    )(*arrs, s_all, *_arrays(consts))


def _ssm_kernel(reset_first, nb, tm, zx_ref, zg_ref, u_ref,
                conv0_ref, h0_ref, cw_ref, cb_ref, wa_ref, ba_ref, wi_ref, bi_ref, lam_ref,
                re0_ref, im0_ref, are_ref, aim_ref, bre_ref, bim_ref, cre_ref, cim_ref, d_ref, wglu_ref, bglu_ref,
                y_ref, conv_out_ref, h_out_ref, yc_ref, re_out_ref, im_out_ref,
                xbuf, abuf, bbuf, hcar, xre, xim, car_re, car_im):
    i = pl.program_id(0)
    hist = (CONV_W - 1) * nb

    @pl.when(i == 0)
    def _():
        xbuf[0:hist, :] = conv0_ref[...]
        hcar[...] = h0_ref[...]
        car_re[...] = re0_ref[...]
        car_im[...] = im0_ref[...]

    u = u_ref[...]
    xre[...] = _bdot(u, bre_ref[...])
    xim[...] = _bdot(u, bim_ref[...])

    xbuf[hist:hist + tm, :] = zx_ref[...]
    xc = cb_ref[...]
    for j in range(CONV_W):
        xc = xc + xbuf[j * nb:j * nb + tm, :] * cw_ref[j:j + 1, :]
    conv_out_ref[...] = xbuf[tm:tm + hist, :]
    xbuf[0:hist, :] = xbuf[tm:tm + hist, :]

    gate_a = _sigmoid(_bdot(xc, wa_ref[...]) + ba_ref[...])
    gate_i = _sigmoid(_bdot(xc, wi_ref[...]) + bi_ref[...])
    log_a = -LRU_C * gate_a * _softplus(-lam_ref[...])
    a_sq = jnp.exp(2.0 * log_a)
    mult = jnp.sqrt(-jnp.tanh(log_a) * (a_sq + 1.0))
    if reset_first:
        row = lax.broadcasted_iota(jnp.int32, (tm, 1), 0)
        mult = jnp.where(row < jnp.where(i == 0, nb, 0), 1.0, mult)
    abuf[...] = jnp.exp(log_a)
    bbuf[...] = mult * gate_i * xc

    a_re = jnp.broadcast_to(are_ref[...], (nb, S5_WIDTH))
    a_im = jnp.broadcast_to(aim_ref[...], (nb, S5_WIDTH))

    def step(t, carry):
        h, s_re, s_im = carry
        rows = pl.ds(pl.multiple_of(t * nb, nb), nb)
        h = abuf[rows, :] * h + bbuf[rows, :]
        bbuf[rows, :] = h
        n_re = a_re * s_re - a_im * s_im + xre[rows, :]
        n_im = a_re * s_im + a_im * s_re + xim[rows, :]
        xre[rows, :] = n_re
        xim[rows, :] = n_im
        return h, n_re, n_im

    h, s_re, s_im = lax.fori_loop(0, tm // nb, step, (hcar[...], car_re[...], car_im[...]),
                                  unroll=min(4, tm // nb))
    hcar[...] = h
    h_out_ref[...] = h
    car_re[...] = s_re
    car_im[...] = s_im
    re_out_ref[...] = s_re
    im_out_ref[...] = s_im

    y_ref[...] = bbuf[...] * _gelu(zg_ref[...])
    y = _bdot(xre[...], cre_ref[...]) - _bdot(xim[...], cim_ref[...]) + d_ref[...] * u
    o = _bdot(_gelu(y), wglu_ref[...]) + bglu_ref[...]
    yc_ref[...] = o[:, 0:S5_DIM] * _sigmoid(o[:, S5_DIM:2 * S5_DIM])


def _ssm(zrest, conv0, h0, re0, im0, p, reset_first, nb):
    assert LRU_DIM == S5_DIM
    m = zrest.shape[0]
    tm = _tile(m, max(nb, 8), 512)
    hist = (CONV_W - 1) * nb
    consts = [conv0, h0, p["conv_w"], p["conv_b"], p["lru_wa"], p["lru_ba"], p["lru_wi"], p["lru_bi"],
              p["lru_lambda"], re0, im0, p["s5_abar_re"], p["s5_abar_im"], p["s5_bbar_re"], p["s5_bbar_im"],
              p["s5_c_re"], p["s5_c_im"], p["s5_d"], p["s5_w_glu"], p["s5_b_glu"]]
    col = lambda j: pl.BlockSpec((tm, LRU_DIM), lambda i: (i, j))
    st = jax.ShapeDtypeStruct((nb, S5_WIDTH), F32)
    return pl.pallas_call(
        functools.partial(_ssm_kernel, reset_first, nb, tm),
        grid=(m // tm,),
        in_specs=[col(0), col(1), col(2)] + [_spec(a) for a in consts],
        out_specs=[col(0), _const_spec((hist, LRU_DIM)), _const_spec((nb, LRU_DIM)),
                   col(0), _const_spec((nb, S5_WIDTH)), _const_spec((nb, S5_WIDTH))],
        out_shape=[jax.ShapeDtypeStruct((m, LRU_DIM), F32), jax.ShapeDtypeStruct((hist, LRU_DIM), F32),
                   jax.ShapeDtypeStruct((nb, LRU_DIM), F32), jax.ShapeDtypeStruct((m, S5_DIM), F32), st, st],
        scratch_shapes=[pltpu.VMEM((tm + hist, LRU_DIM), F32), pltpu.VMEM((tm, LRU_DIM), F32),
                        pltpu.VMEM((tm, LRU_DIM), F32), pltpu.VMEM((nb, LRU_DIM), F32),
                        pltpu.VMEM((tm, S5_WIDTH), F32), pltpu.VMEM((tm, S5_WIDTH), F32),
                        pltpu.VMEM((nb, S5_WIDTH), F32), pltpu.VMEM((nb, S5_WIDTH), F32)],
        compiler_params=_params("arbitrary"),
        name="ssm",
    )(zrest, zrest, zrest, *_arrays(consts))


def _mix_out_kernel(x_ref, ya_ref, yb_ref, yc_ref, wa_ref, wb_ref, wc_ref, gq_ref, wq_ref, x1_ref, q_ref):
    ya = jnp.concatenate([ya_ref[p] for p in range(N_PAIRS)], axis=-1)
    x1 = (x_ref[...] + _bdot(ya, wa_ref[...]) + _bdot(yb_ref[...], wb_ref[...])
          + _bdot(yc_ref[...], wc_ref[...]))
    q = _bdot(_rms(x1, gq_ref[...]), wq_ref[...])
    for cb in range(LANE_BLOCKS):
        cols = slice(cb * LANES, (cb + 1) * LANES)
        x1_ref[cb] = x1[:, cols]
        q_ref[cb] = q[:, cols]


def _mix_out(x, ya, yb, yc, p):
    m = x.shape[0]
    tm = _tile(m, 8, 512)
    row = lambda w: pl.BlockSpec((tm, w), lambda i: (i, 0))
    split = pl.BlockSpec((LANE_BLOCKS, tm, LANES), lambda i: (0, i, 0))
    consts = [p["w_out_a"], p["w_out_b"], p["w_out_c"], p["norm_mem_q"], p["mem_wq"]]
    out = jax.ShapeDtypeStruct((LANE_BLOCKS, m, LANES), F32)
    return pl.pallas_call(
        _mix_out_kernel,
        grid=(m // tm,),
        in_specs=[row(D_MODEL), pl.BlockSpec((N_PAIRS, tm, HEAD_PAIR), lambda i: (0, i, 0)), row(LRU_DIM),
                  row(S5_DIM)] + [_spec(a) for a in consts],
        out_specs=[split, split],
        out_shape=[out, out],
        compiler_params=_params("parallel"),
        name="mix_out",
    )(x, ya, yb, yc, *_arrays(consts))


def _load_seq_rows(ref, rows):
    return jnp.concatenate([ref[cb, rows, :] for cb in range(LANE_BLOCKS)], axis=-1)


def _store_seq_rows(ref, rows, val):
    for cb in range(LANE_BLOCKS):
        ref[cb, rows, :] = val[:, cb * LANES:(cb + 1) * LANES]


def _softmax_rows(s):
    e = jnp.exp(s - jnp.max(s, axis=-1, keepdims=True))
    return e / jnp.sum(e, axis=-1, keepdims=True)


def _attn_kernel(tl, n_seq, stride, q_ref, x_ref, mk_ref, mv_ref, wo_ref, o_ref):
    scale = MEM_HEAD_DIM ** -0.5
    first_seq = pl.program_id(0) * n_seq
    head_cols = [slice(hh * MEM_HEAD_DIM, (hh + 1) * MEM_HEAD_DIM) for hh in range(MEM_HEADS)]

    def one_seq(j):
        rows = pl.ds(first_seq + j, tl, stride=stride)
        q = _load_seq_rows(q_ref, rows)
        scores = [_bdot_nt(q[:, cols], mk_ref[j, :, cols]) * scale for cols in head_cols]
        yield
        heads = [_bdot(_softmax_rows(s), mv_ref[j, :, cols]) for s, cols in zip(scores, head_cols)]
        yield
        out = _load_seq_rows(x_ref, rows) + _bdot(jnp.concatenate(heads, axis=-1), wo_ref[...])
        yield
        _store_seq_rows(o_ref, rows, out)

    for j0 in range(0, n_seq, ATTN_LOCKSTEP):
        _run_lockstep([one_seq(j) for j in range(j0, min(j0 + ATTN_LOCKSTEP, n_seq))])


def _attn_cache_kernel(tl, n_seq, stride, q_ref, x_ref, mk_ref, mv_ref, wo_ref, o_ref):
    scale = MEM_HEAD_DIM ** -0.5
    first_seq = pl.program_id(0) * n_seq
    qrow = lax.broadcasted_iota(jnp.int32, (MEM_HEADS * tl, N_MEM * MEM_HEADS), 0)
    kcol = lax.broadcasted_iota(jnp.int32, (MEM_HEADS * tl, N_MEM * MEM_HEADS), 1)
    q_head = jnp.zeros_like(qrow)
    for hh in range(1, MEM_HEADS):
        q_head = q_head + (qrow >= hh * tl).astype(jnp.int32)
    same_head = q_head == jnp.bitwise_and(kcol, MEM_HEADS - 1)

    def one_seq(j):
        rows = pl.ds(first_seq + j, tl, stride=stride)
        q = _load_seq_rows(q_ref, rows)
        qh = jnp.concatenate([q[:, hh * MEM_HEAD_DIM:(hh + 1) * MEM_HEAD_DIM] for hh in range(MEM_HEADS)], axis=0)
        k2 = mk_ref[j].reshape(N_MEM * MEM_HEADS, MEM_HEAD_DIM)
        s = jnp.where(same_head, _bdot_nt(qh, k2) * scale, MASKED_SCORE)
        yield
        v2 = mv_ref[j].reshape(N_MEM * MEM_HEADS, MEM_HEAD_DIM)
        oh = _bdot(_softmax_rows(s), v2)
        yield
        o = jnp.concatenate([oh[hh * tl:(hh + 1) * tl, :] for hh in range(MEM_HEADS)], axis=-1)
        out = _load_seq_rows(x_ref, rows) + _bdot(o, wo_ref[...])
        yield
        _store_seq_rows(o_ref, rows, out)

    _run_lockstep([one_seq(j) for j in range(n_seq)])


MASKED_SCORE = -1e30
ATTN_SEQ_PER_STEP = 4
ATTN_ROWS = 128
ATTN_LOCKSTEP = 4


def _attn(q, x1, mk, mv, kv_offset, wo, nb, seq):
    tl = _tile(seq, 8, ATTN_ROWS)
    n_seq = nb if seq > tl else min(nb, ATTN_SEQ_PER_STEP)
    assert nb % n_seq == 0 and kv_offset % n_seq == 0 and MEM_HEADS & (MEM_HEADS - 1) == 0
    blk = pl.BlockSpec((LANE_BLOCKS, tl * nb, LANES), lambda g, l: (0, l, 0))
    kv_index = lambda g, l: (kv_offset // n_seq + g,) + (0,) * (mk.ndim - 1)
    kv = pl.BlockSpec((n_seq,) + mk.shape[1:], kv_index)
    body = _attn_kernel if mk.ndim == 3 else _attn_cache_kernel
    return pl.pallas_call(
        functools.partial(body, tl, n_seq, nb),
        grid=(nb // n_seq, seq // tl),
        in_specs=[blk, blk, kv, kv, _spec(wo)],
        out_specs=blk,
        out_shape=jax.ShapeDtypeStruct((LANE_BLOCKS, seq * nb, LANES), F32),
        compiler_params=_params("arbitrary", "arbitrary"),
        name="mem_attn",
    )(q, x1, mk, mv, *_arrays([wo]))


def _ffn_kernel(d_ff, final_norm, n_seq, x_ref, g_ref, wup_ref, wdown_ref, *rest):
    x = jnp.concatenate([x_ref[cb] for cb in range(LANE_BLOCKS)], axis=-1)
    h = _rms(x, g_ref[...]).astype(BF16)
    gt = jnp.dot(h, wup_ref[:, 0:d_ff], preferred_element_type=F32)
    up = jnp.dot(h, wup_ref[:, d_ff:2 * d_ff], preferred_element_type=F32)
    act = gt * _sigmoid(gt) * up
    out = x + _bdot(act, wdown_ref[...])
    if final_norm:
        out = _rms(out, rest[0][...])
    if n_seq:
        o_ref, os = rest[-2:]
        for cb in range(LANE_BLOCKS):
            os[cb] = out[:, cb * LANES:(cb + 1) * LANES]
        for b in range(n_seq):
            o_ref[b] = _load_seq_rows(os, pl.ds(b, out.shape[0] // n_seq, stride=n_seq))
    else:
        rest[-1][...] = out


def _ffn(x, p, g_final=None, seq_major_out=None):
    m = x.shape[1]
    tm = _tile(m, 8, 512)
    d_ff = p["ffn_w_down"].shape[0]
    consts = [p["norm_ffn"], p["ffn_w_up"], p["ffn_w_down"]] + ([] if g_final is None else [g_final])
    if seq_major_out is None:
        n_seq, scratch = 0, []
        out_spec = pl.BlockSpec((tm, D_MODEL), lambda i: (i, 0))
        out_shape = jax.ShapeDtypeStruct((m, D_MODEL), F32)
    else:
        n_seq, seq = seq_major_out
        assert tm % n_seq == 0 and (tm // n_seq) % 8 == 0 and n_seq * seq == m
        scratch = [pltpu.VMEM((LANE_BLOCKS, tm, LANES), F32)]
        out_spec = pl.BlockSpec((n_seq, tm // n_seq, D_MODEL), lambda i: (0, i, 0))
        out_shape = jax.ShapeDtypeStruct((n_seq, seq, D_MODEL), F32)
    return pl.pallas_call(
        functools.partial(_ffn_kernel, d_ff, g_final is not None, n_seq),
        grid=(m // tm,),
        in_specs=[pl.BlockSpec((LANE_BLOCKS, tm, LANES), lambda i: (0, i, 0))] + [_spec(a) for a in consts],
        out_specs=out_spec,
        out_shape=out_shape,
        scratch_shapes=scratch,
        compiler_params=_params("parallel"),
        name="ffn",
    )(x, *_arrays(consts))


def _mem_project_kernel(x_ref, g_ref, wk_ref, wv_ref, k_ref, v_ref, kb_ref, vb_ref):
    m = _rms(x_ref[...], g_ref[...]).astype(BF16)
    k = jnp.dot(m, wk_ref[...], preferred_element_type=F32)
    v = jnp.dot(m, wv_ref[...], preferred_element_type=F32)
    k_ref[...] = k.reshape(k_ref.shape)
    v_ref[...] = v.reshape(v_ref.shape)
    kb_ref[...] = k.astype(BF16)
    vb_ref[...] = v.astype(BF16)


MEM_SEQ_PER_STEP = 2


def _mem_project(mem, g, wk, wv):
    n, n_mem, _ = mem.shape
    tm = MEM_SEQ_PER_STEP * n_mem
    assert n % MEM_SEQ_PER_STEP == 0
    row = pl.BlockSpec((tm, D_MODEL), lambda i: (i, 0))
    cache = pl.BlockSpec((MEM_SEQ_PER_STEP, n_mem, MEM_HEADS, MEM_HEAD_DIM), lambda i: (i, 0, 0, 0))
    out = jax.ShapeDtypeStruct((n, n_mem, MEM_HEADS, MEM_HEAD_DIM), F32)
    outb = jax.ShapeDtypeStruct((n * n_mem, D_MODEL), BF16)
    return pl.pallas_call(
        _mem_project_kernel, grid=(n // MEM_SEQ_PER_STEP,),
        in_specs=[row, _spec(g), _spec(wk), _spec(wv)],
        out_specs=[cache, cache, row, row], out_shape=[out, out, outb, outb], compiler_params=_params("parallel"),
        name="mem_project",
    )(mem.reshape(n * n_mem, D_MODEL), *_arrays([g, wk, wv]))


def _block_diag(blocks):
    n, r, c = blocks.shape[-3:]
    eye = jnp.eye(n, dtype=blocks.dtype)
    out = eye[:, None, :, None] * blocks[..., :, :, None, :]
    return out.reshape(blocks.shape[:-3] + (n * r, n * c))


def _stack_params(w):
    depth = w["norm_mix"].shape[0]
    row = lambda a: a.reshape(a.shape[0], 1, -1).astype(F32)
    bf = lambda a: a.astype(BF16)
    t = lambda a: jnp.swapaxes(a, -1, -2)
    p = {k: row(w[k]) for k in ("norm_mix", "mu_rkv", "w0", "a0", "k_k", "k_a", "r_k", "ln_x_w", "ln_x_b", "conv_b",
                                "lru_ba", "lru_bi", "lru_lambda", "s5_d", "s5_b_glu", "norm_mem_q", "norm_mem_kv",
                                "norm_ffn", "mu_v", "v0")}
    p.update({k: bf(w[k]) for k in ("w_in", "w1", "w2", "a1", "a2", "g1", "g2", "s5_w_glu", "mem_wq", "mem_wk",
                                    "mem_wv", "mem_wo", "ffn_w_up", "ffn_w_down", "v1", "v2")})
    p["mu_wag"], p["conv_w"] = w["mu_wag"], w["conv_w"]
    p["lru_wa"], p["lru_wi"] = bf(_block_diag(w["lru_wa"])), bf(_block_diag(w["lru_wi"]))
    w_out = bf(w["w_out"])
    p["w_out_a"] = w_out[:, 0:RWKV_DIM]
    p["w_out_b"] = w_out[:, RWKV_DIM:RWKV_DIM + LRU_DIM]
    p["w_out_c"] = w_out[:, RWKV_DIM + LRU_DIM:]
    lam_re, lam_im = w["s5_a_re"].astype(F32), w["s5_a_im"].astype(F32)
    dt = jnp.exp(w["s5_log_dt"].astype(F32))[..., None]
    mag = jnp.exp(lam_re * dt)
    abar_re, abar_im = mag * jnp.cos(lam_im * dt), mag * jnp.sin(lam_im * dt)
    den = lam_re * lam_re + lam_im * lam_im
    q_re = ((abar_re - 1.0) * lam_re + abar_im * lam_im) / den
    q_im = (abar_im * lam_re - (abar_re - 1.0) * lam_im) / den
    b_re, b_im = w["s5_b_re"].astype(F32), w["s5_b_im"].astype(F32)
    bbar_re = q_re[..., None] * b_re - q_im[..., None] * b_im
    bbar_im = q_re[..., None] * b_im + q_im[..., None] * b_re
    p["s5_abar_re"] = abar_re.reshape(depth, 1, S5_WIDTH)
    p["s5_abar_im"] = abar_im.reshape(depth, 1, S5_WIDTH)
    p["s5_bbar_re"] = bf(_block_diag(t(bbar_re)))
    p["s5_bbar_im"] = bf(_block_diag(t(bbar_im)))
    p["s5_c_re"] = bf(_block_diag(t(w["s5_c_re"])))
    p["s5_c_im"] = bf(_block_diag(t(w["s5_c_im"])))
    return p


def _layer_params(stacked, l, consts):
    vres_keys = ("mu_v", "v0", "v1", "v2")
    p = {k: _Layer(a, l) for k, a in stacked.items() if k not in vres_keys}
    p.update(consts)
    vres = {k: _Layer(stacked[k], l - 1) for k in vres_keys} if l > 0 else None
    return p, vres


def _time_major(a):
    return jnp.swapaxes(a, 0, 1).reshape((a.shape[0] * a.shape[1],) + a.shape[2:])


def _run_group(x, mem_k, mem_v, kv_stride, shift0, wkv0, conv0, lru0, re0, im0, reset_first, layers, norm_final,
               relayout_in_kernel):
    nb, seq, _ = x.shape
    xt = x if relayout_in_kernel else _time_major(x)
    outs = {k: [] for k in ("shift", "conv", "lru", "re", "im")}
    wkv_all = wkv0
    v_first = None
    for l, (p, vres) in enumerate(layers):
        last = l == len(layers) - 1
        mixed = _mix_in(xt, _Layer(shift0, l), p, vres, v_first, nb, seq)
        if xt.ndim == 3:
            xt, mixed = mixed[0], mixed[1:]
        zrest, r, lw, k, v, na, bb, gate, h_last = mixed
        if l == 0:
            v_first = v
        ya, wkv_all = _wkv((r, lw, k, v, na, bb, gate), wkv_all, l, p, nb, seq)
        yb, conv_new, lru_new, yc, re_new, im_new = _ssm(
            zrest, _time_major(conv0[l]), _Layer(lru0, l), _Layer(re0.reshape(-1, nb, S5_WIDTH), l),
            _Layer(im0.reshape(-1, nb, S5_WIDTH), l), p, reset_first, nb)
        x1, q = _mix_out(xt, ya, yb, yc, p)
        x2 = _attn(q, x1, mem_k, mem_v, l * kv_stride, p["mem_wo"], nb, seq)
        xt = _ffn(x2, p, norm_final if last else None, (nb, seq) if last and relayout_in_kernel else None)
        outs["shift"].append(h_last)
        outs["conv"].append(jnp.swapaxes(conv_new.reshape(CONV_W - 1, nb, LRU_DIM), 0, 1))
        outs["lru"].append(lru_new)
        outs["re"].append(re_new.reshape(nb, S5_GROUPS, S5_STATE))
        outs["im"].append(im_new.reshape(nb, S5_GROUPS, S5_STATE))
    y = xt if relayout_in_kernel else jnp.swapaxes(xt.reshape(seq, nb, D_MODEL), 0, 1)
    st = {k: jnp.stack(v) for k, v in outs.items()}
    return (y, st["shift"], wkv_all, st["conv"], st["lru"], st["re"], st["im"])


def kernel(x_prompt, x_sample, mem_prompt, state_shift, state_wkv, state_conv, state_lru, state_s5_re, state_s5_im, cache_mem_k, cache_mem_v, norm_mix, w_in, w_out, mu_rkv, mu_wag, mu_v, w0, w1, w2, a0, a1, a2, v0, v1, v2, g1, g2, k_k, k_a, r_k, ln_x_w, ln_x_b, conv_w, conv_b, lru_wa, lru_ba, lru_wi, lru_bi, lru_lambda, s5_a_re, s5_a_im, s5_log_dt, s5_b_re, s5_b_im, s5_c_re, s5_c_im, s5_d, s5_w_glu, s5_b_glu, norm_mem_q, norm_mem_kv, mem_wq, mem_wk, mem_wv, mem_wo, norm_ffn, ffn_w_up, ffn_w_down, norm_final):
    w = dict(norm_mix=norm_mix, w_in=w_in, w_out=w_out, mu_rkv=mu_rkv, mu_wag=mu_wag, mu_v=mu_v, w0=w0, w1=w1,
             w2=w2, a0=a0, a1=a1, a2=a2, v0=v0, v1=v1, v2=v2, g1=g1, g2=g2, k_k=k_k, k_a=k_a, r_k=r_k,
             ln_x_w=ln_x_w, ln_x_b=ln_x_b, conv_w=conv_w, conv_b=conv_b, lru_wa=lru_wa, lru_ba=lru_ba,
             lru_wi=lru_wi, lru_bi=lru_bi, lru_lambda=lru_lambda, s5_a_re=s5_a_re, s5_a_im=s5_a_im,
             s5_log_dt=s5_log_dt, s5_b_re=s5_b_re, s5_b_im=s5_b_im, s5_c_re=s5_c_re, s5_c_im=s5_c_im,
             s5_d=s5_d, s5_w_glu=s5_w_glu, s5_b_glu=s5_b_glu, norm_mem_q=norm_mem_q, norm_mem_kv=norm_mem_kv,
             mem_wq=mem_wq, mem_wk=mem_wk, mem_wv=mem_wv, mem_wo=mem_wo, norm_ffn=norm_ffn, ffn_w_up=ffn_w_up,
             ffn_w_down=ffn_w_down)
    depth = norm_mix.shape[0]
    ones64 = jnp.ones((RWKV_HEADS, RWKV_HEAD, RWKV_HEAD), F32)
    consts = {"ones_head": _block_diag(ones64).astype(BF16),
              "mean_pair": (_block_diag(ones64[:2]) / RWKV_HEAD).astype(BF16)}
    stacked = _stack_params(w)
    layers = [_layer_params(stacked, l, consts) for l in range(depth)]
    g_final = norm_final.reshape(1, D_MODEL)

    bp, n_mem, _ = mem_prompt.shape
    mem_kv = [_mem_project(mem_prompt, p["norm_mem_kv"], p["mem_wk"], p["mem_wv"]) for p, _ in layers]
    mem_k_p = jnp.stack([kv[0] for kv in mem_kv])
    mem_v_p = jnp.stack([kv[1] for kv in mem_kv])
    mem_k_b = jnp.stack([kv[2] for kv in mem_kv]).reshape(depth * bp, n_mem, D_MODEL)
    mem_v_b = jnp.stack([kv[3] for kv in mem_kv]).reshape(depth * bp, n_mem, D_MODEL)

    zeros = lambda *s: jnp.zeros((depth, bp) + s, F32)
    prompt = _run_group(
        x_prompt, mem_k_b, mem_v_b, bp,
        zeros(D_MODEL), zeros(RWKV_HEADS, RWKV_HEAD, RWKV_HEAD), zeros(CONV_W - 1, LRU_DIM), zeros(LRU_DIM),
        zeros(S5_GROUPS, S5_STATE), zeros(S5_GROUPS, S5_STATE), True, layers, g_final, True)
    bs = x_sample.shape[0]
    cache_shape = (depth * bs, n_mem, MEM_HEADS, MEM_HEAD_DIM)
    sample = _run_group(
        x_sample, cache_mem_k.reshape(cache_shape), cache_mem_v.reshape(cache_shape),
        bs, state_shift, state_wkv, state_conv, state_lru, state_s5_re, state_s5_im, False, layers, g_final, False)
    (y_p, sh_p, wkv_p, conv_p, lru_p, re_p, im_p) = prompt
    (y_s, sh_s, wkv_s, conv_s, lru_s, re_s, im_s) = sample
    return (y_p, y_s, sh_p, sh_s, wkv_p, wkv_s, conv_p, conv_s, lru_p, lru_s, re_p, re_s, im_p, im_s,
            mem_k_p, mem_v_p)
```

```python
import functools
import math

import jax
import jax.numpy as jnp
from jax import lax
from jax.experimental import pallas as pl
from jax.experimental.pallas import tpu as pltpu

F32 = jnp.float32
BF16 = jnp.bfloat16

D_MODEL = 1024
RWKV_DIM = 512
RWKV_HEAD = 64
RWKV_HEADS = 8
LANES = 128
LANE_BLOCKS = D_MODEL // LANES
HEAD_PAIR = 2 * RWKV_HEAD
N_PAIRS = RWKV_DIM // HEAD_PAIR
LRU_DIM = 256
LRU_BLOCKS = 4
CONV_W = 4
LRU_C = 8.0
S5_DIM = 256
S5_GROUP = 16
S5_GROUPS = 16
S5_STATE = 64
S5_WIDTH = S5_GROUPS * S5_STATE
R3 = 3 * RWKV_DIM
D_IN = R3 + 2 * LRU_DIM + S5_DIM
N_MEM = 256
MEM_HEADS = 4
MEM_HEAD_DIM = 256
RMS_EPS = 1e-6
GN_EPS = 64e-5
WKV_CHUNK = 64

V7X_VMEM_LIMIT = 56 * 1024 * 1024


def _params(*sem):
    return pltpu.CompilerParams(dimension_semantics=sem, vmem_limit_bytes=V7X_VMEM_LIMIT)


def _tile(m, mult, target):
    best = mult
    t = mult
    while t <= min(m, target):
        if m % t == 0:
            best = t
        t += mult
    assert m % best == 0, (m, mult, target)
    return best


def _const_spec(shape):
    nd = len(shape)
    return pl.BlockSpec(shape, lambda *_: (0,) * nd)


class _Layer:
    def __init__(self, arr, layer):
        self.arr, self.layer, self.shape = arr, layer, arr.shape[1:]


def _spec(a):
    if isinstance(a, _Layer):
        nd = len(a.shape)
        return pl.BlockSpec((None,) + a.shape, lambda *_: (a.layer,) + (0,) * nd)
    return _const_spec(a.shape)


def _arrays(operands):
    return [a.arr if isinstance(a, _Layer) else a for a in operands]


def _bdot(a, b):
    return jnp.dot(a.astype(BF16), b.astype(BF16), preferred_element_type=F32)


def _bdot_nt(a, b):
    return lax.dot_general(a.astype(BF16), b.astype(BF16), (((1,), (1,)), ((), ())),
                           preferred_element_type=F32)


def _bdot_tn(a, b):
    return lax.dot_general(a.astype(BF16), b.astype(BF16), (((0,), (0,)), ((), ())),
                           preferred_element_type=F32)


def _split2(x):
    hi = x.astype(BF16)
    lo = (x - hi.astype(F32)).astype(BF16)
    return hi, lo


def _run_lockstep(gens):
    results = {}
    while len(results) < len(gens):
        for i, gen in enumerate(gens):
            if i not in results:
                try:
                    next(gen)
                except StopIteration as stop:
                    results[i] = stop.value
    return [results[i] for i in range(len(gens))]


def _rms(x, g):
    return x * lax.rsqrt(jnp.mean(x * x, axis=-1, keepdims=True) + RMS_EPS) * g


def _softplus(x):
    return jnp.maximum(x, 0.0) + jnp.log1p(jnp.exp(-jnp.abs(x)))


def _sigmoid(x):
    return 1.0 / (1.0 + jnp.exp(-x))


def _gelu(x):
    c = math.sqrt(2.0 / math.pi)
    return x * (0.5 * (1.0 + jnp.tanh(c * (x + 0.044715 * (x * x * x)))))


def _mix_in_kernel(has_vres, seq_major_x, nb, tm, *refs):
    refs = list(refs)
    (x_ref, first_ref, g_ref, win_ref, murkv_ref, muwag_ref, w0_ref, w1_ref, w2_ref,
     a0_ref, a1_ref, a2_ref, g1_ref, g2_ref, kk_ref, ka_ref, ones_ref) = refs[:17]
    del refs[:17]
    if has_vres:
        muv_ref, v0_ref, v1_ref, v2_ref, vfirst_ref = refs[:5]
        del refs[:5]
    if seq_major_x:
        xt_ref = refs.pop(0)
    zrest_ref, r_ref, lw_ref, k_ref, v_ref, na_ref, bb_ref, gate_ref, hlast_ref, hbuf, zbuf = refs[:11]

    @pl.when(pl.program_id(0) == 0)
    def _():
        first = first_ref[...]
        hbuf[0:nb, :] = first
        zbuf[0:nb, :] = _bdot(first, win_ref[:, 0:R3])

    if seq_major_x:
        xs = refs[-1]
        for b in range(nb):
            _store_seq_rows(xs, pl.ds(b, tm // nb, stride=nb), x_ref[b])
        x = jnp.concatenate([xs[cb] for cb in range(LANE_BLOCKS)], axis=-1)
        xt_ref[...] = x
    else:
        x = x_ref[...]
    h_all = _rms(x, g_ref[...])
    hbuf[nb:nb + tm, :] = h_all
    hlast_ref[...] = h_all[tm - nb:tm, :]

    def part(lo, n):
        h = hbuf[nb + lo:nb + lo + n, :]
        dh = hbuf[lo:lo + n, :] - h
        z = _bdot(h, win_ref[...])
        p_w = _bdot(h + dh * muwag_ref[0:1, :], w1_ref[...])
        p_a = _bdot(h + dh * muwag_ref[1:2, :], a1_ref[...])
        p_g = _bdot(h + dh * muwag_ref[2:3, :], g1_ref[...])
        if has_vres:
            p_v = _bdot(h + dh * muv_ref[...], v1_ref[...])
        zrest_ref[lo:lo + n, :] = z[:, R3:D_IN]
        z_rkv = z[:, 0:R3]
        zbuf[nb + lo:nb + lo + n, :] = z_rkv
        yield
        zmix = z_rkv + (zbuf[lo:lo + n, :] - z_rkv) * murkv_ref[...]
        r = zmix[:, 0:RWKV_DIM]
        k = zmix[:, RWKV_DIM:2 * RWKV_DIM]
        v = zmix[:, 2 * RWKV_DIM:R3]
        kk = k * kk_ref[...]
        w_pre = w0_ref[...] + _bdot(jnp.tanh(p_w), w2_ref[...])
        a = _sigmoid(a0_ref[...] + _bdot(p_a, a2_ref[...]))
        gate = _bdot(_sigmoid(p_g), g2_ref[...])
        ss = _bdot(kk * kk, ones_ref[...])
        if has_vres:
            mix = _sigmoid(v0_ref[...] + _bdot(p_v, v2_ref[...]))
            v_first = jnp.concatenate([vfirst_ref[p, lo:lo + n, :] for p in range(N_PAIRS)], axis=-1)
            v = v + (v_first - v) * mix
        yield
        w_log = -_softplus(-w_pre) - 0.5
        kk = kk / jnp.maximum(jnp.sqrt(ss), 1e-12)
        outs = ((r_ref, r), (lw_ref, -jnp.exp(w_log)), (k_ref, k * (1.0 + (a - 1.0) * ka_ref[...])), (v_ref, v),
                (na_ref, -kk), (bb_ref, kk * a), (gate_ref, gate))
        for ref, val in outs:
            for p in range(N_PAIRS):
                ref[p, lo:lo + n, :] = val[:, p * HEAD_PAIR:(p + 1) * HEAD_PAIR]

    n_part = tm // MIX_IN_PARTS
    _run_lockstep([part(i * n_part, n_part) for i in range(MIX_IN_PARTS)])
    hbuf[0:nb, :] = hbuf[tm:tm + nb, :]
    zbuf[0:nb, :] = zbuf[tm:tm + nb, :]


MIX_IN_PARTS = 2


def _mix_in(x, first, p, vres, v_first, nb, seq):
    m = nb * seq
    tm = _tile(m, max(nb, 8) * MIX_IN_PARTS, 512)
    has_vres = vres is not None
    seq_major_x = x.ndim == 3
    row = lambda w: pl.BlockSpec((tm, w), lambda i: (i, 0))
    pair_row = pl.BlockSpec((N_PAIRS, tm, HEAD_PAIR), lambda i: (0, i, 0))
    ins = [x, first, p["norm_mix"], p["w_in"], p["mu_rkv"], p["mu_wag"], p["w0"], p["w1"], p["w2"],
           p["a0"], p["a1"], p["a2"], p["g1"], p["g2"], p["k_k"], p["k_a"], p["ones_head"]]
    x_spec = pl.BlockSpec((nb, tm // nb, D_MODEL), lambda i: (0, i, 0)) if seq_major_x else row(D_MODEL)
    specs = [x_spec] + [_spec(a) for a in ins[1:]]
    if has_vres:
        extra = [vres["mu_v"], vres["v0"], vres["v1"], vres["v2"]]
        ins += extra + [v_first]
        specs += [_spec(a) for a in extra] + [pair_row]
    wide = jax.ShapeDtypeStruct((N_PAIRS, m, HEAD_PAIR), F32)
    outs = [jax.ShapeDtypeStruct((m, D_IN - R3), F32)] + [wide] * 7 + [jax.ShapeDtypeStruct((nb, D_MODEL), F32)]
    out_specs = [row(D_IN - R3)] + [pair_row] * 7 + [_const_spec((nb, D_MODEL))]
    scratch = [pltpu.VMEM((tm + nb, D_MODEL), F32), pltpu.VMEM((tm + nb, R3), F32)]
    if seq_major_x:
        assert (tm // nb) % 8 == 0
        outs.insert(0, jax.ShapeDtypeStruct((m, D_MODEL), F32))
        out_specs.insert(0, row(D_MODEL))
        scratch.append(pltpu.VMEM((LANE_BLOCKS, tm, LANES), F32))
    return pl.pallas_call(
        functools.partial(_mix_in_kernel, has_vres, seq_major_x, nb, tm),
        grid=(m // tm,),
        in_specs=specs,
        out_specs=out_specs,
        out_shape=outs,
        scratch_shapes=scratch,
        compiler_params=_params("arbitrary"),
        name="mix_in",
    )(*_arrays(ins))


def _wkv_kernel(c, n_seq, group, stride, r_ref, lw_ref, k_ref, v_ref, na_ref, bb_ref, gate_ref, s0_ref,
                rk_ref, lnw_ref, lnb_ref, tri_ref, mean_ref,
                y_ref, sout_ref, s_scr):
    li = pl.program_id(1)

    @pl.when(li == 0)
    def _():
        zero = jnp.zeros((RWKV_HEAD, RWKV_HEAD), F32)
        for j in range(n_seq):
            for p in range(N_PAIRS):
                s_scr[j, p] = jnp.concatenate(
                    [jnp.concatenate([s0_ref[j, 2 * p], zero], axis=1),
                     jnp.concatenate([zero, s0_ref[j, 2 * p + 1]], axis=1)], axis=0)

    lane = lax.broadcasted_iota(jnp.int32, (1, HEAD_PAIR), 1)
    head_masks = (lane < RWKV_HEAD, lane >= RWKV_HEAD)
    trow = lax.broadcasted_iota(jnp.int32, (c, 2 * c), 0)
    tcol = lax.broadcasted_iota(jnp.int32, (c, 2 * c), 1)
    left_half = tcol < c
    scol = jnp.where(left_half, tcol, tcol - c)
    strict = scol < trow
    strict_left = strict & left_half
    strict_right = strict & (tcol >= c)
    incl = scol <= trow
    eye2 = (scol == trow).astype(F32)
    drow = lax.broadcasted_iota(jnp.int32, (2 * c, 2 * c), 0)
    dcol = lax.broadcasted_iota(jnp.int32, (2 * c, 2 * c), 1)
    diag_blocks = jnp.bitwise_xor(drow - c, dcol - c) >= 0
    brow = lax.broadcasted_iota(jnp.int32, (HEAD_PAIR, HEAD_PAIR), 0)
    bcol = lax.broadcasted_iota(jnp.int32, (HEAD_PAIR, HEAD_PAIR), 1)
    same_head = jnp.bitwise_xor(brow - RWKV_HEAD, bcol - RWKV_HEAD) >= 0
    n_double = max(int(math.log2(c)) - 1, 0)
    tri = tri_ref[...]
    zeros_c = jnp.zeros((c, HEAD_PAIR), F32)

    def pair_chunk(lw, r, k, v, na, bb, gate, sp, cols):
        lw_hi, lw_lo = _split2(lw)
        cl = jnp.dot(tri, lw_hi, preferred_element_type=F32) + jnp.dot(tri, lw_lo, preferred_element_type=F32)
        yield
        e_in = jnp.exp(cl)
        e_neg = jnp.exp(-cl)
        g_last = jnp.exp(jnp.sum(lw, axis=0, keepdims=True))
        at = na * jnp.exp(cl - lw)
        rt = r * e_in
        bt = bb * e_neg
        kt = k * e_neg
        ar = jnp.concatenate([at, rt], axis=0)
        bk = jnp.concatenate([bt, kt], axis=0)
        a_s = _bdot_nt(ar, sp)
        g0 = _bdot_nt(jnp.where(head_masks[0], ar, 0.0), bk)
        g1 = _bdot_nt(jnp.where(head_masks[1], ar, 0.0), jnp.concatenate([kt, bt], axis=0))
        yield
        ga0, ga1 = g0[0:c, :], g1[0:c, :]
        low = jnp.where(strict, jnp.where(left_half, ga0, ga1), 0.0)
        ak0 = _bdot(jnp.where(strict_right, ga0, 0.0), jnp.concatenate([zeros_c, v], axis=0))
        ak1 = _bdot(jnp.where(strict_left, ga1, 0.0), jnp.concatenate([v, zeros_c], axis=0))
        rhs = jnp.concatenate([jnp.where(head_masks[0], a_s[0:c, :] + ak0, 0.0),
                               jnp.where(head_masks[1], a_s[0:c, :] + ak1, 0.0)], axis=0)
        blockdiag = lambda x: jnp.where(diag_blocks, jnp.concatenate([x, x], axis=0), 0.0)
        t_inv = eye2 + low
        pw = low
        if n_double > 0:
            pw = _bdot(pw, blockdiag(pw))
            yield
        for level in range(n_double):
            if level == n_double - 1:
                t_inv = t_inv + _bdot(t_inv, blockdiag(pw))
            else:
                both = _bdot(jnp.concatenate([t_inv, pw], axis=0), blockdiag(pw))
                t_inv = t_inv + both[0:c, :]
                pw = both[c:2 * c, :]
            yield
        u = _bdot(t_inv, rhs)
        yield
        uv = jnp.concatenate([u, v], axis=0)
        y = a_s[c:2 * c, :] + jnp.where(
            head_masks[0], _bdot(jnp.where(incl, g0[c:2 * c, :], 0.0), uv),
            _bdot(jnp.where(incl, g1[c:2 * c, :], 0.0), jnp.concatenate([v, u], axis=0)))
        upd = _bdot_tn(uv, bk * g_last)
        s_new = sp * g_last + jnp.where(same_head, upd, 0.0)
        yield
        stats = _bdot(jnp.concatenate([y, r * k * rk_ref[:, cols]], axis=0), mean_ref[...])
        mu = stats[0:c, :]
        bonus = stats[c:2 * c, :] * float(RWKV_HEAD) * v
        yield
        d = y - mu
        var = _bdot(d * d, mean_ref[...])
        yield
        yn = d * lax.rsqrt(var + GN_EPS) * lnw_ref[:, cols] + lnb_ref[:, cols]
        return (yn + bonus) * gate, s_new

    first_seq = pl.program_id(0) * n_seq

    def seq_group(gi, carry):
        units = []
        for jj in range(group):
            j = gi * group + jj
            rows = pl.ds(first_seq + j, c, stride=stride)
            for p in range(N_PAIRS):
                cols = slice(p * HEAD_PAIR, (p + 1) * HEAD_PAIR)
                vals = [ref[p, rows, :] for ref in (lw_ref, r_ref, k_ref, v_ref, na_ref, bb_ref, gate_ref)]
                units.append((j, p, rows, pair_chunk(*vals, s_scr[j, p], cols)))
        results = _run_lockstep([gen for (_, _, _, gen) in units])
        for (j, p, rows, _), (y, s_new) in zip(units, results):
            y_ref[p, rows, :] = y
            s_scr[j, p] = s_new
        return carry

    lax.fori_loop(0, n_seq // group, seq_group, 0)

    @pl.when(li == pl.num_programs(1) - 1)
    def _():
        for j in range(n_seq):
            for p in range(N_PAIRS):
                sp = s_scr[j, p]
                sout_ref[j, 2 * p] = sp[0:RWKV_HEAD, 0:RWKV_HEAD]
                sout_ref[j, 2 * p + 1] = sp[RWKV_HEAD:HEAD_PAIR, RWKV_HEAD:HEAD_PAIR]


WKV_SEQ_PER_STEP = 8
WKV_SEQ_GROUP = 8


def _wkv(arrs, s_all, layer, p, nb, seq):
    c = min(WKV_CHUNK, seq)
    n_seq = nb if seq > c else min(nb, WKV_SEQ_PER_STEP)
    assert seq % c == 0 and c % 8 == 0 and nb % n_seq == 0 and n_seq % WKV_SEQ_GROUP == 0
    blk = pl.BlockSpec((N_PAIRS, c * nb, HEAD_PAIR), lambda g, l: (0, l, 0))
    tri = jnp.tril(jnp.ones((c, c), F32)).astype(BF16)
    st_spec = pl.BlockSpec((None, n_seq, RWKV_HEADS, RWKV_HEAD, RWKV_HEAD), lambda g, l: (layer, g, 0, 0, 0))
    consts = [p["r_k"], p["ln_x_w"], p["ln_x_b"], tri, p["mean_pair"]]
    return pl.pallas_call(
        functools.partial(_wkv_kernel, c, n_seq, WKV_SEQ_GROUP, nb),
        grid=(nb // n_seq, seq // c),
        in_specs=[blk] * 7 + [st_spec] + [_spec(a) for a in consts],
        out_specs=[blk, st_spec],
        out_shape=[jax.ShapeDtypeStruct((N_PAIRS, seq * nb, HEAD_PAIR), F32), jax.ShapeDtypeStruct(s_all.shape, F32)],
        input_output_aliases={7: 1},
        scratch_shapes=[pltpu.VMEM((n_seq, N_PAIRS, HEAD_PAIR, HEAD_PAIR), F32)],
        compiler_params=_params("arbitrary", "arbitrary"),
        name="wkv",
    )(*arrs, s_all, *_arrays(consts))


def _ssm_kernel(reset_first, nb, tm, zx_ref, zg_ref, u_ref,
                conv0_ref, h0_ref, cw_ref, cb_ref, wa_ref, ba_ref, wi_ref, bi_ref, lam_ref,
                re0_ref, im0_ref, are_ref, aim_ref, bre_ref, bim_ref, cre_ref, cim_ref, d_ref, wglu_ref, bglu_ref,
                y_ref, conv_out_ref, h_out_ref, yc_ref, re_out_ref, im_out_ref,
                xbuf, abuf, bbuf, hcar, xre, xim, car_re, car_im):
    i = pl.program_id(0)
    hist = (CONV_W - 1) * nb

    @pl.when(i == 0)
    def _():
        xbuf[0:hist, :] = conv0_ref[...]
        hcar[...] = h0_ref[...]
        car_re[...] = re0_ref[...]
        car_im[...] = im0_ref[...]

    u = u_ref[...]
    xre[...] = _bdot(u, bre_ref[...])
    xim[...] = _bdot(u, bim_ref[...])

    xbuf[hist:hist + tm, :] = zx_ref[...]
    xc = cb_ref[...]
    for j in range(CONV_W):
        xc = xc + xbuf[j * nb:j * nb + tm, :] * cw_ref[j:j + 1, :]
    conv_out_ref[...] = xbuf[tm:tm + hist, :]
    xbuf[0:hist, :] = xbuf[tm:tm + hist, :]

    gate_a = _sigmoid(_bdot(xc, wa_ref[...]) + ba_ref[...])
    gate_i = _sigmoid(_bdot(xc, wi_ref[...]) + bi_ref[...])
    log_a = -LRU_C * gate_a * _softplus(-lam_ref[...])
    a_sq = jnp.exp(2.0 * log_a)
    mult = jnp.sqrt(-jnp.tanh(log_a) * (a_sq + 1.0))
    if reset_first:
        row = lax.broadcasted_iota(jnp.int32, (tm, 1), 0)
        mult = jnp.where(row < jnp.where(i == 0, nb, 0), 1.0, mult)
    abuf[...] = jnp.exp(log_a)
    bbuf[...] = mult * gate_i * xc

    a_re = jnp.broadcast_to(are_ref[...], (nb, S5_WIDTH))
    a_im = jnp.broadcast_to(aim_ref[...], (nb, S5_WIDTH))

    def step(t, carry):
        h, s_re, s_im = carry
        rows = pl.ds(pl.multiple_of(t * nb, nb), nb)
        h = abuf[rows, :] * h + bbuf[rows, :]
        bbuf[rows, :] = h
        n_re = a_re * s_re - a_im * s_im + xre[rows, :]
        n_im = a_re * s_im + a_im * s_re + xim[rows, :]
        xre[rows, :] = n_re
        xim[rows, :] = n_im
        return h, n_re, n_im

    h, s_re, s_im = lax.fori_loop(0, tm // nb, step, (hcar[...], car_re[...], car_im[...]),
                                  unroll=min(4, tm // nb))
    hcar[...] = h
    h_out_ref[...] = h
    car_re[...] = s_re
    car_im[...] = s_im
    re_out_ref[...] = s_re
    im_out_ref[...] = s_im

    y_ref[...] = bbuf[...] * _gelu(zg_ref[...])
    y = _bdot(xre[...], cre_ref[...]) - _bdot(xim[...], cim_ref[...]) + d_ref[...] * u
    o = _bdot(_gelu(y), wglu_ref[...]) + bglu_ref[...]
    yc_ref[...] = o[:, 0:S5_DIM] * _sigmoid(o[:, S5_DIM:2 * S5_DIM])


def _ssm(zrest, conv0, h0, re0, im0, p, reset_first, nb):
    assert LRU_DIM == S5_DIM
    m = zrest.shape[0]
    tm = _tile(m, max(nb, 8), 512)
    hist = (CONV_W - 1) * nb
    consts = [conv0, h0, p["conv_w"], p["conv_b"], p["lru_wa"], p["lru_ba"], p["lru_wi"], p["lru_bi"],
              p["lru_lambda"], re0, im0, p["s5_abar_re"], p["s5_abar_im"], p["s5_bbar_re"], p["s5_bbar_im"],
              p["s5_c_re"], p["s5_c_im"], p["s5_d"], p["s5_w_glu"], p["s5_b_glu"]]
    col = lambda j: pl.BlockSpec((tm, LRU_DIM), lambda i: (i, j))
    st = jax.ShapeDtypeStruct((nb, S5_WIDTH), F32)
    return pl.pallas_call(
        functools.partial(_ssm_kernel, reset_first, nb, tm),
        grid=(m // tm,),
        in_specs=[col(0), col(1), col(2)] + [_spec(a) for a in consts],
        out_specs=[col(0), _const_spec((hist, LRU_DIM)), _const_spec((nb, LRU_DIM)),
                   col(0), _const_spec((nb, S5_WIDTH)), _const_spec((nb, S5_WIDTH))],
        out_shape=[jax.ShapeDtypeStruct((m, LRU_DIM), F32), jax.ShapeDtypeStruct((hist, LRU_DIM), F32),
                   jax.ShapeDtypeStruct((nb, LRU_DIM), F32), jax.ShapeDtypeStruct((m, S5_DIM), F32), st, st],
        scratch_shapes=[pltpu.VMEM((tm + hist, LRU_DIM), F32), pltpu.VMEM((tm, LRU_DIM), F32),
                        pltpu.VMEM((tm, LRU_DIM), F32), pltpu.VMEM((nb, LRU_DIM), F32),
                        pltpu.VMEM((tm, S5_WIDTH), F32), pltpu.VMEM((tm, S5_WIDTH), F32),
                        pltpu.VMEM((nb, S5_WIDTH), F32), pltpu.VMEM((nb, S5_WIDTH), F32)],
        compiler_params=_params("arbitrary"),
        name="ssm",
    )(zrest, zrest, zrest, *_arrays(consts))


def _mix_out_kernel(x_ref, ya_ref, yb_ref, yc_ref, wa_ref, wb_ref, wc_ref, gq_ref, wq_ref, x1_ref, q_ref):
    ya = jnp.concatenate([ya_ref[p] for p in range(N_PAIRS)], axis=-1)
    x1 = (x_ref[...] + _bdot(ya, wa_ref[...]) + _bdot(yb_ref[...], wb_ref[...])
          + _bdot(yc_ref[...], wc_ref[...]))
    q = _bdot(_rms(x1, gq_ref[...]), wq_ref[...])
    for cb in range(LANE_BLOCKS):
        cols = slice(cb * LANES, (cb + 1) * LANES)
        x1_ref[cb] = x1[:, cols]
        q_ref[cb] = q[:, cols]


def _mix_out(x, ya, yb, yc, p):
    m = x.shape[0]
    tm = _tile(m, 8, 512)
    row = lambda w: pl.BlockSpec((tm, w), lambda i: (i, 0))
    split = pl.BlockSpec((LANE_BLOCKS, tm, LANES), lambda i: (0, i, 0))
    consts = [p["w_out_a"], p["w_out_b"], p["w_out_c"], p["norm_mem_q"], p["mem_wq"]]
    out = jax.ShapeDtypeStruct((LANE_BLOCKS, m, LANES), F32)
    return pl.pallas_call(
        _mix_out_kernel,
        grid=(m // tm,),
        in_specs=[row(D_MODEL), pl.BlockSpec((N_PAIRS, tm, HEAD_PAIR), lambda i: (0, i, 0)), row(LRU_DIM),
                  row(S5_DIM)] + [_spec(a) for a in consts],
        out_specs=[split, split],
        out_shape=[out, out],
        compiler_params=_params("parallel"),
        name="mix_out",
    )(x, ya, yb, yc, *_arrays(consts))


def _load_seq_rows(ref, rows):
    return jnp.concatenate([ref[cb, rows, :] for cb in range(LANE_BLOCKS)], axis=-1)


def _store_seq_rows(ref, rows, val):
    for cb in range(LANE_BLOCKS):
        ref[cb, rows, :] = val[:, cb * LANES:(cb + 1) * LANES]


def _softmax_rows(s):
    e = jnp.exp(s - jnp.max(s, axis=-1, keepdims=True))
    return e / jnp.sum(e, axis=-1, keepdims=True)


def _attn_kernel(tl, n_seq, stride, q_ref, x_ref, mk_ref, mv_ref, wo_ref, o_ref):
    scale = MEM_HEAD_DIM ** -0.5
    first_seq = pl.program_id(0) * n_seq
    head_cols = [slice(hh * MEM_HEAD_DIM, (hh + 1) * MEM_HEAD_DIM) for hh in range(MEM_HEADS)]

    def one_seq(j):
        rows = pl.ds(first_seq + j, tl, stride=stride)
        q = _load_seq_rows(q_ref, rows)
        scores = [_bdot_nt(q[:, cols], mk_ref[j, :, cols]) * scale for cols in head_cols]
        yield
        heads = [_bdot(_softmax_rows(s), mv_ref[j, :, cols]) for s, cols in zip(scores, head_cols)]
        yield
        out = _load_seq_rows(x_ref, rows) + _bdot(jnp.concatenate(heads, axis=-1), wo_ref[...])
        yield
        _store_seq_rows(o_ref, rows, out)

    for j0 in range(0, n_seq, ATTN_LOCKSTEP):
        _run_lockstep([one_seq(j) for j in range(j0, min(j0 + ATTN_LOCKSTEP, n_seq))])


def _attn_cache_kernel(tl, n_seq, stride, q_ref, x_ref, mk_ref, mv_ref, wo_ref, o_ref):
    scale = MEM_HEAD_DIM ** -0.5
    first_seq = pl.program_id(0) * n_seq
    qrow = lax.broadcasted_iota(jnp.int32, (MEM_HEADS * tl, N_MEM * MEM_HEADS), 0)
    kcol = lax.broadcasted_iota(jnp.int32, (MEM_HEADS * tl, N_MEM * MEM_HEADS), 1)
    q_head = jnp.zeros_like(qrow)
    for hh in range(1, MEM_HEADS):
        q_head = q_head + (qrow >= hh * tl).astype(jnp.int32)
    same_head = q_head == jnp.bitwise_and(kcol, MEM_HEADS - 1)

    def one_seq(j):
        rows = pl.ds(first_seq + j, tl, stride=stride)
        q = _load_seq_rows(q_ref, rows)
        qh = jnp.concatenate([q[:, hh * MEM_HEAD_DIM:(hh + 1) * MEM_HEAD_DIM] for hh in range(MEM_HEADS)], axis=0)
        k2 = mk_ref[j].reshape(N_MEM * MEM_HEADS, MEM_HEAD_DIM)
        s = jnp.where(same_head, _bdot_nt(qh, k2) * scale, MASKED_SCORE)
        yield
        v2 = mv_ref[j].reshape(N_MEM * MEM_HEADS, MEM_HEAD_DIM)
        oh = _bdot(_softmax_rows(s), v2)
        yield
        o = jnp.concatenate([oh[hh * tl:(hh + 1) * tl, :] for hh in range(MEM_HEADS)], axis=-1)
        out = _load_seq_rows(x_ref, rows) + _bdot(o, wo_ref[...])
        yield
        _store_seq_rows(o_ref, rows, out)

    _run_lockstep([one_seq(j) for j in range(n_seq)])


MASKED_SCORE = -1e30
ATTN_SEQ_PER_STEP = 4
ATTN_ROWS = 128
ATTN_LOCKSTEP = 4


def _attn(q, x1, mk, mv, kv_offset, wo, nb, seq):
    tl = _tile(seq, 8, ATTN_ROWS)
    n_seq = nb if seq > tl else min(nb, ATTN_SEQ_PER_STEP)
    assert nb % n_seq == 0 and kv_offset % n_seq == 0 and MEM_HEADS & (MEM_HEADS - 1) == 0
    blk = pl.BlockSpec((LANE_BLOCKS, tl * nb, LANES), lambda g, l: (0, l, 0))
    kv_index = lambda g, l: (kv_offset // n_seq + g,) + (0,) * (mk.ndim - 1)
    kv = pl.BlockSpec((n_seq,) + mk.shape[1:], kv_index)
    body = _attn_kernel if mk.ndim == 3 else _attn_cache_kernel
    return pl.pallas_call(
        functools.partial(body, tl, n_seq, nb),
        grid=(nb // n_seq, seq // tl),
        in_specs=[blk, blk, kv, kv, _spec(wo)],
        out_specs=blk,
        out_shape=jax.ShapeDtypeStruct((LANE_BLOCKS, seq * nb, LANES), F32),
        compiler_params=_params("arbitrary", "arbitrary"),
        name="mem_attn",
    )(q, x1, mk, mv, *_arrays([wo]))


def _ffn_kernel(d_ff, final_norm, n_seq, x_ref, g_ref, wup_ref, wdown_ref, *rest):
    x = jnp.concatenate([x_ref[cb] for cb in range(LANE_BLOCKS)], axis=-1)
    h = _rms(x, g_ref[...]).astype(BF16)
    gt = jnp.dot(h, wup_ref[:, 0:d_ff], preferred_element_type=F32)
    up = jnp.dot(h, wup_ref[:, d_ff:2 * d_ff], preferred_element_type=F32)
    act = gt * _sigmoid(gt) * up
    out = x + _bdot(act, wdown_ref[...])
    if final_norm:
        out = _rms(out, rest[0][...])
    if n_seq:
        o_ref, os = rest[-2:]
        for cb in range(LANE_BLOCKS):
            os[cb] = out[:, cb * LANES:(cb + 1) * LANES]
        for b in range(n_seq):
            o_ref[b] = _load_seq_rows(os, pl.ds(b, out.shape[0] // n_seq, stride=n_seq))
    else:
        rest[-1][...] = out


def _ffn(x, p, g_final=None, seq_major_out=None):
    m = x.shape[1]
    tm = _tile(m, 8, 512)
    d_ff = p["ffn_w_down"].shape[0]
    consts = [p["norm_ffn"], p["ffn_w_up"], p["ffn_w_down"]] + ([] if g_final is None else [g_final])
    if seq_major_out is None:
        n_seq, scratch = 0, []
        out_spec = pl.BlockSpec((tm, D_MODEL), lambda i: (i, 0))
        out_shape = jax.ShapeDtypeStruct((m, D_MODEL), F32)
    else:
        n_seq, seq = seq_major_out
        assert tm % n_seq == 0 and (tm // n_seq) % 8 == 0 and n_seq * seq == m
        scratch = [pltpu.VMEM((LANE_BLOCKS, tm, LANES), F32)]
        out_spec = pl.BlockSpec((n_seq, tm // n_seq, D_MODEL), lambda i: (0, i, 0))
        out_shape = jax.ShapeDtypeStruct((n_seq, seq, D_MODEL), F32)
    return pl.pallas_call(
        functools.partial(_ffn_kernel, d_ff, g_final is not None, n_seq),
        grid=(m // tm,),
        in_specs=[pl.BlockSpec((LANE_BLOCKS, tm, LANES), lambda i: (0, i, 0))] + [_spec(a) for a in consts],
        out_specs=out_spec,
        out_shape=out_shape,
        scratch_shapes=scratch,
        compiler_params=_params("parallel"),
        name="ffn",
    )(x, *_arrays(consts))


def _mem_project_kernel(x_ref, g_ref, wk_ref, wv_ref, k_ref, v_ref, kb_ref, vb_ref):
    m = _rms(x_ref[...], g_ref[...]).astype(BF16)
    k = jnp.dot(m, wk_ref[...], preferred_element_type=F32)
    v = jnp.dot(m, wv_ref[...], preferred_element_type=F32)
    k_ref[...] = k.reshape(k_ref.shape)
    v_ref[...] = v.reshape(v_ref.shape)
    kb_ref[...] = k.astype(BF16)
    vb_ref[...] = v.astype(BF16)


MEM_SEQ_PER_STEP = 2


def _mem_project(mem, g, wk, wv):
    n, n_mem, _ = mem.shape
    depth = wk.shape[0]
    tm = MEM_SEQ_PER_STEP * n_mem
    assert n % MEM_SEQ_PER_STEP == 0
    per_layer = lambda a: pl.BlockSpec((None,) + a.shape[1:], lambda l, i: (l,) + (0,) * (a.ndim - 1))
    cache = pl.BlockSpec((None, MEM_SEQ_PER_STEP, n_mem, MEM_HEADS, MEM_HEAD_DIM), lambda l, i: (l, i, 0, 0, 0))
    rows_b = pl.BlockSpec((None, tm, D_MODEL), lambda l, i: (l, i, 0))
    out = jax.ShapeDtypeStruct((depth, n, n_mem, MEM_HEADS, MEM_HEAD_DIM), F32)
    outb = jax.ShapeDtypeStruct((depth, n * n_mem, D_MODEL), BF16)
    return pl.pallas_call(
        _mem_project_kernel, grid=(depth, n // MEM_SEQ_PER_STEP),
        in_specs=[pl.BlockSpec((tm, D_MODEL), lambda l, i: (i, 0)), per_layer(g), per_layer(wk), per_layer(wv)],
        out_specs=[cache, cache, rows_b, rows_b], out_shape=[out, out, outb, outb],
        compiler_params=_params("parallel", "parallel"),
        name="mem_project",
    )(mem.reshape(n * n_mem, D_MODEL), g, wk, wv)


def _block_diag(blocks):
    n, r, c = blocks.shape[-3:]
    eye = jnp.eye(n, dtype=blocks.dtype)
    out = eye[:, None, :, None] * blocks[..., :, :, None, :]
    return out.reshape(blocks.shape[:-3] + (n * r, n * c))


def _stack_params(w):
    depth = w["norm_mix"].shape[0]
    row = lambda a: a.reshape(a.shape[0], 1, -1).astype(F32)
    bf = lambda a: a.astype(BF16)
    t = lambda a: jnp.swapaxes(a, -1, -2)
    p = {k: row(w[k]) for k in ("norm_mix", "mu_rkv", "w0", "a0", "k_k", "k_a", "r_k", "ln_x_w", "ln_x_b", "conv_b",
                                "lru_ba", "lru_bi", "lru_lambda", "s5_d", "s5_b_glu", "norm_mem_q", "norm_mem_kv",
                                "norm_ffn", "mu_v", "v0")}
    p.update({k: bf(w[k]) for k in ("w_in", "w1", "w2", "a1", "a2", "g1", "g2", "s5_w_glu", "mem_wq", "mem_wk",
                                    "mem_wv", "mem_wo", "ffn_w_up", "ffn_w_down", "v1", "v2")})
    p["mu_wag"], p["conv_w"] = w["mu_wag"], w["conv_w"]
    p["lru_wa"], p["lru_wi"] = bf(_block_diag(w["lru_wa"])), bf(_block_diag(w["lru_wi"]))
    w_out = bf(w["w_out"])
    p["w_out_a"] = w_out[:, 0:RWKV_DIM]
    p["w_out_b"] = w_out[:, RWKV_DIM:RWKV_DIM + LRU_DIM]
    p["w_out_c"] = w_out[:, RWKV_DIM + LRU_DIM:]
    lam_re, lam_im = w["s5_a_re"].astype(F32), w["s5_a_im"].astype(F32)
    dt = jnp.exp(w["s5_log_dt"].astype(F32))[..., None]
    mag = jnp.exp(lam_re * dt)
    abar_re, abar_im = mag * jnp.cos(lam_im * dt), mag * jnp.sin(lam_im * dt)
    den = lam_re * lam_re + lam_im * lam_im
    q_re = ((abar_re - 1.0) * lam_re + abar_im * lam_im) / den
    q_im = (abar_im * lam_re - (abar_re - 1.0) * lam_im) / den
    b_re, b_im = w["s5_b_re"].astype(F32), w["s5_b_im"].astype(F32)
    bbar_re = q_re[..., None] * b_re - q_im[..., None] * b_im
    bbar_im = q_re[..., None] * b_im + q_im[..., None] * b_re
    p["s5_abar_re"] = abar_re.reshape(depth, 1, S5_WIDTH)
    p["s5_abar_im"] = abar_im.reshape(depth, 1, S5_WIDTH)
    p["s5_bbar_re"] = bf(_block_diag(t(bbar_re)))
    p["s5_bbar_im"] = bf(_block_diag(t(bbar_im)))
    p["s5_c_re"] = bf(_block_diag(t(w["s5_c_re"])))
    p["s5_c_im"] = bf(_block_diag(t(w["s5_c_im"])))
    return p


def _layer_params(stacked, l, consts):
    vres_keys = ("mu_v", "v0", "v1", "v2")
    p = {k: _Layer(a, l) for k, a in stacked.items() if k not in vres_keys}
    p.update(consts)
    vres = {k: _Layer(stacked[k], l - 1) for k in vres_keys} if l > 0 else None
    return p, vres


def _time_major(a):
    return jnp.swapaxes(a, 0, 1).reshape((a.shape[0] * a.shape[1],) + a.shape[2:])


def _run_group(x, mem_k, mem_v, kv_stride, shift0, wkv0, conv0, lru0, re0, im0, reset_first, layers, norm_final,
               relayout_in_kernel):
    nb, seq, _ = x.shape
    xt = x if relayout_in_kernel else _time_major(x)
    outs = {k: [] for k in ("shift", "conv", "lru", "re", "im")}
    wkv_all = wkv0
    v_first = None
    for l, (p, vres) in enumerate(layers):
        last = l == len(layers) - 1
        mixed = _mix_in(xt, _Layer(shift0, l), p, vres, v_first, nb, seq)
        if xt.ndim == 3:
            xt, mixed = mixed[0], mixed[1:]
        zrest, r, lw, k, v, na, bb, gate, h_last = mixed
        if l == 0:
            v_first = v
        ya, wkv_all = _wkv((r, lw, k, v, na, bb, gate), wkv_all, l, p, nb, seq)
        yb, conv_new, lru_new, yc, re_new, im_new = _ssm(
            zrest, _time_major(conv0[l]), _Layer(lru0, l), _Layer(re0.reshape(-1, nb, S5_WIDTH), l),
            _Layer(im0.reshape(-1, nb, S5_WIDTH), l), p, reset_first, nb)
        x1, q = _mix_out(xt, ya, yb, yc, p)
        x2 = _attn(q, x1, mem_k, mem_v, l * kv_stride, p["mem_wo"], nb, seq)
        xt = _ffn(x2, p, norm_final if last else None, (nb, seq) if last and relayout_in_kernel else None)
        outs["shift"].append(h_last)
        outs["conv"].append(jnp.swapaxes(conv_new.reshape(CONV_W - 1, nb, LRU_DIM), 0, 1))
        outs["lru"].append(lru_new)
        outs["re"].append(re_new.reshape(nb, S5_GROUPS, S5_STATE))
        outs["im"].append(im_new.reshape(nb, S5_GROUPS, S5_STATE))
    y = xt if relayout_in_kernel else jnp.swapaxes(xt.reshape(seq, nb, D_MODEL), 0, 1)
    st = {k: jnp.stack(v) for k, v in outs.items()}
    return (y, st["shift"], wkv_all, st["conv"], st["lru"], st["re"], st["im"])


def kernel(x_prompt, x_sample, mem_prompt, state_shift, state_wkv, state_conv, state_lru, state_s5_re, state_s5_im, cache_mem_k, cache_mem_v, norm_mix, w_in, w_out, mu_rkv, mu_wag, mu_v, w0, w1, w2, a0, a1, a2, v0, v1, v2, g1, g2, k_k, k_a, r_k, ln_x_w, ln_x_b, conv_w, conv_b, lru_wa, lru_ba, lru_wi, lru_bi, lru_lambda, s5_a_re, s5_a_im, s5_log_dt, s5_b_re, s5_b_im, s5_c_re, s5_c_im, s5_d, s5_w_glu, s5_b_glu, norm_mem_q, norm_mem_kv, mem_wq, mem_wk, mem_wv, mem_wo, norm_ffn, ffn_w_up, ffn_w_down, norm_final):
    w = dict(norm_mix=norm_mix, w_in=w_in, w_out=w_out, mu_rkv=mu_rkv, mu_wag=mu_wag, mu_v=mu_v, w0=w0, w1=w1,
             w2=w2, a0=a0, a1=a1, a2=a2, v0=v0, v1=v1, v2=v2, g1=g1, g2=g2, k_k=k_k, k_a=k_a, r_k=r_k,
             ln_x_w=ln_x_w, ln_x_b=ln_x_b, conv_w=conv_w, conv_b=conv_b, lru_wa=lru_wa, lru_ba=lru_ba,
             lru_wi=lru_wi, lru_bi=lru_bi, lru_lambda=lru_lambda, s5_a_re=s5_a_re, s5_a_im=s5_a_im,
             s5_log_dt=s5_log_dt, s5_b_re=s5_b_re, s5_b_im=s5_b_im, s5_c_re=s5_c_re, s5_c_im=s5_c_im,
             s5_d=s5_d, s5_w_glu=s5_w_glu, s5_b_glu=s5_b_glu, norm_mem_q=norm_mem_q, norm_mem_kv=norm_mem_kv,
             mem_wq=mem_wq, mem_wk=mem_wk, mem_wv=mem_wv, mem_wo=mem_wo, norm_ffn=norm_ffn, ffn_w_up=ffn_w_up,
             ffn_w_down=ffn_w_down)
    depth = norm_mix.shape[0]
    ones64 = jnp.ones((RWKV_HEADS, RWKV_HEAD, RWKV_HEAD), F32)
    consts = {"ones_head": _block_diag(ones64).astype(BF16),
              "mean_pair": (_block_diag(ones64[:2]) / RWKV_HEAD).astype(BF16)}
    stacked = _stack_params(w)
    layers = [_layer_params(stacked, l, consts) for l in range(depth)]
    g_final = norm_final.reshape(1, D_MODEL)

    bp, n_mem, _ = mem_prompt.shape
    mem_k_p, mem_v_p, mem_k_b, mem_v_b = _mem_project(
        mem_prompt, stacked["norm_mem_kv"], stacked["mem_wk"], stacked["mem_wv"])
    mem_k_b = mem_k_b.reshape(depth * bp, n_mem, D_MODEL)
    mem_v_b = mem_v_b.reshape(depth * bp, n_mem, D_MODEL)

    zeros = lambda *s: jnp.zeros((depth, bp) + s, F32)
    prompt = _run_group(
        x_prompt, mem_k_b, mem_v_b, bp,
        zeros(D_MODEL), zeros(RWKV_HEADS, RWKV_HEAD, RWKV_HEAD), zeros(CONV_W - 1, LRU_DIM), zeros(LRU_DIM),
        zeros(S5_GROUPS, S5_STATE), zeros(S5_GROUPS, S5_STATE), True, layers, g_final, True)
    bs = x_sample.shape[0]
    cache_shape = (depth * bs, n_mem, MEM_HEADS, MEM_HEAD_DIM)
    sample = _run_group(
        x_sample, cache_mem_k.reshape(cache_shape), cache_mem_v.reshape(cache_shape),
        bs, state_shift, state_wkv, state_conv, state_lru, state_s5_re, state_s5_im, False, layers, g_final, False)
    (y_p, sh_p, wkv_p, conv_p, lru_p, re_p, im_p) = prompt
    (y_s, sh_s, wkv_s, conv_s, lru_s, re_s, im_s) = sample
    return (y_p, y_s, sh_p, sh_s, wkv_p, wkv_s, conv_p, conv_s, lru_p, lru_s, re_p, re_s, im_p, im_s,
            mem_k_p, mem_v_p)
```

```python
import functools
import math

import jax
import jax.numpy as jnp
from jax import lax
from jax.experimental import pallas as pl
from jax.experimental.pallas import tpu as pltpu

F32 = jnp.float32
BF16 = jnp.bfloat16

D_MODEL = 1024
RWKV_DIM = 512
RWKV_HEAD = 64
RWKV_HEADS = 8
LANES = 128
LANE_BLOCKS = D_MODEL // LANES
HEAD_PAIR = 2 * RWKV_HEAD
N_PAIRS = RWKV_DIM // HEAD_PAIR
LRU_DIM = 256
LRU_BLOCKS = 4
CONV_W = 4
LRU_C = 8.0
S5_DIM = 256
S5_GROUP = 16
S5_GROUPS = 16
S5_STATE = 64
S5_WIDTH = S5_GROUPS * S5_STATE
R3 = 3 * RWKV_DIM
D_IN = R3 + 2 * LRU_DIM + S5_DIM
N_MEM = 256
MEM_HEADS = 4
MEM_HEAD_DIM = 256
RMS_EPS = 1e-6
GN_EPS = 64e-5
WKV_CHUNK = 64

V7X_VMEM_LIMIT = 56 * 1024 * 1024


def _params(*sem):
    return pltpu.CompilerParams(dimension_semantics=sem, vmem_limit_bytes=V7X_VMEM_LIMIT)


def _tile(m, mult, target):
    best = mult
    t = mult
    while t <= min(m, target):
        if m % t == 0:
            best = t
        t += mult
    assert m % best == 0, (m, mult, target)
    return best


def _const_spec(shape):
    nd = len(shape)
    return pl.BlockSpec(shape, lambda *_: (0,) * nd)


class _Layer:
    def __init__(self, arr, layer):
        self.arr, self.layer, self.shape = arr, layer, arr.shape[1:]


def _spec(a):
    if isinstance(a, _Layer):
        nd = len(a.shape)
        return pl.BlockSpec((None,) + a.shape, lambda *_: (a.layer,) + (0,) * nd)
    return _const_spec(a.shape)


def _arrays(operands):
    return [a.arr if isinstance(a, _Layer) else a for a in operands]


def _bdot(a, b):
    return jnp.dot(a.astype(BF16), b.astype(BF16), preferred_element_type=F32)


def _bdot_nt(a, b):
    return lax.dot_general(a.astype(BF16), b.astype(BF16), (((1,), (1,)), ((), ())),
                           preferred_element_type=F32)


def _bdot_tn(a, b):
    return lax.dot_general(a.astype(BF16), b.astype(BF16), (((0,), (0,)), ((), ())),
                           preferred_element_type=F32)


def _split2(x):
    hi = x.astype(BF16)
    lo = (x - hi.astype(F32)).astype(BF16)
    return hi, lo


def _run_lockstep(gens):
    results = {}
    while len(results) < len(gens):
        for i, gen in enumerate(gens):
            if i not in results:
                try:
                    next(gen)
                except StopIteration as stop:
                    results[i] = stop.value
    return [results[i] for i in range(len(gens))]


def _rms(x, g):
    return x * lax.rsqrt(jnp.mean(x * x, axis=-1, keepdims=True) + RMS_EPS) * g


def _softplus(x):
    return jnp.maximum(x, 0.0) + jnp.log1p(jnp.exp(-jnp.abs(x)))


def _sigmoid(x):
    return 1.0 / (1.0 + jnp.exp(-x))


def _gelu(x):
    c = math.sqrt(2.0 / math.pi)
    return x * (0.5 * (1.0 + jnp.tanh(c * (x + 0.044715 * (x * x * x)))))


N_SSM_CONSTS, N_SSM_OUTS, N_SSM_SCRATCH = 20, 6, 8


def _mix_in_kernel(has_vres, seq_major_x, reset_first, nb, tm, *refs):
    refs = list(refs)
    (x_ref, first_ref, g_ref, win_ref, murkv_ref, muwag_ref, w0_ref, w1_ref, w2_ref,
     a0_ref, a1_ref, a2_ref, g1_ref, g2_ref, kk_ref, ka_ref, ones_ref) = refs[:17]
    del refs[:17]
    if has_vres:
        muv_ref, v0_ref, v1_ref, v2_ref, vfirst_ref = refs[:5]
        del refs[:5]
    ssm_consts = refs[:N_SSM_CONSTS]
    del refs[:N_SSM_CONSTS]
    if seq_major_x:
        xt_ref = refs.pop(0)
    r_ref, lw_ref, k_ref, v_ref, na_ref, bb_ref, gate_ref, hlast_ref = refs[:8]
    del refs[:8]
    ssm_outs = refs[:N_SSM_OUTS]
    del refs[:N_SSM_OUTS]
    hbuf, zbuf, zrest_ref = refs[:3]
    ssm_scratch = refs[3:3 + N_SSM_SCRATCH]

    @pl.when(pl.program_id(0) == 0)
    def _():
        first = first_ref[...]
        hbuf[0:nb, :] = first
        zbuf[0:nb, :] = _bdot(first, win_ref[:, 0:R3])
        _ssm_init(nb, ssm_consts, ssm_scratch)

    if seq_major_x:
        xs = refs[-1]
        for b in range(nb):
            _store_seq_rows(xs, pl.ds(b, tm // nb, stride=nb), x_ref[b])
        x = jnp.concatenate([xs[cb] for cb in range(LANE_BLOCKS)], axis=-1)
        xt_ref[...] = x
    else:
        x = x_ref[...]
    h_all = _rms(x, g_ref[...])
    hbuf[nb:nb + tm, :] = h_all
    hlast_ref[...] = h_all[tm - nb:tm, :]

    def part(lo, n):
        h = hbuf[nb + lo:nb + lo + n, :]
        dh = hbuf[lo:lo + n, :] - h
        z = _bdot(h, win_ref[...])
        p_w = _bdot(h + dh * muwag_ref[0:1, :], w1_ref[...])
        p_a = _bdot(h + dh * muwag_ref[1:2, :], a1_ref[...])
        p_g = _bdot(h + dh * muwag_ref[2:3, :], g1_ref[...])
        if has_vres:
            p_v = _bdot(h + dh * muv_ref[...], v1_ref[...])
        zrest_ref[lo:lo + n, :] = z[:, R3:D_IN]
        z_rkv = z[:, 0:R3]
        zbuf[nb + lo:nb + lo + n, :] = z_rkv
        yield
        zmix = z_rkv + (zbuf[lo:lo + n, :] - z_rkv) * murkv_ref[...]
        r = zmix[:, 0:RWKV_DIM]
        k = zmix[:, RWKV_DIM:2 * RWKV_DIM]
        v = zmix[:, 2 * RWKV_DIM:R3]
        kk = k * kk_ref[...]
        w_pre = w0_ref[...] + _bdot(jnp.tanh(p_w), w2_ref[...])
        a = _sigmoid(a0_ref[...] + _bdot(p_a, a2_ref[...]))
        gate = _bdot(_sigmoid(p_g), g2_ref[...])
        ss = _bdot(kk * kk, ones_ref[...])
        if has_vres:
            mix = _sigmoid(v0_ref[...] + _bdot(p_v, v2_ref[...]))
            v_first = jnp.concatenate([vfirst_ref[p, lo:lo + n, :] for p in range(N_PAIRS)], axis=-1)
            v = v + (v_first - v) * mix
        yield
        w_log = -_softplus(-w_pre) - 0.5
        kk = kk / jnp.maximum(jnp.sqrt(ss), 1e-12)
        outs = ((r_ref, r), (lw_ref, -jnp.exp(w_log)), (k_ref, k * (1.0 + (a - 1.0) * ka_ref[...])), (v_ref, v),
                (na_ref, -kk), (bb_ref, kk * a), (gate_ref, gate))
        for ref, val in outs:
            for p in range(N_PAIRS):
                ref[p, lo:lo + n, :] = val[:, p * HEAD_PAIR:(p + 1) * HEAD_PAIR]

    n_part = tm // MIX_IN_PARTS
    parts = [part(i * n_part, n_part) for i in range(MIX_IN_PARTS)]
    for gen in parts:
        next(gen)
    cols = [zrest_ref.at[:, j * LRU_DIM:(j + 1) * LRU_DIM] for j in range(3)]
    _run_lockstep(parts + [_ssm_steps(reset_first, nb, tm, *cols, *ssm_consts, *ssm_outs, *ssm_scratch)])
    hbuf[0:nb, :] = hbuf[tm:tm + nb, :]
    zbuf[0:nb, :] = zbuf[tm:tm + nb, :]


MIX_IN_PARTS = 2


def _mix_in(x, first, p, vres, v_first, conv0, h0, re0, im0, reset_first, nb, seq):
    assert LRU_DIM == S5_DIM
    m = nb * seq
    tm = _tile(m, max(nb, 8) * MIX_IN_PARTS, 512)
    hist = (CONV_W - 1) * nb
    has_vres = vres is not None
    seq_major_x = x.ndim == 3
    row = lambda w: pl.BlockSpec((tm, w), lambda i: (i, 0))
    pair_row = pl.BlockSpec((N_PAIRS, tm, HEAD_PAIR), lambda i: (0, i, 0))
    ins = [x, first, p["norm_mix"], p["w_in"], p["mu_rkv"], p["mu_wag"], p["w0"], p["w1"], p["w2"],
           p["a0"], p["a1"], p["a2"], p["g1"], p["g2"], p["k_k"], p["k_a"], p["ones_head"]]
    x_spec = pl.BlockSpec((nb, tm // nb, D_MODEL), lambda i: (0, i, 0)) if seq_major_x else row(D_MODEL)
    specs = [x_spec] + [_spec(a) for a in ins[1:]]
    if has_vres:
        extra = [vres["mu_v"], vres["v0"], vres["v1"], vres["v2"]]
        ins += extra + [v_first]
        specs += [_spec(a) for a in extra] + [pair_row]
    ssm_consts = [conv0, h0, p["conv_w"], p["conv_b"], p["lru_wa"], p["lru_ba"], p["lru_wi"], p["lru_bi"],
                  p["lru_lambda"], re0, im0, p["s5_abar_re"], p["s5_abar_im"], p["s5_bbar_re"], p["s5_bbar_im"],
                  p["s5_c_re"], p["s5_c_im"], p["s5_d"], p["s5_w_glu"], p["s5_b_glu"]]
    assert len(ssm_consts) == N_SSM_CONSTS
    ins += ssm_consts
    specs += [_spec(a) for a in ssm_consts]
    wide = jax.ShapeDtypeStruct((N_PAIRS, m, HEAD_PAIR), F32)
    st = jax.ShapeDtypeStruct((nb, S5_WIDTH), F32)
    outs = ([wide] * 7 + [jax.ShapeDtypeStruct((nb, D_MODEL), F32)]
            + [jax.ShapeDtypeStruct((m, LRU_DIM), F32), jax.ShapeDtypeStruct((hist, LRU_DIM), F32),
               jax.ShapeDtypeStruct((nb, LRU_DIM), F32), jax.ShapeDtypeStruct((m, S5_DIM), F32), st, st])
    out_specs = ([pair_row] * 7 + [_const_spec((nb, D_MODEL))]
                 + [row(LRU_DIM), _const_spec((hist, LRU_DIM)), _const_spec((nb, LRU_DIM)),
                    row(S5_DIM), _const_spec((nb, S5_WIDTH)), _const_spec((nb, S5_WIDTH))])
    scratch = [pltpu.VMEM((tm + nb, D_MODEL), F32), pltpu.VMEM((tm + nb, R3), F32),
               pltpu.VMEM((tm, D_IN - R3), F32),
               pltpu.VMEM((tm + hist, LRU_DIM), F32), pltpu.VMEM((tm, LRU_DIM), F32),
               pltpu.VMEM((tm, LRU_DIM), F32), pltpu.VMEM((nb, LRU_DIM), F32),
               pltpu.VMEM((tm, S5_WIDTH), F32), pltpu.VMEM((tm, S5_WIDTH), F32),
               pltpu.VMEM((nb, S5_WIDTH), F32), pltpu.VMEM((nb, S5_WIDTH), F32)]
    assert len(scratch) == 3 + N_SSM_SCRATCH
    if seq_major_x:
        assert (tm // nb) % 8 == 0
        outs.insert(0, jax.ShapeDtypeStruct((m, D_MODEL), F32))
        out_specs.insert(0, row(D_MODEL))
        scratch.append(pltpu.VMEM((LANE_BLOCKS, tm, LANES), F32))
    return pl.pallas_call(
        functools.partial(_mix_in_kernel, has_vres, seq_major_x, reset_first, nb, tm),
        grid=(m // tm,),
        in_specs=specs,
        out_specs=out_specs,
        out_shape=outs,
        scratch_shapes=scratch,
        compiler_params=_params("arbitrary"),
        name="mix_in",
    )(*_arrays(ins))


def _wkv_kernel(c, n_seq, group, stride, r_ref, lw_ref, k_ref, v_ref, na_ref, bb_ref, gate_ref, s0_ref,
                rk_ref, lnw_ref, lnb_ref, tri_ref, mean_ref,
                y_ref, sout_ref, s_scr):
    li = pl.program_id(1)

    @pl.when(li == 0)
    def _():
        zero = jnp.zeros((RWKV_HEAD, RWKV_HEAD), F32)
        for j in range(n_seq):
            for p in range(N_PAIRS):
                s_scr[j, p] = jnp.concatenate(
                    [jnp.concatenate([s0_ref[j, 2 * p], zero], axis=1),
                     jnp.concatenate([zero, s0_ref[j, 2 * p + 1]], axis=1)], axis=0)

    lane = lax.broadcasted_iota(jnp.int32, (1, HEAD_PAIR), 1)
    head_masks = (lane < RWKV_HEAD, lane >= RWKV_HEAD)
    trow = lax.broadcasted_iota(jnp.int32, (c, 2 * c), 0)
    tcol = lax.broadcasted_iota(jnp.int32, (c, 2 * c), 1)
    left_half = tcol < c
    scol = jnp.where(left_half, tcol, tcol - c)
    strict = scol < trow
    strict_left = strict & left_half
    strict_right = strict & (tcol >= c)
    incl = scol <= trow
    eye2 = (scol == trow).astype(F32)
    drow = lax.broadcasted_iota(jnp.int32, (2 * c, 2 * c), 0)
    dcol = lax.broadcasted_iota(jnp.int32, (2 * c, 2 * c), 1)
    diag_blocks = jnp.bitwise_xor(drow - c, dcol - c) >= 0
    brow = lax.broadcasted_iota(jnp.int32, (HEAD_PAIR, HEAD_PAIR), 0)
    bcol = lax.broadcasted_iota(jnp.int32, (HEAD_PAIR, HEAD_PAIR), 1)
    same_head = jnp.bitwise_xor(brow - RWKV_HEAD, bcol - RWKV_HEAD) >= 0
    n_double = max(int(math.log2(c)) - 1, 0)
    tri = tri_ref[...]
    zeros_c = jnp.zeros((c, HEAD_PAIR), F32)

    def pair_chunk(lw, r, k, v, na, bb, gate, sp, cols):
        lw_hi, lw_lo = _split2(lw)
        cl = jnp.dot(tri, lw_hi, preferred_element_type=F32) + jnp.dot(tri, lw_lo, preferred_element_type=F32)
        yield
        e_in = jnp.exp(cl)
        e_neg = jnp.exp(-cl)
        g_last = jnp.exp(jnp.sum(lw, axis=0, keepdims=True))
        at = na * jnp.exp(cl - lw)
        rt = r * e_in
        bt = bb * e_neg
        kt = k * e_neg
        ar = jnp.concatenate([at, rt], axis=0)
        bk = jnp.concatenate([bt, kt], axis=0)
        a_s = _bdot_nt(ar, sp)
        g0 = _bdot_nt(jnp.where(head_masks[0], ar, 0.0), bk)
        g1 = _bdot_nt(jnp.where(head_masks[1], ar, 0.0), jnp.concatenate([kt, bt], axis=0))
        yield
        ga0, ga1 = g0[0:c, :], g1[0:c, :]
        low = jnp.where(strict, jnp.where(left_half, ga0, ga1), 0.0)
        ak0 = _bdot(jnp.where(strict_right, ga0, 0.0), jnp.concatenate([zeros_c, v], axis=0))
        ak1 = _bdot(jnp.where(strict_left, ga1, 0.0), jnp.concatenate([v, zeros_c], axis=0))
        rhs = jnp.concatenate([jnp.where(head_masks[0], a_s[0:c, :] + ak0, 0.0),
                               jnp.where(head_masks[1], a_s[0:c, :] + ak1, 0.0)], axis=0)
        blockdiag = lambda x: jnp.where(diag_blocks, jnp.concatenate([x, x], axis=0), 0.0)
        t_inv = eye2 + low
        pw = low
        if n_double > 0:
            pw = _bdot(pw, blockdiag(pw))
            yield
        for level in range(n_double):
            if level == n_double - 1:
                t_inv = t_inv + _bdot(t_inv, blockdiag(pw))
            else:
                both = _bdot(jnp.concatenate([t_inv, pw], axis=0), blockdiag(pw))
                t_inv = t_inv + both[0:c, :]
                pw = both[c:2 * c, :]
            yield
        u = _bdot(t_inv, rhs)
        yield
        uv = jnp.concatenate([u, v], axis=0)
        y = a_s[c:2 * c, :] + jnp.where(
            head_masks[0], _bdot(jnp.where(incl, g0[c:2 * c, :], 0.0), uv),
            _bdot(jnp.where(incl, g1[c:2 * c, :], 0.0), jnp.concatenate([v, u], axis=0)))
        upd = _bdot_tn(uv, bk * g_last)
        s_new = sp * g_last + jnp.where(same_head, upd, 0.0)
        yield
        stats = _bdot(jnp.concatenate([y, r * k * rk_ref[:, cols]], axis=0), mean_ref[...])
        mu = stats[0:c, :]
        bonus = stats[c:2 * c, :] * float(RWKV_HEAD) * v
        yield
        d = y - mu
        var = _bdot(d * d, mean_ref[...])
        yield
        yn = d * lax.rsqrt(var + GN_EPS) * lnw_ref[:, cols] + lnb_ref[:, cols]
        return (yn + bonus) * gate, s_new

    first_seq = pl.program_id(0) * n_seq

    def seq_group(gi, carry):
        units = []
        for jj in range(group):
            j = gi * group + jj
            rows = pl.ds(first_seq + j, c, stride=stride)
            for p in range(N_PAIRS):
                cols = slice(p * HEAD_PAIR, (p + 1) * HEAD_PAIR)
                vals = [ref[p, rows, :] for ref in (lw_ref, r_ref, k_ref, v_ref, na_ref, bb_ref, gate_ref)]
                units.append((j, p, rows, pair_chunk(*vals, s_scr[j, p], cols)))
        results = _run_lockstep([gen for (_, _, _, gen) in units])
        for (j, p, rows, _), (y, s_new) in zip(units, results):
            y_ref[p, rows, :] = y
            s_scr[j, p] = s_new
        return carry

    lax.fori_loop(0, n_seq // group, seq_group, 0)

    @pl.when(li == pl.num_programs(1) - 1)
    def _():
        for j in range(n_seq):
            for p in range(N_PAIRS):
                sp = s_scr[j, p]
                sout_ref[j, 2 * p] = sp[0:RWKV_HEAD, 0:RWKV_HEAD]
                sout_ref[j, 2 * p + 1] = sp[RWKV_HEAD:HEAD_PAIR, RWKV_HEAD:HEAD_PAIR]


WKV_SEQ_PER_STEP = 8
WKV_SEQ_GROUP = 8


def _wkv(arrs, s_all, layer, p, nb, seq):
    c = min(WKV_CHUNK, seq)
    n_seq = nb if seq > c else min(nb, WKV_SEQ_PER_STEP)
    assert seq % c == 0 and c % 8 == 0 and nb % n_seq == 0 and n_seq % WKV_SEQ_GROUP == 0
    blk = pl.BlockSpec((N_PAIRS, c * nb, HEAD_PAIR), lambda g, l: (0, l, 0))
    tri = jnp.tril(jnp.ones((c, c), F32)).astype(BF16)
    st_spec = pl.BlockSpec((None, n_seq, RWKV_HEADS, RWKV_HEAD, RWKV_HEAD), lambda g, l: (layer, g, 0, 0, 0))
    consts = [p["r_k"], p["ln_x_w"], p["ln_x_b"], tri, p["mean_pair"]]
    return pl.pallas_call(
        functools.partial(_wkv_kernel, c, n_seq, WKV_SEQ_GROUP, nb),
        grid=(nb // n_seq, seq // c),
        in_specs=[blk] * 7 + [st_spec] + [_spec(a) for a in consts],
        out_specs=[blk, st_spec],
        out_shape=[jax.ShapeDtypeStruct((N_PAIRS, seq * nb, HEAD_PAIR), F32), jax.ShapeDtypeStruct(s_all.shape, F32)],
        input_output_aliases={7: 1},
        scratch_shapes=[pltpu.VMEM((n_seq, N_PAIRS, HEAD_PAIR, HEAD_PAIR), F32)],
        compiler_params=_params("arbitrary", "arbitrary"),
        name="wkv",
    )(*arrs, s_all, *_arrays(consts))


def _ssm_init(nb, consts, scratch):
    conv0_ref, h0_ref, re0_ref, im0_ref = consts[0], consts[1], consts[9], consts[10]
    xbuf, _, _, hcar, _, _, car_re, car_im = scratch
    xbuf[0:(CONV_W - 1) * nb, :] = conv0_ref[...]
    hcar[...] = h0_ref[...]
    car_re[...] = re0_ref[...]
    car_im[...] = im0_ref[...]


def _ssm_steps(reset_first, nb, tm, zx_ref, zg_ref, u_ref,
               conv0_ref, h0_ref, cw_ref, cb_ref, wa_ref, ba_ref, wi_ref, bi_ref, lam_ref,
               re0_ref, im0_ref, are_ref, aim_ref, bre_ref, bim_ref, cre_ref, cim_ref, d_ref, wglu_ref, bglu_ref,
               y_ref, conv_out_ref, h_out_ref, yc_ref, re_out_ref, im_out_ref,
               xbuf, abuf, bbuf, hcar, xre, xim, car_re, car_im):
    i = pl.program_id(0)
    hist = (CONV_W - 1) * nb

    u = u_ref[...]
    xre[...] = _bdot(u, bre_ref[...])
    xim[...] = _bdot(u, bim_ref[...])

    xbuf[hist:hist + tm, :] = zx_ref[...]
    xc = cb_ref[...]
    for j in range(CONV_W):
        xc = xc + xbuf[j * nb:j * nb + tm, :] * cw_ref[j:j + 1, :]
    conv_out_ref[...] = xbuf[tm:tm + hist, :]
    xbuf[0:hist, :] = xbuf[tm:tm + hist, :]

    gate_a = _sigmoid(_bdot(xc, wa_ref[...]) + ba_ref[...])
    gate_i = _sigmoid(_bdot(xc, wi_ref[...]) + bi_ref[...])
    log_a = -LRU_C * gate_a * _softplus(-lam_ref[...])
    a_sq = jnp.exp(2.0 * log_a)
    mult = jnp.sqrt(-jnp.tanh(log_a) * (a_sq + 1.0))
    if reset_first:
        row = lax.broadcasted_iota(jnp.int32, (tm, 1), 0)
        mult = jnp.where(row < jnp.where(i == 0, nb, 0), 1.0, mult)
    abuf[...] = jnp.exp(log_a)
    bbuf[...] = mult * gate_i * xc
    yield

    a_re = jnp.broadcast_to(are_ref[...], (nb, S5_WIDTH))
    a_im = jnp.broadcast_to(aim_ref[...], (nb, S5_WIDTH))
    h, s_re, s_im = hcar[...], car_re[...], car_im[...]
    for t in range(tm // nb):
        rows = slice(t * nb, (t + 1) * nb)
        h = abuf[rows, :] * h + bbuf[rows, :]
        bbuf[rows, :] = h
        s_re, s_im = (a_re * s_re - a_im * s_im + xre[rows, :], a_re * s_im + a_im * s_re + xim[rows, :])
        xre[rows, :] = s_re
        xim[rows, :] = s_im
    hcar[...] = h
    h_out_ref[...] = h
    car_re[...] = s_re
    car_im[...] = s_im
    re_out_ref[...] = s_re
    im_out_ref[...] = s_im

    y_ref[...] = bbuf[...] * _gelu(zg_ref[...])
    y = _bdot(xre[...], cre_ref[...]) - _bdot(xim[...], cim_ref[...]) + d_ref[...] * u
    yield
    o = _bdot(_gelu(y), wglu_ref[...]) + bglu_ref[...]
    yield
    yc_ref[...] = o[:, 0:S5_DIM] * _sigmoid(o[:, S5_DIM:2 * S5_DIM])


def _mix_out_kernel(x_ref, ya_ref, yb_ref, yc_ref, wa_ref, wb_ref, wc_ref, gq_ref, wq_ref, x1_ref, q_ref):
    ya = jnp.concatenate([ya_ref[p] for p in range(N_PAIRS)], axis=-1)
    x1 = (x_ref[...] + _bdot(ya, wa_ref[...]) + _bdot(yb_ref[...], wb_ref[...])
          + _bdot(yc_ref[...], wc_ref[...]))
    q = _bdot(_rms(x1, gq_ref[...]), wq_ref[...])
    for cb in range(LANE_BLOCKS):
        cols = slice(cb * LANES, (cb + 1) * LANES)
        x1_ref[cb] = x1[:, cols]
        q_ref[cb] = q[:, cols]


def _mix_out(x, ya, yb, yc, p):
    m = x.shape[0]
    tm = _tile(m, 8, 512)
    row = lambda w: pl.BlockSpec((tm, w), lambda i: (i, 0))
    split = pl.BlockSpec((LANE_BLOCKS, tm, LANES), lambda i: (0, i, 0))
    consts = [p["w_out_a"], p["w_out_b"], p["w_out_c"], p["norm_mem_q"], p["mem_wq"]]
    out = jax.ShapeDtypeStruct((LANE_BLOCKS, m, LANES), F32)
    return pl.pallas_call(
        _mix_out_kernel,
        grid=(m // tm,),
        in_specs=[row(D_MODEL), pl.BlockSpec((N_PAIRS, tm, HEAD_PAIR), lambda i: (0, i, 0)), row(LRU_DIM),
                  row(S5_DIM)] + [_spec(a) for a in consts],
        out_specs=[split, split],
        out_shape=[out, out],
        compiler_params=_params("parallel"),
        name="mix_out",
    )(x, ya, yb, yc, *_arrays(consts))


def _load_seq_rows(ref, rows):
    return jnp.concatenate([ref[cb, rows, :] for cb in range(LANE_BLOCKS)], axis=-1)


def _store_seq_rows(ref, rows, val):
    for cb in range(LANE_BLOCKS):
        ref[cb, rows, :] = val[:, cb * LANES:(cb + 1) * LANES]


def _softmax_rows(s):
    e = jnp.exp(s - jnp.max(s, axis=-1, keepdims=True))
    return e / jnp.sum(e, axis=-1, keepdims=True)


def _attn_kernel(tl, n_seq, stride, q_ref, x_ref, mk_ref, mv_ref, wo_ref, o_ref):
    scale = MEM_HEAD_DIM ** -0.5
    first_seq = pl.program_id(0) * n_seq
    head_cols = [slice(hh * MEM_HEAD_DIM, (hh + 1) * MEM_HEAD_DIM) for hh in range(MEM_HEADS)]

    def one_seq(j):
        rows = pl.ds(first_seq + j, tl, stride=stride)
        q = _load_seq_rows(q_ref, rows)
        scores = [_bdot_nt(q[:, cols], mk_ref[j, :, cols]) * scale for cols in head_cols]
        yield
        heads = [_bdot(_softmax_rows(s), mv_ref[j, :, cols]) for s, cols in zip(scores, head_cols)]
        yield
        out = _load_seq_rows(x_ref, rows) + _bdot(jnp.concatenate(heads, axis=-1), wo_ref[...])
        yield
        _store_seq_rows(o_ref, rows, out)

    for j0 in range(0, n_seq, ATTN_LOCKSTEP):
        _run_lockstep([one_seq(j) for j in range(j0, min(j0 + ATTN_LOCKSTEP, n_seq))])


def _attn_cache_kernel(tl, n_seq, stride, q_ref, x_ref, mk_ref, mv_ref, wo_ref, o_ref):
    scale = MEM_HEAD_DIM ** -0.5
    first_seq = pl.program_id(0) * n_seq
    qrow = lax.broadcasted_iota(jnp.int32, (MEM_HEADS * tl, N_MEM * MEM_HEADS), 0)
    kcol = lax.broadcasted_iota(jnp.int32, (MEM_HEADS * tl, N_MEM * MEM_HEADS), 1)
    q_head = jnp.zeros_like(qrow)
    for hh in range(1, MEM_HEADS):
        q_head = q_head + (qrow >= hh * tl).astype(jnp.int32)
    same_head = q_head == jnp.bitwise_and(kcol, MEM_HEADS - 1)

    def one_seq(j):
        rows = pl.ds(first_seq + j, tl, stride=stride)
        q = _load_seq_rows(q_ref, rows)
        qh = jnp.concatenate([q[:, hh * MEM_HEAD_DIM:(hh + 1) * MEM_HEAD_DIM] for hh in range(MEM_HEADS)], axis=0)
        k2 = mk_ref[j].reshape(N_MEM * MEM_HEADS, MEM_HEAD_DIM)
        s = jnp.where(same_head, _bdot_nt(qh, k2) * scale, MASKED_SCORE)
        yield
        v2 = mv_ref[j].reshape(N_MEM * MEM_HEADS, MEM_HEAD_DIM)
        oh = _bdot(_softmax_rows(s), v2)
        yield
        o = jnp.concatenate([oh[hh * tl:(hh + 1) * tl, :] for hh in range(MEM_HEADS)], axis=-1)
        out = _load_seq_rows(x_ref, rows) + _bdot(o, wo_ref[...])
        yield
        _store_seq_rows(o_ref, rows, out)

    _run_lockstep([one_seq(j) for j in range(n_seq)])


MASKED_SCORE = -1e30
ATTN_SEQ_PER_STEP = 4
ATTN_ROWS = 128
ATTN_LOCKSTEP = 4


def _attn(q, x1, mk, mv, kv_offset, wo, nb, seq):
    tl = _tile(seq, 8, ATTN_ROWS)
    n_seq = nb if seq > tl else min(nb, ATTN_SEQ_PER_STEP)
    assert nb % n_seq == 0 and kv_offset % n_seq == 0 and MEM_HEADS & (MEM_HEADS - 1) == 0
    blk = pl.BlockSpec((LANE_BLOCKS, tl * nb, LANES), lambda g, l: (0, l, 0))
    kv_index = lambda g, l: (kv_offset // n_seq + g,) + (0,) * (mk.ndim - 1)
    kv = pl.BlockSpec((n_seq,) + mk.shape[1:], kv_index)
    body = _attn_kernel if mk.ndim == 3 else _attn_cache_kernel
    return pl.pallas_call(
        functools.partial(body, tl, n_seq, nb),
        grid=(nb // n_seq, seq // tl),
        in_specs=[blk, blk, kv, kv, _spec(wo)],
        out_specs=blk,
        out_shape=jax.ShapeDtypeStruct((LANE_BLOCKS, seq * nb, LANES), F32),
        compiler_params=_params("arbitrary", "arbitrary"),
        name="mem_attn",
    )(q, x1, mk, mv, *_arrays([wo]))


def _ffn_kernel(d_ff, final_norm, n_seq, x_ref, g_ref, wup_ref, wdown_ref, *rest):
    x = jnp.concatenate([x_ref[cb] for cb in range(LANE_BLOCKS)], axis=-1)
    h = _rms(x, g_ref[...]).astype(BF16)
    gt = jnp.dot(h, wup_ref[:, 0:d_ff], preferred_element_type=F32)
    up = jnp.dot(h, wup_ref[:, d_ff:2 * d_ff], preferred_element_type=F32)
    act = gt * _sigmoid(gt) * up
    out = x + _bdot(act, wdown_ref[...])
    if final_norm:
        out = _rms(out, rest[0][...])
    if n_seq:
        o_ref, os = rest[-2:]
        for cb in range(LANE_BLOCKS):
            os[cb] = out[:, cb * LANES:(cb + 1) * LANES]
        for b in range(n_seq):
            o_ref[b] = _load_seq_rows(os, pl.ds(b, out.shape[0] // n_seq, stride=n_seq))
    else:
        rest[-1][...] = out


def _ffn(x, p, g_final=None, seq_major_out=None):
    m = x.shape[1]
    tm = _tile(m, 8, 512)
    d_ff = p["ffn_w_down"].shape[0]
    consts = [p["norm_ffn"], p["ffn_w_up"], p["ffn_w_down"]] + ([] if g_final is None else [g_final])
    if seq_major_out is None:
        n_seq, scratch = 0, []
        out_spec = pl.BlockSpec((tm, D_MODEL), lambda i: (i, 0))
        out_shape = jax.ShapeDtypeStruct((m, D_MODEL), F32)
    else:
        n_seq, seq = seq_major_out
        assert tm % n_seq == 0 and (tm // n_seq) % 8 == 0 and n_seq * seq == m
        scratch = [pltpu.VMEM((LANE_BLOCKS, tm, LANES), F32)]
        out_spec = pl.BlockSpec((n_seq, tm // n_seq, D_MODEL), lambda i: (0, i, 0))
        out_shape = jax.ShapeDtypeStruct((n_seq, seq, D_MODEL), F32)
    return pl.pallas_call(
        functools.partial(_ffn_kernel, d_ff, g_final is not None, n_seq),
        grid=(m // tm,),
        in_specs=[pl.BlockSpec((LANE_BLOCKS, tm, LANES), lambda i: (0, i, 0))] + [_spec(a) for a in consts],
        out_specs=out_spec,
        out_shape=out_shape,
        scratch_shapes=scratch,
        compiler_params=_params("parallel"),
        name="ffn",
    )(x, *_arrays(consts))


def _mem_project_kernel(x_ref, g_ref, wk_ref, wv_ref, k_ref, v_ref, kb_ref, vb_ref):
    m = _rms(x_ref[...], g_ref[...]).astype(BF16)
    k = jnp.dot(m, wk_ref[...], preferred_element_type=F32)
    v = jnp.dot(m, wv_ref[...], preferred_element_type=F32)
    k_ref[...] = k.reshape(k_ref.shape)
    v_ref[...] = v.reshape(v_ref.shape)
    kb_ref[...] = k.astype(BF16)
    vb_ref[...] = v.astype(BF16)


MEM_SEQ_PER_STEP = 2


def _mem_project(mem, g, wk, wv):
    n, n_mem, _ = mem.shape
    depth = wk.shape[0]
    tm = MEM_SEQ_PER_STEP * n_mem
    assert n % MEM_SEQ_PER_STEP == 0
    per_layer = lambda a: pl.BlockSpec((None,) + a.shape[1:], lambda l, i: (l,) + (0,) * (a.ndim - 1))
    cache = pl.BlockSpec((None, MEM_SEQ_PER_STEP, n_mem, MEM_HEADS, MEM_HEAD_DIM), lambda l, i: (l, i, 0, 0, 0))
    rows_b = pl.BlockSpec((None, tm, D_MODEL), lambda l, i: (l, i, 0))
    out = jax.ShapeDtypeStruct((depth, n, n_mem, MEM_HEADS, MEM_HEAD_DIM), F32)
    outb = jax.ShapeDtypeStruct((depth, n * n_mem, D_MODEL), BF16)
    return pl.pallas_call(
        _mem_project_kernel, grid=(depth, n // MEM_SEQ_PER_STEP),
        in_specs=[pl.BlockSpec((tm, D_MODEL), lambda l, i: (i, 0)), per_layer(g), per_layer(wk), per_layer(wv)],
        out_specs=[cache, cache, rows_b, rows_b], out_shape=[out, out, outb, outb],
        compiler_params=_params("parallel", "parallel"),
        name="mem_project",
    )(mem.reshape(n * n_mem, D_MODEL), g, wk, wv)


def _block_diag(blocks):
    n, r, c = blocks.shape[-3:]
    eye = jnp.eye(n, dtype=blocks.dtype)
    out = eye[:, None, :, None] * blocks[..., :, :, None, :]
    return out.reshape(blocks.shape[:-3] + (n * r, n * c))


def _stack_params(w):
    depth = w["norm_mix"].shape[0]
    row = lambda a: a.reshape(a.shape[0], 1, -1).astype(F32)
    bf = lambda a: a.astype(BF16)
    t = lambda a: jnp.swapaxes(a, -1, -2)
    p = {k: row(w[k]) for k in ("norm_mix", "mu_rkv", "w0", "a0", "k_k", "k_a", "r_k", "ln_x_w", "ln_x_b", "conv_b",
                                "lru_ba", "lru_bi", "lru_lambda", "s5_d", "s5_b_glu", "norm_mem_q", "norm_mem_kv",
                                "norm_ffn", "mu_v", "v0")}
    p.update({k: bf(w[k]) for k in ("w_in", "w1", "w2", "a1", "a2", "g1", "g2", "s5_w_glu", "mem_wq", "mem_wk",
                                    "mem_wv", "mem_wo", "ffn_w_up", "ffn_w_down", "v1", "v2")})
    p["mu_wag"], p["conv_w"] = w["mu_wag"], w["conv_w"]
    p["lru_wa"], p["lru_wi"] = bf(_block_diag(w["lru_wa"])), bf(_block_diag(w["lru_wi"]))
    w_out = bf(w["w_out"])
    p["w_out_a"] = w_out[:, 0:RWKV_DIM]
    p["w_out_b"] = w_out[:, RWKV_DIM:RWKV_DIM + LRU_DIM]
    p["w_out_c"] = w_out[:, RWKV_DIM + LRU_DIM:]
    lam_re, lam_im = w["s5_a_re"].astype(F32), w["s5_a_im"].astype(F32)
    dt = jnp.exp(w["s5_log_dt"].astype(F32))[..., None]
    mag = jnp.exp(lam_re * dt)
    abar_re, abar_im = mag * jnp.cos(lam_im * dt), mag * jnp.sin(lam_im * dt)
    den = lam_re * lam_re + lam_im * lam_im
    q_re = ((abar_re - 1.0) * lam_re + abar_im * lam_im) / den
    q_im = (abar_im * lam_re - (abar_re - 1.0) * lam_im) / den
    b_re, b_im = w["s5_b_re"].astype(F32), w["s5_b_im"].astype(F32)
    bbar_re = q_re[..., None] * b_re - q_im[..., None] * b_im
    bbar_im = q_re[..., None] * b_im + q_im[..., None] * b_re
    p["s5_abar_re"] = abar_re.reshape(depth, 1, S5_WIDTH)
    p["s5_abar_im"] = abar_im.reshape(depth, 1, S5_WIDTH)
    p["s5_bbar_re"] = bf(_block_diag(t(bbar_re)))
    p["s5_bbar_im"] = bf(_block_diag(t(bbar_im)))
    p["s5_c_re"] = bf(_block_diag(t(w["s5_c_re"])))
    p["s5_c_im"] = bf(_block_diag(t(w["s5_c_im"])))
    return p


def _layer_params(stacked, l, consts):
    vres_keys = ("mu_v", "v0", "v1", "v2")
    p = {k: _Layer(a, l) for k, a in stacked.items() if k not in vres_keys}
    p.update(consts)
    vres = {k: _Layer(stacked[k], l - 1) for k in vres_keys} if l > 0 else None
    return p, vres


def _time_major(a):
    return jnp.swapaxes(a, 0, 1).reshape((a.shape[0] * a.shape[1],) + a.shape[2:])


def _run_group(x, mem_k, mem_v, kv_stride, shift0, wkv0, conv0, lru0, re0, im0, reset_first, layers, norm_final,
               relayout_in_kernel):
    nb, seq, _ = x.shape
    xt = x if relayout_in_kernel else _time_major(x)
    outs = {k: [] for k in ("shift", "conv", "lru", "re", "im")}
    wkv_all = wkv0
    v_first = None
    for l, (p, vres) in enumerate(layers):
        last = l == len(layers) - 1
        mixed = _mix_in(xt, _Layer(shift0, l), p, vres, v_first, _time_major(conv0[l]), _Layer(lru0, l),
                        _Layer(re0.reshape(-1, nb, S5_WIDTH), l), _Layer(im0.reshape(-1, nb, S5_WIDTH), l),
                        reset_first, nb, seq)
        if xt.ndim == 3:
            xt, mixed = mixed[0], mixed[1:]
        r, lw, k, v, na, bb, gate, h_last, yb, conv_new, lru_new, yc, re_new, im_new = mixed
        if l == 0:
            v_first = v
        ya, wkv_all = _wkv((r, lw, k, v, na, bb, gate), wkv_all, l, p, nb, seq)
        x1, q = _mix_out(xt, ya, yb, yc, p)
        x2 = _attn(q, x1, mem_k, mem_v, l * kv_stride, p["mem_wo"], nb, seq)
        xt = _ffn(x2, p, norm_final if last else None, (nb, seq) if last and relayout_in_kernel else None)
        outs["shift"].append(h_last)
        outs["conv"].append(jnp.swapaxes(conv_new.reshape(CONV_W - 1, nb, LRU_DIM), 0, 1))
        outs["lru"].append(lru_new)
        outs["re"].append(re_new.reshape(nb, S5_GROUPS, S5_STATE))
        outs["im"].append(im_new.reshape(nb, S5_GROUPS, S5_STATE))
    y = xt if relayout_in_kernel else jnp.swapaxes(xt.reshape(seq, nb, D_MODEL), 0, 1)
    st = {k: jnp.stack(v) for k, v in outs.items()}
    return (y, st["shift"], wkv_all, st["conv"], st["lru"], st["re"], st["im"])


def kernel(x_prompt, x_sample, mem_prompt, state_shift, state_wkv, state_conv, state_lru, state_s5_re, state_s5_im, cache_mem_k, cache_mem_v, norm_mix, w_in, w_out, mu_rkv, mu_wag, mu_v, w0, w1, w2, a0, a1, a2, v0, v1, v2, g1, g2, k_k, k_a, r_k, ln_x_w, ln_x_b, conv_w, conv_b, lru_wa, lru_ba, lru_wi, lru_bi, lru_lambda, s5_a_re, s5_a_im, s5_log_dt, s5_b_re, s5_b_im, s5_c_re, s5_c_im, s5_d, s5_w_glu, s5_b_glu, norm_mem_q, norm_mem_kv, mem_wq, mem_wk, mem_wv, mem_wo, norm_ffn, ffn_w_up, ffn_w_down, norm_final):
    w = dict(norm_mix=norm_mix, w_in=w_in, w_out=w_out, mu_rkv=mu_rkv, mu_wag=mu_wag, mu_v=mu_v, w0=w0, w1=w1,
             w2=w2, a0=a0, a1=a1, a2=a2, v0=v0, v1=v1, v2=v2, g1=g1, g2=g2, k_k=k_k, k_a=k_a, r_k=r_k,
             ln_x_w=ln_x_w, ln_x_b=ln_x_b, conv_w=conv_w, conv_b=conv_b, lru_wa=lru_wa, lru_ba=lru_ba,
             lru_wi=lru_wi, lru_bi=lru_bi, lru_lambda=lru_lambda, s5_a_re=s5_a_re, s5_a_im=s5_a_im,
             s5_log_dt=s5_log_dt, s5_b_re=s5_b_re, s5_b_im=s5_b_im, s5_c_re=s5_c_re, s5_c_im=s5_c_im,
             s5_d=s5_d, s5_w_glu=s5_w_glu, s5_b_glu=s5_b_glu, norm_mem_q=norm_mem_q, norm_mem_kv=norm_mem_kv,
             mem_wq=mem_wq, mem_wk=mem_wk, mem_wv=mem_wv, mem_wo=mem_wo, norm_ffn=norm_ffn, ffn_w_up=ffn_w_up,
             ffn_w_down=ffn_w_down)
    depth = norm_mix.shape[0]
    ones64 = jnp.ones((RWKV_HEADS, RWKV_HEAD, RWKV_HEAD), F32)
    consts = {"ones_head": _block_diag(ones64).astype(BF16),
              "mean_pair": (_block_diag(ones64[:2]) / RWKV_HEAD).astype(BF16)}
    stacked = _stack_params(w)
    layers = [_layer_params(stacked, l, consts) for l in range(depth)]
    g_final = norm_final.reshape(1, D_MODEL)

    bp, n_mem, _ = mem_prompt.shape
    mem_k_p, mem_v_p, mem_k_b, mem_v_b = _mem_project(
        mem_prompt, stacked["norm_mem_kv"], stacked["mem_wk"], stacked["mem_wv"])
    mem_k_b = mem_k_b.reshape(depth * bp, n_mem, D_MODEL)
    mem_v_b = mem_v_b.reshape(depth * bp, n_mem, D_MODEL)

    zeros = lambda *s: jnp.zeros((depth, bp) + s, F32)
    prompt = _run_group(
        x_prompt, mem_k_b, mem_v_b, bp,
        zeros(D_MODEL), zeros(RWKV_HEADS, RWKV_HEAD, RWKV_HEAD), zeros(CONV_W - 1, LRU_DIM), zeros(LRU_DIM),
        zeros(S5_GROUPS, S5_STATE), zeros(S5_GROUPS, S5_STATE), True, layers, g_final, True)
    bs = x_sample.shape[0]
    cache_shape = (depth * bs, n_mem, MEM_HEADS, MEM_HEAD_DIM)
    sample = _run_group(
        x_sample, cache_mem_k.reshape(cache_shape), cache_mem_v.reshape(cache_shape),
        bs, state_shift, state_wkv, state_conv, state_lru, state_s5_re, state_s5_im, False, layers, g_final, False)
    (y_p, sh_p, wkv_p, conv_p, lru_p, re_p, im_p) = prompt
    (y_s, sh_s, wkv_s, conv_s, lru_s, re_s, im_s) = sample
    return (y_p, y_s, sh_p, sh_s, wkv_p, wkv_s, conv_p, conv_s, lru_p, lru_s, re_p, re_s, im_p, im_s,
            mem_k_p, mem_v_p)
```

```python
import functools
import math

import jax
import jax.numpy as jnp
from jax import lax
from jax.experimental import pallas as pl
from jax.experimental.pallas import tpu as pltpu

F32 = jnp.float32
BF16 = jnp.bfloat16

D_MODEL = 1024
RWKV_DIM = 512
RWKV_HEAD = 64
RWKV_HEADS = 8
LANES = 128
LANE_BLOCKS = D_MODEL // LANES
HEAD_PAIR = 2 * RWKV_HEAD
N_PAIRS = RWKV_DIM // HEAD_PAIR
LRU_DIM = 256
LRU_BLOCKS = 4
CONV_W = 4
LRU_C = 8.0
S5_DIM = 256
S5_GROUP = 16
S5_GROUPS = 16
S5_STATE = 64
S5_WIDTH = S5_GROUPS * S5_STATE
R3 = 3 * RWKV_DIM
D_IN = R3 + 2 * LRU_DIM + S5_DIM
N_MEM = 256
MEM_HEADS = 4
MEM_HEAD_DIM = 256
RMS_EPS = 1e-6
GN_EPS = 64e-5
WKV_CHUNK = 64

V7X_VMEM_LIMIT = 56 * 1024 * 1024


def _params(*sem):
    return pltpu.CompilerParams(dimension_semantics=sem, vmem_limit_bytes=V7X_VMEM_LIMIT)


def _tile(m, mult, target):
    best = mult
    t = mult
    while t <= min(m, target):
        if m % t == 0:
            best = t
        t += mult
    assert m % best == 0, (m, mult, target)
    return best


def _const_spec(shape):
    nd = len(shape)
    return pl.BlockSpec(shape, lambda *_: (0,) * nd)


class _Layer:
    def __init__(self, arr, layer):
        self.arr, self.layer, self.shape = arr, layer, arr.shape[1:]


def _spec(a):
    if isinstance(a, _Layer):
        nd = len(a.shape)
        return pl.BlockSpec((None,) + a.shape, lambda *_: (a.layer,) + (0,) * nd)
    return _const_spec(a.shape)


def _arrays(operands):
    return [a.arr if isinstance(a, _Layer) else a for a in operands]


def _bdot(a, b):
    return jnp.dot(a.astype(BF16), b.astype(BF16), preferred_element_type=F32)


def _bdot_nt(a, b):
    return lax.dot_general(a.astype(BF16), b.astype(BF16), (((1,), (1,)), ((), ())),
                           preferred_element_type=F32)


def _bdot_tn(a, b):
    return lax.dot_general(a.astype(BF16), b.astype(BF16), (((0,), (0,)), ((), ())),
                           preferred_element_type=F32)


def _split2(x):
    hi = x.astype(BF16)
    lo = (x - hi.astype(F32)).astype(BF16)
    return hi, lo


def _run_lockstep(gens):
    results = {}
    while len(results) < len(gens):
        for i, gen in enumerate(gens):
            if i not in results:
                try:
                    next(gen)
                except StopIteration as stop:
                    results[i] = stop.value
    return [results[i] for i in range(len(gens))]


def _rms(x, g):
    return x * lax.rsqrt(jnp.mean(x * x, axis=-1, keepdims=True) + RMS_EPS) * g


def _softplus(x):
    return jnp.maximum(x, 0.0) + jnp.log1p(jnp.exp(-jnp.abs(x)))


def _sigmoid(x):
    return 1.0 / (1.0 + jnp.exp(-x))


def _gelu(x):
    c = math.sqrt(2.0 / math.pi)
    return x * (0.5 * (1.0 + jnp.tanh(c * (x + 0.044715 * (x * x * x)))))


N_SSM_CONSTS, N_SSM_OUTS, N_SSM_SCRATCH = 20, 6, 8


def _mix_in_kernel(has_vres, seq_major_x, reset_first, nb, tm, *refs):
    refs = list(refs)
    (x_ref, first_ref, g_ref, win_ref, murkv_ref, muwag_ref, w0_ref, w1_ref, w2_ref,
     a0_ref, a1_ref, a2_ref, g1_ref, g2_ref, kk_ref, ka_ref, ones_ref) = refs[:17]
    del refs[:17]
    if has_vres:
        muv_ref, v0_ref, v1_ref, v2_ref, vfirst_ref = refs[:5]
        del refs[:5]
    ssm_consts = refs[:N_SSM_CONSTS]
    del refs[:N_SSM_CONSTS]
    if seq_major_x:
        xt_ref = refs.pop(0)
    r_ref, lw_ref, k_ref, v_ref, na_ref, bb_ref, gate_ref, hlast_ref = refs[:8]
    del refs[:8]
    ssm_outs = refs[:N_SSM_OUTS]
    del refs[:N_SSM_OUTS]
    hbuf, zbuf, zrest_ref = refs[:3]
    ssm_scratch = refs[3:3 + N_SSM_SCRATCH]

    @pl.when(pl.program_id(0) == 0)
    def _():
        first = first_ref[...]
        hbuf[0:nb, :] = first
        zbuf[0:nb, :] = _bdot(first, win_ref[:, 0:R3])
        _ssm_init(nb, ssm_consts, ssm_scratch)

    if seq_major_x:
        xs = refs[-1]
        for b in range(nb):
            _store_seq_rows(xs, pl.ds(b, tm // nb, stride=nb), x_ref[b])
        x = jnp.concatenate([xs[cb] for cb in range(LANE_BLOCKS)], axis=-1)
        xt_ref[...] = x
    else:
        x = x_ref[...]
    h_all = _rms(x, g_ref[...])
    hbuf[nb:nb + tm, :] = h_all
    hlast_ref[...] = h_all[tm - nb:tm, :]

    def part(lo, n):
        h = hbuf[nb + lo:nb + lo + n, :]
        dh = hbuf[lo:lo + n, :] - h
        z = _bdot(h, win_ref[...])
        p_w = _bdot(h + dh * muwag_ref[0:1, :], w1_ref[...])
        p_a = _bdot(h + dh * muwag_ref[1:2, :], a1_ref[...])
        p_g = _bdot(h + dh * muwag_ref[2:3, :], g1_ref[...])
        if has_vres:
            p_v = _bdot(h + dh * muv_ref[...], v1_ref[...])
        zrest_ref[lo:lo + n, :] = z[:, R3:D_IN]
        z_rkv = z[:, 0:R3]
        zbuf[nb + lo:nb + lo + n, :] = z_rkv
        yield
        zmix = z_rkv + (zbuf[lo:lo + n, :] - z_rkv) * murkv_ref[...]
        r = zmix[:, 0:RWKV_DIM]
        k = zmix[:, RWKV_DIM:2 * RWKV_DIM]
        v = zmix[:, 2 * RWKV_DIM:R3]
        kk = k * kk_ref[...]
        w_pre = w0_ref[...] + _bdot(jnp.tanh(p_w), w2_ref[...])
        a = _sigmoid(a0_ref[...] + _bdot(p_a, a2_ref[...]))
        gate = _bdot(_sigmoid(p_g), g2_ref[...])
        ss = _bdot(kk * kk, ones_ref[...])
        if has_vres:
            mix = _sigmoid(v0_ref[...] + _bdot(p_v, v2_ref[...]))
            v_first = jnp.concatenate([vfirst_ref[p, lo:lo + n, :] for p in range(N_PAIRS)], axis=-1)
            v = v + (v_first - v) * mix
        yield
        w_log = -_softplus(-w_pre) - 0.5
        kk = kk / jnp.maximum(jnp.sqrt(ss), 1e-12)
        outs = ((r_ref, r), (lw_ref, -jnp.exp(w_log)), (k_ref, k * (1.0 + (a - 1.0) * ka_ref[...])), (v_ref, v),
                (na_ref, -kk), (bb_ref, kk * a), (gate_ref, gate))
        for ref, val in outs:
            for p in range(N_PAIRS):
                ref[p, lo:lo + n, :] = val[:, p * HEAD_PAIR:(p + 1) * HEAD_PAIR]

    n_part = tm // MIX_IN_PARTS
    parts = [part(i * n_part, n_part) for i in range(MIX_IN_PARTS)]
    for gen in parts:
        next(gen)
    cols = [zrest_ref.at[:, j * LRU_DIM:(j + 1) * LRU_DIM] for j in range(3)]
    _run_lockstep(parts + [_ssm_steps(reset_first, nb, tm, *cols, *ssm_consts, *ssm_outs, *ssm_scratch)])
    hbuf[0:nb, :] = hbuf[tm:tm + nb, :]
    zbuf[0:nb, :] = zbuf[tm:tm + nb, :]


MIX_IN_PARTS = 2


def _mix_in(x, first, p, vres, v_first, conv0, h0, re0, im0, reset_first, nb, seq):
    assert LRU_DIM == S5_DIM
    m = nb * seq
    tm = _tile(m, max(nb, 8) * MIX_IN_PARTS, 512)
    hist = (CONV_W - 1) * nb
    has_vres = vres is not None
    seq_major_x = x.ndim == 3
    row = lambda w: pl.BlockSpec((tm, w), lambda i: (i, 0))
    pair_row = pl.BlockSpec((N_PAIRS, tm, HEAD_PAIR), lambda i: (0, i, 0))
    ins = [x, first, p["norm_mix"], p["w_in"], p["mu_rkv"], p["mu_wag"], p["w0"], p["w1"], p["w2"],
           p["a0"], p["a1"], p["a2"], p["g1"], p["g2"], p["k_k"], p["k_a"], p["ones_head"]]
    x_spec = pl.BlockSpec((nb, tm // nb, D_MODEL), lambda i: (0, i, 0)) if seq_major_x else row(D_MODEL)
    specs = [x_spec] + [_spec(a) for a in ins[1:]]
    if has_vres:
        extra = [vres["mu_v"], vres["v0"], vres["v1"], vres["v2"]]
        ins += extra + [v_first]
        specs += [_spec(a) for a in extra] + [pair_row]
    ssm_consts = [conv0, h0, p["conv_w"], p["conv_b"], p["lru_wa"], p["lru_ba"], p["lru_wi"], p["lru_bi"],
                  p["lru_lambda"], re0, im0, p["s5_abar_re"], p["s5_abar_im"], p["s5_bbar_re"], p["s5_bbar_im"],
                  p["s5_c_re"], p["s5_c_im"], p["s5_d"], p["s5_w_glu"], p["s5_b_glu"]]
    assert len(ssm_consts) == N_SSM_CONSTS
    ins += ssm_consts
    specs += [_spec(a) for a in ssm_consts]
    wide = jax.ShapeDtypeStruct((N_PAIRS, m, HEAD_PAIR), F32)
    st = jax.ShapeDtypeStruct((nb, S5_WIDTH), F32)
    outs = ([wide] * 7 + [jax.ShapeDtypeStruct((nb, D_MODEL), F32)]
            + [jax.ShapeDtypeStruct((m, LRU_DIM), F32), jax.ShapeDtypeStruct((hist, LRU_DIM), F32),
               jax.ShapeDtypeStruct((nb, LRU_DIM), F32), jax.ShapeDtypeStruct((m, S5_DIM), F32), st, st])
    out_specs = ([pair_row] * 7 + [_const_spec((nb, D_MODEL))]
                 + [row(LRU_DIM), _const_spec((hist, LRU_DIM)), _const_spec((nb, LRU_DIM)),
                    row(S5_DIM), _const_spec((nb, S5_WIDTH)), _const_spec((nb, S5_WIDTH))])
    scratch = [pltpu.VMEM((tm + nb, D_MODEL), F32), pltpu.VMEM((tm + nb, R3), F32),
               pltpu.VMEM((tm, D_IN - R3), F32),
               pltpu.VMEM((tm + hist, LRU_DIM), F32), pltpu.VMEM((tm, LRU_DIM), F32),
               pltpu.VMEM((tm, LRU_DIM), F32), pltpu.VMEM((nb, LRU_DIM), F32),
               pltpu.VMEM((tm, S5_WIDTH), F32), pltpu.VMEM((tm, S5_WIDTH), F32),
               pltpu.VMEM((nb, S5_WIDTH), F32), pltpu.VMEM((nb, S5_WIDTH), F32)]
    assert len(scratch) == 3 + N_SSM_SCRATCH
    if seq_major_x:
        assert (tm // nb) % 8 == 0
        outs.insert(0, jax.ShapeDtypeStruct((m, D_MODEL), F32))
        out_specs.insert(0, row(D_MODEL))
        scratch.append(pltpu.VMEM((LANE_BLOCKS, tm, LANES), F32))
    return pl.pallas_call(
        functools.partial(_mix_in_kernel, has_vres, seq_major_x, reset_first, nb, tm),
        grid=(m // tm,),
        in_specs=specs,
        out_specs=out_specs,
        out_shape=outs,
        scratch_shapes=scratch,
        compiler_params=_params("arbitrary"),
        name="mix_in",
    )(*_arrays(ins))


def _wkv_kernel(c, n_seq, group, stride, r_ref, lw_ref, k_ref, v_ref, na_ref, bb_ref, gate_ref, s0_ref,
                rk_ref, lnw_ref, lnb_ref, tri_ref, mean_ref,
                y_ref, sout_ref, s_scr):
    li = pl.program_id(1)

    @pl.when(li == 0)
    def _():
        zero = jnp.zeros((RWKV_HEAD, RWKV_HEAD), F32)
        for j in range(n_seq):
            for p in range(N_PAIRS):
                s_scr[j, p] = jnp.concatenate(
                    [jnp.concatenate([s0_ref[j, 2 * p], zero], axis=1),
                     jnp.concatenate([zero, s0_ref[j, 2 * p + 1]], axis=1)], axis=0)

    lane = lax.broadcasted_iota(jnp.int32, (1, HEAD_PAIR), 1)
    head_masks = (lane < RWKV_HEAD, lane >= RWKV_HEAD)
    trow = lax.broadcasted_iota(jnp.int32, (c, 2 * c), 0)
    tcol = lax.broadcasted_iota(jnp.int32, (c, 2 * c), 1)
    left_half = tcol < c
    scol = jnp.where(left_half, tcol, tcol - c)
    strict = scol < trow
    strict_left = strict & left_half
    strict_right = strict & (tcol >= c)
    incl = scol <= trow
    eye2 = (scol == trow).astype(F32)
    drow = lax.broadcasted_iota(jnp.int32, (2 * c, 2 * c), 0)
    dcol = lax.broadcasted_iota(jnp.int32, (2 * c, 2 * c), 1)
    diag_blocks = jnp.bitwise_xor(drow - c, dcol - c) >= 0
    brow = lax.broadcasted_iota(jnp.int32, (HEAD_PAIR, HEAD_PAIR), 0)
    bcol = lax.broadcasted_iota(jnp.int32, (HEAD_PAIR, HEAD_PAIR), 1)
    same_head = jnp.bitwise_xor(brow - RWKV_HEAD, bcol - RWKV_HEAD) >= 0
    n_double = max(int(math.log2(c)) - 1, 0)
    tri = tri_ref[...]
    zeros_c = jnp.zeros((c, HEAD_PAIR), F32)

    def pair_chunk(lw, r, k, v, na, bb, gate, sp, cols):
        lw_hi, lw_lo = _split2(lw)
        cl = jnp.dot(tri, lw_hi, preferred_element_type=F32) + jnp.dot(tri, lw_lo, preferred_element_type=F32)
        yield
        e_in = jnp.exp(cl)
        e_neg = jnp.exp(-cl)
        g_last = jnp.exp(jnp.sum(lw, axis=0, keepdims=True))
        at = na * jnp.exp(cl - lw)
        rt = r * e_in
        bt = bb * e_neg
        kt = k * e_neg
        ar = jnp.concatenate([at, rt], axis=0)
        bk = jnp.concatenate([bt, kt], axis=0)
        a_s = _bdot_nt(ar, sp)
        g0 = _bdot_nt(jnp.where(head_masks[0], ar, 0.0), bk)
        g1 = _bdot_nt(jnp.where(head_masks[1], ar, 0.0), jnp.concatenate([kt, bt], axis=0))
        yield
        ga0, ga1 = g0[0:c, :], g1[0:c, :]
        low = jnp.where(strict, jnp.where(left_half, ga0, ga1), 0.0)
        ak0 = _bdot(jnp.where(strict_right, ga0, 0.0), jnp.concatenate([zeros_c, v], axis=0))
        ak1 = _bdot(jnp.where(strict_left, ga1, 0.0), jnp.concatenate([v, zeros_c], axis=0))
        rhs = jnp.concatenate([jnp.where(head_masks[0], a_s[0:c, :] + ak0, 0.0),
                               jnp.where(head_masks[1], a_s[0:c, :] + ak1, 0.0)], axis=0)
        blockdiag = lambda x: jnp.where(diag_blocks, jnp.concatenate([x, x], axis=0), 0.0)
        t_inv = eye2 + low
        pw = low
        if n_double > 0:
            pw = _bdot(pw, blockdiag(pw))
            yield
        for level in range(n_double):
            if level == n_double - 1:
                t_inv = t_inv + _bdot(t_inv, blockdiag(pw))
            else:
                both = _bdot(jnp.concatenate([t_inv, pw], axis=0), blockdiag(pw))
                t_inv = t_inv + both[0:c, :]
                pw = both[c:2 * c, :]
            yield
        u = _bdot(t_inv, rhs)
        yield
        uv = jnp.concatenate([u, v], axis=0)
        y = a_s[c:2 * c, :] + jnp.where(
            head_masks[0], _bdot(jnp.where(incl, g0[c:2 * c, :], 0.0), uv),
            _bdot(jnp.where(incl, g1[c:2 * c, :], 0.0), jnp.concatenate([v, u], axis=0)))
        upd = _bdot_tn(uv, bk * g_last)
        s_new = sp * g_last + jnp.where(same_head, upd, 0.0)
        yield
        stats = _bdot(jnp.concatenate([y, r * k * rk_ref[:, cols]], axis=0), mean_ref[...])
        mu = stats[0:c, :]
        bonus = stats[c:2 * c, :] * float(RWKV_HEAD) * v
        yield
        d = y - mu
        var = _bdot(d * d, mean_ref[...])
        yield
        yn = d * lax.rsqrt(var + GN_EPS) * lnw_ref[:, cols] + lnb_ref[:, cols]
        return (yn + bonus) * gate, s_new

    first_seq = pl.program_id(0) * n_seq

    def seq_group(gi, carry):
        units = []
        for jj in range(group):
            j = gi * group + jj
            rows = pl.ds(first_seq + j, c, stride=stride)
            for p in range(N_PAIRS):
                cols = slice(p * HEAD_PAIR, (p + 1) * HEAD_PAIR)
                vals = [ref[p, rows, :] for ref in (lw_ref, r_ref, k_ref, v_ref, na_ref, bb_ref, gate_ref)]
                units.append((j, p, rows, pair_chunk(*vals, s_scr[j, p], cols)))
        results = _run_lockstep([gen for (_, _, _, gen) in units])
        for (j, p, rows, _), (y, s_new) in zip(units, results):
            y_ref[p, rows, :] = y
            s_scr[j, p] = s_new
        return carry

    lax.fori_loop(0, n_seq // group, seq_group, 0)

    @pl.when(li == pl.num_programs(1) - 1)
    def _():
        for j in range(n_seq):
            for p in range(N_PAIRS):
                sp = s_scr[j, p]
                sout_ref[j, 2 * p] = sp[0:RWKV_HEAD, 0:RWKV_HEAD]
                sout_ref[j, 2 * p + 1] = sp[RWKV_HEAD:HEAD_PAIR, RWKV_HEAD:HEAD_PAIR]


WKV_SEQ_PER_STEP = 8
WKV_SEQ_GROUP = 8


def _wkv(arrs, s_all, layer, p, nb, seq):
    c = min(WKV_CHUNK, seq)
    n_seq = nb if seq > c else min(nb, WKV_SEQ_PER_STEP)
    assert seq % c == 0 and c % 8 == 0 and nb % n_seq == 0 and n_seq % WKV_SEQ_GROUP == 0
    blk = pl.BlockSpec((N_PAIRS, c * nb, HEAD_PAIR), lambda g, l: (0, l, 0))
    tri = jnp.tril(jnp.ones((c, c), F32)).astype(BF16)
    st_spec = pl.BlockSpec((None, n_seq, RWKV_HEADS, RWKV_HEAD, RWKV_HEAD), lambda g, l: (layer, g, 0, 0, 0))
    consts = [p["r_k"], p["ln_x_w"], p["ln_x_b"], tri, p["mean_pair"]]
    return pl.pallas_call(
        functools.partial(_wkv_kernel, c, n_seq, WKV_SEQ_GROUP, nb),
        grid=(nb // n_seq, seq // c),
        in_specs=[blk] * 7 + [st_spec] + [_spec(a) for a in consts],
        out_specs=[blk, st_spec],
        out_shape=[jax.ShapeDtypeStruct((N_PAIRS, seq * nb, HEAD_PAIR), F32), jax.ShapeDtypeStruct(s_all.shape, F32)],
        input_output_aliases={7: 1},
        scratch_shapes=[pltpu.VMEM((n_seq, N_PAIRS, HEAD_PAIR, HEAD_PAIR), F32)],
        compiler_params=_params("arbitrary", "arbitrary"),
        name="wkv",
    )(*arrs, s_all, *_arrays(consts))


def _ssm_init(nb, consts, scratch):
    conv0_ref, h0_ref, re0_ref, im0_ref = consts[0], consts[1], consts[9], consts[10]
    xbuf, _, _, hcar, _, _, car_re, car_im = scratch
    xbuf[0:(CONV_W - 1) * nb, :] = conv0_ref[...]
    hcar[...] = h0_ref[...]
    car_re[...] = re0_ref[...]
    car_im[...] = im0_ref[...]


def _ssm_steps(reset_first, nb, tm, zx_ref, zg_ref, u_ref,
               conv0_ref, h0_ref, cw_ref, cb_ref, wa_ref, ba_ref, wi_ref, bi_ref, lam_ref,
               re0_ref, im0_ref, are_ref, aim_ref, bre_ref, bim_ref, cre_ref, cim_ref, d_ref, wglu_ref, bglu_ref,
               y_ref, conv_out_ref, h_out_ref, yc_ref, re_out_ref, im_out_ref,
               xbuf, abuf, bbuf, hcar, xre, xim, car_re, car_im):
    i = pl.program_id(0)
    hist = (CONV_W - 1) * nb

    u = u_ref[...]
    xre[...] = _bdot(u, bre_ref[...])
    xim[...] = _bdot(u, bim_ref[...])

    xbuf[hist:hist + tm, :] = zx_ref[...]
    xc = cb_ref[...]
    for j in range(CONV_W):
        xc = xc + xbuf[j * nb:j * nb + tm, :] * cw_ref[j:j + 1, :]
    conv_out_ref[...] = xbuf[tm:tm + hist, :]
    xbuf[0:hist, :] = xbuf[tm:tm + hist, :]

    gate_a = _sigmoid(_bdot(xc, wa_ref[...]) + ba_ref[...])
    gate_i = _sigmoid(_bdot(xc, wi_ref[...]) + bi_ref[...])
    log_a = -LRU_C * gate_a * _softplus(-lam_ref[...])
    a_sq = jnp.exp(2.0 * log_a)
    mult = jnp.sqrt(-jnp.tanh(log_a) * (a_sq + 1.0))
    if reset_first:
        row = lax.broadcasted_iota(jnp.int32, (tm, 1), 0)
        mult = jnp.where(row < jnp.where(i == 0, nb, 0), 1.0, mult)
    abuf[...] = jnp.exp(log_a)
    bbuf[...] = mult * gate_i * xc
    yield

    a_re = jnp.broadcast_to(are_ref[...], (nb, S5_WIDTH))
    a_im = jnp.broadcast_to(aim_ref[...], (nb, S5_WIDTH))
    h, s_re, s_im = hcar[...], car_re[...], car_im[...]
    for t in range(tm // nb):
        rows = slice(t * nb, (t + 1) * nb)
        h = abuf[rows, :] * h + bbuf[rows, :]
        bbuf[rows, :] = h
        s_re, s_im = (a_re * s_re - a_im * s_im + xre[rows, :], a_re * s_im + a_im * s_re + xim[rows, :])
        xre[rows, :] = s_re
        xim[rows, :] = s_im
    hcar[...] = h
    h_out_ref[...] = h
    car_re[...] = s_re
    car_im[...] = s_im
    re_out_ref[...] = s_re
    im_out_ref[...] = s_im

    y_ref[...] = bbuf[...] * _gelu(zg_ref[...])
    y = _bdot(xre[...], cre_ref[...]) - _bdot(xim[...], cim_ref[...]) + d_ref[...] * u
    yield
    o = _bdot(_gelu(y), wglu_ref[...]) + bglu_ref[...]
    yield
    yc_ref[...] = o[:, 0:S5_DIM] * _sigmoid(o[:, S5_DIM:2 * S5_DIM])


def _mix_out_kernel(n_part, x_ref, ya_ref, yb_ref, yc_ref, wa_ref, wb_ref, wc_ref, gq_ref, wq_ref, x1_ref, q_ref):
    def part(lo, n):
        rows = slice(lo, lo + n)
        ya = jnp.concatenate([ya_ref[p, rows, :] for p in range(N_PAIRS)], axis=-1)
        x1 = (x_ref[rows, :] + _bdot(ya, wa_ref[...]) + _bdot(yb_ref[rows, :], wb_ref[...])
              + _bdot(yc_ref[rows, :], wc_ref[...]))
        yield
        q = _bdot(_rms(x1, gq_ref[...]), wq_ref[...])
        yield
        for cb in range(LANE_BLOCKS):
            cols = slice(cb * LANES, (cb + 1) * LANES)
            x1_ref[cb, rows, :] = x1[:, cols]
            q_ref[cb, rows, :] = q[:, cols]

    n = x_ref.shape[0] // n_part
    _run_lockstep([part(i * n, n) for i in range(n_part)])


MIX_OUT_PARTS = 2


def _mix_out(x, ya, yb, yc, p):
    m = x.shape[0]
    tm = _tile(m, 8 * MIX_OUT_PARTS, 512)
    row = lambda w: pl.BlockSpec((tm, w), lambda i: (i, 0))
    split = pl.BlockSpec((LANE_BLOCKS, tm, LANES), lambda i: (0, i, 0))
    consts = [p["w_out_a"], p["w_out_b"], p["w_out_c"], p["norm_mem_q"], p["mem_wq"]]
    out = jax.ShapeDtypeStruct((LANE_BLOCKS, m, LANES), F32)
    return pl.pallas_call(
        functools.partial(_mix_out_kernel, MIX_OUT_PARTS),
        grid=(m // tm,),
        in_specs=[row(D_MODEL), pl.BlockSpec((N_PAIRS, tm, HEAD_PAIR), lambda i: (0, i, 0)), row(LRU_DIM),
                  row(S5_DIM)] + [_spec(a) for a in consts],
        out_specs=[split, split],
        out_shape=[out, out],
        compiler_params=_params("parallel"),
        name="mix_out",
    )(x, ya, yb, yc, *_arrays(consts))


def _load_seq_rows(ref, rows):
    return jnp.concatenate([ref[cb, rows, :] for cb in range(LANE_BLOCKS)], axis=-1)


def _store_seq_rows(ref, rows, val):
    for cb in range(LANE_BLOCKS):
        ref[cb, rows, :] = val[:, cb * LANES:(cb + 1) * LANES]


def _softmax_rows(s):
    e = jnp.exp(s - jnp.max(s, axis=-1, keepdims=True))
    return e / jnp.sum(e, axis=-1, keepdims=True)


def _attn_kernel(tl, n_seq, stride, q_ref, x_ref, mk_ref, mv_ref, wo_ref, o_ref):
    scale = MEM_HEAD_DIM ** -0.5
    first_seq = pl.program_id(0) * n_seq
    head_cols = [slice(hh * MEM_HEAD_DIM, (hh + 1) * MEM_HEAD_DIM) for hh in range(MEM_HEADS)]

    def one_seq(j):
        rows = pl.ds(first_seq + j, tl, stride=stride)
        q = _load_seq_rows(q_ref, rows)
        scores = [_bdot_nt(q[:, cols], mk_ref[j, :, cols]) * scale for cols in head_cols]
        yield
        heads = [_bdot(_softmax_rows(s), mv_ref[j, :, cols]) for s, cols in zip(scores, head_cols)]
        yield
        out = _load_seq_rows(x_ref, rows) + _bdot(jnp.concatenate(heads, axis=-1), wo_ref[...])
        yield
        _store_seq_rows(o_ref, rows, out)

    for j0 in range(0, n_seq, ATTN_LOCKSTEP):
        _run_lockstep([one_seq(j) for j in range(j0, min(j0 + ATTN_LOCKSTEP, n_seq))])


def _attn_cache_kernel(tl, n_seq, stride, q_ref, x_ref, mk_ref, mv_ref, wo_ref, o_ref):
    scale = MEM_HEAD_DIM ** -0.5
    first_seq = pl.program_id(0) * n_seq
    qrow = lax.broadcasted_iota(jnp.int32, (MEM_HEADS * tl, N_MEM * MEM_HEADS), 0)
    kcol = lax.broadcasted_iota(jnp.int32, (MEM_HEADS * tl, N_MEM * MEM_HEADS), 1)
    q_head = jnp.zeros_like(qrow)
    for hh in range(1, MEM_HEADS):
        q_head = q_head + (qrow >= hh * tl).astype(jnp.int32)
    same_head = q_head == jnp.bitwise_and(kcol, MEM_HEADS - 1)

    def one_seq(j):
        rows = pl.ds(first_seq + j, tl, stride=stride)
        q = _load_seq_rows(q_ref, rows)
        qh = jnp.concatenate([q[:, hh * MEM_HEAD_DIM:(hh + 1) * MEM_HEAD_DIM] for hh in range(MEM_HEADS)], axis=0)
        k2 = mk_ref[j].reshape(N_MEM * MEM_HEADS, MEM_HEAD_DIM)
        s = jnp.where(same_head, _bdot_nt(qh, k2) * scale, MASKED_SCORE)
        yield
        v2 = mv_ref[j].reshape(N_MEM * MEM_HEADS, MEM_HEAD_DIM)
        oh = _bdot(_softmax_rows(s), v2)
        yield
        o = jnp.concatenate([oh[hh * tl:(hh + 1) * tl, :] for hh in range(MEM_HEADS)], axis=-1)
        out = _load_seq_rows(x_ref, rows) + _bdot(o, wo_ref[...])
        yield
        _store_seq_rows(o_ref, rows, out)

    _run_lockstep([one_seq(j) for j in range(n_seq)])


MASKED_SCORE = -1e30
ATTN_SEQ_PER_STEP = 4
ATTN_ROWS = 128
ATTN_LOCKSTEP = 4


def _attn(q, x1, mk, mv, kv_offset, wo, nb, seq):
    tl = _tile(seq, 8, ATTN_ROWS)
    n_seq = nb if seq > tl else min(nb, ATTN_SEQ_PER_STEP)
    assert nb % n_seq == 0 and kv_offset % n_seq == 0 and MEM_HEADS & (MEM_HEADS - 1) == 0
    blk = pl.BlockSpec((LANE_BLOCKS, tl * nb, LANES), lambda g, l: (0, l, 0))
    kv_index = lambda g, l: (kv_offset // n_seq + g,) + (0,) * (mk.ndim - 1)
    kv = pl.BlockSpec((n_seq,) + mk.shape[1:], kv_index)
    body = _attn_kernel if mk.ndim == 3 else _attn_cache_kernel
    return pl.pallas_call(
        functools.partial(body, tl, n_seq, nb),
        grid=(nb // n_seq, seq // tl),
        in_specs=[blk, blk, kv, kv, _spec(wo)],
        out_specs=blk,
        out_shape=jax.ShapeDtypeStruct((LANE_BLOCKS, seq * nb, LANES), F32),
        compiler_params=_params("arbitrary", "arbitrary"),
        name="mem_attn",
    )(q, x1, mk, mv, *_arrays([wo]))


def _ffn_kernel(d_ff, final_norm, n_seq, x_ref, g_ref, wup_ref, wdown_ref, *rest):
    x = jnp.concatenate([x_ref[cb] for cb in range(LANE_BLOCKS)], axis=-1)
    h = _rms(x, g_ref[...]).astype(BF16)
    gt = jnp.dot(h, wup_ref[:, 0:d_ff], preferred_element_type=F32)
    up = jnp.dot(h, wup_ref[:, d_ff:2 * d_ff], preferred_element_type=F32)
    act = gt * _sigmoid(gt) * up
    out = x + _bdot(act, wdown_ref[...])
    if final_norm:
        out = _rms(out, rest[0][...])
    if n_seq:
        o_ref, os = rest[-2:]
        for cb in range(LANE_BLOCKS):
            os[cb] = out[:, cb * LANES:(cb + 1) * LANES]
        for b in range(n_seq):
            o_ref[b] = _load_seq_rows(os, pl.ds(b, out.shape[0] // n_seq, stride=n_seq))
    else:
        rest[-1][...] = out


def _ffn(x, p, g_final=None, seq_major_out=None):
    m = x.shape[1]
    tm = _tile(m, 8, 512)
    d_ff = p["ffn_w_down"].shape[0]
    consts = [p["norm_ffn"], p["ffn_w_up"], p["ffn_w_down"]] + ([] if g_final is None else [g_final])
    if seq_major_out is None:
        n_seq, scratch = 0, []
        out_spec = pl.BlockSpec((tm, D_MODEL), lambda i: (i, 0))
        out_shape = jax.ShapeDtypeStruct((m, D_MODEL), F32)
    else:
        n_seq, seq = seq_major_out
        assert tm % n_seq == 0 and (tm // n_seq) % 8 == 0 and n_seq * seq == m
        scratch = [pltpu.VMEM((LANE_BLOCKS, tm, LANES), F32)]
        out_spec = pl.BlockSpec((n_seq, tm // n_seq, D_MODEL), lambda i: (0, i, 0))
        out_shape = jax.ShapeDtypeStruct((n_seq, seq, D_MODEL), F32)
    return pl.pallas_call(
        functools.partial(_ffn_kernel, d_ff, g_final is not None, n_seq),
        grid=(m // tm,),
        in_specs=[pl.BlockSpec((LANE_BLOCKS, tm, LANES), lambda i: (0, i, 0))] + [_spec(a) for a in consts],
        out_specs=out_spec,
        out_shape=out_shape,
        scratch_shapes=scratch,
        compiler_params=_params("parallel"),
        name="ffn",
    )(x, *_arrays(consts))


def _mem_project_kernel(x_ref, g_ref, wk_ref, wv_ref, k_ref, v_ref, kb_ref, vb_ref):
    m = _rms(x_ref[...], g_ref[...]).astype(BF16)
    k = jnp.dot(m, wk_ref[...], preferred_element_type=F32)
    v = jnp.dot(m, wv_ref[...], preferred_element_type=F32)
    k_ref[...] = k.reshape(k_ref.shape)
    v_ref[...] = v.reshape(v_ref.shape)
    kb_ref[...] = k.astype(BF16)
    vb_ref[...] = v.astype(BF16)


MEM_SEQ_PER_STEP = 2


def _mem_project(mem, g, wk, wv):
    n, n_mem, _ = mem.shape
    depth = wk.shape[0]
    tm = MEM_SEQ_PER_STEP * n_mem
    assert n % MEM_SEQ_PER_STEP == 0
    per_layer = lambda a: pl.BlockSpec((None,) + a.shape[1:], lambda l, i: (l,) + (0,) * (a.ndim - 1))
    cache = pl.BlockSpec((None, MEM_SEQ_PER_STEP, n_mem, MEM_HEADS, MEM_HEAD_DIM), lambda l, i: (l, i, 0, 0, 0))
    rows_b = pl.BlockSpec((None, tm, D_MODEL), lambda l, i: (l, i, 0))
    out = jax.ShapeDtypeStruct((depth, n, n_mem, MEM_HEADS, MEM_HEAD_DIM), F32)
    outb = jax.ShapeDtypeStruct((depth, n * n_mem, D_MODEL), BF16)
    return pl.pallas_call(
        _mem_project_kernel, grid=(depth, n // MEM_SEQ_PER_STEP),
        in_specs=[pl.BlockSpec((tm, D_MODEL), lambda l, i: (i, 0)), per_layer(g), per_layer(wk), per_layer(wv)],
        out_specs=[cache, cache, rows_b, rows_b], out_shape=[out, out, outb, outb],
        compiler_params=_params("parallel", "parallel"),
        name="mem_project",
    )(mem.reshape(n * n_mem, D_MODEL), g, wk, wv)


def _block_diag(blocks):
    n, r, c = blocks.shape[-3:]
    eye = jnp.eye(n, dtype=blocks.dtype)
    out = eye[:, None, :, None] * blocks[..., :, :, None, :]
    return out.reshape(blocks.shape[:-3] + (n * r, n * c))


def _stack_params(w):
    depth = w["norm_mix"].shape[0]
    row = lambda a: a.reshape(a.shape[0], 1, -1).astype(F32)
    bf = lambda a: a.astype(BF16)
    t = lambda a: jnp.swapaxes(a, -1, -2)
    p = {k: row(w[k]) for k in ("norm_mix", "mu_rkv", "w0", "a0", "k_k", "k_a", "r_k", "ln_x_w", "ln_x_b", "conv_b",
                                "lru_ba", "lru_bi", "lru_lambda", "s5_d", "s5_b_glu", "norm_mem_q", "norm_mem_kv",
                                "norm_ffn", "mu_v", "v0")}
    p.update({k: bf(w[k]) for k in ("w_in", "w1", "w2", "a1", "a2", "g1", "g2", "s5_w_glu", "mem_wq", "mem_wk",
                                    "mem_wv", "mem_wo", "ffn_w_up", "ffn_w_down", "v1", "v2")})
    p["mu_wag"], p["conv_w"] = w["mu_wag"], w["conv_w"]
    p["lru_wa"], p["lru_wi"] = bf(_block_diag(w["lru_wa"])), bf(_block_diag(w["lru_wi"]))
    w_out = bf(w["w_out"])
    p["w_out_a"] = w_out[:, 0:RWKV_DIM]
    p["w_out_b"] = w_out[:, RWKV_DIM:RWKV_DIM + LRU_DIM]
    p["w_out_c"] = w_out[:, RWKV_DIM + LRU_DIM:]
    lam_re, lam_im = w["s5_a_re"].astype(F32), w["s5_a_im"].astype(F32)
    dt = jnp.exp(w["s5_log_dt"].astype(F32))[..., None]
    mag = jnp.exp(lam_re * dt)
    abar_re, abar_im = mag * jnp.cos(lam_im * dt), mag * jnp.sin(lam_im * dt)
    den = lam_re * lam_re + lam_im * lam_im
    q_re = ((abar_re - 1.0) * lam_re + abar_im * lam_im) / den
    q_im = (abar_im * lam_re - (abar_re - 1.0) * lam_im) / den
    b_re, b_im = w["s5_b_re"].astype(F32), w["s5_b_im"].astype(F32)
    bbar_re = q_re[..., None] * b_re - q_im[..., None] * b_im
    bbar_im = q_re[..., None] * b_im + q_im[..., None] * b_re
    p["s5_abar_re"] = abar_re.reshape(depth, 1, S5_WIDTH)
    p["s5_abar_im"] = abar_im.reshape(depth, 1, S5_WIDTH)
    p["s5_bbar_re"] = bf(_block_diag(t(bbar_re)))
    p["s5_bbar_im"] = bf(_block_diag(t(bbar_im)))
    p["s5_c_re"] = bf(_block_diag(t(w["s5_c_re"])))
    p["s5_c_im"] = bf(_block_diag(t(w["s5_c_im"])))
    return p


def _layer_params(stacked, l, consts):
    vres_keys = ("mu_v", "v0", "v1", "v2")
    p = {k: _Layer(a, l) for k, a in stacked.items() if k not in vres_keys}
    p.update(consts)
    vres = {k: _Layer(stacked[k], l - 1) for k in vres_keys} if l > 0 else None
    return p, vres


def _time_major(a):
    return jnp.swapaxes(a, 0, 1).reshape((a.shape[0] * a.shape[1],) + a.shape[2:])


def _run_group(x, mem_k, mem_v, kv_stride, shift0, wkv0, conv0, lru0, re0, im0, reset_first, layers, norm_final,
               relayout_in_kernel):
    nb, seq, _ = x.shape
    xt = x if relayout_in_kernel else _time_major(x)
    outs = {k: [] for k in ("shift", "conv", "lru", "re", "im")}
    wkv_all = wkv0
    v_first = None
    for l, (p, vres) in enumerate(layers):
        last = l == len(layers) - 1
        mixed = _mix_in(xt, _Layer(shift0, l), p, vres, v_first, _time_major(conv0[l]), _Layer(lru0, l),
                        _Layer(re0.reshape(-1, nb, S5_WIDTH), l), _Layer(im0.reshape(-1, nb, S5_WIDTH), l),
                        reset_first, nb, seq)
        if xt.ndim == 3:
            xt, mixed = mixed[0], mixed[1:]
        r, lw, k, v, na, bb, gate, h_last, yb, conv_new, lru_new, yc, re_new, im_new = mixed
        if l == 0:
            v_first = v
        ya, wkv_all = _wkv((r, lw, k, v, na, bb, gate), wkv_all, l, p, nb, seq)
        x1, q = _mix_out(xt, ya, yb, yc, p)
        x2 = _attn(q, x1, mem_k, mem_v, l * kv_stride, p["mem_wo"], nb, seq)
        xt = _ffn(x2, p, norm_final if last else None, (nb, seq) if last and relayout_in_kernel else None)
        outs["shift"].append(h_last)
        outs["conv"].append(jnp.swapaxes(conv_new.reshape(CONV_W - 1, nb, LRU_DIM), 0, 1))
        outs["lru"].append(lru_new)
        outs["re"].append(re_new.reshape(nb, S5_GROUPS, S5_STATE))
        outs["im"].append(im_new.reshape(nb, S5_GROUPS, S5_STATE))
    y = xt if relayout_in_kernel else jnp.swapaxes(xt.reshape(seq, nb, D_MODEL), 0, 1)
    st = {k: jnp.stack(v) for k, v in outs.items()}
    return (y, st["shift"], wkv_all, st["conv"], st["lru"], st["re"], st["im"])


def kernel(x_prompt, x_sample, mem_prompt, state_shift, state_wkv, state_conv, state_lru, state_s5_re, state_s5_im, cache_mem_k, cache_mem_v, norm_mix, w_in, w_out, mu_rkv, mu_wag, mu_v, w0, w1, w2, a0, a1, a2, v0, v1, v2, g1, g2, k_k, k_a, r_k, ln_x_w, ln_x_b, conv_w, conv_b, lru_wa, lru_ba, lru_wi, lru_bi, lru_lambda, s5_a_re, s5_a_im, s5_log_dt, s5_b_re, s5_b_im, s5_c_re, s5_c_im, s5_d, s5_w_glu, s5_b_glu, norm_mem_q, norm_mem_kv, mem_wq, mem_wk, mem_wv, mem_wo, norm_ffn, ffn_w_up, ffn_w_down, norm_final):
    w = dict(norm_mix=norm_mix, w_in=w_in, w_out=w_out, mu_rkv=mu_rkv, mu_wag=mu_wag, mu_v=mu_v, w0=w0, w1=w1,
             w2=w2, a0=a0, a1=a1, a2=a2, v0=v0, v1=v1, v2=v2, g1=g1, g2=g2, k_k=k_k, k_a=k_a, r_k=r_k,
             ln_x_w=ln_x_w, ln_x_b=ln_x_b, conv_w=conv_w, conv_b=conv_b, lru_wa=lru_wa, lru_ba=lru_ba,
             lru_wi=lru_wi, lru_bi=lru_bi, lru_lambda=lru_lambda, s5_a_re=s5_a_re, s5_a_im=s5_a_im,
             s5_log_dt=s5_log_dt, s5_b_re=s5_b_re, s5_b_im=s5_b_im, s5_c_re=s5_c_re, s5_c_im=s5_c_im,
             s5_d=s5_d, s5_w_glu=s5_w_glu, s5_b_glu=s5_b_glu, norm_mem_q=norm_mem_q, norm_mem_kv=norm_mem_kv,
             mem_wq=mem_wq, mem_wk=mem_wk, mem_wv=mem_wv, mem_wo=mem_wo, norm_ffn=norm_ffn, ffn_w_up=ffn_w_up,
             ffn_w_down=ffn_w_down)
    depth = norm_mix.shape[0]
    ones64 = jnp.ones((RWKV_HEADS, RWKV_HEAD, RWKV_HEAD), F32)
    consts = {"ones_head": _block_diag(ones64).astype(BF16),
              "mean_pair": (_block_diag(ones64[:2]) / RWKV_HEAD).astype(BF16)}
    stacked = _stack_params(w)
    layers = [_layer_params(stacked, l, consts) for l in range(depth)]
    g_final = norm_final.reshape(1, D_MODEL)

    bp, n_mem, _ = mem_prompt.shape
    mem_k_p, mem_v_p, mem_k_b, mem_v_b = _mem_project(
        mem_prompt, stacked["norm_mem_kv"], stacked["mem_wk"], stacked["mem_wv"])
    mem_k_b = mem_k_b.reshape(depth * bp, n_mem, D_MODEL)
    mem_v_b = mem_v_b.reshape(depth * bp, n_mem, D_MODEL)

    zeros = lambda *s: jnp.zeros((depth, bp) + s, F32)
    prompt = _run_group(
        x_prompt, mem_k_b, mem_v_b, bp,
        zeros(D_MODEL), zeros(RWKV_HEADS, RWKV_HEAD, RWKV_HEAD), zeros(CONV_W - 1, LRU_DIM), zeros(LRU_DIM),
        zeros(S5_GROUPS, S5_STATE), zeros(S5_GROUPS, S5_STATE), True, layers, g_final, True)
    bs = x_sample.shape[0]
    cache_shape = (depth * bs, n_mem, MEM_HEADS, MEM_HEAD_DIM)
    sample = _run_group(
        x_sample, cache_mem_k.reshape(cache_shape), cache_mem_v.reshape(cache_shape),
        bs, state_shift, state_wkv, state_conv, state_lru, state_s5_re, state_s5_im, False, layers, g_final, False)
    (y_p, sh_p, wkv_p, conv_p, lru_p, re_p, im_p) = prompt
    (y_s, sh_s, wkv_s, conv_s, lru_s, re_s, im_s) = sample
    return (y_p, y_s, sh_p, sh_s, wkv_p, wkv_s, conv_p, conv_s, lru_p, lru_s, re_p, re_s, im_p, im_s,
            mem_k_p, mem_v_p)
```

```python
import functools
import math

import jax
import jax.numpy as jnp
from jax import lax
from jax.experimental import pallas as pl
from jax.experimental.pallas import tpu as pltpu

F32 = jnp.float32
BF16 = jnp.bfloat16

D_MODEL = 1024
RWKV_DIM = 512
RWKV_HEAD = 64
RWKV_HEADS = 8
LANES = 128
LANE_BLOCKS = D_MODEL // LANES
HEAD_PAIR = 2 * RWKV_HEAD
N_PAIRS = RWKV_DIM // HEAD_PAIR
LRU_DIM = 256
LRU_BLOCKS = 4
CONV_W = 4
LRU_C = 8.0
S5_DIM = 256
S5_GROUP = 16
S5_GROUPS = 16
S5_STATE = 64
S5_WIDTH = S5_GROUPS * S5_STATE
R3 = 3 * RWKV_DIM
D_IN = R3 + 2 * LRU_DIM + S5_DIM
N_MEM = 256
MEM_HEADS = 4
MEM_HEAD_DIM = 256
RMS_EPS = 1e-6
GN_EPS = 64e-5
WKV_CHUNK = 64

V7X_VMEM_LIMIT = 56 * 1024 * 1024


def _params(*sem):
    return pltpu.CompilerParams(dimension_semantics=sem, vmem_limit_bytes=V7X_VMEM_LIMIT)


def _tile(m, mult, target):
    best = mult
    t = mult
    while t <= min(m, target):
        if m % t == 0:
            best = t
        t += mult
    assert m % best == 0, (m, mult, target)
    return best


def _const_spec(shape):
    nd = len(shape)
    return pl.BlockSpec(shape, lambda *_: (0,) * nd)


class _Layer:
    def __init__(self, arr, layer):
        self.arr, self.layer, self.shape = arr, layer, arr.shape[1:]


def _spec(a):
    if isinstance(a, _Layer):
        nd = len(a.shape)
        return pl.BlockSpec((None,) + a.shape, lambda *_: (a.layer,) + (0,) * nd)
    return _const_spec(a.shape)


def _arrays(operands):
    return [a.arr if isinstance(a, _Layer) else a for a in operands]


def _bdot(a, b):
    return jnp.dot(a.astype(BF16), b.astype(BF16), preferred_element_type=F32)


def _bdot_nt(a, b):
    return lax.dot_general(a.astype(BF16), b.astype(BF16), (((1,), (1,)), ((), ())),
                           preferred_element_type=F32)


def _bdot_tn(a, b):
    return lax.dot_general(a.astype(BF16), b.astype(BF16), (((0,), (0,)), ((), ())),
                           preferred_element_type=F32)


def _split2(x):
    hi = x.astype(BF16)
    lo = (x - hi.astype(F32)).astype(BF16)
    return hi, lo


def _run_lockstep(gens):
    results = {}
    while len(results) < len(gens):
        for i, gen in enumerate(gens):
            if i not in results:
                try:
                    next(gen)
                except StopIteration as stop:
                    results[i] = stop.value
    return [results[i] for i in range(len(gens))]


def _rms(x, g):
    return x * lax.rsqrt(jnp.mean(x * x, axis=-1, keepdims=True) + RMS_EPS) * g


def _softplus(x):
    return jnp.maximum(x, 0.0) + jnp.log1p(jnp.exp(-jnp.abs(x)))


def _sigmoid(x):
    return 1.0 / (1.0 + jnp.exp(-x))


def _gelu(x):
    c = math.sqrt(2.0 / math.pi)
    return x * (0.5 * (1.0 + jnp.tanh(c * (x + 0.044715 * (x * x * x)))))


N_SSM_CONSTS, N_SSM_OUTS, N_SSM_SCRATCH = 20, 6, 8


def _mix_in_kernel(has_vres, seq_major_x, reset_first, nb, tm, *refs):
    refs = list(refs)
    (x_ref, first_ref, g_ref, win_ref, murkv_ref, muwag_ref, w0_ref, w1_ref, w2_ref,
     a0_ref, a1_ref, a2_ref, g1_ref, g2_ref, kk_ref, ka_ref, ones_ref) = refs[:17]
    del refs[:17]
    if has_vres:
        muv_ref, v0_ref, v1_ref, v2_ref, vfirst_ref = refs[:5]
        del refs[:5]
    ssm_consts = refs[:N_SSM_CONSTS]
    del refs[:N_SSM_CONSTS]
    if seq_major_x:
        xt_ref = refs.pop(0)
    r_ref, lw_ref, k_ref, v_ref, na_ref, bb_ref, gate_ref, hlast_ref = refs[:8]
    del refs[:8]
    ssm_outs = refs[:N_SSM_OUTS]
    del refs[:N_SSM_OUTS]
    hbuf, zbuf, zrest_ref = refs[:3]
    ssm_scratch = refs[3:3 + N_SSM_SCRATCH]

    @pl.when(pl.program_id(0) == 0)
    def _():
        first = first_ref[...]
        hbuf[0:nb, :] = first
        zbuf[0:nb, :] = _bdot(first, win_ref[:, 0:R3])
        _ssm_init(nb, ssm_consts, ssm_scratch)

    if seq_major_x:
        xs = refs[-1]
        for b in range(nb):
            _store_seq_rows(xs, pl.ds(b, tm // nb, stride=nb), x_ref[b])
        x = jnp.concatenate([xs[cb] for cb in range(LANE_BLOCKS)], axis=-1)
        xt_ref[...] = x
    else:
        x = x_ref[...]
    h_all = _rms(x, g_ref[...])
    hbuf[nb:nb + tm, :] = h_all
    hlast_ref[...] = h_all[tm - nb:tm, :]

    def part(lo, n):
        h = hbuf[nb + lo:nb + lo + n, :]
        dh = hbuf[lo:lo + n, :] - h
        z = _bdot(h, win_ref[...])
        p_w = _bdot(h + dh * muwag_ref[0:1, :], w1_ref[...])
        p_a = _bdot(h + dh * muwag_ref[1:2, :], a1_ref[...])
        p_g = _bdot(h + dh * muwag_ref[2:3, :], g1_ref[...])
        if has_vres:
            p_v = _bdot(h + dh * muv_ref[...], v1_ref[...])
        zrest_ref[lo:lo + n, :] = z[:, R3:D_IN]
        z_rkv = z[:, 0:R3]
        zbuf[nb + lo:nb + lo + n, :] = z_rkv
        yield
        zmix = z_rkv + (zbuf[lo:lo + n, :] - z_rkv) * murkv_ref[...]
        r = zmix[:, 0:RWKV_DIM]
        k = zmix[:, RWKV_DIM:2 * RWKV_DIM]
        v = zmix[:, 2 * RWKV_DIM:R3]
        kk = k * kk_ref[...]
        w_pre = w0_ref[...] + _bdot(jnp.tanh(p_w), w2_ref[...])
        a = _sigmoid(a0_ref[...] + _bdot(p_a, a2_ref[...]))
        gate = _bdot(_sigmoid(p_g), g2_ref[...])
        ss = _bdot(kk * kk, ones_ref[...])
        if has_vres:
            mix = _sigmoid(v0_ref[...] + _bdot(p_v, v2_ref[...]))
            v_first = jnp.concatenate([vfirst_ref[p, lo:lo + n, :] for p in range(N_PAIRS)], axis=-1)
            v = v + (v_first - v) * mix
        yield
        w_log = -_softplus(-w_pre) - 0.5
        kk = kk / jnp.maximum(jnp.sqrt(ss), 1e-12)
        outs = ((r_ref, r), (lw_ref, -jnp.exp(w_log)), (k_ref, k * (1.0 + (a - 1.0) * ka_ref[...])), (v_ref, v),
                (na_ref, -kk), (bb_ref, kk * a), (gate_ref, gate))
        for ref, val in outs:
            for p in range(N_PAIRS):
                ref[p, lo:lo + n, :] = val[:, p * HEAD_PAIR:(p + 1) * HEAD_PAIR]

    n_part = tm // MIX_IN_PARTS
    parts = [part(i * n_part, n_part) for i in range(MIX_IN_PARTS)]
    for gen in parts:
        next(gen)
    cols = [zrest_ref.at[:, j * LRU_DIM:(j + 1) * LRU_DIM] for j in range(3)]
    _run_lockstep(parts + [_ssm_steps(reset_first, nb, tm, *cols, *ssm_consts, *ssm_outs, *ssm_scratch)])
    hbuf[0:nb, :] = hbuf[tm:tm + nb, :]
    zbuf[0:nb, :] = zbuf[tm:tm + nb, :]


MIX_IN_PARTS = 2


def _mix_in(x, first, p, vres, v_first, conv0, h0, re0, im0, reset_first, nb, seq):
    assert LRU_DIM == S5_DIM
    m = nb * seq
    tm = _tile(m, max(nb, 8) * MIX_IN_PARTS, 512)
    hist = (CONV_W - 1) * nb
    has_vres = vres is not None
    seq_major_x = x.ndim == 3
    row = lambda w: pl.BlockSpec((tm, w), lambda i: (i, 0))
    pair_row = pl.BlockSpec((N_PAIRS, tm, HEAD_PAIR), lambda i: (0, i, 0))
    ins = [x, first, p["norm_mix"], p["w_in"], p["mu_rkv"], p["mu_wag"], p["w0"], p["w1"], p["w2"],
           p["a0"], p["a1"], p["a2"], p["g1"], p["g2"], p["k_k"], p["k_a"], p["ones_head"]]
    x_spec = pl.BlockSpec((nb, tm // nb, D_MODEL), lambda i: (0, i, 0)) if seq_major_x else row(D_MODEL)
    specs = [x_spec] + [_spec(a) for a in ins[1:]]
    if has_vres:
        extra = [vres["mu_v"], vres["v0"], vres["v1"], vres["v2"]]
        ins += extra + [v_first]
        specs += [_spec(a) for a in extra] + [pair_row]
    ssm_consts = [conv0, h0, p["conv_w"], p["conv_b"], p["lru_wa"], p["lru_ba"], p["lru_wi"], p["lru_bi"],
                  p["lru_lambda"], re0, im0, p["s5_abar_re"], p["s5_abar_im"], p["s5_bbar_re"], p["s5_bbar_im"],
                  p["s5_c_re"], p["s5_c_im"], p["s5_d"], p["s5_w_glu"], p["s5_b_glu"]]
    assert len(ssm_consts) == N_SSM_CONSTS
    ins += ssm_consts
    specs += [_spec(a) for a in ssm_consts]
    wide = jax.ShapeDtypeStruct((N_PAIRS, m, HEAD_PAIR), F32)
    st = jax.ShapeDtypeStruct((nb, S5_WIDTH), F32)
    outs = ([wide] * 7 + [jax.ShapeDtypeStruct((nb, D_MODEL), F32)]
            + [jax.ShapeDtypeStruct((m, LRU_DIM), F32), jax.ShapeDtypeStruct((hist, LRU_DIM), F32),
               jax.ShapeDtypeStruct((nb, LRU_DIM), F32), jax.ShapeDtypeStruct((m, S5_DIM), F32), st, st])
    out_specs = ([pair_row] * 7 + [_const_spec((nb, D_MODEL))]
                 + [row(LRU_DIM), _const_spec((hist, LRU_DIM)), _const_spec((nb, LRU_DIM)),
                    row(S5_DIM), _const_spec((nb, S5_WIDTH)), _const_spec((nb, S5_WIDTH))])
    scratch = [pltpu.VMEM((tm + nb, D_MODEL), F32), pltpu.VMEM((tm + nb, R3), F32),
               pltpu.VMEM((tm, D_IN - R3), F32),
               pltpu.VMEM((tm + hist, LRU_DIM), F32), pltpu.VMEM((tm, LRU_DIM), F32),
               pltpu.VMEM((tm, LRU_DIM), F32), pltpu.VMEM((nb, LRU_DIM), F32),
               pltpu.VMEM((tm, S5_WIDTH), F32), pltpu.VMEM((tm, S5_WIDTH), F32),
               pltpu.VMEM((nb, S5_WIDTH), F32), pltpu.VMEM((nb, S5_WIDTH), F32)]
    assert len(scratch) == 3 + N_SSM_SCRATCH
    if seq_major_x:
        assert (tm // nb) % 8 == 0
        outs.insert(0, jax.ShapeDtypeStruct((m, D_MODEL), F32))
        out_specs.insert(0, row(D_MODEL))
        scratch.append(pltpu.VMEM((LANE_BLOCKS, tm, LANES), F32))
    return pl.pallas_call(
        functools.partial(_mix_in_kernel, has_vres, seq_major_x, reset_first, nb, tm),
        grid=(m // tm,),
        in_specs=specs,
        out_specs=out_specs,
        out_shape=outs,
        scratch_shapes=scratch,
        compiler_params=_params("arbitrary"),
        name="mix_in",
    )(*_arrays(ins))


def _wkv_kernel(c, n_seq, group, stride, r_ref, lw_ref, k_ref, v_ref, na_ref, bb_ref, gate_ref, s0_ref,
                rk_ref, lnw_ref, lnb_ref, tri_ref, mean_ref,
                y_ref, sout_ref, s_scr):
    li = pl.program_id(1)

    @pl.when(li == 0)
    def _():
        zero = jnp.zeros((RWKV_HEAD, RWKV_HEAD), F32)
        for j in range(n_seq):
            for p in range(N_PAIRS):
                s_scr[j, p] = jnp.concatenate(
                    [jnp.concatenate([s0_ref[j, 2 * p], zero], axis=1),
                     jnp.concatenate([zero, s0_ref[j, 2 * p + 1]], axis=1)], axis=0)

    lane = lax.broadcasted_iota(jnp.int32, (1, HEAD_PAIR), 1)
    head_masks = (lane < RWKV_HEAD, lane >= RWKV_HEAD)
    trow = lax.broadcasted_iota(jnp.int32, (c, 2 * c), 0)
    tcol = lax.broadcasted_iota(jnp.int32, (c, 2 * c), 1)
    left_half = tcol < c
    scol = jnp.where(left_half, tcol, tcol - c)
    strict = scol < trow
    strict_left = strict & left_half
    strict_right = strict & (tcol >= c)
    incl = scol <= trow
    eye2 = (scol == trow).astype(F32)
    drow = lax.broadcasted_iota(jnp.int32, (2 * c, 2 * c), 0)
    dcol = lax.broadcasted_iota(jnp.int32, (2 * c, 2 * c), 1)
    diag_blocks = jnp.bitwise_xor(drow - c, dcol - c) >= 0
    brow = lax.broadcasted_iota(jnp.int32, (HEAD_PAIR, HEAD_PAIR), 0)
    bcol = lax.broadcasted_iota(jnp.int32, (HEAD_PAIR, HEAD_PAIR), 1)
    same_head = jnp.bitwise_xor(brow - RWKV_HEAD, bcol - RWKV_HEAD) >= 0
    n_double = max(int(math.log2(c)) - 1, 0)
    tri = tri_ref[...]
    zeros_c = jnp.zeros((c, HEAD_PAIR), F32)

    def pair_chunk(lw, r, k, v, na, bb, gate, sp, cols):
        lw_hi, lw_lo = _split2(lw)
        cl = jnp.dot(tri, lw_hi, preferred_element_type=F32) + jnp.dot(tri, lw_lo, preferred_element_type=F32)
        yield
        e_in = jnp.exp(cl)
        e_neg = jnp.exp(-cl)
        g_last = jnp.exp(jnp.sum(lw, axis=0, keepdims=True))
        at = na * jnp.exp(cl - lw)
        rt = r * e_in
        bt = bb * e_neg
        kt = k * e_neg
        ar = jnp.concatenate([at, rt], axis=0)
        bk = jnp.concatenate([bt, kt], axis=0)
        a_s = _bdot_nt(ar, sp)
        g0 = _bdot_nt(jnp.where(head_masks[0], ar, 0.0), bk)
        g1 = _bdot_nt(jnp.where(head_masks[1], ar, 0.0), jnp.concatenate([kt, bt], axis=0))
        yield
        ga0, ga1 = g0[0:c, :], g1[0:c, :]
        low = jnp.where(strict, jnp.where(left_half, ga0, ga1), 0.0)
        ak0 = _bdot(jnp.where(strict_right, ga0, 0.0), jnp.concatenate([zeros_c, v], axis=0))
        ak1 = _bdot(jnp.where(strict_left, ga1, 0.0), jnp.concatenate([v, zeros_c], axis=0))
        rhs = jnp.concatenate([jnp.where(head_masks[0], a_s[0:c, :] + ak0, 0.0),
                               jnp.where(head_masks[1], a_s[0:c, :] + ak1, 0.0)], axis=0)
        blockdiag = lambda x: jnp.where(diag_blocks, jnp.concatenate([x, x], axis=0), 0.0)
        t_inv = eye2 + low
        pw = low
        if n_double > 0:
            pw = _bdot(pw, blockdiag(pw))
            yield
        for level in range(n_double):
            if level == n_double - 1:
                t_inv = t_inv + _bdot(t_inv, blockdiag(pw))
            else:
                both = _bdot(jnp.concatenate([t_inv, pw], axis=0), blockdiag(pw))
                t_inv = t_inv + both[0:c, :]
                pw = both[c:2 * c, :]
            yield
        u = _bdot(t_inv, rhs)
        yield
        uv = jnp.concatenate([u, v], axis=0)
        y = a_s[c:2 * c, :] + jnp.where(
            head_masks[0], _bdot(jnp.where(incl, g0[c:2 * c, :], 0.0), uv),
            _bdot(jnp.where(incl, g1[c:2 * c, :], 0.0), jnp.concatenate([v, u], axis=0)))
        upd = _bdot_tn(uv, bk * g_last)
        s_new = sp * g_last + jnp.where(same_head, upd, 0.0)
        yield
        stats = _bdot(jnp.concatenate([y, r * k * rk_ref[:, cols]], axis=0), mean_ref[...])
        mu = stats[0:c, :]
        bonus = stats[c:2 * c, :] * float(RWKV_HEAD) * v
        yield
        d = y - mu
        var = _bdot(d * d, mean_ref[...])
        yield
        yn = d * lax.rsqrt(var + GN_EPS) * lnw_ref[:, cols] + lnb_ref[:, cols]
        return (yn + bonus) * gate, s_new

    first_seq = pl.program_id(0) * n_seq

    def seq_group(gi, carry):
        units = []
        for jj in range(group):
            j = gi * group + jj
            rows = pl.ds(first_seq + j, c, stride=stride)
            for p in range(N_PAIRS):
                cols = slice(p * HEAD_PAIR, (p + 1) * HEAD_PAIR)
                vals = [ref[p, rows, :] for ref in (lw_ref, r_ref, k_ref, v_ref, na_ref, bb_ref, gate_ref)]
                units.append((j, p, rows, pair_chunk(*vals, s_scr[j, p], cols)))
        results = _run_lockstep([gen for (_, _, _, gen) in units])
        for (j, p, rows, _), (y, s_new) in zip(units, results):
            y_ref[p, rows, :] = y
            s_scr[j, p] = s_new
        return carry

    lax.fori_loop(0, n_seq // group, seq_group, 0)

    @pl.when(li == pl.num_programs(1) - 1)
    def _():
        for j in range(n_seq):
            for p in range(N_PAIRS):
                sp = s_scr[j, p]
                sout_ref[j, 2 * p] = sp[0:RWKV_HEAD, 0:RWKV_HEAD]
                sout_ref[j, 2 * p + 1] = sp[RWKV_HEAD:HEAD_PAIR, RWKV_HEAD:HEAD_PAIR]


WKV_SEQ_PER_STEP = 8
WKV_SEQ_GROUP = 8


def _wkv(arrs, s_all, layer, p, nb, seq):
    c = min(WKV_CHUNK, seq)
    n_seq = nb if seq > c else min(nb, WKV_SEQ_PER_STEP)
    assert seq % c == 0 and c % 8 == 0 and nb % n_seq == 0 and n_seq % WKV_SEQ_GROUP == 0
    blk = pl.BlockSpec((N_PAIRS, c * nb, HEAD_PAIR), lambda g, l: (0, l, 0))
    tri = jnp.tril(jnp.ones((c, c), F32)).astype(BF16)
    st_spec = pl.BlockSpec((None, n_seq, RWKV_HEADS, RWKV_HEAD, RWKV_HEAD), lambda g, l: (layer, g, 0, 0, 0))
    consts = [p["r_k"], p["ln_x_w"], p["ln_x_b"], tri, p["mean_pair"]]
    return pl.pallas_call(
        functools.partial(_wkv_kernel, c, n_seq, WKV_SEQ_GROUP, nb),
        grid=(nb // n_seq, seq // c),
        in_specs=[blk] * 7 + [st_spec] + [_spec(a) for a in consts],
        out_specs=[blk, st_spec],
        out_shape=[jax.ShapeDtypeStruct((N_PAIRS, seq * nb, HEAD_PAIR), F32), jax.ShapeDtypeStruct(s_all.shape, F32)],
        input_output_aliases={7: 1},
        scratch_shapes=[pltpu.VMEM((n_seq, N_PAIRS, HEAD_PAIR, HEAD_PAIR), F32)],
        compiler_params=_params("arbitrary", "arbitrary"),
        name="wkv",
    )(*arrs, s_all, *_arrays(consts))


def _ssm_init(nb, consts, scratch):
    conv0_ref, h0_ref, re0_ref, im0_ref = consts[0], consts[1], consts[9], consts[10]
    xbuf, _, _, hcar, _, _, car_re, car_im = scratch
    xbuf[0:(CONV_W - 1) * nb, :] = conv0_ref[...]
    hcar[...] = h0_ref[...]
    car_re[...] = re0_ref[...]
    car_im[...] = im0_ref[...]


def _ssm_steps(reset_first, nb, tm, zx_ref, zg_ref, u_ref,
               conv0_ref, h0_ref, cw_ref, cb_ref, wa_ref, ba_ref, wi_ref, bi_ref, lam_ref,
               re0_ref, im0_ref, are_ref, aim_ref, bre_ref, bim_ref, cre_ref, cim_ref, d_ref, wglu_ref, bglu_ref,
               y_ref, conv_out_ref, h_out_ref, yc_ref, re_out_ref, im_out_ref,
               xbuf, abuf, bbuf, hcar, xre, xim, car_re, car_im):
    i = pl.program_id(0)
    hist = (CONV_W - 1) * nb

    u = u_ref[...]
    xre[...] = _bdot(u, bre_ref[...])
    xim[...] = _bdot(u, bim_ref[...])

    xbuf[hist:hist + tm, :] = zx_ref[...]
    xc = cb_ref[...]
    for j in range(CONV_W):
        xc = xc + xbuf[j * nb:j * nb + tm, :] * cw_ref[j:j + 1, :]
    conv_out_ref[...] = xbuf[tm:tm + hist, :]
    xbuf[0:hist, :] = xbuf[tm:tm + hist, :]

    gate_a = _sigmoid(_bdot(xc, wa_ref[...]) + ba_ref[...])
    gate_i = _sigmoid(_bdot(xc, wi_ref[...]) + bi_ref[...])
    log_a = -LRU_C * gate_a * _softplus(-lam_ref[...])
    a_sq = jnp.exp(2.0 * log_a)
    mult = jnp.sqrt(-jnp.tanh(log_a) * (a_sq + 1.0))
    if reset_first:
        row = lax.broadcasted_iota(jnp.int32, (tm, 1), 0)
        mult = jnp.where(row < jnp.where(i == 0, nb, 0), 1.0, mult)
    abuf[...] = jnp.exp(log_a)
    bbuf[...] = mult * gate_i * xc
    yield

    a_re = jnp.broadcast_to(are_ref[...], (nb, S5_WIDTH))
    a_im = jnp.broadcast_to(aim_ref[...], (nb, S5_WIDTH))
    h, s_re, s_im = hcar[...], car_re[...], car_im[...]
    for t in range(tm // nb):
        rows = slice(t * nb, (t + 1) * nb)
        h = abuf[rows, :] * h + bbuf[rows, :]
        bbuf[rows, :] = h
        s_re, s_im = (a_re * s_re - a_im * s_im + xre[rows, :], a_re * s_im + a_im * s_re + xim[rows, :])
        xre[rows, :] = s_re
        xim[rows, :] = s_im
    hcar[...] = h
    h_out_ref[...] = h
    car_re[...] = s_re
    car_im[...] = s_im
    re_out_ref[...] = s_re
    im_out_ref[...] = s_im

    y_ref[...] = bbuf[...] * _gelu(zg_ref[...])
    y = _bdot(xre[...], cre_ref[...]) - _bdot(xim[...], cim_ref[...]) + d_ref[...] * u
    yield
    o = _bdot(_gelu(y), wglu_ref[...]) + bglu_ref[...]
    yield
    yc_ref[...] = o[:, 0:S5_DIM] * _sigmoid(o[:, S5_DIM:2 * S5_DIM])


def _mix_out_kernel(n_part, x_ref, ya_ref, yb_ref, yc_ref, wa_ref, wb_ref, wc_ref, gq_ref, wq_ref, x1_ref, q_ref):
    def part(lo, n):
        rows = slice(lo, lo + n)
        ya = jnp.concatenate([ya_ref[p, rows, :] for p in range(N_PAIRS)], axis=-1)
        x1 = (x_ref[rows, :] + _bdot(ya, wa_ref[...]) + _bdot(yb_ref[rows, :], wb_ref[...])
              + _bdot(yc_ref[rows, :], wc_ref[...]))
        yield
        q = _bdot(_rms(x1, gq_ref[...]), wq_ref[...])
        yield
        for cb in range(LANE_BLOCKS):
            cols = slice(cb * LANES, (cb + 1) * LANES)
            x1_ref[cb, rows, :] = x1[:, cols]
            q_ref[cb, rows, :] = q[:, cols]

    n = x_ref.shape[0] // n_part
    _run_lockstep([part(i * n, n) for i in range(n_part)])


MIX_OUT_PARTS = 2


def _mix_out(x, ya, yb, yc, p):
    m = x.shape[0]
    tm = _tile(m, 8 * MIX_OUT_PARTS, 512)
    row = lambda w: pl.BlockSpec((tm, w), lambda i: (i, 0))
    split = pl.BlockSpec((LANE_BLOCKS, tm, LANES), lambda i: (0, i, 0))
    consts = [p["w_out_a"], p["w_out_b"], p["w_out_c"], p["norm_mem_q"], p["mem_wq"]]
    out = jax.ShapeDtypeStruct((LANE_BLOCKS, m, LANES), F32)
    return pl.pallas_call(
        functools.partial(_mix_out_kernel, MIX_OUT_PARTS),
        grid=(m // tm,),
        in_specs=[row(D_MODEL), pl.BlockSpec((N_PAIRS, tm, HEAD_PAIR), lambda i: (0, i, 0)), row(LRU_DIM),
                  row(S5_DIM)] + [_spec(a) for a in consts],
        out_specs=[split, split],
        out_shape=[out, out],
        compiler_params=_params("parallel"),
        name="mix_out",
    )(x, ya, yb, yc, *_arrays(consts))


def _load_seq_rows(ref, rows):
    return jnp.concatenate([ref[cb, rows, :] for cb in range(LANE_BLOCKS)], axis=-1)


def _store_seq_rows(ref, rows, val):
    for cb in range(LANE_BLOCKS):
        ref[cb, rows, :] = val[:, cb * LANES:(cb + 1) * LANES]


def _softmax_rows(s):
    e = jnp.exp(s - jnp.max(s, axis=-1, keepdims=True))
    return e / jnp.sum(e, axis=-1, keepdims=True)


def _attn_kernel(tl, n_seq, stride, q_ref, x_ref, mk_ref, mv_ref, wo_ref, o_ref, seq_axis=0):
    scale = MEM_HEAD_DIM ** -0.5
    first_seq = 0 if seq_axis is None else pl.program_id(seq_axis) * n_seq
    head_cols = [slice(hh * MEM_HEAD_DIM, (hh + 1) * MEM_HEAD_DIM) for hh in range(MEM_HEADS)]

    def one_seq(j):
        rows = pl.ds(first_seq + j, tl, stride=stride)
        q = _load_seq_rows(q_ref, rows)
        scores = [_bdot_nt(q[:, cols], mk_ref[j, :, cols]) * scale for cols in head_cols]
        yield
        heads = [_bdot(_softmax_rows(s), mv_ref[j, :, cols]) for s, cols in zip(scores, head_cols)]
        yield
        out = _load_seq_rows(x_ref, rows) + _bdot(jnp.concatenate(heads, axis=-1), wo_ref[...])
        yield
        _store_seq_rows(o_ref, rows, out)

    for j0 in range(0, n_seq, ATTN_LOCKSTEP):
        _run_lockstep([one_seq(j) for j in range(j0, min(j0 + ATTN_LOCKSTEP, n_seq))])


def _attn_cache_kernel(tl, n_seq, stride, q_ref, x_ref, mk_ref, mv_ref, wo_ref, o_ref):
    scale = MEM_HEAD_DIM ** -0.5
    first_seq = pl.program_id(0) * n_seq
    qrow = lax.broadcasted_iota(jnp.int32, (MEM_HEADS * tl, N_MEM * MEM_HEADS), 0)
    kcol = lax.broadcasted_iota(jnp.int32, (MEM_HEADS * tl, N_MEM * MEM_HEADS), 1)
    q_head = jnp.zeros_like(qrow)
    for hh in range(1, MEM_HEADS):
        q_head = q_head + (qrow >= hh * tl).astype(jnp.int32)
    same_head = q_head == jnp.bitwise_and(kcol, MEM_HEADS - 1)

    def one_seq(j):
        rows = pl.ds(first_seq + j, tl, stride=stride)
        q = _load_seq_rows(q_ref, rows)
        qh = jnp.concatenate([q[:, hh * MEM_HEAD_DIM:(hh + 1) * MEM_HEAD_DIM] for hh in range(MEM_HEADS)], axis=0)
        k2 = mk_ref[j].reshape(N_MEM * MEM_HEADS, MEM_HEAD_DIM)
        s = jnp.where(same_head, _bdot_nt(qh, k2) * scale, MASKED_SCORE)
        yield
        v2 = mv_ref[j].reshape(N_MEM * MEM_HEADS, MEM_HEAD_DIM)
        oh = _bdot(_softmax_rows(s), v2)
        yield
        o = jnp.concatenate([oh[hh * tl:(hh + 1) * tl, :] for hh in range(MEM_HEADS)], axis=-1)
        out = _load_seq_rows(x_ref, rows) + _bdot(o, wo_ref[...])
        yield
        _store_seq_rows(o_ref, rows, out)

    _run_lockstep([one_seq(j) for j in range(n_seq)])


MASKED_SCORE = -1e30
ATTN_SEQ_PER_STEP = 4
ATTN_ROWS = 128
ATTN_LOCKSTEP = 4


def _attn(q, x1, mk, mv, kv_offset, wo, nb, seq):
    tl = _tile(seq, 8, ATTN_ROWS)
    n_seq = nb if seq > tl else min(nb, ATTN_SEQ_PER_STEP)
    assert nb % n_seq == 0 and kv_offset % n_seq == 0 and MEM_HEADS & (MEM_HEADS - 1) == 0
    blk = pl.BlockSpec((LANE_BLOCKS, tl * nb, LANES), lambda g, l: (0, l, 0))
    kv_index = lambda g, l: (kv_offset // n_seq + g,) + (0,) * (mk.ndim - 1)
    kv = pl.BlockSpec((n_seq,) + mk.shape[1:], kv_index)
    body = _attn_kernel if mk.ndim == 3 else _attn_cache_kernel
    return pl.pallas_call(
        functools.partial(body, tl, n_seq, nb),
        grid=(nb // n_seq, seq // tl),
        in_specs=[blk, blk, kv, kv, _spec(wo)],
        out_specs=blk,
        out_shape=jax.ShapeDtypeStruct((LANE_BLOCKS, seq * nb, LANES), F32),
        compiler_params=_params("arbitrary", "arbitrary"),
        name="mem_attn",
    )(q, x1, mk, mv, *_arrays([wo]))


def _mix_attn_kernel(tl, n_seq, x_ref, ya_ref, yb_ref, yc_ref, wa_ref, wb_ref, wc_ref, gq_ref, wq_ref,
                     mk_ref, mv_ref, wo_ref, o_ref, x1_scr, q_scr):
    _mix_out_kernel(MIX_OUT_PARTS, x_ref, ya_ref, yb_ref, yc_ref, wa_ref, wb_ref, wc_ref, gq_ref, wq_ref,
                    x1_scr, q_scr)
    _attn_kernel(tl, n_seq, n_seq, q_scr, x1_scr, mk_ref, mv_ref, wo_ref, o_ref, seq_axis=None)


ATTN_ROWS_FUSED = 128


def _mix_attn(x, ya, yb, yc, mk, mv, kv_offset, p, nb, seq):
    tl = _tile(seq, 8, ATTN_ROWS_FUSED)
    tm = tl * nb
    assert tm % (8 * MIX_OUT_PARTS) == 0 and kv_offset % nb == 0 and mk.ndim == 3
    row = lambda w: pl.BlockSpec((tm, w), lambda i: (i, 0))
    split = pl.BlockSpec((LANE_BLOCKS, tm, LANES), lambda i: (0, i, 0))
    kv = pl.BlockSpec((nb,) + mk.shape[1:], lambda i: (kv_offset // nb, 0, 0))
    consts = [p["w_out_a"], p["w_out_b"], p["w_out_c"], p["norm_mem_q"], p["mem_wq"]]
    return pl.pallas_call(
        functools.partial(_mix_attn_kernel, tl, nb),
        grid=(seq // tl,),
        in_specs=[row(D_MODEL), pl.BlockSpec((N_PAIRS, tm, HEAD_PAIR), lambda i: (0, i, 0)), row(LRU_DIM),
                  row(S5_DIM)] + [_spec(a) for a in consts] + [kv, kv, _spec(p["mem_wo"])],
        out_specs=split,
        out_shape=jax.ShapeDtypeStruct((LANE_BLOCKS, seq * nb, LANES), F32),
        scratch_shapes=[pltpu.VMEM((LANE_BLOCKS, tm, LANES), F32), pltpu.VMEM((LANE_BLOCKS, tm, LANES), F32)],
        compiler_params=_params("arbitrary"),
        name="mix_attn",
    )(x, ya, yb, yc, *_arrays(consts), mk, mv, *_arrays([p["mem_wo"]]))


def _ffn_kernel(d_ff, final_norm, n_seq, x_ref, g_ref, wup_ref, wdown_ref, *rest):
    x = jnp.concatenate([x_ref[cb] for cb in range(LANE_BLOCKS)], axis=-1)
    h = _rms(x, g_ref[...]).astype(BF16)
    gt = jnp.dot(h, wup_ref[:, 0:d_ff], preferred_element_type=F32)
    up = jnp.dot(h, wup_ref[:, d_ff:2 * d_ff], preferred_element_type=F32)
    act = gt * _sigmoid(gt) * up
    out = x + _bdot(act, wdown_ref[...])
    if final_norm:
        out = _rms(out, rest[0][...])
    if n_seq:
        o_ref, os = rest[-2:]
        for cb in range(LANE_BLOCKS):
            os[cb] = out[:, cb * LANES:(cb + 1) * LANES]
        for b in range(n_seq):
            o_ref[b] = _load_seq_rows(os, pl.ds(b, out.shape[0] // n_seq, stride=n_seq))
    else:
        rest[-1][...] = out


def _ffn(x, p, g_final=None, seq_major_out=None):
    m = x.shape[1]
    tm = _tile(m, 8, 512)
    d_ff = p["ffn_w_down"].shape[0]
    consts = [p["norm_ffn"], p["ffn_w_up"], p["ffn_w_down"]] + ([] if g_final is None else [g_final])
    if seq_major_out is None:
        n_seq, scratch = 0, []
        out_spec = pl.BlockSpec((tm, D_MODEL), lambda i: (i, 0))
        out_shape = jax.ShapeDtypeStruct((m, D_MODEL), F32)
    else:
        n_seq, seq = seq_major_out
        assert tm % n_seq == 0 and (tm // n_seq) % 8 == 0 and n_seq * seq == m
        scratch = [pltpu.VMEM((LANE_BLOCKS, tm, LANES), F32)]
        out_spec = pl.BlockSpec((n_seq, tm // n_seq, D_MODEL), lambda i: (0, i, 0))
        out_shape = jax.ShapeDtypeStruct((n_seq, seq, D_MODEL), F32)
    return pl.pallas_call(
        functools.partial(_ffn_kernel, d_ff, g_final is not None, n_seq),
        grid=(m // tm,),
        in_specs=[pl.BlockSpec((LANE_BLOCKS, tm, LANES), lambda i: (0, i, 0))] + [_spec(a) for a in consts],
        out_specs=out_spec,
        out_shape=out_shape,
        scratch_shapes=scratch,
        compiler_params=_params("parallel"),
        name="ffn",
    )(x, *_arrays(consts))


def _mem_project_kernel(x_ref, g_ref, wk_ref, wv_ref, k_ref, v_ref, kb_ref, vb_ref):
    m = _rms(x_ref[...], g_ref[...]).astype(BF16)
    k = jnp.dot(m, wk_ref[...], preferred_element_type=F32)
    v = jnp.dot(m, wv_ref[...], preferred_element_type=F32)
    k_ref[...] = k.reshape(k_ref.shape)
    v_ref[...] = v.reshape(v_ref.shape)
    kb_ref[...] = k.astype(BF16)
    vb_ref[...] = v.astype(BF16)


MEM_SEQ_PER_STEP = 2


def _mem_project(mem, g, wk, wv):
    n, n_mem, _ = mem.shape
    depth = wk.shape[0]
    tm = MEM_SEQ_PER_STEP * n_mem
    assert n % MEM_SEQ_PER_STEP == 0
    per_layer = lambda a: pl.BlockSpec((None,) + a.shape[1:], lambda l, i: (l,) + (0,) * (a.ndim - 1))
    cache = pl.BlockSpec((None, MEM_SEQ_PER_STEP, n_mem, MEM_HEADS, MEM_HEAD_DIM), lambda l, i: (l, i, 0, 0, 0))
    rows_b = pl.BlockSpec((None, tm, D_MODEL), lambda l, i: (l, i, 0))
    out = jax.ShapeDtypeStruct((depth, n, n_mem, MEM_HEADS, MEM_HEAD_DIM), F32)
    outb = jax.ShapeDtypeStruct((depth, n * n_mem, D_MODEL), BF16)
    return pl.pallas_call(
        _mem_project_kernel, grid=(depth, n // MEM_SEQ_PER_STEP),
        in_specs=[pl.BlockSpec((tm, D_MODEL), lambda l, i: (i, 0)), per_layer(g), per_layer(wk), per_layer(wv)],
        out_specs=[cache, cache, rows_b, rows_b], out_shape=[out, out, outb, outb],
        compiler_params=_params("parallel", "parallel"),
        name="mem_project",
    )(mem.reshape(n * n_mem, D_MODEL), g, wk, wv)


def _block_diag(blocks):
    n, r, c = blocks.shape[-3:]
    eye = jnp.eye(n, dtype=blocks.dtype)
    out = eye[:, None, :, None] * blocks[..., :, :, None, :]
    return out.reshape(blocks.shape[:-3] + (n * r, n * c))


def _stack_params(w):
    depth = w["norm_mix"].shape[0]
    row = lambda a: a.reshape(a.shape[0], 1, -1).astype(F32)
    bf = lambda a: a.astype(BF16)
    t = lambda a: jnp.swapaxes(a, -1, -2)
    p = {k: row(w[k]) for k in ("norm_mix", "mu_rkv", "w0", "a0", "k_k", "k_a", "r_k", "ln_x_w", "ln_x_b", "conv_b",
                                "lru_ba", "lru_bi", "lru_lambda", "s5_d", "s5_b_glu", "norm_mem_q", "norm_mem_kv",
                                "norm_ffn", "mu_v", "v0")}
    p.update({k: bf(w[k]) for k in ("w_in", "w1", "w2", "a1", "a2", "g1", "g2", "s5_w_glu", "mem_wq", "mem_wk",
                                    "mem_wv", "mem_wo", "ffn_w_up", "ffn_w_down", "v1", "v2")})
    p["mu_wag"], p["conv_w"] = w["mu_wag"], w["conv_w"]
    p["lru_wa"], p["lru_wi"] = bf(_block_diag(w["lru_wa"])), bf(_block_diag(w["lru_wi"]))
    w_out = bf(w["w_out"])
    p["w_out_a"] = w_out[:, 0:RWKV_DIM]
    p["w_out_b"] = w_out[:, RWKV_DIM:RWKV_DIM + LRU_DIM]
    p["w_out_c"] = w_out[:, RWKV_DIM + LRU_DIM:]
    lam_re, lam_im = w["s5_a_re"].astype(F32), w["s5_a_im"].astype(F32)
    dt = jnp.exp(w["s5_log_dt"].astype(F32))[..., None]
    mag = jnp.exp(lam_re * dt)
    abar_re, abar_im = mag * jnp.cos(lam_im * dt), mag * jnp.sin(lam_im * dt)
    den = lam_re * lam_re + lam_im * lam_im
    q_re = ((abar_re - 1.0) * lam_re + abar_im * lam_im) / den
    q_im = (abar_im * lam_re - (abar_re - 1.0) * lam_im) / den
    b_re, b_im = w["s5_b_re"].astype(F32), w["s5_b_im"].astype(F32)
    bbar_re = q_re[..., None] * b_re - q_im[..., None] * b_im
    bbar_im = q_re[..., None] * b_im + q_im[..., None] * b_re
    p["s5_abar_re"] = abar_re.reshape(depth, 1, S5_WIDTH)
    p["s5_abar_im"] = abar_im.reshape(depth, 1, S5_WIDTH)
    p["s5_bbar_re"] = bf(_block_diag(t(bbar_re)))
    p["s5_bbar_im"] = bf(_block_diag(t(bbar_im)))
    p["s5_c_re"] = bf(_block_diag(t(w["s5_c_re"])))
    p["s5_c_im"] = bf(_block_diag(t(w["s5_c_im"])))
    return p


def _layer_params(stacked, l, consts):
    vres_keys = ("mu_v", "v0", "v1", "v2")
    p = {k: _Layer(a, l) for k, a in stacked.items() if k not in vres_keys}
    p.update(consts)
    vres = {k: _Layer(stacked[k], l - 1) for k in vres_keys} if l > 0 else None
    return p, vres


def _time_major(a):
    return jnp.swapaxes(a, 0, 1).reshape((a.shape[0] * a.shape[1],) + a.shape[2:])


def _run_group(x, mem_k, mem_v, kv_stride, shift0, wkv0, conv0, lru0, re0, im0, reset_first, layers, norm_final,
               relayout_in_kernel):
    nb, seq, _ = x.shape
    xt = x if relayout_in_kernel else _time_major(x)
    outs = {k: [] for k in ("shift", "conv", "lru", "re", "im")}
    wkv_all = wkv0
    v_first = None
    for l, (p, vres) in enumerate(layers):
        last = l == len(layers) - 1
        mixed = _mix_in(xt, _Layer(shift0, l), p, vres, v_first, _time_major(conv0[l]), _Layer(lru0, l),
                        _Layer(re0.reshape(-1, nb, S5_WIDTH), l), _Layer(im0.reshape(-1, nb, S5_WIDTH), l),
                        reset_first, nb, seq)
        if xt.ndim == 3:
            xt, mixed = mixed[0], mixed[1:]
        r, lw, k, v, na, bb, gate, h_last, yb, conv_new, lru_new, yc, re_new, im_new = mixed
        if l == 0:
            v_first = v
        ya, wkv_all = _wkv((r, lw, k, v, na, bb, gate), wkv_all, l, p, nb, seq)
        if mem_k.ndim == 3 and seq > ATTN_ROWS_FUSED:
            x2 = _mix_attn(xt, ya, yb, yc, mem_k, mem_v, l * kv_stride, p, nb, seq)
        else:
            x1, q = _mix_out(xt, ya, yb, yc, p)
            x2 = _attn(q, x1, mem_k, mem_v, l * kv_stride, p["mem_wo"], nb, seq)
        xt = _ffn(x2, p, norm_final if last else None, (nb, seq) if last and relayout_in_kernel else None)
        outs["shift"].append(h_last)
        outs["conv"].append(jnp.swapaxes(conv_new.reshape(CONV_W - 1, nb, LRU_DIM), 0, 1))
        outs["lru"].append(lru_new)
        outs["re"].append(re_new.reshape(nb, S5_GROUPS, S5_STATE))
        outs["im"].append(im_new.reshape(nb, S5_GROUPS, S5_STATE))
    y = xt if relayout_in_kernel else jnp.swapaxes(xt.reshape(seq, nb, D_MODEL), 0, 1)
    st = {k: jnp.stack(v) for k, v in outs.items()}
    return (y, st["shift"], wkv_all, st["conv"], st["lru"], st["re"], st["im"])


def kernel(x_prompt, x_sample, mem_prompt, state_shift, state_wkv, state_conv, state_lru, state_s5_re, state_s5_im, cache_mem_k, cache_mem_v, norm_mix, w_in, w_out, mu_rkv, mu_wag, mu_v, w0, w1, w2, a0, a1, a2, v0, v1, v2, g1, g2, k_k, k_a, r_k, ln_x_w, ln_x_b, conv_w, conv_b, lru_wa, lru_ba, lru_wi, lru_bi, lru_lambda, s5_a_re, s5_a_im, s5_log_dt, s5_b_re, s5_b_im, s5_c_re, s5_c_im, s5_d, s5_w_glu, s5_b_glu, norm_mem_q, norm_mem_kv, mem_wq, mem_wk, mem_wv, mem_wo, norm_ffn, ffn_w_up, ffn_w_down, norm_final):
    w = dict(norm_mix=norm_mix, w_in=w_in, w_out=w_out, mu_rkv=mu_rkv, mu_wag=mu_wag, mu_v=mu_v, w0=w0, w1=w1,
             w2=w2, a0=a0, a1=a1, a2=a2, v0=v0, v1=v1, v2=v2, g1=g1, g2=g2, k_k=k_k, k_a=k_a, r_k=r_k,
             ln_x_w=ln_x_w, ln_x_b=ln_x_b, conv_w=conv_w, conv_b=conv_b, lru_wa=lru_wa, lru_ba=lru_ba,
             lru_wi=lru_wi, lru_bi=lru_bi, lru_lambda=lru_lambda, s5_a_re=s5_a_re, s5_a_im=s5_a_im,
             s5_log_dt=s5_log_dt, s5_b_re=s5_b_re, s5_b_im=s5_b_im, s5_c_re=s5_c_re, s5_c_im=s5_c_im,
             s5_d=s5_d, s5_w_glu=s5_w_glu, s5_b_glu=s5_b_glu, norm_mem_q=norm_mem_q, norm_mem_kv=norm_mem_kv,
             mem_wq=mem_wq, mem_wk=mem_wk, mem_wv=mem_wv, mem_wo=mem_wo, norm_ffn=norm_ffn, ffn_w_up=ffn_w_up,
             ffn_w_down=ffn_w_down)
    depth = norm_mix.shape[0]
    ones64 = jnp.ones((RWKV_HEADS, RWKV_HEAD, RWKV_HEAD), F32)
    consts = {"ones_head": _block_diag(ones64).astype(BF16),
              "mean_pair": (_block_diag(ones64[:2]) / RWKV_HEAD).astype(BF16)}
    stacked = _stack_params(w)
    layers = [_layer_params(stacked, l, consts) for l in range(depth)]
    g_final = norm_final.reshape(1, D_MODEL)

    bp, n_mem, _ = mem_prompt.shape
    mem_k_p, mem_v_p, mem_k_b, mem_v_b = _mem_project(
        mem_prompt, stacked["norm_mem_kv"], stacked["mem_wk"], stacked["mem_wv"])
    mem_k_b = mem_k_b.reshape(depth * bp, n_mem, D_MODEL)
    mem_v_b = mem_v_b.reshape(depth * bp, n_mem, D_MODEL)

    zeros = lambda *s: jnp.zeros((depth, bp) + s, F32)
    prompt = _run_group(
        x_prompt, mem_k_b, mem_v_b, bp,
        zeros(D_MODEL), zeros(RWKV_HEADS, RWKV_HEAD, RWKV_HEAD), zeros(CONV_W - 1, LRU_DIM), zeros(LRU_DIM),
        zeros(S5_GROUPS, S5_STATE), zeros(S5_GROUPS, S5_STATE), True, layers, g_final, True)
    bs = x_sample.shape[0]
    cache_shape = (depth * bs, n_mem, MEM_HEADS, MEM_HEAD_DIM)
    sample = _run_group(
        x_sample, cache_mem_k.reshape(cache_shape), cache_mem_v.reshape(cache_shape),
        bs, state_shift, state_wkv, state_conv, state_lru, state_s5_re, state_s5_im, False, layers, g_final, False)
    (y_p, sh_p, wkv_p, conv_p, lru_p, re_p, im_p) = prompt
    (y_s, sh_s, wkv_s, conv_s, lru_s, re_s, im_s) = sample
    return (y_p, y_s, sh_p, sh_s, wkv_p, wkv_s, conv_p, conv_s, lru_p, lru_s, re_p, re_s, im_p, im_s,
            mem_k_p, mem_v_p)
```

```python
import functools
import math

import jax
import jax.numpy as jnp
from jax import lax
from jax.experimental import pallas as pl
from jax.experimental.pallas import tpu as pltpu

F32 = jnp.float32
BF16 = jnp.bfloat16

D_MODEL = 1024
RWKV_DIM = 512
RWKV_HEAD = 64
RWKV_HEADS = 8
LANES = 128
LANE_BLOCKS = D_MODEL // LANES
HEAD_PAIR = 2 * RWKV_HEAD
N_PAIRS = RWKV_DIM // HEAD_PAIR
LRU_DIM = 256
LRU_BLOCKS = 4
CONV_W = 4
LRU_C = 8.0
S5_DIM = 256
S5_GROUP = 16
S5_GROUPS = 16
S5_STATE = 64
S5_WIDTH = S5_GROUPS * S5_STATE
R3 = 3 * RWKV_DIM
D_IN = R3 + 2 * LRU_DIM + S5_DIM
N_MEM = 256
MEM_HEADS = 4
MEM_HEAD_DIM = 256
RMS_EPS = 1e-6
GN_EPS = 64e-5
WKV_CHUNK = 64

V7X_VMEM_LIMIT = 56 * 1024 * 1024


def _params(*sem):
    return pltpu.CompilerParams(dimension_semantics=sem, vmem_limit_bytes=V7X_VMEM_LIMIT)


def _tile(m, mult, target):
    best = mult
    t = mult
    while t <= min(m, target):
        if m % t == 0:
            best = t
        t += mult
    assert m % best == 0, (m, mult, target)
    return best


def _const_spec(shape):
    nd = len(shape)
    return pl.BlockSpec(shape, lambda *_: (0,) * nd)


class _Layer:
    def __init__(self, arr, layer):
        self.arr, self.layer, self.shape = arr, layer, arr.shape[1:]


def _spec(a):
    if isinstance(a, _Layer):
        nd = len(a.shape)
        return pl.BlockSpec((None,) + a.shape, lambda *_: (a.layer,) + (0,) * nd)
    return _const_spec(a.shape)


def _arrays(operands):
    return [a.arr if isinstance(a, _Layer) else a for a in operands]


def _bdot(a, b):
    return jnp.dot(a.astype(BF16), b.astype(BF16), preferred_element_type=F32)


def _bdot_nt(a, b):
    return lax.dot_general(a.astype(BF16), b.astype(BF16), (((1,), (1,)), ((), ())),
                           preferred_element_type=F32)


def _bdot_tn(a, b):
    return lax.dot_general(a.astype(BF16), b.astype(BF16), (((0,), (0,)), ((), ())),
                           preferred_element_type=F32)


def _split2(x):
    hi = x.astype(BF16)
    lo = (x - hi.astype(F32)).astype(BF16)
    return hi, lo


def _run_lockstep(gens):
    results = {}
    while len(results) < len(gens):
        for i, gen in enumerate(gens):
            if i not in results:
                try:
                    next(gen)
                except StopIteration as stop:
                    results[i] = stop.value
    return [results[i] for i in range(len(gens))]


def _rms(x, g):
    return x * lax.rsqrt(jnp.mean(x * x, axis=-1, keepdims=True) + RMS_EPS) * g


def _softplus(x):
    return jnp.maximum(x, 0.0) + jnp.log1p(jnp.exp(-jnp.abs(x)))


def _sigmoid(x):
    return 1.0 / (1.0 + jnp.exp(-x))


def _gelu(x):
    c = math.sqrt(2.0 / math.pi)
    return x * (0.5 * (1.0 + jnp.tanh(c * (x + 0.044715 * (x * x * x)))))


N_SSM_CONSTS, N_SSM_OUTS, N_SSM_SCRATCH = 20, 6, 8


def _mix_in_kernel(has_vres, seq_major_x, reset_first, nb, tm, *refs):
    refs = list(refs)
    (x_ref, first_ref, g_ref, win_ref, murkv_ref, muwag_ref, w0_ref, w1_ref, w2_ref,
     a0_ref, a1_ref, a2_ref, g1_ref, g2_ref, kk_ref, ka_ref, ones_ref) = refs[:17]
    del refs[:17]
    if has_vres:
        muv_ref, v0_ref, v1_ref, v2_ref, vfirst_ref = refs[:5]
        del refs[:5]
    ssm_consts = refs[:N_SSM_CONSTS]
    del refs[:N_SSM_CONSTS]
    if seq_major_x:
        xt_ref = refs.pop(0)
    r_ref, lw_ref, k_ref, v_ref, na_ref, bb_ref, gate_ref, hlast_ref = refs[:8]
    del refs[:8]
    ssm_outs = refs[:N_SSM_OUTS]
    del refs[:N_SSM_OUTS]
    hbuf, zbuf, zrest_ref = refs[:3]
    ssm_scratch = refs[3:3 + N_SSM_SCRATCH]

    @pl.when(pl.program_id(0) == 0)
    def _():
        first = first_ref[...]
        hbuf[0:nb, :] = first
        zbuf[0:nb, :] = _bdot(first, win_ref[:, 0:R3])
        _ssm_init(nb, ssm_consts, ssm_scratch)

    if seq_major_x:
        xs = refs[-1]
        for b in range(nb):
            _store_seq_rows(xs, pl.ds(b, tm // nb, stride=nb), x_ref[b])
        x = jnp.concatenate([xs[cb] for cb in range(LANE_BLOCKS)], axis=-1)
        xt_ref[...] = x
    else:
        x = x_ref[...]
    h_all = _rms(x, g_ref[...])
    hbuf[nb:nb + tm, :] = h_all
    hlast_ref[...] = h_all[tm - nb:tm, :]

    def part(lo, n):
        h = hbuf[nb + lo:nb + lo + n, :]
        dh = hbuf[lo:lo + n, :] - h
        z = _bdot(h, win_ref[...])
        p_w = _bdot(h + dh * muwag_ref[0:1, :], w1_ref[...])
        p_a = _bdot(h + dh * muwag_ref[1:2, :], a1_ref[...])
        p_g = _bdot(h + dh * muwag_ref[2:3, :], g1_ref[...])
        if has_vres:
            p_v = _bdot(h + dh * muv_ref[...], v1_ref[...])
        zrest_ref[lo:lo + n, :] = z[:, R3:D_IN]
        z_rkv = z[:, 0:R3]
        zbuf[nb + lo:nb + lo + n, :] = z_rkv
        yield
        zmix = z_rkv + (zbuf[lo:lo + n, :] - z_rkv) * murkv_ref[...]
        r = zmix[:, 0:RWKV_DIM]
        k = zmix[:, RWKV_DIM:2 * RWKV_DIM]
        v = zmix[:, 2 * RWKV_DIM:R3]
        kk = k * kk_ref[...]
        w_pre = w0_ref[...] + _bdot(jnp.tanh(p_w), w2_ref[...])
        a = _sigmoid(a0_ref[...] + _bdot(p_a, a2_ref[...]))
        gate = _bdot(_sigmoid(p_g), g2_ref[...])
        ss = _bdot(kk * kk, ones_ref[...])
        if has_vres:
            mix = _sigmoid(v0_ref[...] + _bdot(p_v, v2_ref[...]))
            v_first = jnp.concatenate([vfirst_ref[p, lo:lo + n, :] for p in range(N_PAIRS)], axis=-1)
            v = v + (v_first - v) * mix
        yield
        w_log = -_softplus(-w_pre) - 0.5
        kk = kk / jnp.maximum(jnp.sqrt(ss), 1e-12)
        outs = ((r_ref, r), (lw_ref, -jnp.exp(w_log)), (k_ref, k * (1.0 + (a - 1.0) * ka_ref[...])), (v_ref, v),
                (na_ref, -kk), (bb_ref, kk * a), (gate_ref, gate))
        for ref, val in outs:
            for p in range(N_PAIRS):
                ref[p, lo:lo + n, :] = val[:, p * HEAD_PAIR:(p + 1) * HEAD_PAIR]

    n_part = tm // MIX_IN_PARTS
    parts = [part(i * n_part, n_part) for i in range(MIX_IN_PARTS)]
    for gen in parts:
        next(gen)
    cols = [zrest_ref.at[:, j * LRU_DIM:(j + 1) * LRU_DIM] for j in range(3)]
    _run_lockstep(parts + [_ssm_steps(reset_first, nb, tm, *cols, *ssm_consts, *ssm_outs, *ssm_scratch)])
    hbuf[0:nb, :] = hbuf[tm:tm + nb, :]
    zbuf[0:nb, :] = zbuf[tm:tm + nb, :]


MIX_IN_PARTS = 2


def _mix_in(x, first, p, vres, v_first, conv0, h0, re0, im0, reset_first, nb, seq):
    assert LRU_DIM == S5_DIM
    m = nb * seq
    tm = _tile(m, max(nb, 8) * MIX_IN_PARTS, 512)
    hist = (CONV_W - 1) * nb
    has_vres = vres is not None
    seq_major_x = x.ndim == 3
    row = lambda w: pl.BlockSpec((tm, w), lambda i: (i, 0))
    pair_row = pl.BlockSpec((N_PAIRS, tm, HEAD_PAIR), lambda i: (0, i, 0))
    ins = [x, first, p["norm_mix"], p["w_in"], p["mu_rkv"], p["mu_wag"], p["w0"], p["w1"], p["w2"],
           p["a0"], p["a1"], p["a2"], p["g1"], p["g2"], p["k_k"], p["k_a"], p["ones_head"]]
    x_spec = pl.BlockSpec((nb, tm // nb, D_MODEL), lambda i: (0, i, 0)) if seq_major_x else row(D_MODEL)
    specs = [x_spec] + [_spec(a) for a in ins[1:]]
    if has_vres:
        extra = [vres["mu_v"], vres["v0"], vres["v1"], vres["v2"]]
        ins += extra + [v_first]
        specs += [_spec(a) for a in extra] + [pair_row]
    ssm_consts = [conv0, h0, p["conv_w"], p["conv_b"], p["lru_wa"], p["lru_ba"], p["lru_wi"], p["lru_bi"],
                  p["lru_lambda"], re0, im0, p["s5_abar_re"], p["s5_abar_im"], p["s5_bbar_re"], p["s5_bbar_im"],
                  p["s5_c_re"], p["s5_c_im"], p["s5_d"], p["s5_w_glu"], p["s5_b_glu"]]
    assert len(ssm_consts) == N_SSM_CONSTS
    ins += ssm_consts
    specs += [_spec(a) for a in ssm_consts]
    wide = jax.ShapeDtypeStruct((N_PAIRS, m, HEAD_PAIR), F32)
    st = jax.ShapeDtypeStruct((nb, S5_WIDTH), F32)
    outs = ([wide] * 7 + [jax.ShapeDtypeStruct((nb, D_MODEL), F32)]
            + [jax.ShapeDtypeStruct((m, LRU_DIM), F32), jax.ShapeDtypeStruct((hist, LRU_DIM), F32),
               jax.ShapeDtypeStruct((nb, LRU_DIM), F32), jax.ShapeDtypeStruct((m, S5_DIM), F32), st, st])
    out_specs = ([pair_row] * 7 + [_const_spec((nb, D_MODEL))]
                 + [row(LRU_DIM), _const_spec((hist, LRU_DIM)), _const_spec((nb, LRU_DIM)),
                    row(S5_DIM), _const_spec((nb, S5_WIDTH)), _const_spec((nb, S5_WIDTH))])
    scratch = [pltpu.VMEM((tm + nb, D_MODEL), F32), pltpu.VMEM((tm + nb, R3), F32),
               pltpu.VMEM((tm, D_IN - R3), F32),
               pltpu.VMEM((tm + hist, LRU_DIM), F32), pltpu.VMEM((tm, LRU_DIM), F32),
               pltpu.VMEM((tm, LRU_DIM), F32), pltpu.VMEM((nb, LRU_DIM), F32),
               pltpu.VMEM((tm, S5_WIDTH), F32), pltpu.VMEM((tm, S5_WIDTH), F32),
               pltpu.VMEM((nb, S5_WIDTH), F32), pltpu.VMEM((nb, S5_WIDTH), F32)]
    assert len(scratch) == 3 + N_SSM_SCRATCH
    if seq_major_x:
        assert (tm // nb) % 8 == 0
        outs.insert(0, jax.ShapeDtypeStruct((m, D_MODEL), F32))
        out_specs.insert(0, row(D_MODEL))
        scratch.append(pltpu.VMEM((LANE_BLOCKS, tm, LANES), F32))
    return pl.pallas_call(
        functools.partial(_mix_in_kernel, has_vres, seq_major_x, reset_first, nb, tm),
        grid=(m // tm,),
        in_specs=specs,
        out_specs=out_specs,
        out_shape=outs,
        scratch_shapes=scratch,
        compiler_params=_params("arbitrary"),
        name="mix_in",
    )(*_arrays(ins))


def _wkv_kernel(c, n_seq, group, stride, r_ref, lw_ref, k_ref, v_ref, na_ref, bb_ref, gate_ref, s0_ref,
                rk_ref, lnw_ref, lnb_ref, tri_ref, mean_ref,
                y_ref, sout_ref, s_scr):
    li = pl.program_id(1)

    @pl.when(li == 0)
    def _():
        zero = jnp.zeros((RWKV_HEAD, RWKV_HEAD), F32)
        for j in range(n_seq):
            for p in range(N_PAIRS):
                s_scr[j, p] = jnp.concatenate(
                    [jnp.concatenate([s0_ref[j, 2 * p], zero], axis=1),
                     jnp.concatenate([zero, s0_ref[j, 2 * p + 1]], axis=1)], axis=0)

    lane = lax.broadcasted_iota(jnp.int32, (1, HEAD_PAIR), 1)
    head_masks = (lane < RWKV_HEAD, lane >= RWKV_HEAD)
    trow = lax.broadcasted_iota(jnp.int32, (c, 2 * c), 0)
    tcol = lax.broadcasted_iota(jnp.int32, (c, 2 * c), 1)
    left_half = tcol < c
    scol = jnp.where(left_half, tcol, tcol - c)
    strict = scol < trow
    strict_left = strict & left_half
    strict_right = strict & (tcol >= c)
    incl = scol <= trow
    eye2 = (scol == trow).astype(F32)
    drow = lax.broadcasted_iota(jnp.int32, (2 * c, 2 * c), 0)
    dcol = lax.broadcasted_iota(jnp.int32, (2 * c, 2 * c), 1)
    diag_blocks = jnp.bitwise_xor(drow - c, dcol - c) >= 0
    brow = lax.broadcasted_iota(jnp.int32, (HEAD_PAIR, HEAD_PAIR), 0)
    bcol = lax.broadcasted_iota(jnp.int32, (HEAD_PAIR, HEAD_PAIR), 1)
    same_head = jnp.bitwise_xor(brow - RWKV_HEAD, bcol - RWKV_HEAD) >= 0
    n_double = max(int(math.log2(c)) - 1, 0)
    tri = tri_ref[...]
    zeros_c = jnp.zeros((c, HEAD_PAIR), F32)

    def pair_chunk(lw, r, k, v, na, bb, gate, sp, cols):
        lw_hi, lw_lo = _split2(lw)
        cl = jnp.dot(tri, lw_hi, preferred_element_type=F32) + jnp.dot(tri, lw_lo, preferred_element_type=F32)
        yield
        e_in = jnp.exp(cl)
        e_neg = jnp.exp(-cl)
        g_last = jnp.exp(jnp.sum(lw, axis=0, keepdims=True))
        at = na * jnp.exp(cl - lw)
        rt = r * e_in
        bt = bb * e_neg
        kt = k * e_neg
        ar = jnp.concatenate([at, rt], axis=0)
        bk = jnp.concatenate([bt, kt], axis=0)
        a_s = _bdot_nt(ar, sp)
        g0 = _bdot_nt(jnp.where(head_masks[0], ar, 0.0), bk)
        g1 = _bdot_nt(jnp.where(head_masks[1], ar, 0.0), jnp.concatenate([kt, bt], axis=0))
        yield
        ga0, ga1 = g0[0:c, :], g1[0:c, :]
        low = jnp.where(strict, jnp.where(left_half, ga0, ga1), 0.0)
        ak0 = _bdot(jnp.where(strict_right, ga0, 0.0), jnp.concatenate([zeros_c, v], axis=0))
        ak1 = _bdot(jnp.where(strict_left, ga1, 0.0), jnp.concatenate([v, zeros_c], axis=0))
        rhs = jnp.concatenate([jnp.where(head_masks[0], a_s[0:c, :] + ak0, 0.0),
                               jnp.where(head_masks[1], a_s[0:c, :] + ak1, 0.0)], axis=0)
        blockdiag = lambda x: jnp.where(diag_blocks, jnp.concatenate([x, x], axis=0), 0.0)
        t_inv = eye2 + low
        pw = low
        if n_double > 0:
            pw = _bdot(pw, blockdiag(pw))
            yield
        for level in range(n_double):
            if level == n_double - 1:
                t_inv = t_inv + _bdot(t_inv, blockdiag(pw))
            else:
                both = _bdot(jnp.concatenate([t_inv, pw], axis=0), blockdiag(pw))
                t_inv = t_inv + both[0:c, :]
                pw = both[c:2 * c, :]
            yield
        u = _bdot(t_inv, rhs)
        yield
        uv = jnp.concatenate([u, v], axis=0)
        y = a_s[c:2 * c, :] + jnp.where(
            head_masks[0], _bdot(jnp.where(incl, g0[c:2 * c, :], 0.0), uv),
            _bdot(jnp.where(incl, g1[c:2 * c, :], 0.0), jnp.concatenate([v, u], axis=0)))
        upd = _bdot_tn(uv, bk * g_last)
        s_new = sp * g_last + jnp.where(same_head, upd, 0.0)
        yield
        stats = _bdot(jnp.concatenate([y, r * k * rk_ref[:, cols]], axis=0), mean_ref[...])
        mu = stats[0:c, :]
        bonus = stats[c:2 * c, :] * float(RWKV_HEAD) * v
        yield
        d = y - mu
        var = _bdot(d * d, mean_ref[...])
        yield
        yn = d * lax.rsqrt(var + GN_EPS) * lnw_ref[:, cols] + lnb_ref[:, cols]
        return (yn + bonus) * gate, s_new

    first_seq = pl.program_id(0) * n_seq

    def seq_group(gi, carry):
        units = []
        for jj in range(group):
            j = gi * group + jj
            rows = pl.ds(first_seq + j, c, stride=stride)
            for p in range(N_PAIRS):
                cols = slice(p * HEAD_PAIR, (p + 1) * HEAD_PAIR)
                vals = [ref[p, rows, :] for ref in (lw_ref, r_ref, k_ref, v_ref, na_ref, bb_ref, gate_ref)]
                units.append((j, p, rows, pair_chunk(*vals, s_scr[j, p], cols)))
        results = _run_lockstep([gen for (_, _, _, gen) in units])
        for (j, p, rows, _), (y, s_new) in zip(units, results):
            y_ref[p, rows, :] = y
            s_scr[j, p] = s_new
        return carry

    lax.fori_loop(0, n_seq // group, seq_group, 0)

    @pl.when(li == pl.num_programs(1) - 1)
    def _():
        for j in range(n_seq):
            for p in range(N_PAIRS):
                sp = s_scr[j, p]
                sout_ref[j, 2 * p] = sp[0:RWKV_HEAD, 0:RWKV_HEAD]
                sout_ref[j, 2 * p + 1] = sp[RWKV_HEAD:HEAD_PAIR, RWKV_HEAD:HEAD_PAIR]


WKV_SEQ_PER_STEP = 8
WKV_SEQ_GROUP = 8
WKV_SEQ_GROUP_LONG = 4


def _wkv(arrs, s_all, layer, p, nb, seq):
    c = min(WKV_CHUNK, seq)
    n_seq = nb if seq > c else min(nb, WKV_SEQ_PER_STEP)
    assert seq % c == 0 and c % 8 == 0 and nb % n_seq == 0 and n_seq % WKV_SEQ_GROUP == 0
    blk = pl.BlockSpec((N_PAIRS, c * nb, HEAD_PAIR), lambda g, l: (0, l, 0))
    tri = jnp.tril(jnp.ones((c, c), F32)).astype(BF16)
    st_spec = pl.BlockSpec((None, n_seq, RWKV_HEADS, RWKV_HEAD, RWKV_HEAD), lambda g, l: (layer, g, 0, 0, 0))
    consts = [p["r_k"], p["ln_x_w"], p["ln_x_b"], tri, p["mean_pair"]]
    return pl.pallas_call(
        functools.partial(_wkv_kernel, c, n_seq, WKV_SEQ_GROUP if seq == c else WKV_SEQ_GROUP_LONG, nb),
        grid=(nb // n_seq, seq // c),
        in_specs=[blk] * 7 + [st_spec] + [_spec(a) for a in consts],
        out_specs=[blk, st_spec],
        out_shape=[jax.ShapeDtypeStruct((N_PAIRS, seq * nb, HEAD_PAIR), F32), jax.ShapeDtypeStruct(s_all.shape, F32)],
        input_output_aliases={7: 1},
        scratch_shapes=[pltpu.VMEM((n_seq, N_PAIRS, HEAD_PAIR, HEAD_PAIR), F32)],
        compiler_params=_params("arbitrary", "arbitrary"),
        name="wkv",
    )(*arrs, s_all, *_arrays(consts))


def _ssm_init(nb, consts, scratch):
    conv0_ref, h0_ref, re0_ref, im0_ref = consts[0], consts[1], consts[9], consts[10]
    xbuf, _, _, hcar, _, _, car_re, car_im = scratch
    xbuf[0:(CONV_W - 1) * nb, :] = conv0_ref[...]
    hcar[...] = h0_ref[...]
    car_re[...] = re0_ref[...]
    car_im[...] = im0_ref[...]


def _ssm_steps(reset_first, nb, tm, zx_ref, zg_ref, u_ref,
               conv0_ref, h0_ref, cw_ref, cb_ref, wa_ref, ba_ref, wi_ref, bi_ref, lam_ref,
               re0_ref, im0_ref, are_ref, aim_ref, bre_ref, bim_ref, cre_ref, cim_ref, d_ref, wglu_ref, bglu_ref,
               y_ref, conv_out_ref, h_out_ref, yc_ref, re_out_ref, im_out_ref,
               xbuf, abuf, bbuf, hcar, xre, xim, car_re, car_im):
    i = pl.program_id(0)
    hist = (CONV_W - 1) * nb

    u = u_ref[...]
    xre[...] = _bdot(u, bre_ref[...])
    xim[...] = _bdot(u, bim_ref[...])

    xbuf[hist:hist + tm, :] = zx_ref[...]
    xc = cb_ref[...]
    for j in range(CONV_W):
        xc = xc + xbuf[j * nb:j * nb + tm, :] * cw_ref[j:j + 1, :]
    conv_out_ref[...] = xbuf[tm:tm + hist, :]
    xbuf[0:hist, :] = xbuf[tm:tm + hist, :]

    gate_a = _sigmoid(_bdot(xc, wa_ref[...]) + ba_ref[...])
    gate_i = _sigmoid(_bdot(xc, wi_ref[...]) + bi_ref[...])
    log_a = -LRU_C * gate_a * _softplus(-lam_ref[...])
    a_sq = jnp.exp(2.0 * log_a)
    mult = jnp.sqrt(-jnp.tanh(log_a) * (a_sq + 1.0))
    if reset_first:
        row = lax.broadcasted_iota(jnp.int32, (tm, 1), 0)
        mult = jnp.where(row < jnp.where(i == 0, nb, 0), 1.0, mult)
    abuf[...] = jnp.exp(log_a)
    bbuf[...] = mult * gate_i * xc
    yield

    a_re = jnp.broadcast_to(are_ref[...], (nb, S5_WIDTH))
    a_im = jnp.broadcast_to(aim_ref[...], (nb, S5_WIDTH))
    h, s_re, s_im = hcar[...], car_re[...], car_im[...]
    for t in range(tm // nb):
        rows = slice(t * nb, (t + 1) * nb)
        h = abuf[rows, :] * h + bbuf[rows, :]
        bbuf[rows, :] = h
        s_re, s_im = (a_re * s_re - a_im * s_im + xre[rows, :], a_re * s_im + a_im * s_re + xim[rows, :])
        xre[rows, :] = s_re
        xim[rows, :] = s_im
    hcar[...] = h
    h_out_ref[...] = h
    car_re[...] = s_re
    car_im[...] = s_im
    re_out_ref[...] = s_re
    im_out_ref[...] = s_im

    y_ref[...] = bbuf[...] * _gelu(zg_ref[...])
    y = _bdot(xre[...], cre_ref[...]) - _bdot(xim[...], cim_ref[...]) + d_ref[...] * u
    yield
    o = _bdot(_gelu(y), wglu_ref[...]) + bglu_ref[...]
    yield
    yc_ref[...] = o[:, 0:S5_DIM] * _sigmoid(o[:, S5_DIM:2 * S5_DIM])


def _mix_out_kernel(n_part, x_ref, ya_ref, yb_ref, yc_ref, wa_ref, wb_ref, wc_ref, gq_ref, wq_ref, x1_ref, q_ref):
    def part(lo, n):
        rows = slice(lo, lo + n)
        ya = jnp.concatenate([ya_ref[p, rows, :] for p in range(N_PAIRS)], axis=-1)
        x1 = (x_ref[rows, :] + _bdot(ya, wa_ref[...]) + _bdot(yb_ref[rows, :], wb_ref[...])
              + _bdot(yc_ref[rows, :], wc_ref[...]))
        yield
        q = _bdot(_rms(x1, gq_ref[...]), wq_ref[...])
        yield
        for cb in range(LANE_BLOCKS):
            cols = slice(cb * LANES, (cb + 1) * LANES)
            x1_ref[cb, rows, :] = x1[:, cols]
            q_ref[cb, rows, :] = q[:, cols]

    n = x_ref.shape[0] // n_part
    _run_lockstep([part(i * n, n) for i in range(n_part)])


MIX_OUT_PARTS = 2


def _mix_out(x, ya, yb, yc, p):
    m = x.shape[0]
    tm = _tile(m, 8 * MIX_OUT_PARTS, 512)
    row = lambda w: pl.BlockSpec((tm, w), lambda i: (i, 0))
    split = pl.BlockSpec((LANE_BLOCKS, tm, LANES), lambda i: (0, i, 0))
    consts = [p["w_out_a"], p["w_out_b"], p["w_out_c"], p["norm_mem_q"], p["mem_wq"]]
    out = jax.ShapeDtypeStruct((LANE_BLOCKS, m, LANES), F32)
    return pl.pallas_call(
        functools.partial(_mix_out_kernel, MIX_OUT_PARTS),
        grid=(m // tm,),
        in_specs=[row(D_MODEL), pl.BlockSpec((N_PAIRS, tm, HEAD_PAIR), lambda i: (0, i, 0)), row(LRU_DIM),
                  row(S5_DIM)] + [_spec(a) for a in consts],
        out_specs=[split, split],
        out_shape=[out, out],
        compiler_params=_params("parallel"),
        name="mix_out",
    )(x, ya, yb, yc, *_arrays(consts))


def _load_seq_rows(ref, rows):
    return jnp.concatenate([ref[cb, rows, :] for cb in range(LANE_BLOCKS)], axis=-1)


def _store_seq_rows(ref, rows, val):
    for cb in range(LANE_BLOCKS):
        ref[cb, rows, :] = val[:, cb * LANES:(cb + 1) * LANES]


def _softmax_rows(s):
    e = jnp.exp(s - jnp.max(s, axis=-1, keepdims=True))
    return e / jnp.sum(e, axis=-1, keepdims=True)


def _attn_kernel(tl, n_seq, stride, q_ref, x_ref, mk_ref, mv_ref, wo_ref, o_ref, seq_axis=0):
    scale = MEM_HEAD_DIM ** -0.5
    first_seq = 0 if seq_axis is None else pl.program_id(seq_axis) * n_seq
    head_cols = [slice(hh * MEM_HEAD_DIM, (hh + 1) * MEM_HEAD_DIM) for hh in range(MEM_HEADS)]

    def one_seq(j):
        rows = pl.ds(first_seq + j, tl, stride=stride)
        q = _load_seq_rows(q_ref, rows)
        scores = [_bdot_nt(q[:, cols], mk_ref[j, :, cols]) * scale for cols in head_cols]
        yield
        heads = [_bdot(_softmax_rows(s), mv_ref[j, :, cols]) for s, cols in zip(scores, head_cols)]
        yield
        out = _load_seq_rows(x_ref, rows) + _bdot(jnp.concatenate(heads, axis=-1), wo_ref[...])
        yield
        _store_seq_rows(o_ref, rows, out)

    for j0 in range(0, n_seq, ATTN_LOCKSTEP):
        _run_lockstep([one_seq(j) for j in range(j0, min(j0 + ATTN_LOCKSTEP, n_seq))])


def _attn_cache_kernel(tl, n_seq, stride, q_ref, x_ref, mk_ref, mv_ref, wo_ref, o_ref):
    scale = MEM_HEAD_DIM ** -0.5
    first_seq = pl.program_id(0) * n_seq
    qrow = lax.broadcasted_iota(jnp.int32, (MEM_HEADS * tl, N_MEM * MEM_HEADS), 0)
    kcol = lax.broadcasted_iota(jnp.int32, (MEM_HEADS * tl, N_MEM * MEM_HEADS), 1)
    q_head = jnp.zeros_like(qrow)
    for hh in range(1, MEM_HEADS):
        q_head = q_head + (qrow >= hh * tl).astype(jnp.int32)
    same_head = q_head == jnp.bitwise_and(kcol, MEM_HEADS - 1)

    def one_seq(j):
        rows = pl.ds(first_seq + j, tl, stride=stride)
        q = _load_seq_rows(q_ref, rows)
        qh = jnp.concatenate([q[:, hh * MEM_HEAD_DIM:(hh + 1) * MEM_HEAD_DIM] for hh in range(MEM_HEADS)], axis=0)
        k2 = mk_ref[j].reshape(N_MEM * MEM_HEADS, MEM_HEAD_DIM)
        s = jnp.where(same_head, _bdot_nt(qh, k2) * scale, MASKED_SCORE)
        yield
        v2 = mv_ref[j].reshape(N_MEM * MEM_HEADS, MEM_HEAD_DIM)
        oh = _bdot(_softmax_rows(s), v2)
        yield
        o = jnp.concatenate([oh[hh * tl:(hh + 1) * tl, :] for hh in range(MEM_HEADS)], axis=-1)
        out = _load_seq_rows(x_ref, rows) + _bdot(o, wo_ref[...])
        yield
        _store_seq_rows(o_ref, rows, out)

    _run_lockstep([one_seq(j) for j in range(n_seq)])


MASKED_SCORE = -1e30
ATTN_SEQ_PER_STEP = 4
ATTN_ROWS = 128
ATTN_LOCKSTEP = 4


def _attn(q, x1, mk, mv, kv_offset, wo, nb, seq):
    tl = _tile(seq, 8, ATTN_ROWS)
    n_seq = nb if seq > tl else min(nb, ATTN_SEQ_PER_STEP)
    assert nb % n_seq == 0 and kv_offset % n_seq == 0 and MEM_HEADS & (MEM_HEADS - 1) == 0
    blk = pl.BlockSpec((LANE_BLOCKS, tl * nb, LANES), lambda g, l: (0, l, 0))
    kv_index = lambda g, l: (kv_offset // n_seq + g,) + (0,) * (mk.ndim - 1)
    kv = pl.BlockSpec((n_seq,) + mk.shape[1:], kv_index)
    body = _attn_kernel if mk.ndim == 3 else _attn_cache_kernel
    return pl.pallas_call(
        functools.partial(body, tl, n_seq, nb),
        grid=(nb // n_seq, seq // tl),
        in_specs=[blk, blk, kv, kv, _spec(wo)],
        out_specs=blk,
        out_shape=jax.ShapeDtypeStruct((LANE_BLOCKS, seq * nb, LANES), F32),
        compiler_params=_params("arbitrary", "arbitrary"),
        name="mem_attn",
    )(q, x1, mk, mv, *_arrays([wo]))


def _mix_attn_kernel(tl, n_seq, x_ref, ya_ref, yb_ref, yc_ref, wa_ref, wb_ref, wc_ref, gq_ref, wq_ref,
                     mk_ref, mv_ref, wo_ref, o_ref, x1_scr, q_scr):
    _mix_out_kernel(MIX_OUT_PARTS, x_ref, ya_ref, yb_ref, yc_ref, wa_ref, wb_ref, wc_ref, gq_ref, wq_ref,
                    x1_scr, q_scr)
    _attn_kernel(tl, n_seq, n_seq, q_scr, x1_scr, mk_ref, mv_ref, wo_ref, o_ref, seq_axis=None)


ATTN_ROWS_FUSED = 128


def _mix_attn(x, ya, yb, yc, mk, mv, kv_offset, p, nb, seq):
    tl = _tile(seq, 8, ATTN_ROWS_FUSED)
    tm = tl * nb
    assert tm % (8 * MIX_OUT_PARTS) == 0 and kv_offset % nb == 0 and mk.ndim == 3
    row = lambda w: pl.BlockSpec((tm, w), lambda i: (i, 0))
    split = pl.BlockSpec((LANE_BLOCKS, tm, LANES), lambda i: (0, i, 0))
    kv = pl.BlockSpec((nb,) + mk.shape[1:], lambda i: (kv_offset // nb, 0, 0))
    consts = [p["w_out_a"], p["w_out_b"], p["w_out_c"], p["norm_mem_q"], p["mem_wq"]]
    return pl.pallas_call(
        functools.partial(_mix_attn_kernel, tl, nb),
        grid=(seq // tl,),
        in_specs=[row(D_MODEL), pl.BlockSpec((N_PAIRS, tm, HEAD_PAIR), lambda i: (0, i, 0)), row(LRU_DIM),
                  row(S5_DIM)] + [_spec(a) for a in consts] + [kv, kv, _spec(p["mem_wo"])],
        out_specs=split,
        out_shape=jax.ShapeDtypeStruct((LANE_BLOCKS, seq * nb, LANES), F32),
        scratch_shapes=[pltpu.VMEM((LANE_BLOCKS, tm, LANES), F32), pltpu.VMEM((LANE_BLOCKS, tm, LANES), F32)],
        compiler_params=_params("arbitrary"),
        name="mix_attn",
    )(x, ya, yb, yc, *_arrays(consts), mk, mv, *_arrays([p["mem_wo"]]))


def _ffn_kernel(d_ff, final_norm, n_seq, x_ref, g_ref, wup_ref, wdown_ref, *rest):
    x = jnp.concatenate([x_ref[cb] for cb in range(LANE_BLOCKS)], axis=-1)
    h = _rms(x, g_ref[...]).astype(BF16)
    gt = jnp.dot(h, wup_ref[:, 0:d_ff], preferred_element_type=F32)
    up = jnp.dot(h, wup_ref[:, d_ff:2 * d_ff], preferred_element_type=F32)
    act = gt * _sigmoid(gt) * up
    out = x + _bdot(act, wdown_ref[...])
    if final_norm:
        out = _rms(out, rest[0][...])
    if n_seq:
        o_ref, os = rest[-2:]
        for cb in range(LANE_BLOCKS):
            os[cb] = out[:, cb * LANES:(cb + 1) * LANES]
        for b in range(n_seq):
            o_ref[b] = _load_seq_rows(os, pl.ds(b, out.shape[0] // n_seq, stride=n_seq))
    else:
        rest[-1][...] = out


def _ffn(x, p, g_final=None, seq_major_out=None):
    m = x.shape[1]
    tm = _tile(m, 8, 512)
    d_ff = p["ffn_w_down"].shape[0]
    consts = [p["norm_ffn"], p["ffn_w_up"], p["ffn_w_down"]] + ([] if g_final is None else [g_final])
    if seq_major_out is None:
        n_seq, scratch = 0, []
        out_spec = pl.BlockSpec((tm, D_MODEL), lambda i: (i, 0))
        out_shape = jax.ShapeDtypeStruct((m, D_MODEL), F32)
    else:
        n_seq, seq = seq_major_out
        assert tm % n_seq == 0 and (tm // n_seq) % 8 == 0 and n_seq * seq == m
        scratch = [pltpu.VMEM((LANE_BLOCKS, tm, LANES), F32)]
        out_spec = pl.BlockSpec((n_seq, tm // n_seq, D_MODEL), lambda i: (0, i, 0))
        out_shape = jax.ShapeDtypeStruct((n_seq, seq, D_MODEL), F32)
    return pl.pallas_call(
        functools.partial(_ffn_kernel, d_ff, g_final is not None, n_seq),
        grid=(m // tm,),
        in_specs=[pl.BlockSpec((LANE_BLOCKS, tm, LANES), lambda i: (0, i, 0))] + [_spec(a) for a in consts],
        out_specs=out_spec,
        out_shape=out_shape,
        scratch_shapes=scratch,
        compiler_params=_params("parallel"),
        name="ffn",
    )(x, *_arrays(consts))


def _mem_project_kernel(x_ref, g_ref, wk_ref, wv_ref, k_ref, v_ref, kb_ref, vb_ref):
    m = _rms(x_ref[...], g_ref[...]).astype(BF16)
    k = jnp.dot(m, wk_ref[...], preferred_element_type=F32)
    v = jnp.dot(m, wv_ref[...], preferred_element_type=F32)
    k_ref[...] = k.reshape(k_ref.shape)
    v_ref[...] = v.reshape(v_ref.shape)
    kb_ref[...] = k.astype(BF16)
    vb_ref[...] = v.astype(BF16)


MEM_SEQ_PER_STEP = 2


def _mem_project(mem, g, wk, wv):
    n, n_mem, _ = mem.shape
    depth = wk.shape[0]
    tm = MEM_SEQ_PER_STEP * n_mem
    assert n % MEM_SEQ_PER_STEP == 0
    per_layer = lambda a: pl.BlockSpec((None,) + a.shape[1:], lambda l, i: (l,) + (0,) * (a.ndim - 1))
    cache = pl.BlockSpec((None, MEM_SEQ_PER_STEP, n_mem, MEM_HEADS, MEM_HEAD_DIM), lambda l, i: (l, i, 0, 0, 0))
    rows_b = pl.BlockSpec((None, tm, D_MODEL), lambda l, i: (l, i, 0))
    out = jax.ShapeDtypeStruct((depth, n, n_mem, MEM_HEADS, MEM_HEAD_DIM), F32)
    outb = jax.ShapeDtypeStruct((depth, n * n_mem, D_MODEL), BF16)
    return pl.pallas_call(
        _mem_project_kernel, grid=(depth, n // MEM_SEQ_PER_STEP),
        in_specs=[pl.BlockSpec((tm, D_MODEL), lambda l, i: (i, 0)), per_layer(g), per_layer(wk), per_layer(wv)],
        out_specs=[cache, cache, rows_b, rows_b], out_shape=[out, out, outb, outb],
        compiler_params=_params("parallel", "parallel"),
        name="mem_project",
    )(mem.reshape(n * n_mem, D_MODEL), g, wk, wv)


def _block_diag(blocks):
    n, r, c = blocks.shape[-3:]
    eye = jnp.eye(n, dtype=blocks.dtype)
    out = eye[:, None, :, None] * blocks[..., :, :, None, :]
    return out.reshape(blocks.shape[:-3] + (n * r, n * c))


def _stack_params(w):
    depth = w["norm_mix"].shape[0]
    row = lambda a: a.reshape(a.shape[0], 1, -1).astype(F32)
    bf = lambda a: a.astype(BF16)
    t = lambda a: jnp.swapaxes(a, -1, -2)
    p = {k: row(w[k]) for k in ("norm_mix", "mu_rkv", "w0", "a0", "k_k", "k_a", "r_k", "ln_x_w", "ln_x_b", "conv_b",
                                "lru_ba", "lru_bi", "lru_lambda", "s5_d", "s5_b_glu", "norm_mem_q", "norm_mem_kv",
                                "norm_ffn", "mu_v", "v0")}
    p.update({k: bf(w[k]) for k in ("w_in", "w1", "w2", "a1", "a2", "g1", "g2", "s5_w_glu", "mem_wq", "mem_wk",
                                    "mem_wv", "mem_wo", "ffn_w_up", "ffn_w_down", "v1", "v2")})
    p["mu_wag"], p["conv_w"] = w["mu_wag"], w["conv_w"]
    p["lru_wa"], p["lru_wi"] = bf(_block_diag(w["lru_wa"])), bf(_block_diag(w["lru_wi"]))
    w_out = bf(w["w_out"])
    p["w_out_a"] = w_out[:, 0:RWKV_DIM]
    p["w_out_b"] = w_out[:, RWKV_DIM:RWKV_DIM + LRU_DIM]
    p["w_out_c"] = w_out[:, RWKV_DIM + LRU_DIM:]
    lam_re, lam_im = w["s5_a_re"].astype(F32), w["s5_a_im"].astype(F32)
    dt = jnp.exp(w["s5_log_dt"].astype(F32))[..., None]
    mag = jnp.exp(lam_re * dt)
    abar_re, abar_im = mag * jnp.cos(lam_im * dt), mag * jnp.sin(lam_im * dt)
    den = lam_re * lam_re + lam_im * lam_im
    q_re = ((abar_re - 1.0) * lam_re + abar_im * lam_im) / den
    q_im = (abar_im * lam_re - (abar_re - 1.0) * lam_im) / den
    b_re, b_im = w["s5_b_re"].astype(F32), w["s5_b_im"].astype(F32)
    bbar_re = q_re[..., None] * b_re - q_im[..., None] * b_im
    bbar_im = q_re[..., None] * b_im + q_im[..., None] * b_re
    p["s5_abar_re"] = abar_re.reshape(depth, 1, S5_WIDTH)
    p["s5_abar_im"] = abar_im.reshape(depth, 1, S5_WIDTH)
    p["s5_bbar_re"] = bf(_block_diag(t(bbar_re)))
    p["s5_bbar_im"] = bf(_block_diag(t(bbar_im)))
    p["s5_c_re"] = bf(_block_diag(t(w["s5_c_re"])))
    p["s5_c_im"] = bf(_block_diag(t(w["s5_c_im"])))
    return p


def _layer_params(stacked, l, consts):
    vres_keys = ("mu_v", "v0", "v1", "v2")
    p = {k: _Layer(a, l) for k, a in stacked.items() if k not in vres_keys}
    p.update(consts)
    vres = {k: _Layer(stacked[k], l - 1) for k in vres_keys} if l > 0 else None
    return p, vres


def _time_major(a):
    return jnp.swapaxes(a, 0, 1).reshape((a.shape[0] * a.shape[1],) + a.shape[2:])


def _run_group(x, mem_k, mem_v, kv_stride, shift0, wkv0, conv0, lru0, re0, im0, reset_first, layers, norm_final,
               relayout_in_kernel):
    nb, seq, _ = x.shape
    xt = x if relayout_in_kernel else _time_major(x)
    outs = {k: [] for k in ("shift", "conv", "lru", "re", "im")}
    wkv_all = wkv0
    v_first = None
    for l, (p, vres) in enumerate(layers):
        last = l == len(layers) - 1
        mixed = _mix_in(xt, _Layer(shift0, l), p, vres, v_first, _time_major(conv0[l]), _Layer(lru0, l),
                        _Layer(re0.reshape(-1, nb, S5_WIDTH), l), _Layer(im0.reshape(-1, nb, S5_WIDTH), l),
                        reset_first, nb, seq)
        if xt.ndim == 3:
            xt, mixed = mixed[0], mixed[1:]
        r, lw, k, v, na, bb, gate, h_last, yb, conv_new, lru_new, yc, re_new, im_new = mixed
        if l == 0:
            v_first = v
        ya, wkv_all = _wkv((r, lw, k, v, na, bb, gate), wkv_all, l, p, nb, seq)
        if mem_k.ndim == 3 and seq > ATTN_ROWS_FUSED:
            x2 = _mix_attn(xt, ya, yb, yc, mem_k, mem_v, l * kv_stride, p, nb, seq)
        else:
            x1, q = _mix_out(xt, ya, yb, yc, p)
            x2 = _attn(q, x1, mem_k, mem_v, l * kv_stride, p["mem_wo"], nb, seq)
        xt = _ffn(x2, p, norm_final if last else None, (nb, seq) if last and relayout_in_kernel else None)
        outs["shift"].append(h_last)
        outs["conv"].append(jnp.swapaxes(conv_new.reshape(CONV_W - 1, nb, LRU_DIM), 0, 1))
        outs["lru"].append(lru_new)
        outs["re"].append(re_new.reshape(nb, S5_GROUPS, S5_STATE))
        outs["im"].append(im_new.reshape(nb, S5_GROUPS, S5_STATE))
    y = xt if relayout_in_kernel else jnp.swapaxes(xt.reshape(seq, nb, D_MODEL), 0, 1)
    st = {k: jnp.stack(v) for k, v in outs.items()}
    return (y, st["shift"], wkv_all, st["conv"], st["lru"], st["re"], st["im"])


def kernel(x_prompt, x_sample, mem_prompt, state_shift, state_wkv, state_conv, state_lru, state_s5_re, state_s5_im, cache_mem_k, cache_mem_v, norm_mix, w_in, w_out, mu_rkv, mu_wag, mu_v, w0, w1, w2, a0, a1, a2, v0, v1, v2, g1, g2, k_k, k_a, r_k, ln_x_w, ln_x_b, conv_w, conv_b, lru_wa, lru_ba, lru_wi, lru_bi, lru_lambda, s5_a_re, s5_a_im, s5_log_dt, s5_b_re, s5_b_im, s5_c_re, s5_c_im, s5_d, s5_w_glu, s5_b_glu, norm_mem_q, norm_mem_kv, mem_wq, mem_wk, mem_wv, mem_wo, norm_ffn, ffn_w_up, ffn_w_down, norm_final):
    w = dict(norm_mix=norm_mix, w_in=w_in, w_out=w_out, mu_rkv=mu_rkv, mu_wag=mu_wag, mu_v=mu_v, w0=w0, w1=w1,
             w2=w2, a0=a0, a1=a1, a2=a2, v0=v0, v1=v1, v2=v2, g1=g1, g2=g2, k_k=k_k, k_a=k_a, r_k=r_k,
             ln_x_w=ln_x_w, ln_x_b=ln_x_b, conv_w=conv_w, conv_b=conv_b, lru_wa=lru_wa, lru_ba=lru_ba,
             lru_wi=lru_wi, lru_bi=lru_bi, lru_lambda=lru_lambda, s5_a_re=s5_a_re, s5_a_im=s5_a_im,
             s5_log_dt=s5_log_dt, s5_b_re=s5_b_re, s5_b_im=s5_b_im, s5_c_re=s5_c_re, s5_c_im=s5_c_im,
             s5_d=s5_d, s5_w_glu=s5_w_glu, s5_b_glu=s5_b_glu, norm_mem_q=norm_mem_q, norm_mem_kv=norm_mem_kv,
             mem_wq=mem_wq, mem_wk=mem_wk, mem_wv=mem_wv, mem_wo=mem_wo, norm_ffn=norm_ffn, ffn_w_up=ffn_w_up,
             ffn_w_down=ffn_w_down)
    depth = norm_mix.shape[0]
    ones64 = jnp.ones((RWKV_HEADS, RWKV_HEAD, RWKV_HEAD), F32)
    consts = {"ones_head": _block_diag(ones64).astype(BF16),
              "mean_pair": (_block_diag(ones64[:2]) / RWKV_HEAD).astype(BF16)}
    stacked = _stack_params(w)
    layers = [_layer_params(stacked, l, consts) for l in range(depth)]
    g_final = norm_final.reshape(1, D_MODEL)

    bp, n_mem, _ = mem_prompt.shape
    mem_k_p, mem_v_p, mem_k_b, mem_v_b = _mem_project(
        mem_prompt, stacked["norm_mem_kv"], stacked["mem_wk"], stacked["mem_wv"])
    mem_k_b = mem_k_b.reshape(depth * bp, n_mem, D_MODEL)
    mem_v_b = mem_v_b.reshape(depth * bp, n_mem, D_MODEL)

    zeros = lambda *s: jnp.zeros((depth, bp) + s, F32)
    prompt = _run_group(
        x_prompt, mem_k_b, mem_v_b, bp,
        zeros(D_MODEL), zeros(RWKV_HEADS, RWKV_HEAD, RWKV_HEAD), zeros(CONV_W - 1, LRU_DIM), zeros(LRU_DIM),
        zeros(S5_GROUPS, S5_STATE), zeros(S5_GROUPS, S5_STATE), True, layers, g_final, True)
    bs = x_sample.shape[0]
    cache_shape = (depth * bs, n_mem, MEM_HEADS, MEM_HEAD_DIM)
    sample = _run_group(
        x_sample, cache_mem_k.reshape(cache_shape), cache_mem_v.reshape(cache_shape),
        bs, state_shift, state_wkv, state_conv, state_lru, state_s5_re, state_s5_im, False, layers, g_final, False)
    (y_p, sh_p, wkv_p, conv_p, lru_p, re_p, im_p) = prompt
    (y_s, sh_s, wkv_s, conv_s, lru_s, re_s, im_s) = sample
    return (y_p, y_s, sh_p, sh_s, wkv_p, wkv_s, conv_p, conv_s, lru_p, lru_s, re_p, re_s, im_p, im_s,
            mem_k_p, mem_v_p)
```
